```python
import jax, jax.numpy as jnp
from jax import lax
import numpy as np

D_MODEL = 1024
BATCH = 16
SEQ = 2048
DEPTH = 1

CTX_LEN = 256
GRID_W = 64
NA_HEADS = 8
HEAD_DIM = 64
NA_WIDTH = NA_HEADS * HEAD_DIM
NA_WIN_ROWS = 8
NA_WIN_COLS = 16
GM_GROUPS = 8
GM_CHUNK = 128
GM_WIDTH = D_MODEL // 2
GM_GROUP_DIM = GM_WIDTH // GM_GROUPS
PEER_HEADS = 8
PEER_N_KEYS = 128
PEER_EXPERTS = PEER_N_KEYS * PEER_N_KEYS
PEER_TOPK = 16
PEER_QDIM = 256
PEER_HALF = PEER_QDIM // 2
PEER_BLOCK = 128
ADA_CHUNKS = 6
EPS = 1e-6
NEG_INF = -1e30
IN_SPLITS = (NA_WIDTH, 2 * NA_WIDTH, 3 * NA_WIDTH, 3 * NA_WIDTH + GM_WIDTH,
             3 * NA_WIDTH + 2 * GM_WIDTH, 3 * NA_WIDTH + 2 * GM_WIDTH + D_MODEL)
IN_COLS = 3 * NA_WIDTH + 2 * GM_WIDTH + 2 * D_MODEL

kernel_name = 'hybrid_natten_sgmlp_peer_block'


def rmsnorm(x, g):
    xf = x.astype(jnp.float32)
    y = xf * lax.rsqrt(jnp.mean(xf * xf, axis=-1, keepdims=True) + EPS)
    return (y * g.astype(jnp.float32)).astype(x.dtype)


def layernorm(x, g):
    xf = x.astype(jnp.float32)
    xc = xf - jnp.mean(xf, axis=-1, keepdims=True)
    y = xc * lax.rsqrt(jnp.mean(xc * xc, axis=-1, keepdims=True) + EPS)
    return (y * g.astype(jnp.float32)).astype(x.dtype)


def modulate(h, shift, scale):
    return h * (1 + scale) + shift


def heads(z):
    return z.reshape(z.shape[:-1] + (NA_HEADS, HEAD_DIM))


def neighborhood_attention(q, k, v, k_ctx, v_ctx, rpb):
    B, S = q.shape[0], q.shape[1]
    rows = S // GRID_W
    kh = min(NA_WIN_ROWS, rows)
    band = kh * GRID_W
    scale = HEAD_DIM ** -0.5
    qg = q.reshape(B, rows, GRID_W, NA_HEADS, HEAD_DIM)
    kg = k.reshape(B, rows, GRID_W, NA_HEADS, HEAD_DIM)
    vg = v.reshape(B, rows, GRID_W, NA_HEADS, HEAD_DIM)
    col = jnp.arange(GRID_W)
    col_start = jnp.clip(col - NA_WIN_COLS // 2, 0, GRID_W - NA_WIN_COLS)
    col_in = (col[None, :] >= col_start[:, None]) & (col[None, :] < col_start[:, None] + NA_WIN_COLS)
    band_mask = jnp.broadcast_to(col_in[:, None, :], (GRID_W, kh, GRID_W)).reshape(GRID_W, band)
    dc_idx = jnp.clip(col[None, :] - col[:, None] + NA_WIN_COLS - 1, 0, 2 * NA_WIN_COLS - 2)

    def row_step(r):
        rs = jnp.clip(r - NA_WIN_ROWS // 2, 0, rows - kh)
        q_r = lax.dynamic_index_in_dim(qg, r, axis=1, keepdims=False)
        k_b = lax.dynamic_slice_in_dim(kg, rs, kh, axis=1).reshape(B, band, NA_HEADS, HEAD_DIM)
        v_b = lax.dynamic_slice_in_dim(vg, rs, kh, axis=1).reshape(B, band, NA_HEADS, HEAD_DIM)
        dr_idx = rs + jnp.arange(kh) - r + NA_WIN_ROWS - 1
        bias = rpb[:, dr_idx][:, :, dc_idx]
        bias = bias.transpose(0, 2, 1, 3).reshape(NA_HEADS, GRID_W, band).astype(jnp.float32)
        s_loc = jnp.einsum('bqhd,bkhd->bhqk', q_r, k_b).astype(jnp.float32) * scale + bias
        s_loc = jnp.where(band_mask, s_loc, NEG_INF)
        s_ctx = jnp.einsum('bqhd,bchd->bhqc', q_r, k_ctx).astype(jnp.float32) * scale
        p = jax.nn.softmax(jnp.concatenate([s_loc, s_ctx], axis=-1), axis=-1).astype(v.dtype)
        return (jnp.einsum('bhqk,bkhd->bqhd', p[..., :band], v_b)
                + jnp.einsum('bhqc,bchd->bqhd', p[..., band:], v_ctx))

    o = lax.map(row_step, jnp.arange(rows))
    return jnp.moveaxis(o, 0, 1).reshape(B, S, NA_WIDTH)


def context_attention(q, k, v):
    s = jnp.einsum('bqhd,bkhd->bhqk', q, k).astype(jnp.float32) * (HEAD_DIM ** -0.5)
    p = jax.nn.softmax(s, axis=-1).astype(v.dtype)
    o = jnp.einsum('bhqk,bkhd->bqhd', p, v)
    return o.reshape(o.shape[0], o.shape[1], NA_WIDTH)


def spatial_gating(gu, gv, ln_g, ws, bs):
    B, L, _ = gv.shape
    n = L // GM_CHUNK
    u = jax.nn.gelu(gu)
    vn = layernorm(jax.nn.gelu(gv), ln_g)
    vg = vn.reshape(B, n, GM_CHUNK, GM_GROUPS, GM_GROUP_DIM)
    mixed = jnp.einsum('gpq,bnqgd->bnpgd', ws, vg) + bs.T[:, :, None]
    return u * mixed.reshape(B, L, GM_WIDTH)


def merge_branches(y_a, y_b, ga, gb, w_pa, w_pb, w_out):
    m = jax.nn.sigmoid(ga) * (y_a @ w_pa) + jax.nn.sigmoid(gb) * (y_b @ w_pb)
    return m @ w_out


def peer_ffn(h, wq, sub_keys, expert_u, expert_v):
    B, L, D = h.shape
    blocks = h.reshape(B * L // PEER_BLOCK, PEER_BLOCK, D)

    def block(xb):
        q = (xb @ wq).reshape(PEER_BLOCK, PEER_HEADS, 2, PEER_HALF)
        s = jnp.einsum('thpd,hpkd->thpk', q, sub_keys).astype(jnp.float32)
        s_top, i_top = lax.top_k(s, PEER_TOPK)
        cand_s = (s_top[:, :, 0, :, None] + s_top[:, :, 1, None, :]).reshape(PEER_BLOCK, PEER_HEADS, PEER_TOPK * PEER_TOPK)
        cand_i = (i_top[:, :, 0, :, None] * PEER_N_KEYS + i_top[:, :, 1, None, :]).reshape(PEER_BLOCK, PEER_HEADS, PEER_TOPK * PEER_TOPK)
        best_s, best_pos = lax.top_k(cand_s, PEER_TOPK)
        experts = jnp.take_along_axis(cand_i, best_pos, axis=-1)
        g = jax.nn.softmax(best_s, axis=-1)
        act = jax.nn.gelu(jnp.einsum('thkd,td->thk', expert_u[experts], xb).astype(jnp.float32))
        wgt = (g * act).astype(xb.dtype)
        return jnp.einsum('thk,thkd->td', wgt, expert_v[experts])

    return lax.map(block, blocks).reshape(B, L, D)


def setup_inputs(seed: int = 0) -> dict:
    key = jax.random.key(seed)
    ks = jax.random.split(key, 21)
    nrm = jax.random.normal
    return {
        'x': nrm(ks[0], (BATCH, SEQ, D_MODEL), jnp.float32),
        'c': nrm(ks[1], (BATCH, D_MODEL), jnp.float32),
        'ctx': nrm(ks[2], (BATCH, CTX_LEN, D_MODEL), jnp.float32),
        'c_ctx': nrm(ks[3], (D_MODEL,), jnp.float32),
        'ada_w': nrm(ks[4], (DEPTH, D_MODEL, ADA_CHUNKS * D_MODEL), jnp.float32) * (0.3 * D_MODEL ** -0.5),
        'ada_b': nrm(ks[5], (DEPTH, ADA_CHUNKS * D_MODEL), jnp.float32) * 0.02,
        'norm1_g': 1.0 + 0.01 * nrm(ks[6], (DEPTH, D_MODEL), jnp.float32),
        'norm2_g': 1.0 + 0.01 * nrm(ks[7], (DEPTH, D_MODEL), jnp.float32),
        'w_in': nrm(ks[8], (DEPTH, D_MODEL, IN_COLS), jnp.float32) * D_MODEL ** -0.5,
        'na_rpb': nrm(ks[9], (DEPTH, NA_HEADS, 2 * NA_WIN_ROWS - 1, 2 * NA_WIN_COLS - 1), jnp.float32) * 0.1,
        'gm_ln_g': 1.0 + 0.01 * nrm(ks[10], (DEPTH, GM_WIDTH), jnp.float32),
        'gm_ws': nrm(ks[11], (DEPTH, GM_GROUPS, GM_CHUNK, GM_CHUNK), jnp.float32) * GM_CHUNK ** -0.5,
        'gm_bs': 1.0 + 0.02 * nrm(ks[12], (DEPTH, GM_GROUPS, GM_CHUNK), jnp.float32),
        'w_proj_a': nrm(ks[13], (DEPTH, NA_WIDTH, D_MODEL), jnp.float32) * NA_WIDTH ** -0.5,
        'w_proj_b': nrm(ks[14], (DEPTH, GM_WIDTH, D_MODEL), jnp.float32) * GM_WIDTH ** -0.5,
        'w_out': nrm(ks[15], (DEPTH, D_MODEL, D_MODEL), jnp.float32) * D_MODEL ** -0.5,
        'peer_wq': nrm(ks[16], (DEPTH, D_MODEL, PEER_HEADS * PEER_QDIM), jnp.float32) * D_MODEL ** -0.5,
        'peer_keys': nrm(ks[17], (DEPTH, PEER_HEADS, 2, PEER_N_KEYS, PEER_HALF), jnp.float32) * PEER_HALF ** -0.5,
        'peer_u': nrm(ks[18], (DEPTH, PEER_EXPERTS, D_MODEL), jnp.float32) * D_MODEL ** -0.5,
        'peer_v': nrm(ks[19], (DEPTH, PEER_EXPERTS, D_MODEL), jnp.float32) * 0.5,
        'final_g': 1.0 + 0.01 * nrm(ks[20], (D_MODEL,), jnp.float32),
    }


def reference(x, c, ctx, c_ctx, ada_w, ada_b, norm1_g, norm2_g, w_in, na_rpb, gm_ln_g, gm_ws, gm_bs,
              w_proj_a, w_proj_b, w_out, peer_wq, peer_keys, peer_u, peer_v, final_g):
    for layer in range(DEPTH):
        last = layer == DEPTH - 1
        sh1, sc1, g1, sh2, sc2, g2 = jnp.split(
            (jax.nn.silu(c) @ ada_w[layer] + ada_b[layer])[:, None, :], ADA_CHUNKS, axis=-1)
        csh1, csc1, cg1, csh2, csc2, cg2 = jnp.split(
            jax.nn.silu(c_ctx) @ ada_w[layer] + ada_b[layer], ADA_CHUNKS, axis=-1)
        w = w_in[layer]
        h = modulate(rmsnorm(x, norm1_g[layer]), sh1, sc1)
        hc = modulate(rmsnorm(ctx, norm1_g[layer]), csh1, csc1)
        q, k, v, gu, gv, ga, gb = jnp.split(h @ w, IN_SPLITS, axis=-1)
        if last:
            k_c, v_c = jnp.split(hc @ w[:, NA_WIDTH:3 * NA_WIDTH], 2, axis=-1)
        else:
            qc, k_c, v_c, guc, gvc, gac, gbc = jnp.split(hc @ w, IN_SPLITS, axis=-1)
        y_a = neighborhood_attention(heads(q), heads(k), heads(v), heads(k_c), heads(v_c), na_rpb[layer])
        y_b = spatial_gating(gu, gv, gm_ln_g[layer], gm_ws[layer], gm_bs[layer])
        x = x + g1 * merge_branches(y_a, y_b, ga, gb, w_proj_a[layer], w_proj_b[layer], w_out[layer])
        h2 = modulate(rmsnorm(x, norm2_g[layer]), sh2, sc2)
        x = x + g2 * peer_ffn(h2, peer_wq[layer], peer_keys[layer], peer_u[layer], peer_v[layer])
        if not last:
            yc_a = context_attention(heads(qc), heads(k_c), heads(v_c))
            yc_b = spatial_gating(guc, gvc, gm_ln_g[layer], gm_ws[layer], gm_bs[layer])
            ctx = ctx + cg1 * merge_branches(yc_a, yc_b, gac, gbc, w_proj_a[layer], w_proj_b[layer], w_out[layer])
            hc2 = modulate(rmsnorm(ctx, norm2_g[layer]), csh2, csc2)
            ctx = ctx + cg2 * peer_ffn(hc2, peer_wq[layer], peer_keys[layer], peer_u[layer], peer_v[layer])
    return rmsnorm(x, final_g)
```

```python
import functools

import jax
import jax.numpy as jnp
from jax import lax
from jax.experimental import pallas as pl
from jax.experimental.pallas import tpu as pltpu
from jax.experimental.pallas import tpu_sc as plsc

F32 = jnp.float32
BF16 = jnp.bfloat16

GRID_W = 64
NA_HEADS = 8
HEAD_DIM = 64
NA_WIN_ROWS = 8
NA_WIN_COLS = 16
GM_GROUPS = 8
GM_CHUNK = 128
PEER_HEADS = 8
PEER_N_KEYS = 128
PEER_TOPK = 16
PEER_HALF = 128
ADA_CHUNKS = 6
EPS = 1e-6
NEG_INF = -1e30

LANES = 128
SUBLANES = 8
VMEM_LIMIT = 56 * 1024 * 1024


def _dot(a, b):
    return lax.dot_general(a, b, (((1,), (0,)), ((), ())), preferred_element_type=F32)


def _dot_nt(a, b):
    return lax.dot_general(a, b, (((1,), (1,)), ((), ())), preferred_element_type=F32)


def _cparams(sem):
    return pltpu.CompilerParams(dimension_semantics=sem, vmem_limit_bytes=VMEM_LIMIT)


def _adaln_kernel(c_ref, w_ref, b_ref, o_ref):
    c = c_ref[...]
    s = c * jax.nn.sigmoid(c)
    o_ref[...] = lax.dot_general(s, w_ref[...], (((1,), (0,)), ((), ())),
                                 precision=lax.Precision.HIGHEST,
                                 preferred_element_type=F32) + b_ref[...]


def _adaln(c_all, w, b):
    m, d = c_all.shape
    n = w.shape[1]
    tn = 1024
    return pl.pallas_call(
        _adaln_kernel,
        grid=(n // tn,),
        in_specs=[pl.BlockSpec((m, d), lambda j: (0, 0)),
                  pl.BlockSpec((d, tn), lambda j: (0, j)),
                  pl.BlockSpec((1, tn), lambda j: (0, j))],
        out_specs=pl.BlockSpec((m, tn), lambda j: (0, j)),
        out_shape=jax.ShapeDtypeStruct((m, n), F32),
        compiler_params=_cparams(("arbitrary",)),
        name="adaln",
    )(c_all, w, b.reshape(1, n))


def _norm_proj_kernel(widths, x_ref, g_ref, sh_ref, sc_ref, w_ref, *o_refs):
    x = x_ref[0]
    ms = jnp.mean(x * x, axis=-1, keepdims=True)
    y = x * lax.rsqrt(ms + EPS) * g_ref[...]
    h = (y * (1.0 + sc_ref[0]) + sh_ref[0]).astype(BF16)
    off = 0
    for o_ref, wd in zip(o_refs, widths):
        o_ref[0] = _dot(h, w_ref[:, off:off + wd]).astype(o_ref.dtype)
        off += wd


def _norm_proj(x, gain, shift, scale, w, widths, tm):
    b, s, d = x.shape
    n = w.shape[1]
    assert sum(widths) == n and s % tm == 0
    vec = pl.BlockSpec((1, 1, d), lambda i, j: (i, 0, 0))
    return pl.pallas_call(
        functools.partial(_norm_proj_kernel, widths),
        grid=(b, s // tm),
        in_specs=[pl.BlockSpec((1, tm, d), lambda i, j: (i, j, 0)),
                  pl.BlockSpec((1, d), lambda i, j: (0, 0)),
                  vec, vec,
                  pl.BlockSpec((d, n), lambda i, j: (0, 0))],
        out_specs=[pl.BlockSpec((1, tm, wd), lambda i, j: (i, j, 0)) for wd in widths],
        out_shape=[jax.ShapeDtypeStruct((b, s, wd), BF16) for wd in widths],
        compiler_params=_cparams(("arbitrary", "arbitrary")),
        name="norm_proj",
    )(x, gain, shift, scale, w)


def _bias_table_kernel(rpb_ref, o_ref):
    h = pl.program_id(0)
    d0 = pl.program_id(1)
    q = lax.broadcasted_iota(jnp.int32, (GRID_W, GRID_W), 0)
    kc = lax.broadcasted_iota(jnp.int32, (GRID_W, GRID_W), 1)
    dc = jnp.clip(kc - q + NA_WIN_COLS - 1, 0, 2 * NA_WIN_COLS - 2)
    cs = jnp.clip(q - NA_WIN_COLS // 2, 0, GRID_W - NA_WIN_COLS)
    col_in = (kc >= cs) & (kc < cs + NA_WIN_COLS)
    n_dc = 2 * NA_WIN_COLS - 1
    n_dr = 2 * NA_WIN_ROWS - 1
    for j in range(NA_WIN_ROWS):
        row = h * n_dr + d0 + j
        t = jnp.zeros((GRID_W, GRID_W), F32)
        for c in range(n_dc):
            t = jnp.where(dc == c, rpb_ref[row, c], t)
        o_ref[0, 0, :, j * GRID_W:(j + 1) * GRID_W] = jnp.where(col_in, t, NEG_INF)


def _bias_table(rpb):
    nh, n_dr, n_dc = rpb.shape
    band = NA_WIN_ROWS * GRID_W
    return pl.pallas_call(
        _bias_table_kernel,
        grid=(nh, NA_WIN_ROWS),
        in_specs=[pl.BlockSpec(memory_space=pltpu.SMEM)],
        out_specs=pl.BlockSpec((1, 1, GRID_W, band), lambda h, d: (h, d, 0, 0)),
        out_shape=jax.ShapeDtypeStruct((nh, NA_WIN_ROWS, GRID_W, band), F32),
        compiler_params=_cparams(("arbitrary", "arbitrary")),
        name="bias_table",
    )(rpb.reshape(nh * n_dr, n_dc))


def _attn_kernel(rows, q_ref, k_ref, v_ref, kc_ref, vc_ref, bias_ref, o_ref):
    r = pl.program_id(1)
    rs = jnp.clip(r - NA_WIN_ROWS // 2, 0, rows - NA_WIN_ROWS)
    d0 = rs - r + NA_WIN_ROWS - 1
    band = NA_WIN_ROWS * GRID_W
    start = pl.multiple_of(rs * GRID_W, GRID_W)
    scale = HEAD_DIM ** -0.5
    for h in range(NA_HEADS):
        sl = slice(h * HEAD_DIM, (h + 1) * HEAD_DIM)
        qh = q_ref[0, :, sl]
        kb = k_ref[0, pl.ds(start, band), sl]
        vb = v_ref[0, pl.ds(start, band), sl]
        s_loc = _dot_nt(qh, kb) * scale + bias_ref[h, d0]
        s_ctx = _dot_nt(qh, kc_ref[0, :, sl]) * scale
        m = jnp.maximum(jnp.max(s_loc, axis=-1, keepdims=True),
                        jnp.max(s_ctx, axis=-1, keepdims=True))
        p_loc = jnp.exp(s_loc - m)
        p_ctx = jnp.exp(s_ctx - m)
        den = jnp.sum(p_loc, axis=-1, keepdims=True) + jnp.sum(p_ctx, axis=-1, keepdims=True)
        o = _dot(p_loc.astype(BF16), vb) + _dot(p_ctx.astype(BF16), vc_ref[0, :, sl])
        o_ref[0, :, sl] = (o / den).astype(o_ref.dtype)


def _attention(q, k, v, kc, vc, bias):
    b, s, w = q.shape
    rows = s // GRID_W
    c = kc.shape[1]
    full = lambda n: pl.BlockSpec((1, n, w), lambda i, j: (i, 0, 0))
    return pl.pallas_call(
        functools.partial(_attn_kernel, rows),
        grid=(b, rows),
        in_specs=[pl.BlockSpec((1, GRID_W, w), lambda i, j: (i, j, 0)),
                  full(s), full(s), full(c), full(c),
                  pl.BlockSpec(bias.shape, lambda i, j: (0, 0, 0, 0))],
        out_specs=pl.BlockSpec((1, GRID_W, w), lambda i, j: (i, j, 0)),
        out_shape=jax.ShapeDtypeStruct((b, s, w), BF16),
        compiler_params=_cparams(("arbitrary", "arbitrary")),
        name="nbr_attention",
    )(q, k, v, kc, vc, bias)


def _mix_kernel(x_ref, gu_ref, gv_ref, ga_ref, gb_ref, ya_ref, g1_ref, sh2_ref, sc2_ref,
                lng_ref, ws_ref, bs_ref, wpa_ref, wpb_ref, wout_ref, n2g_ref, wq_ref, keys_ref,
                x1_ref, h2_ref, st_ref):
    tm = x_ref.shape[1]
    gw = gu_ref.shape[2]
    u = jax.nn.gelu(gu_ref[0].astype(F32))
    t = jax.nn.gelu(gv_ref[0].astype(F32))
    tc = t - jnp.mean(t, axis=-1, keepdims=True)
    vn = tc * lax.rsqrt(jnp.mean(tc * tc, axis=-1, keepdims=True) + EPS) * lng_ref[...]
    vnb = vn.astype(BF16)
    lane = lax.broadcasted_iota(jnp.int32, (GM_CHUNK, LANES), 1)
    gd = gw // GM_GROUPS
    chunks = []
    for c in range(tm // GM_CHUNK):
        pairs = []
        for gp in range(gw // LANES):
            vp = vnb[c * GM_CHUNK:(c + 1) * GM_CHUNK, gp * LANES:(gp + 1) * LANES]
            r0 = _dot(ws_ref[2 * gp], vp)
            r1 = _dot(ws_ref[2 * gp + 1], vp)
            pairs.append(jnp.where(lane < gd, r0, r1))
        chunks.append(jnp.concatenate(pairs, axis=1) + bs_ref[...])
    mixed = jnp.concatenate(chunks, axis=0)
    yb = (u * mixed).astype(BF16)
    pa = _dot(ya_ref[0], wpa_ref[...])
    pb = _dot(yb, wpb_ref[...])
    m = jax.nn.sigmoid(ga_ref[0].astype(F32)) * pa + jax.nn.sigmoid(gb_ref[0].astype(F32)) * pb
    out = _dot(m.astype(BF16), wout_ref[...])
    x1 = x_ref[0] + g1_ref[0] * out
    x1_ref[0] = x1
    ms = jnp.mean(x1 * x1, axis=-1, keepdims=True)
    h2 = x1 * lax.rsqrt(ms + EPS) * n2g_ref[...]
    h2 = (h2 * (1.0 + sc2_ref[0]) + sh2_ref[0]).astype(BF16)
    h2_ref[0] = h2
    qp = _dot(h2, wq_ref[...]).astype(BF16)
    for hp in range(keys_ref.shape[0]):
        st_ref[hp] = _dot_nt(keys_ref[hp], qp[:, hp * PEER_HALF:(hp + 1) * PEER_HALF])


def _mix(x, gu, gv, ga, gb, ya, g1, sh2, sc2, lng, ws, bs_full, wpa, wpb, wout, n2g, wq, keys, tm):
    b, s, d = x.shape
    gw = gu.shape[2]
    nt = s // tm
    nhp = keys.shape[0]
    act = lambda w: pl.BlockSpec((1, tm, w), lambda i, j: (i, j, 0))
    vec = pl.BlockSpec((1, 1, d), lambda i, j: (i, 0, 0))

    def const(a):
        nd = a.ndim
        return pl.BlockSpec(a.shape, lambda i, j: (0,) * nd)

    return pl.pallas_call(
        _mix_kernel,
        grid=(b, nt),
        in_specs=[act(d), act(gw), act(gw), act(d), act(d), act(gw), vec, vec, vec,
                  const(lng), const(ws), const(bs_full), const(wpa), const(wpb), const(wout),
                  const(n2g), const(wq), const(keys)],
        out_specs=[act(d), act(d),
                   pl.BlockSpec((nhp, PEER_N_KEYS, tm), lambda i, j: (0, 0, i * nt + j))],
        out_shape=[jax.ShapeDtypeStruct((b, s, d), F32),
                   jax.ShapeDtypeStruct((b, s, d), BF16),
                   jax.ShapeDtypeStruct((nhp, PEER_N_KEYS, b * s), F32)],
        compiler_params=_cparams(("arbitrary", "arbitrary")),
        name="mix_peer_scores",
    )(x, gu, gv, ga, gb, ya, g1, sh2, sc2, lng, ws, bs_full, wpa, wpb, wout, n2g, wq, keys)


def _extract_top(vals, order, payload, k):
    big = jnp.int32(2 ** 30)
    out_v, out_p = [], []
    for _ in range(k):
        m = jnp.max(vals, axis=0, keepdims=True)
        o = jnp.min(jnp.where(vals == m, order, big), axis=0, keepdims=True)
        sel = order == o
        out_v.append(m)
        out_p.append(jnp.max(jnp.where(sel, payload, -1), axis=0, keepdims=True))
        vals = jnp.where(sel, -jnp.inf, vals)
    return jnp.concatenate(out_v, axis=0), jnp.concatenate(out_p, axis=0)


def _topk_kernel(tb, st_ref, idx_ref, gate_ref, sv_ref, si_ref):
    nhp = st_ref.shape[0]
    tm = st_ref.shape[2]
    k = PEER_TOPK
    kidx = lax.broadcasted_iota(jnp.int32, (PEER_N_KEYS, LANES), 0)
    sub = lax.broadcasted_iota(jnp.int32, (SUBLANES, LANES), 0)

    def lane_tile(lt, carry):
        lanes = pl.ds(pl.multiple_of(lt * LANES, LANES), LANES)

        def stage1(hp, c):
            v, i = _extract_top(st_ref[hp, :, lanes], kidx, kidx, k)
            sv_ref[hp] = v
            si_ref[hp] = i
            return c

        lax.fori_loop(0, nhp, stage1, 0)

        experts, gates = [], []
        for h in range(nhp // 2):
            s1, s2 = sv_ref[2 * h], sv_ref[2 * h + 1]
            i1, i2 = si_ref[2 * h], si_ref[2 * h + 1]
            cv, co, ce = [], [], []
            for half in range(2):
                b0 = half * SUBLANES
                cv.append(s1[0:1] + s2[b0:b0 + SUBLANES])
                co.append(sub + b0)
                ce.append(i1[0:1] * PEER_N_KEYS + i2[b0:b0 + SUBLANES])
            for a in range(1, SUBLANES):
                cv.append(s1[a:a + 1] + s2[0:SUBLANES])
                co.append(sub + a * k)
                ce.append(i1[a:a + 1] * PEER_N_KEYS + i2[0:SUBLANES])
            cv.append(s1[SUBLANES:k] + s2[0:1])
            co.append((sub + SUBLANES) * k)
            ce.append(i1[SUBLANES:k] * PEER_N_KEYS + i2[0:1])
            bv, be = _extract_top(jnp.concatenate(cv, axis=0), jnp.concatenate(co, axis=0),
                                  jnp.concatenate(ce, axis=0), k)
            ex = jnp.exp(bv - bv[0:1])
            gates.append(ex / jnp.sum(ex, axis=0, keepdims=True))
            experts.append(be)
        e_all = jnp.concatenate(experts, axis=0)
        g_all = jnp.concatenate(gates, axis=0)
        idx_ref[pl.ds(pl.multiple_of(lt * LANES, LANES), LANES), :] = e_all.T
        for j in range(LANES // tb):
            gate_ref[lt * (LANES // tb) + j] = g_all[:, j * tb:(j + 1) * tb]
        return carry

    lax.fori_loop(0, tm // LANES, lane_tile, 0)


def _topk(st, tm, tb):
    nhp, nk, t = st.shape
    ne = (nhp // 2) * PEER_TOPK
    return pl.pallas_call(
        functools.partial(_topk_kernel, tb),
        grid=(t // tm,),
        in_specs=[pl.BlockSpec((nhp, nk, tm), lambda i: (0, 0, i))],
        out_specs=[pl.BlockSpec((tm, ne), lambda i: (i, 0)),
                   pl.BlockSpec((tm // tb, ne, tb), lambda i: (i, 0, 0))],
        out_shape=[jax.ShapeDtypeStruct((t, ne), jnp.int32),
                   jax.ShapeDtypeStruct((t // tb, ne, tb), F32)],
        scratch_shapes=[pltpu.VMEM((nhp, PEER_TOPK, LANES), F32),
                        pltpu.VMEM((nhp, PEER_TOPK, LANES), jnp.int32)],
        compiler_params=_cparams(("arbitrary",)),
        name="peer_topk",
    )(st)


def _sc_gather(table, idx, window):
    info = plsc.get_sparse_core_info()
    nc, ns = info.num_cores, info.num_subcores
    n, d = idx.shape[0], table.shape[1]
    per_w = n // (nc * ns)
    steps = per_w // window
    assert per_w * nc * ns == n and steps * window == per_w and steps % 2 == 0
    mesh = plsc.VectorSubcoreMesh(core_axis_name="c", subcore_axis_name="s")

    @functools.partial(
        pl.kernel, mesh=mesh,
        out_type=jax.ShapeDtypeStruct((n, d), table.dtype),
        scratch_types=[pltpu.VMEM((2, window), jnp.int32),
                       pltpu.VMEM((2, window, d), table.dtype),
                       pltpu.SemaphoreType.DMA((2,)),
                       pltpu.SemaphoreType.DMA((2,)),
                       pltpu.SemaphoreType.DMA((2,))],
        name="peer_gather",
    )
    def gather_kernel(table_hbm, idx_hbm, out_hbm, idx_v, rows_v, isem, gsem, wsem):
        base = (lax.axis_index("s") * nc + lax.axis_index("c")) * per_w

        def idx_copy(i, b):
            return pltpu.make_async_copy(idx_hbm.at[pl.ds(base + i * window, window)], idx_v.at[b], isem.at[b])

        def gather(b):
            return pltpu.make_async_copy(table_hbm.at[idx_v.at[b]], rows_v.at[b], gsem.at[b])

        def write(i, b):
            return pltpu.make_async_copy(rows_v.at[b], out_hbm.at[pl.ds(base + i * window, window)], wsem.at[b])

        idx_copy(0, 0).start()
        idx_copy(0, 0).wait()
        gather(0).start()
        idx_copy(1, 1).start()

        @pl.loop(0, steps, step=2)
        def _(i0):
            for b in range(2):
                i = i0 + b
                gather(b).wait()
                write(i, b).start()

                @pl.when(i + 1 < steps)
                def _():
                    idx_copy(i + 1, 1 - b).wait()

                    @pl.when(i >= 1)
                    def _():
                        write(i - 1, 1 - b).wait()

                    gather(1 - b).start()

                @pl.when(i + 2 < steps)
                def _():
                    idx_copy(i + 2, b).start()

        write(steps - 2, 0).wait()
        write(steps - 1, 1).wait()

    return gather_kernel(table, idx)


def _pack_table(t):
    n, d = t.shape
    tb = t.astype(BF16)
    lo = lax.bitcast_convert_type(tb[:, :d // 2], jnp.uint16).astype(jnp.uint32)
    hi = lax.bitcast_convert_type(tb[:, d // 2:], jnp.uint16).astype(jnp.uint32)
    return lo | (hi << 16)


def _unpack(words):
    lo = pltpu.bitcast(words << 16, F32)
    hi = pltpu.bitcast(words & jnp.uint32(0xFFFF0000), F32)
    return lo, hi


def _peer_act_kernel(ne, ug_ref, h2_ref, gate_ref, w_ref):
    tb = h2_ref.shape[0]
    half = ug_ref.shape[1]
    for t in range(tb):
        lo, hi = _unpack(ug_ref[t * ne:(t + 1) * ne, :])
        x = h2_ref[t:t + 1, :].astype(F32)
        prod = lo * x[:, :half] + hi * x[:, half:]
        act = jnp.sum(prod, axis=1, keepdims=True)
        w_ref[0, :, t:t + 1] = jax.nn.gelu(act) * gate_ref[0, :, t:t + 1]


def _peer_act(ug, h2, gate, ne, tb):
    t, d = h2.shape
    half = ug.shape[1]
    return pl.pallas_call(
        functools.partial(_peer_act_kernel, ne),
        grid=(t // tb,),
        in_specs=[pl.BlockSpec((tb * ne, half), lambda i: (i, 0)),
                  pl.BlockSpec((tb, d), lambda i: (i, 0)),
                  pl.BlockSpec((1, ne, tb), lambda i: (i, 0, 0))],
        out_specs=pl.BlockSpec((1, ne, tb), lambda i: (i, 0, 0)),
        out_shape=jax.ShapeDtypeStruct((t // tb, ne, tb), F32),
        compiler_params=_cparams(("arbitrary",)),
        name="peer_act",
    )(ug, h2, gate)


def _peer_out_kernel(ne, vg_ref, w_ref, x1_ref, g2_ref, fg_ref, o_ref, acc_ref):
    tb = x1_ref.shape[0]
    for t in range(tb):
        lo, hi = _unpack(vg_ref[t * ne:(t + 1) * ne, :])
        wcol = w_ref[0, :, t:t + 1]
        acc_ref[t:t + 1, :] = jnp.concatenate(
            [jnp.sum(lo * wcol, axis=0, keepdims=True), jnp.sum(hi * wcol, axis=0, keepdims=True)], axis=1)
    x2 = x1_ref[...] + g2_ref[0] * acc_ref[...]
    ms = jnp.mean(x2 * x2, axis=-1, keepdims=True)
    o_ref[...] = x2 * lax.rsqrt(ms + EPS) * fg_ref[...]


def _peer_out(vg, w, x1, g2, fg, ne, tb, tokens_per_batch):
    t, d = x1.shape
    half = vg.shape[1]
    per_b = tokens_per_batch // tb
    return pl.pallas_call(
        functools.partial(_peer_out_kernel, ne),
        grid=(t // tb,),
        in_specs=[pl.BlockSpec((tb * ne, half), lambda i: (i, 0)),
                  pl.BlockSpec((1, ne, tb), lambda i: (i, 0, 0)),
                  pl.BlockSpec((tb, d), lambda i: (i, 0)),
                  pl.BlockSpec((1, 1, d), lambda i: (i // per_b, 0, 0)),
                  pl.BlockSpec((1, d), lambda i: (0, 0))],
        out_specs=pl.BlockSpec((tb, d), lambda i: (i, 0)),
        out_shape=jax.ShapeDtypeStruct((t, d), F32),
        scratch_shapes=[pltpu.VMEM((tb, d), F32)],
        compiler_params=_cparams(("arbitrary",)),
        name="peer_out",
    )(vg, w, x1, g2, fg)


def kernel(x, c, ctx, c_ctx, ada_w, ada_b, norm1_g, norm2_g, w_in, na_rpb, gm_ln_g, gm_ws, gm_bs,
           w_proj_a, w_proj_b, w_out, peer_wq, peer_keys, peer_u, peer_v, final_g):
    b, s, d = x.shape
    naw = NA_HEADS * HEAD_DIM
    gw = gm_ln_g.shape[1]
    layer = 0

    pad = (-(b + 1)) % SUBLANES
    c_all = jnp.concatenate([c, c_ctx[None, :], jnp.zeros((pad, d), F32)], axis=0)
    mod = _adaln(c_all, ada_w[layer], ada_b[layer])
    sh1, sc1, g1, sh2, sc2, g2 = [mod[:b, i * d:(i + 1) * d].reshape(b, 1, d) for i in range(ADA_CHUNKS)]
    csh1 = jnp.broadcast_to(mod[b, 0:d].reshape(1, 1, d), (b, 1, d))
    csc1 = jnp.broadcast_to(mod[b, d:2 * d].reshape(1, 1, d), (b, 1, d))

    w = w_in[layer].astype(BF16)
    n1g = norm1_g[layer].reshape(1, d)
    q, k, v, gu, gv, ga, gb = _norm_proj(x, n1g, sh1, sc1, w, (naw, naw, naw, gw, gw, d, d), 512)
    k_c, v_c = _norm_proj(ctx, n1g, csh1, csc1, w[:, naw:3 * naw], (naw, naw), ctx.shape[1])

    bias = _bias_table(na_rpb[layer])
    y_a = _attention(q, k, v, k_c, v_c, bias)

    bs_full = jnp.repeat(gm_bs[layer].T, gw // GM_GROUPS, axis=1)
    keys = peer_keys[layer].reshape(2 * PEER_HEADS, PEER_N_KEYS, PEER_HALF).astype(BF16)
    x1, h2, st = _mix(x, gu, gv, ga, gb, y_a, g1, sh2, sc2,
                      gm_ln_g[layer].reshape(1, gw), gm_ws[layer].astype(BF16), bs_full,
                      w_proj_a[layer].astype(BF16), w_proj_b[layer].astype(BF16), w_out[layer].astype(BF16),
                      norm2_g[layer].reshape(1, d), peer_wq[layer].astype(BF16), keys, 256)

    tb = 16
    ne = PEER_HEADS * PEER_TOPK
    idx, gate = _topk(st, 512, tb)

    t = b * s
    idx_flat = idx.reshape(t * ne)
    ug = _sc_gather(_pack_table(peer_u[layer]), idx_flat, 64)
    wgt = _peer_act(ug, h2.reshape(t, d), gate, ne, tb)
    vg = _sc_gather(_pack_table(peer_v[layer]), idx_flat, 64)
    out = _peer_out(vg, wgt, x1.reshape(t, d), g2, final_g.reshape(1, d), ne, tb, s)
    return out.reshape(b, s, d)
```

```python
import functools

import jax
import jax.numpy as jnp
from jax import lax
from jax.experimental import pallas as pl
from jax.experimental.pallas import tpu as pltpu
from jax.experimental.pallas import tpu_sc as plsc

F32 = jnp.float32
BF16 = jnp.bfloat16

GRID_W = 64
NA_HEADS = 8
HEAD_DIM = 64
NA_WIN_ROWS = 8
NA_WIN_COLS = 16
GM_GROUPS = 8
GM_CHUNK = 128
PEER_HEADS = 8
PEER_N_KEYS = 128
PEER_TOPK = 16
PEER_HALF = 128
ADA_CHUNKS = 6
EPS = 1e-6
NEG_INF = -1e30

LANES = 128
SUBLANES = 8
VMEM_LIMIT = 56 * 1024 * 1024


def _dot(a, b):
    return lax.dot_general(a, b, (((1,), (0,)), ((), ())), preferred_element_type=F32)


def _dot_nt(a, b):
    return lax.dot_general(a, b, (((1,), (1,)), ((), ())), preferred_element_type=F32)


def _cparams(sem):
    return pltpu.CompilerParams(dimension_semantics=sem, vmem_limit_bytes=VMEM_LIMIT)


def _adaln_kernel(c_ref, w_ref, b_ref, o_ref):
    c = c_ref[...]
    s = c * jax.nn.sigmoid(c)
    o_ref[...] = lax.dot_general(s, w_ref[...], (((1,), (0,)), ((), ())),
                                 precision=lax.Precision.HIGHEST,
                                 preferred_element_type=F32) + b_ref[...]


def _adaln(c_all, w, b):
    m, d = c_all.shape
    n = w.shape[1]
    tn = 1024
    return pl.pallas_call(
        _adaln_kernel,
        grid=(n // tn,),
        in_specs=[pl.BlockSpec((m, d), lambda j: (0, 0)),
                  pl.BlockSpec((d, tn), lambda j: (0, j)),
                  pl.BlockSpec((1, tn), lambda j: (0, j))],
        out_specs=pl.BlockSpec((m, tn), lambda j: (0, j)),
        out_shape=jax.ShapeDtypeStruct((m, n), F32),
        compiler_params=_cparams(("arbitrary",)),
        name="adaln",
    )(c_all, w, b.reshape(1, n))


def _norm_proj_kernel(widths, x_ref, g_ref, sh_ref, sc_ref, w_ref, *o_refs):
    x = x_ref[0]
    ms = jnp.mean(x * x, axis=-1, keepdims=True)
    y = x * lax.rsqrt(ms + EPS) * g_ref[...]
    h = (y * (1.0 + sc_ref[0]) + sh_ref[0]).astype(BF16)
    off = 0
    for o_ref, wd in zip(o_refs, widths):
        o_ref[0] = _dot(h, w_ref[:, off:off + wd]).astype(o_ref.dtype)
        off += wd


def _norm_proj(x, gain, shift, scale, w, widths, tm):
    b, s, d = x.shape
    n = w.shape[1]
    assert sum(widths) == n and s % tm == 0
    vec = pl.BlockSpec((1, 1, d), lambda i, j: (i, 0, 0))
    return pl.pallas_call(
        functools.partial(_norm_proj_kernel, widths),
        grid=(b, s // tm),
        in_specs=[pl.BlockSpec((1, tm, d), lambda i, j: (i, j, 0)),
                  pl.BlockSpec((1, d), lambda i, j: (0, 0)),
                  vec, vec,
                  pl.BlockSpec((d, n), lambda i, j: (0, 0))],
        out_specs=[pl.BlockSpec((1, tm, wd), lambda i, j: (i, j, 0)) for wd in widths],
        out_shape=[jax.ShapeDtypeStruct((b, s, wd), BF16) for wd in widths],
        compiler_params=_cparams(("arbitrary", "arbitrary")),
        name="norm_proj",
    )(x, gain, shift, scale, w)


def _bias_table_kernel(rpb_ref, o_ref):
    h = pl.program_id(0)
    d0 = pl.program_id(1)
    q = lax.broadcasted_iota(jnp.int32, (GRID_W, GRID_W), 0)
    kc = lax.broadcasted_iota(jnp.int32, (GRID_W, GRID_W), 1)
    dc = jnp.clip(kc - q + NA_WIN_COLS - 1, 0, 2 * NA_WIN_COLS - 2)
    cs = jnp.clip(q - NA_WIN_COLS // 2, 0, GRID_W - NA_WIN_COLS)
    col_in = (kc >= cs) & (kc < cs + NA_WIN_COLS)
    n_dc = 2 * NA_WIN_COLS - 1
    n_dr = 2 * NA_WIN_ROWS - 1
    for j in range(NA_WIN_ROWS):
        row = h * n_dr + d0 + j
        t = jnp.zeros((GRID_W, GRID_W), F32)
        for c in range(n_dc):
            t = jnp.where(dc == c, rpb_ref[row, c], t)
        o_ref[0, 0, :, j * GRID_W:(j + 1) * GRID_W] = jnp.where(col_in, t, NEG_INF)


def _bias_table(rpb):
    nh, n_dr, n_dc = rpb.shape
    band = NA_WIN_ROWS * GRID_W
    return pl.pallas_call(
        _bias_table_kernel,
        grid=(nh, NA_WIN_ROWS),
        in_specs=[pl.BlockSpec(memory_space=pltpu.SMEM)],
        out_specs=pl.BlockSpec((1, 1, GRID_W, band), lambda h, d: (h, d, 0, 0)),
        out_shape=jax.ShapeDtypeStruct((nh, NA_WIN_ROWS, GRID_W, band), F32),
        compiler_params=_cparams(("arbitrary", "arbitrary")),
        name="bias_table",
    )(rpb.reshape(nh * n_dr, n_dc))


def _attn_kernel(rows, q_ref, k_ref, v_ref, kc_ref, vc_ref, bias_ref, o_ref):
    r = pl.program_id(1)
    rs = jnp.clip(r - NA_WIN_ROWS // 2, 0, rows - NA_WIN_ROWS)
    d0 = rs - r + NA_WIN_ROWS - 1
    band = NA_WIN_ROWS * GRID_W
    start = pl.multiple_of(rs * GRID_W, GRID_W)
    scale = HEAD_DIM ** -0.5
    for h in range(NA_HEADS):
        sl = slice(h * HEAD_DIM, (h + 1) * HEAD_DIM)
        qh = q_ref[0, :, sl]
        kb = k_ref[0, pl.ds(start, band), sl]
        vb = v_ref[0, pl.ds(start, band), sl]
        s_loc = _dot_nt(qh, kb) * scale + bias_ref[h, d0]
        s_ctx = _dot_nt(qh, kc_ref[0, :, sl]) * scale
        m = jnp.maximum(jnp.max(s_loc, axis=-1, keepdims=True),
                        jnp.max(s_ctx, axis=-1, keepdims=True))
        p_loc = jnp.exp(s_loc - m)
        p_ctx = jnp.exp(s_ctx - m)
        den = jnp.sum(p_loc, axis=-1, keepdims=True) + jnp.sum(p_ctx, axis=-1, keepdims=True)
        o = _dot(p_loc.astype(BF16), vb) + _dot(p_ctx.astype(BF16), vc_ref[0, :, sl])
        o_ref[0, :, sl] = (o / den).astype(o_ref.dtype)


def _attention(q, k, v, kc, vc, bias):
    b, s, w = q.shape
    rows = s // GRID_W
    c = kc.shape[1]
    full = lambda n: pl.BlockSpec((1, n, w), lambda i, j: (i, 0, 0))
    return pl.pallas_call(
        functools.partial(_attn_kernel, rows),
        grid=(b, rows),
        in_specs=[pl.BlockSpec((1, GRID_W, w), lambda i, j: (i, j, 0)),
                  full(s), full(s), full(c), full(c),
                  pl.BlockSpec(bias.shape, lambda i, j: (0, 0, 0, 0))],
        out_specs=pl.BlockSpec((1, GRID_W, w), lambda i, j: (i, j, 0)),
        out_shape=jax.ShapeDtypeStruct((b, s, w), BF16),
        compiler_params=_cparams(("arbitrary", "arbitrary")),
        name="nbr_attention",
    )(q, k, v, kc, vc, bias)


def _mix_kernel(x_ref, gu_ref, gv_ref, ga_ref, gb_ref, ya_ref, g1_ref, sh2_ref, sc2_ref,
                lng_ref, ws_ref, bs_ref, wpa_ref, wpb_ref, wout_ref, n2g_ref, wq_ref, keys_ref,
                x1_ref, h2_ref, st_ref):
    tm = x_ref.shape[1]
    gw = gu_ref.shape[2]
    u = jax.nn.gelu(gu_ref[0].astype(F32))
    t = jax.nn.gelu(gv_ref[0].astype(F32))
    tc = t - jnp.mean(t, axis=-1, keepdims=True)
    vn = tc * lax.rsqrt(jnp.mean(tc * tc, axis=-1, keepdims=True) + EPS) * lng_ref[...]
    vnb = vn.astype(BF16)
    lane = lax.broadcasted_iota(jnp.int32, (GM_CHUNK, LANES), 1)
    gd = gw // GM_GROUPS
    chunks = []
    for c in range(tm // GM_CHUNK):
        pairs = []
        for gp in range(gw // LANES):
            vp = vnb[c * GM_CHUNK:(c + 1) * GM_CHUNK, gp * LANES:(gp + 1) * LANES]
            r0 = _dot(ws_ref[2 * gp], vp)
            r1 = _dot(ws_ref[2 * gp + 1], vp)
            pairs.append(jnp.where(lane < gd, r0, r1))
        chunks.append(jnp.concatenate(pairs, axis=1) + bs_ref[...])
    mixed = jnp.concatenate(chunks, axis=0)
    yb = (u * mixed).astype(BF16)
    pa = _dot(ya_ref[0], wpa_ref[...])
    pb = _dot(yb, wpb_ref[...])
    m = jax.nn.sigmoid(ga_ref[0].astype(F32)) * pa + jax.nn.sigmoid(gb_ref[0].astype(F32)) * pb
    out = _dot(m.astype(BF16), wout_ref[...])
    x1 = x_ref[0] + g1_ref[0] * out
    x1_ref[0] = x1
    ms = jnp.mean(x1 * x1, axis=-1, keepdims=True)
    h2 = x1 * lax.rsqrt(ms + EPS) * n2g_ref[...]
    h2 = h2 * (1.0 + sc2_ref[0]) + sh2_ref[0]
    h2_ref[0] = h2
    qp = _dot(h2.astype(BF16), wq_ref[...]).astype(BF16)
    for hp in range(keys_ref.shape[0]):
        st_ref[hp] = _dot_nt(keys_ref[hp], qp[:, hp * PEER_HALF:(hp + 1) * PEER_HALF])


def _mix(x, gu, gv, ga, gb, ya, g1, sh2, sc2, lng, ws, bs_full, wpa, wpb, wout, n2g, wq, keys, tm):
    b, s, d = x.shape
    gw = gu.shape[2]
    nt = s // tm
    nhp = keys.shape[0]
    act = lambda w: pl.BlockSpec((1, tm, w), lambda i, j: (i, j, 0))
    vec = pl.BlockSpec((1, 1, d), lambda i, j: (i, 0, 0))

    def const(a):
        nd = a.ndim
        return pl.BlockSpec(a.shape, lambda i, j: (0,) * nd)

    return pl.pallas_call(
        _mix_kernel,
        grid=(b, nt),
        in_specs=[act(d), act(gw), act(gw), act(d), act(d), act(gw), vec, vec, vec,
                  const(lng), const(ws), const(bs_full), const(wpa), const(wpb), const(wout),
                  const(n2g), const(wq), const(keys)],
        out_specs=[act(d), act(d),
                   pl.BlockSpec((nhp, PEER_N_KEYS, tm), lambda i, j: (0, 0, i * nt + j))],
        out_shape=[jax.ShapeDtypeStruct((b, s, d), F32),
                   jax.ShapeDtypeStruct((b, s, d), F32),
                   jax.ShapeDtypeStruct((nhp, PEER_N_KEYS, b * s), F32)],
        compiler_params=_cparams(("arbitrary", "arbitrary")),
        name="mix_peer_scores",
    )(x, gu, gv, ga, gb, ya, g1, sh2, sc2, lng, ws, bs_full, wpa, wpb, wout, n2g, wq, keys)


def _extract_top(vals, order, payload, k):
    big = jnp.int32(2 ** 30)
    out_v, out_p = [], []
    for _ in range(k):
        m = jnp.max(vals, axis=0, keepdims=True)
        o = jnp.min(jnp.where(vals == m, order, big), axis=0, keepdims=True)
        sel = order == o
        out_v.append(m)
        out_p.append(jnp.max(jnp.where(sel, payload, -1), axis=0, keepdims=True))
        vals = jnp.where(sel, -jnp.inf, vals)
    return jnp.concatenate(out_v, axis=0), jnp.concatenate(out_p, axis=0)


def _topk_kernel(st_ref, idx_ref, gate_ref, sv_ref, si_ref):
    nhp = st_ref.shape[0]
    tm = st_ref.shape[2]
    k = PEER_TOPK
    kidx = lax.broadcasted_iota(jnp.int32, (PEER_N_KEYS, LANES), 0)
    sub = lax.broadcasted_iota(jnp.int32, (SUBLANES, LANES), 0)

    def lane_tile(lt, carry):
        lanes = pl.ds(pl.multiple_of(lt * LANES, LANES), LANES)

        def stage1(hp, c):
            v, i = _extract_top(st_ref[hp, :, lanes], kidx, kidx, k)
            sv_ref[hp] = v
            si_ref[hp] = i
            return c

        lax.fori_loop(0, nhp, stage1, 0)

        experts, gates = [], []
        for h in range(nhp // 2):
            s1, s2 = sv_ref[2 * h], sv_ref[2 * h + 1]
            i1, i2 = si_ref[2 * h], si_ref[2 * h + 1]
            cv, co, ce = [], [], []
            for half in range(2):
                b0 = half * SUBLANES
                cv.append(s1[0:1] + s2[b0:b0 + SUBLANES])
                co.append(sub + b0)
                ce.append(i1[0:1] * PEER_N_KEYS + i2[b0:b0 + SUBLANES])
            for a in range(1, SUBLANES):
                cv.append(s1[a:a + 1] + s2[0:SUBLANES])
                co.append(sub + a * k)
                ce.append(i1[a:a + 1] * PEER_N_KEYS + i2[0:SUBLANES])
            cv.append(s1[SUBLANES:k] + s2[0:1])
            co.append((sub + SUBLANES) * k)
            ce.append(i1[SUBLANES:k] * PEER_N_KEYS + i2[0:1])
            bv, be = _extract_top(jnp.concatenate(cv, axis=0), jnp.concatenate(co, axis=0),
                                  jnp.concatenate(ce, axis=0), k)
            ex = jnp.exp(bv - bv[0:1])
            gates.append(ex / jnp.sum(ex, axis=0, keepdims=True))
            experts.append(be)
        e_all = jnp.concatenate(experts, axis=0)
        g_all = jnp.concatenate(gates, axis=0)
        rows = pl.ds(pl.multiple_of(lt * LANES, LANES), LANES)
        idx_ref[rows, :] = e_all.T
        gate_ref[rows, :] = g_all.T
        return carry

    lax.fori_loop(0, tm // LANES, lane_tile, 0)


def _topk(st, tm):
    nhp, nk, t = st.shape
    ne = (nhp // 2) * PEER_TOPK
    return pl.pallas_call(
        _topk_kernel,
        grid=(t // tm,),
        in_specs=[pl.BlockSpec((nhp, nk, tm), lambda i: (0, 0, i))],
        out_specs=[pl.BlockSpec((tm, ne), lambda i: (i, 0)),
                   pl.BlockSpec((tm, ne), lambda i: (i, 0))],
        out_shape=[jax.ShapeDtypeStruct((t, ne), jnp.int32),
                   jax.ShapeDtypeStruct((t, ne), F32)],
        scratch_shapes=[pltpu.VMEM((nhp, PEER_TOPK, LANES), F32),
                        pltpu.VMEM((nhp, PEER_TOPK, LANES), jnp.int32)],
        compiler_params=_cparams(("arbitrary",)),
        name="peer_topk",
    )(st)


SC_LANES = 16
SC_WINDOW = 64
SC_TOKENS_PER_STEP = 8


def _pack_table(t):
    n, d = t.shape
    tb = t.astype(BF16)
    lo = lax.bitcast_convert_type(tb[:, :d // 2], jnp.uint16).astype(jnp.uint32)
    hi = lax.bitcast_convert_type(tb[:, d // 2:], jnp.uint16).astype(jnp.uint32)
    return lo | (hi << 16)


def _sc_unpack(words):
    lo = plsc.bitcast(words << 16, F32)
    hi = plsc.bitcast(words & jnp.uint32(0xFFFF0000), F32)
    return lo, hi


def _sc_peer(table, idx, operand, mode):
    t, ne = idx.shape
    hw = table.shape[1]
    d = 2 * hw
    L = SC_LANES
    win = SC_WINDOW
    tps = SC_TOKENS_PER_STEP
    info = plsc.get_sparse_core_info()
    nc, ns = info.num_cores, info.num_subcores
    nss = t // (nc * ns * tps)
    wps = tps * ne // win
    wpt = ne // win
    blk = tps * ne
    nwin = nss * wps
    assert nss * nc * ns * tps == t and nss % 2 == 0 and wps % 2 == 0 and wpt * win == ne and hw % (4 * L) == 0
    dots = mode == "dots"
    assert dots or wpt == 2
    mesh = plsc.VectorSubcoreMesh(core_axis_name="c", subcore_axis_name="s")
    op_buf = pltpu.VMEM((2, tps, d), F32) if dots else pltpu.VMEM((2 * blk,), F32)
    res_buf = pltpu.VMEM((2 * blk,), F32) if dots else pltpu.VMEM((2, tps, d), F32)
    out_type = jax.ShapeDtypeStruct((t * ne,), F32) if dots else jax.ShapeDtypeStruct((t, d), F32)

    @functools.partial(
        pl.kernel, mesh=mesh, out_type=out_type,
        scratch_types=[pltpu.VMEM((2 * blk,), jnp.int32), op_buf,
                       pltpu.VMEM((2, win, hw), jnp.uint32), res_buf,
                       pltpu.SemaphoreType.DMA((2,)), pltpu.SemaphoreType.DMA((2,)),
                       pltpu.SemaphoreType.DMA((2,)), pltpu.SemaphoreType.DMA((2,))],
        compiler_params=pltpu.CompilerParams(needs_layout_passes=False),
        name="peer_" + mode,
    )
    def sc_kernel(table_hbm, idx_hbm, op_hbm, out_hbm, idx_v, op_v, rows_v, res_v, isem, psem, gsem, osem):
        ss0 = (lax.axis_index("s") * nc + lax.axis_index("c")) * nss
        lane = lax.iota(jnp.int32, L)

        def flat(ref, n):
            return ref.at[pl.ds(pl.multiple_of(n * blk, blk), blk)]

        def tok_rows(ref, n):
            return ref.at[pl.ds(pl.multiple_of(n * tps, tps), tps)]

        def idx_copy(ss, slot):
            return pltpu.make_async_copy(flat(idx_hbm, ss0 + ss), flat(idx_v, slot), isem.at[slot])

        def op_copy(ss, slot):
            if dots:
                return pltpu.make_async_copy(tok_rows(op_hbm, ss0 + ss), op_v.at[slot], psem.at[slot])
            return pltpu.make_async_copy(flat(op_hbm, ss0 + ss), flat(op_v, slot), psem.at[slot])

        def out_copy(ss, slot):
            if dots:
                return pltpu.make_async_copy(flat(res_v, slot), flat(out_hbm, ss0 + ss), osem.at[slot])
            return pltpu.make_async_copy(res_v.at[slot], tok_rows(out_hbm, ss0 + ss), osem.at[slot])

        def gather(slot, hs, b):
            ix = idx_v.at[pl.ds(pl.multiple_of(slot * blk + hs * win, win), win)]
            return pltpu.make_async_copy(table_hbm.at[ix], rows_v.at[b], gsem.at[b])

        def reduce_dots(slot, hs, b):
            tok = hs // wpt

            def head(g, c):
                def chunk(j, accs):
                    off = pl.multiple_of(j * L, L)
                    xlo = op_v[slot, tok, pl.ds(off, L)]
                    xhi = op_v[slot, tok, pl.ds(hw + off, L)]
                    new = []
                    for r in range(L):
                        lo, hi = _sc_unpack(rows_v[b, g * L + r, pl.ds(off, L)])
                        new.append(accs[r] + lo * xlo + hi * xhi)
                    return tuple(new)

                accs = lax.fori_loop(0, hw // L, chunk, tuple(jnp.zeros((L,), F32) for _ in range(L)))
                out = jnp.zeros((L,), F32)
                for r in range(L):
                    out = jnp.where(lane == r, jnp.sum(accs[r]), out)
                res_v[pl.ds(pl.multiple_of(slot * blk + hs * win + g * L, L), L)] = out
                return c

            lax.fori_loop(0, win // L, head, 0)

        def reduce_wsum(slot, hs, b):
            tok = hs // wpt
            wbase = slot * blk + hs * win
            nq = 4

            def colgroup(cg, c):
                col = pl.multiple_of(cg * (nq * L), nq * L)
                if b == 0:
                    accs = tuple(jnp.zeros((L,), F32) for _ in range(2 * nq))
                else:
                    accs = tuple(res_v[slot, tok, pl.ds(col + q * L, L)] for q in range(nq)) + \
                           tuple(res_v[slot, tok, pl.ds(hw + col + q * L, L)] for q in range(nq))

                def row(r, accs):
                    accs = list(accs)
                    wb = plsc.load_gather(op_v, [jnp.full((L,), wbase + r, jnp.int32)])
                    for q in range(nq):
                        lo, hi = _sc_unpack(rows_v[b, r, pl.ds(col + q * L, L)])
                        accs[q] = accs[q] + lo * wb
                        accs[nq + q] = accs[nq + q] + hi * wb
                    return tuple(accs)

                accs = lax.fori_loop(0, win, row, accs, unroll=2)
                for q in range(nq):
                    res_v[slot, tok, pl.ds(col + q * L, L)] = accs[q]
                    res_v[slot, tok, pl.ds(hw + col + q * L, L)] = accs[nq + q]
                return c

            lax.fori_loop(0, hw // (nq * L), colgroup, 0)

        idx_copy(0, 0).start()
        op_copy(0, 0).start()
        idx_copy(1, 1).start()
        op_copy(1, 1).start()
        idx_copy(0, 0).wait()
        op_copy(0, 0).wait()
        gather(0, 0, 0).start()

        @pl.loop(0, nwin, step=2)
        def _(i0):
            for b in range(2):
                i = i0 + b
                ss = i // wps
                hs = i % wps
                slot = ss % 2
                gather(slot, hs, b).wait()

                @pl.when(hs + 1 < wps)
                def _():
                    gather(slot, hs + 1, 1 - b).start()

                @pl.when(jnp.logical_and(hs + 1 == wps, ss + 1 < nss))
                def _():
                    idx_copy(ss + 1, 1 - slot).wait()
                    op_copy(ss + 1, 1 - slot).wait()
                    gather(1 - slot, 0, 1 - b).start()

                @pl.when(jnp.logical_and(hs == 0, ss >= 2))
                def _():
                    out_copy(ss - 2, slot).wait()

                if dots:
                    reduce_dots(slot, hs, b)
                else:
                    reduce_wsum(slot, hs, b)

                @pl.when(hs + 1 == wps)
                def _():
                    out_copy(ss, slot).start()

                    @pl.when(ss + 2 < nss)
                    def _():
                        idx_copy(ss + 2, slot).start()
                        op_copy(ss + 2, slot).start()

        out_copy(nss - 2, 0).wait()
        out_copy(nss - 1, 1).wait()

    op = operand if dots else operand.reshape(t * ne)
    return sc_kernel(table, idx.reshape(t * ne), op)


def _peer_weight_kernel(act_ref, gate_ref, w_ref):
    w_ref[...] = jax.nn.gelu(act_ref[...]) * gate_ref[...]


def _peer_weight(act, gate, tm):
    t, ne = act.shape
    spec = pl.BlockSpec((tm, ne), lambda i: (i, 0))
    return pl.pallas_call(
        _peer_weight_kernel, grid=(t // tm,), in_specs=[spec, spec], out_specs=spec,
        out_shape=jax.ShapeDtypeStruct((t, ne), F32),
        compiler_params=_cparams(("arbitrary",)), name="peer_weight",
    )(act, gate)


def _final_kernel(x1_ref, p_ref, g2_ref, fg_ref, o_ref):
    x2 = x1_ref[0] + g2_ref[0] * p_ref[0]
    ms = jnp.mean(x2 * x2, axis=-1, keepdims=True)
    o_ref[0] = x2 * lax.rsqrt(ms + EPS) * fg_ref[...]


def _final(x1, peer, g2, fg, tm):
    b, s, d = x1.shape
    act = pl.BlockSpec((1, tm, d), lambda i, j: (i, j, 0))
    return pl.pallas_call(
        _final_kernel, grid=(b, s // tm),
        in_specs=[act, act, pl.BlockSpec((1, 1, d), lambda i, j: (i, 0, 0)),
                  pl.BlockSpec((1, d), lambda i, j: (0, 0))],
        out_specs=act,
        out_shape=jax.ShapeDtypeStruct((b, s, d), F32),
        compiler_params=_cparams(("arbitrary", "arbitrary")), name="final_norm",
    )(x1, peer, g2, fg)


def kernel(x, c, ctx, c_ctx, ada_w, ada_b, norm1_g, norm2_g, w_in, na_rpb, gm_ln_g, gm_ws, gm_bs,
           w_proj_a, w_proj_b, w_out, peer_wq, peer_keys, peer_u, peer_v, final_g):
    b, s, d = x.shape
    naw = NA_HEADS * HEAD_DIM
    gw = gm_ln_g.shape[1]
    layer = 0

    pad = (-(b + 1)) % SUBLANES
    c_all = jnp.concatenate([c, c_ctx[None, :], jnp.zeros((pad, d), F32)], axis=0)
    mod = _adaln(c_all, ada_w[layer], ada_b[layer])
    sh1, sc1, g1, sh2, sc2, g2 = [mod[:b, i * d:(i + 1) * d].reshape(b, 1, d) for i in range(ADA_CHUNKS)]
    csh1 = jnp.broadcast_to(mod[b, 0:d].reshape(1, 1, d), (b, 1, d))
    csc1 = jnp.broadcast_to(mod[b, d:2 * d].reshape(1, 1, d), (b, 1, d))

    w = w_in[layer].astype(BF16)
    n1g = norm1_g[layer].reshape(1, d)
    q, k, v, gu, gv, ga, gb = _norm_proj(x, n1g, sh1, sc1, w, (naw, naw, naw, gw, gw, d, d), 512)
    k_c, v_c = _norm_proj(ctx, n1g, csh1, csc1, w[:, naw:3 * naw], (naw, naw), ctx.shape[1])

    bias = _bias_table(na_rpb[layer])
    y_a = _attention(q, k, v, k_c, v_c, bias)

    bs_full = jnp.repeat(gm_bs[layer].T, gw // GM_GROUPS, axis=1)
    keys = peer_keys[layer].reshape(2 * PEER_HEADS, PEER_N_KEYS, PEER_HALF).astype(BF16)
    x1, h2, st = _mix(x, gu, gv, ga, gb, y_a, g1, sh2, sc2,
                      gm_ln_g[layer].reshape(1, gw), gm_ws[layer].astype(BF16), bs_full,
                      w_proj_a[layer].astype(BF16), w_proj_b[layer].astype(BF16), w_out[layer].astype(BF16),
                      norm2_g[layer].reshape(1, d), peer_wq[layer].astype(BF16), keys, 256)

    idx, gate = _topk(st, 512)
    t = b * s
    ne = PEER_HEADS * PEER_TOPK
    act = _sc_peer(_pack_table(peer_u[layer]), idx, h2.reshape(t, d), "dots")
    wgt = _peer_weight(act.reshape(t, ne), gate, 2048)
    peer = _sc_peer(_pack_table(peer_v[layer]), idx, wgt, "wsum")
    return _final(x1, peer.reshape(b, s, d), g2, final_g.reshape(1, d), 512)
```

```python
import functools

import jax
import jax.numpy as jnp
from jax import lax
from jax.experimental import pallas as pl
from jax.experimental.pallas import tpu as pltpu
from jax.experimental.pallas import tpu_sc as plsc

F32 = jnp.float32
BF16 = jnp.bfloat16

GRID_W = 64
NA_HEADS = 8
HEAD_DIM = 64
NA_WIN_ROWS = 8
NA_WIN_COLS = 16
GM_GROUPS = 8
GM_CHUNK = 128
PEER_HEADS = 8
PEER_N_KEYS = 128
PEER_TOPK = 16
PEER_HALF = 128
ADA_CHUNKS = 6
EPS = 1e-6
NEG_INF = -1e30

LANES = 128
SUBLANES = 8
VMEM_LIMIT = 56 * 1024 * 1024
BATCH_CHUNKS = 4


def _dot(a, b):
    return lax.dot_general(a, b, (((1,), (0,)), ((), ())), preferred_element_type=F32)


def _dot_nt(a, b):
    return lax.dot_general(a, b, (((1,), (1,)), ((), ())), preferred_element_type=F32)


def _cparams(sem):
    return pltpu.CompilerParams(dimension_semantics=sem, vmem_limit_bytes=VMEM_LIMIT)


def _adaln_kernel(c_ref, w_ref, b_ref, o_ref):
    c = c_ref[...]
    s = c * jax.nn.sigmoid(c)
    o_ref[...] = lax.dot_general(s, w_ref[...], (((1,), (0,)), ((), ())),
                                 precision=lax.Precision.HIGHEST,
                                 preferred_element_type=F32) + b_ref[...]


def _adaln(c_all, w, b):
    m, d = c_all.shape
    n = w.shape[1]
    tn = 1024
    return pl.pallas_call(
        _adaln_kernel,
        grid=(n // tn,),
        in_specs=[pl.BlockSpec((m, d), lambda j: (0, 0)),
                  pl.BlockSpec((d, tn), lambda j: (0, j)),
                  pl.BlockSpec((1, tn), lambda j: (0, j))],
        out_specs=pl.BlockSpec((m, tn), lambda j: (0, j)),
        out_shape=jax.ShapeDtypeStruct((m, n), F32),
        compiler_params=_cparams(("arbitrary",)),
        name="adaln",
    )(c_all, w, b.reshape(1, n))


def _norm_proj_kernel(widths, x_ref, g_ref, sh_ref, sc_ref, w_ref, *o_refs):
    x = x_ref[0]
    ms = jnp.mean(x * x, axis=-1, keepdims=True)
    y = x * lax.rsqrt(ms + EPS) * g_ref[...]
    h = (y * (1.0 + sc_ref[0]) + sh_ref[0]).astype(BF16)
    off = 0
    for o_ref, wd in zip(o_refs, widths):
        o_ref[0] = _dot(h, w_ref[:, off:off + wd]).astype(o_ref.dtype)
        off += wd


def _norm_proj(x, gain, shift, scale, w, widths, tm):
    b, s, d = x.shape
    n = w.shape[1]
    assert sum(widths) == n and s % tm == 0
    vec = pl.BlockSpec((1, 1, d), lambda i, j: (i, 0, 0))
    return pl.pallas_call(
        functools.partial(_norm_proj_kernel, widths),
        grid=(b, s // tm),
        in_specs=[pl.BlockSpec((1, tm, d), lambda i, j: (i, j, 0)),
                  pl.BlockSpec((1, d), lambda i, j: (0, 0)),
                  vec, vec,
                  pl.BlockSpec((d, n), lambda i, j: (0, 0))],
        out_specs=[pl.BlockSpec((1, tm, wd), lambda i, j: (i, j, 0)) for wd in widths],
        out_shape=[jax.ShapeDtypeStruct((b, s, wd), BF16) for wd in widths],
        compiler_params=_cparams(("arbitrary", "arbitrary")),
        name="norm_proj",
    )(x, gain, shift, scale, w)


def _bias_table_kernel(rpb_ref, o_ref):
    h = pl.program_id(0)
    d0 = pl.program_id(1)
    q = lax.broadcasted_iota(jnp.int32, (GRID_W, GRID_W), 0)
    kc = lax.broadcasted_iota(jnp.int32, (GRID_W, GRID_W), 1)
    dc = jnp.clip(kc - q + NA_WIN_COLS - 1, 0, 2 * NA_WIN_COLS - 2)
    cs = jnp.clip(q - NA_WIN_COLS // 2, 0, GRID_W - NA_WIN_COLS)
    col_in = (kc >= cs) & (kc < cs + NA_WIN_COLS)
    n_dc = 2 * NA_WIN_COLS - 1
    n_dr = 2 * NA_WIN_ROWS - 1
    for j in range(NA_WIN_ROWS):
        row = h * n_dr + d0 + j
        t = jnp.zeros((GRID_W, GRID_W), F32)
        for c in range(n_dc):
            t = jnp.where(dc == c, rpb_ref[row, c], t)
        o_ref[0, 0, :, j * GRID_W:(j + 1) * GRID_W] = jnp.where(col_in, t, NEG_INF)


def _bias_table(rpb):
    nh, n_dr, n_dc = rpb.shape
    band = NA_WIN_ROWS * GRID_W
    return pl.pallas_call(
        _bias_table_kernel,
        grid=(nh, NA_WIN_ROWS),
        in_specs=[pl.BlockSpec(memory_space=pltpu.SMEM)],
        out_specs=pl.BlockSpec((1, 1, GRID_W, band), lambda h, d: (h, d, 0, 0)),
        out_shape=jax.ShapeDtypeStruct((nh, NA_WIN_ROWS, GRID_W, band), F32),
        compiler_params=_cparams(("arbitrary", "arbitrary")),
        name="bias_table",
    )(rpb.reshape(nh * n_dr, n_dc))


def _attn_kernel(rows, q_ref, k_ref, v_ref, kc_ref, vc_ref, bias_ref, o_ref):
    r = pl.program_id(1)
    rs = jnp.clip(r - NA_WIN_ROWS // 2, 0, rows - NA_WIN_ROWS)
    d0 = rs - r + NA_WIN_ROWS - 1
    band = NA_WIN_ROWS * GRID_W
    start = pl.multiple_of(rs * GRID_W, GRID_W)
    scale = HEAD_DIM ** -0.5
    for h in range(NA_HEADS):
        sl = slice(h * HEAD_DIM, (h + 1) * HEAD_DIM)
        qh = q_ref[0, :, sl]
        kb = k_ref[0, pl.ds(start, band), sl]
        vb = v_ref[0, pl.ds(start, band), sl]
        s_loc = _dot_nt(qh, kb) * scale + bias_ref[h, d0]
        s_ctx = _dot_nt(qh, kc_ref[0, :, sl]) * scale
        m = jnp.maximum(jnp.max(s_loc, axis=-1, keepdims=True),
                        jnp.max(s_ctx, axis=-1, keepdims=True))
        p_loc = jnp.exp(s_loc - m)
        p_ctx = jnp.exp(s_ctx - m)
        den = jnp.sum(p_loc, axis=-1, keepdims=True) + jnp.sum(p_ctx, axis=-1, keepdims=True)
        o = _dot(p_loc.astype(BF16), vb) + _dot(p_ctx.astype(BF16), vc_ref[0, :, sl])
        o_ref[0, :, sl] = (o / den).astype(o_ref.dtype)


def _attention(q, k, v, kc, vc, bias):
    b, s, w = q.shape
    rows = s // GRID_W
    c = kc.shape[1]
    full = lambda n: pl.BlockSpec((1, n, w), lambda i, j: (i, 0, 0))
    return pl.pallas_call(
        functools.partial(_attn_kernel, rows),
        grid=(b, rows),
        in_specs=[pl.BlockSpec((1, GRID_W, w), lambda i, j: (i, j, 0)),
                  full(s), full(s), full(c), full(c),
                  pl.BlockSpec(bias.shape, lambda i, j: (0, 0, 0, 0))],
        out_specs=pl.BlockSpec((1, GRID_W, w), lambda i, j: (i, j, 0)),
        out_shape=jax.ShapeDtypeStruct((b, s, w), BF16),
        compiler_params=_cparams(("arbitrary", "arbitrary")),
        name="nbr_attention",
    )(q, k, v, kc, vc, bias)


def _mix_kernel(x_ref, gu_ref, gv_ref, ga_ref, gb_ref, ya_ref, g1_ref, sh2_ref, sc2_ref,
                lng_ref, ws_ref, bs_ref, wpa_ref, wpb_ref, wout_ref, n2g_ref, wq_ref, keys_ref,
                x1_ref, h2_ref, st_ref):
    tm = x_ref.shape[1]
    gw = gu_ref.shape[2]
    u = jax.nn.gelu(gu_ref[0].astype(F32))
    t = jax.nn.gelu(gv_ref[0].astype(F32))
    tc = t - jnp.mean(t, axis=-1, keepdims=True)
    vn = tc * lax.rsqrt(jnp.mean(tc * tc, axis=-1, keepdims=True) + EPS) * lng_ref[...]
    vnb = vn.astype(BF16)
    lane = lax.broadcasted_iota(jnp.int32, (GM_CHUNK, LANES), 1)
    gd = gw // GM_GROUPS
    chunks = []
    for c in range(tm // GM_CHUNK):
        pairs = []
        for gp in range(gw // LANES):
            vp = vnb[c * GM_CHUNK:(c + 1) * GM_CHUNK, gp * LANES:(gp + 1) * LANES]
            r0 = _dot(ws_ref[2 * gp], vp)
            r1 = _dot(ws_ref[2 * gp + 1], vp)
            pairs.append(jnp.where(lane < gd, r0, r1))
        chunks.append(jnp.concatenate(pairs, axis=1) + bs_ref[...])
    mixed = jnp.concatenate(chunks, axis=0)
    yb = (u * mixed).astype(BF16)
    pa = _dot(ya_ref[0], wpa_ref[...])
    pb = _dot(yb, wpb_ref[...])
    m = jax.nn.sigmoid(ga_ref[0].astype(F32)) * pa + jax.nn.sigmoid(gb_ref[0].astype(F32)) * pb
    out = _dot(m.astype(BF16), wout_ref[...])
    x1 = x_ref[0] + g1_ref[0] * out
    x1_ref[0] = x1
    ms = jnp.mean(x1 * x1, axis=-1, keepdims=True)
    h2 = x1 * lax.rsqrt(ms + EPS) * n2g_ref[...]
    h2 = h2 * (1.0 + sc2_ref[0]) + sh2_ref[0]
    h2_ref[0] = h2
    qp = _dot(h2.astype(BF16), wq_ref[...]).astype(BF16)
    for hp in range(keys_ref.shape[0]):
        st_ref[hp] = _dot_nt(keys_ref[hp], qp[:, hp * PEER_HALF:(hp + 1) * PEER_HALF])


def _mix(x, gu, gv, ga, gb, ya, g1, sh2, sc2, lng, ws, bs_full, wpa, wpb, wout, n2g, wq, keys, tm):
    b, s, d = x.shape
    gw = gu.shape[2]
    nt = s // tm
    nhp = keys.shape[0]
    act = lambda w: pl.BlockSpec((1, tm, w), lambda i, j: (i, j, 0))
    vec = pl.BlockSpec((1, 1, d), lambda i, j: (i, 0, 0))

    def const(a):
        nd = a.ndim
        return pl.BlockSpec(a.shape, lambda i, j: (0,) * nd)

    return pl.pallas_call(
        _mix_kernel,
        grid=(b, nt),
        in_specs=[act(d), act(gw), act(gw), act(d), act(d), act(gw), vec, vec, vec,
                  const(lng), const(ws), const(bs_full), const(wpa), const(wpb), const(wout),
                  const(n2g), const(wq), const(keys)],
        out_specs=[act(d), act(d),
                   pl.BlockSpec((nhp, PEER_N_KEYS, tm), lambda i, j: (0, 0, i * nt + j))],
        out_shape=[jax.ShapeDtypeStruct((b, s, d), F32),
                   jax.ShapeDtypeStruct((b, s, d), F32),
                   jax.ShapeDtypeStruct((nhp, PEER_N_KEYS, b * s), F32)],
        compiler_params=_cparams(("arbitrary", "arbitrary")),
        name="mix_peer_scores",
    )(x, gu, gv, ga, gb, ya, g1, sh2, sc2, lng, ws, bs_full, wpa, wpb, wout, n2g, wq, keys)


def _extract_top(vals, order, payload, k):
    big = jnp.int32(2 ** 30)
    out_v, out_p = [], []
    for _ in range(k):
        m = jnp.max(vals, axis=0, keepdims=True)
        o = jnp.min(jnp.where(vals == m, order, big), axis=0, keepdims=True)
        sel = order == o
        out_v.append(m)
        out_p.append(jnp.max(jnp.where(sel, payload, -1), axis=0, keepdims=True))
        vals = jnp.where(sel, -jnp.inf, vals)
    return jnp.concatenate(out_v, axis=0), jnp.concatenate(out_p, axis=0)


def _topk_kernel(st_ref, idx_ref, gate_ref, sv_ref, si_ref):
    nhp = st_ref.shape[0]
    tm = st_ref.shape[2]
    k = PEER_TOPK
    kidx = lax.broadcasted_iota(jnp.int32, (PEER_N_KEYS, LANES), 0)
    sub = lax.broadcasted_iota(jnp.int32, (SUBLANES, LANES), 0)

    def lane_tile(lt, carry):
        lanes = pl.ds(pl.multiple_of(lt * LANES, LANES), LANES)

        def stage1(hp, c):
            v, i = _extract_top(st_ref[hp, :, lanes], kidx, kidx, k)
            sv_ref[hp] = v
            si_ref[hp] = i
            return c

        lax.fori_loop(0, nhp, stage1, 0)

        experts, gates = [], []
        for h in range(nhp // 2):
            s1, s2 = sv_ref[2 * h], sv_ref[2 * h + 1]
            i1, i2 = si_ref[2 * h], si_ref[2 * h + 1]
            cv, co, ce = [], [], []
            for half in range(2):
                b0 = half * SUBLANES
                cv.append(s1[0:1] + s2[b0:b0 + SUBLANES])
                co.append(sub + b0)
                ce.append(i1[0:1] * PEER_N_KEYS + i2[b0:b0 + SUBLANES])
            for a in range(1, SUBLANES):
                cv.append(s1[a:a + 1] + s2[0:SUBLANES])
                co.append(sub + a * k)
                ce.append(i1[a:a + 1] * PEER_N_KEYS + i2[0:SUBLANES])
            cv.append(s1[SUBLANES:k] + s2[0:1])
            co.append((sub + SUBLANES) * k)
            ce.append(i1[SUBLANES:k] * PEER_N_KEYS + i2[0:1])
            bv, be = _extract_top(jnp.concatenate(cv, axis=0), jnp.concatenate(co, axis=0),
                                  jnp.concatenate(ce, axis=0), k)
            ex = jnp.exp(bv - bv[0:1])
            gates.append(ex / jnp.sum(ex, axis=0, keepdims=True))
            experts.append(be)
        e_all = jnp.concatenate(experts, axis=0)
        g_all = jnp.concatenate(gates, axis=0)
        rows = pl.ds(pl.multiple_of(lt * LANES, LANES), LANES)
        idx_ref[rows, :] = e_all.T
        gate_ref[rows, :] = g_all.T
        return carry

    lax.fori_loop(0, tm // LANES, lane_tile, 0)


def _topk(st, tm):
    nhp, nk, t = st.shape
    ne = (nhp // 2) * PEER_TOPK
    return pl.pallas_call(
        _topk_kernel,
        grid=(t // tm,),
        in_specs=[pl.BlockSpec((nhp, nk, tm), lambda i: (0, 0, i))],
        out_specs=[pl.BlockSpec((tm, ne), lambda i: (i, 0)),
                   pl.BlockSpec((tm, ne), lambda i: (i, 0))],
        out_shape=[jax.ShapeDtypeStruct((t, ne), jnp.int32),
                   jax.ShapeDtypeStruct((t, ne), F32)],
        scratch_shapes=[pltpu.VMEM((nhp, PEER_TOPK, LANES), F32),
                        pltpu.VMEM((nhp, PEER_TOPK, LANES), jnp.int32)],
        compiler_params=_cparams(("arbitrary",)),
        name="peer_topk",
    )(st)


SC_LANES = 16
SC_WINDOW = 64
SC_TOKENS_PER_STEP = 8


def _pack_table(t):
    n, d = t.shape
    tb = t.astype(BF16)
    lo = lax.bitcast_convert_type(tb[:, :d // 2], jnp.uint16).astype(jnp.uint32)
    hi = lax.bitcast_convert_type(tb[:, d // 2:], jnp.uint16).astype(jnp.uint32)
    return lo | (hi << 16)


def _sc_unpack(words):
    lo = plsc.bitcast(words << 16, F32)
    hi = plsc.bitcast(words & jnp.uint32(0xFFFF0000), F32)
    return lo, hi


def _sc_peer(table, idx, operand, mode):
    t, ne = idx.shape
    hw = table.shape[1]
    d = 2 * hw
    L = SC_LANES
    win = SC_WINDOW
    tps = SC_TOKENS_PER_STEP
    info = plsc.get_sparse_core_info()
    nc, ns = info.num_cores, info.num_subcores
    nss = t // (nc * ns * tps)
    wps = tps * ne // win
    wpt = ne // win
    blk = tps * ne
    nwin = nss * wps
    assert nss * nc * ns * tps == t and nss % 2 == 0 and wps % 2 == 0 and wpt * win == ne and hw % (4 * L) == 0
    dots = mode == "dots"
    assert dots or wpt == 2
    mesh = plsc.VectorSubcoreMesh(core_axis_name="c", subcore_axis_name="s")
    op_buf = pltpu.VMEM((2, tps, d), F32) if dots else pltpu.VMEM((2 * blk,), F32)
    res_buf = pltpu.VMEM((2 * blk,), F32) if dots else pltpu.VMEM((2, tps, d), F32)
    out_type = jax.ShapeDtypeStruct((t * ne,), F32) if dots else jax.ShapeDtypeStruct((t, d), F32)

    @functools.partial(
        pl.kernel, mesh=mesh, out_type=out_type,
        scratch_types=[pltpu.VMEM((2 * blk,), jnp.int32), op_buf,
                       pltpu.VMEM((2, win, hw), jnp.uint32), res_buf,
                       pltpu.SemaphoreType.DMA((2,)), pltpu.SemaphoreType.DMA((2,)),
                       pltpu.SemaphoreType.DMA((2,)), pltpu.SemaphoreType.DMA((2,))],
        compiler_params=pltpu.CompilerParams(needs_layout_passes=False),
        name="peer_" + mode,
    )
    def sc_kernel(table_hbm, idx_hbm, op_hbm, out_hbm, idx_v, op_v, rows_v, res_v, isem, psem, gsem, osem):
        ss0 = (lax.axis_index("s") * nc + lax.axis_index("c")) * nss
        lane = lax.iota(jnp.int32, L)

        def flat(ref, n):
            return ref.at[pl.ds(pl.multiple_of(n * blk, blk), blk)]

        def tok_rows(ref, n):
            return ref.at[pl.ds(pl.multiple_of(n * tps, tps), tps)]

        def idx_copy(ss, slot):
            return pltpu.make_async_copy(flat(idx_hbm, ss0 + ss), flat(idx_v, slot), isem.at[slot])

        def op_copy(ss, slot):
            if dots:
                return pltpu.make_async_copy(tok_rows(op_hbm, ss0 + ss), op_v.at[slot], psem.at[slot])
            return pltpu.make_async_copy(flat(op_hbm, ss0 + ss), flat(op_v, slot), psem.at[slot])

        def out_copy(ss, slot):
            if dots:
                return pltpu.make_async_copy(flat(res_v, slot), flat(out_hbm, ss0 + ss), osem.at[slot])
            return pltpu.make_async_copy(res_v.at[slot], tok_rows(out_hbm, ss0 + ss), osem.at[slot])

        def gather(slot, hs, b):
            ix = idx_v.at[pl.ds(pl.multiple_of(slot * blk + hs * win, win), win)]
            return pltpu.make_async_copy(table_hbm.at[ix], rows_v.at[b], gsem.at[b])

        def reduce_dots(slot, hs, b):
            tok = hs // wpt

            def head(g, c):
                def chunk(j, accs):
                    off = pl.multiple_of(j * L, L)
                    xlo = op_v[slot, tok, pl.ds(off, L)]
                    xhi = op_v[slot, tok, pl.ds(hw + off, L)]
                    new = []
                    for r in range(L):
                        lo, hi = _sc_unpack(rows_v[b, g * L + r, pl.ds(off, L)])
                        new.append(accs[r] + lo * xlo + hi * xhi)
                    return tuple(new)

                accs = lax.fori_loop(0, hw // L, chunk, tuple(jnp.zeros((L,), F32) for _ in range(L)))
                out = jnp.zeros((L,), F32)
                for r in range(L):
                    out = jnp.where(lane == r, jnp.sum(accs[r]), out)
                res_v[pl.ds(pl.multiple_of(slot * blk + hs * win + g * L, L), L)] = out
                return c

            lax.fori_loop(0, win // L, head, 0)

        def reduce_wsum(slot, hs, b):
            tok = hs // wpt
            wbase = slot * blk + hs * win
            nq = 4

            def colgroup(cg, c):
                col = pl.multiple_of(cg * (nq * L), nq * L)
                if b == 0:
                    accs = tuple(jnp.zeros((L,), F32) for _ in range(2 * nq))
                else:
                    accs = tuple(res_v[slot, tok, pl.ds(col + q * L, L)] for q in range(nq)) + \
                           tuple(res_v[slot, tok, pl.ds(hw + col + q * L, L)] for q in range(nq))

                def row(r, accs):
                    accs = list(accs)
                    wb = plsc.load_gather(op_v, [jnp.full((L,), wbase + r, jnp.int32)])
                    for q in range(nq):
                        lo, hi = _sc_unpack(rows_v[b, r, pl.ds(col + q * L, L)])
                        accs[q] = accs[q] + lo * wb
                        accs[nq + q] = accs[nq + q] + hi * wb
                    return tuple(accs)

                accs = lax.fori_loop(0, win, row, accs, unroll=2)
                for q in range(nq):
                    res_v[slot, tok, pl.ds(col + q * L, L)] = accs[q]
                    res_v[slot, tok, pl.ds(hw + col + q * L, L)] = accs[nq + q]
                return c

            lax.fori_loop(0, hw // (nq * L), colgroup, 0)

        idx_copy(0, 0).start()
        op_copy(0, 0).start()
        idx_copy(1, 1).start()
        op_copy(1, 1).start()
        idx_copy(0, 0).wait()
        op_copy(0, 0).wait()
        gather(0, 0, 0).start()

        @pl.loop(0, nwin, step=2)
        def _(i0):
            for b in range(2):
                i = i0 + b
                ss = i // wps
                hs = i % wps
                slot = ss % 2
                gather(slot, hs, b).wait()

                @pl.when(hs + 1 < wps)
                def _():
                    gather(slot, hs + 1, 1 - b).start()

                @pl.when(jnp.logical_and(hs + 1 == wps, ss + 1 < nss))
                def _():
                    idx_copy(ss + 1, 1 - slot).wait()
                    op_copy(ss + 1, 1 - slot).wait()
                    gather(1 - slot, 0, 1 - b).start()

                @pl.when(jnp.logical_and(hs == 0, ss >= 2))
                def _():
                    out_copy(ss - 2, slot).wait()

                if dots:
                    reduce_dots(slot, hs, b)
                else:
                    reduce_wsum(slot, hs, b)

                @pl.when(hs + 1 == wps)
                def _():
                    out_copy(ss, slot).start()

                    @pl.when(ss + 2 < nss)
                    def _():
                        idx_copy(ss + 2, slot).start()
                        op_copy(ss + 2, slot).start()

        out_copy(nss - 2, 0).wait()
        out_copy(nss - 1, 1).wait()

    op = operand if dots else operand.reshape(t * ne)
    return sc_kernel(table, idx.reshape(t * ne), op)


def _peer_weight_kernel(act_ref, gate_ref, w_ref):
    w_ref[...] = jax.nn.gelu(act_ref[...]) * gate_ref[...]


def _peer_weight(act, gate, tm):
    t, ne = act.shape
    tm = min(tm, t)
    assert t % tm == 0
    spec = pl.BlockSpec((tm, ne), lambda i: (i, 0))
    return pl.pallas_call(
        _peer_weight_kernel, grid=(t // tm,), in_specs=[spec, spec], out_specs=spec,
        out_shape=jax.ShapeDtypeStruct((t, ne), F32),
        compiler_params=_cparams(("arbitrary",)), name="peer_weight",
    )(act, gate)


def _final_kernel(x1_ref, p_ref, g2_ref, fg_ref, o_ref):
    x2 = x1_ref[0] + g2_ref[0] * p_ref[0]
    ms = jnp.mean(x2 * x2, axis=-1, keepdims=True)
    o_ref[0] = x2 * lax.rsqrt(ms + EPS) * fg_ref[...]


def _final(x1, peer, g2, fg, tm):
    b, s, d = x1.shape
    act = pl.BlockSpec((1, tm, d), lambda i, j: (i, j, 0))
    return pl.pallas_call(
        _final_kernel, grid=(b, s // tm),
        in_specs=[act, act, pl.BlockSpec((1, 1, d), lambda i, j: (i, 0, 0)),
                  pl.BlockSpec((1, d), lambda i, j: (0, 0))],
        out_specs=act,
        out_shape=jax.ShapeDtypeStruct((b, s, d), F32),
        compiler_params=_cparams(("arbitrary", "arbitrary")), name="final_norm",
    )(x1, peer, g2, fg)


def kernel(x, c, ctx, c_ctx, ada_w, ada_b, norm1_g, norm2_g, w_in, na_rpb, gm_ln_g, gm_ws, gm_bs,
           w_proj_a, w_proj_b, w_out, peer_wq, peer_keys, peer_u, peer_v, final_g):
    b, s, d = x.shape
    naw = NA_HEADS * HEAD_DIM
    gw = gm_ln_g.shape[1]
    layer = 0

    pad = (-(b + 1)) % SUBLANES
    c_all = jnp.concatenate([c, c_ctx[None, :], jnp.zeros((pad, d), F32)], axis=0)
    mod = _adaln(c_all, ada_w[layer], ada_b[layer])
    sh1, sc1, g1, sh2, sc2, g2 = [mod[:b, i * d:(i + 1) * d].reshape(b, 1, d) for i in range(ADA_CHUNKS)]
    csh1 = jnp.broadcast_to(mod[b, 0:d].reshape(1, 1, d), (b, 1, d))
    csc1 = jnp.broadcast_to(mod[b, d:2 * d].reshape(1, 1, d), (b, 1, d))

    w = w_in[layer].astype(BF16)
    w_kv = w[:, naw:3 * naw]
    n1g = norm1_g[layer].reshape(1, d)
    bias = _bias_table(na_rpb[layer])
    bs_full = jnp.repeat(gm_bs[layer].T, gw // GM_GROUPS, axis=1)
    keys = peer_keys[layer].reshape(2 * PEER_HEADS, PEER_N_KEYS, PEER_HALF).astype(BF16)
    lng = gm_ln_g[layer].reshape(1, gw)
    ws = gm_ws[layer].astype(BF16)
    wpa, wpb, wout = w_proj_a[layer].astype(BF16), w_proj_b[layer].astype(BF16), w_out[layer].astype(BF16)
    n2g = norm2_g[layer].reshape(1, d)
    wq = peer_wq[layer].astype(BF16)
    tab_u, tab_v = _pack_table(peer_u[layer]), _pack_table(peer_v[layer])
    fg = final_g.reshape(1, d)
    ne = PEER_HEADS * PEER_TOPK

    bc = b // BATCH_CHUNKS
    t = bc * s
    outs = []
    for ci in range(BATCH_CHUNKS):
        sl = slice(ci * bc, (ci + 1) * bc)
        xc = x[sl]
        q, k, v, gu, gv, ga, gb = _norm_proj(xc, n1g, sh1[sl], sc1[sl], w, (naw, naw, naw, gw, gw, d, d), 512)
        k_c, v_c = _norm_proj(ctx[sl], n1g, csh1[sl], csc1[sl], w_kv, (naw, naw), ctx.shape[1])
        y_a = _attention(q, k, v, k_c, v_c, bias)
        x1, h2, st = _mix(xc, gu, gv, ga, gb, y_a, g1[sl], sh2[sl], sc2[sl], lng, ws, bs_full,
                          wpa, wpb, wout, n2g, wq, keys, 256)
        idx, gate = _topk(st, 512)
        act = _sc_peer(tab_u, idx, h2.reshape(t, d), "dots")
        wgt = _peer_weight(act.reshape(t, ne), gate, 2048)
        peer = _sc_peer(tab_v, idx, wgt, "wsum")
        outs.append(_final(x1, peer.reshape(bc, s, d), g2[sl], fg, 512))
    return jnp.concatenate(outs, axis=0)
```

```python
import functools

import jax
import jax.numpy as jnp
from jax import lax
from jax.experimental import pallas as pl
from jax.experimental.pallas import tpu as pltpu
from jax.experimental.pallas import tpu_sc as plsc

F32 = jnp.float32
BF16 = jnp.bfloat16

GRID_W = 64
NA_HEADS = 8
HEAD_DIM = 64
NA_WIN_ROWS = 8
NA_WIN_COLS = 16
GM_GROUPS = 8
GM_CHUNK = 128
PEER_HEADS = 8
PEER_N_KEYS = 128
PEER_TOPK = 16
PEER_HALF = 128
ADA_CHUNKS = 6
EPS = 1e-6
NEG_INF = -1e30

LANES = 128
SUBLANES = 8
VMEM_LIMIT = 56 * 1024 * 1024
BATCH_CHUNKS = 4


def _dot(a, b):
    return lax.dot_general(a, b, (((1,), (0,)), ((), ())), preferred_element_type=F32)


def _dot_nt(a, b):
    return lax.dot_general(a, b, (((1,), (1,)), ((), ())), preferred_element_type=F32)


def _cparams(sem):
    return pltpu.CompilerParams(dimension_semantics=sem, vmem_limit_bytes=VMEM_LIMIT)


def _pack_bf16_pairs(lo, hi):
    lo_bits = pltpu.bitcast(lo.astype(F32), jnp.uint32) >> 16
    hi_bits = pltpu.bitcast(hi.astype(F32), jnp.uint32) & jnp.uint32(0xFFFF0000)
    return lo_bits | hi_bits


def _adaln_kernel(c_ref, w_ref, b_ref, o_ref):
    c = c_ref[...]
    s = c * jax.nn.sigmoid(c)
    o_ref[...] = lax.dot_general(s, w_ref[...], (((1,), (0,)), ((), ())),
                                 precision=lax.Precision.HIGHEST,
                                 preferred_element_type=F32) + b_ref[...]


def _adaln(c_all, w, b):
    m, d = c_all.shape
    n = w.shape[1]
    tn = 1024
    return pl.pallas_call(
        _adaln_kernel,
        grid=(n // tn,),
        in_specs=[pl.BlockSpec((m, d), lambda j: (0, 0)),
                  pl.BlockSpec((d, tn), lambda j: (0, j)),
                  pl.BlockSpec((1, tn), lambda j: (0, j))],
        out_specs=pl.BlockSpec((m, tn), lambda j: (0, j)),
        out_shape=jax.ShapeDtypeStruct((m, n), F32),
        compiler_params=_cparams(("arbitrary",)),
        name="adaln",
    )(c_all, w, b.reshape(1, n))


def _norm_proj_kernel(widths, x_ref, g_ref, sh_ref, sc_ref, w_ref, *o_refs):
    x = x_ref[0]
    ms = jnp.mean(x * x, axis=-1, keepdims=True)
    y = x * lax.rsqrt(ms + EPS) * g_ref[...]
    h = (y * (1.0 + sc_ref[0]) + sh_ref[0]).astype(BF16)
    off = 0
    for o_ref, wd in zip(o_refs, widths):
        o_ref[0] = _dot(h, w_ref[:, off:off + wd]).astype(o_ref.dtype)
        off += wd


def _norm_proj(x, gain, shift, scale, w, widths, tm):
    b, s, d = x.shape
    n = w.shape[1]
    assert sum(widths) == n and s % tm == 0
    vec = pl.BlockSpec((1, 1, d), lambda i, j: (i, 0, 0))
    return pl.pallas_call(
        functools.partial(_norm_proj_kernel, widths),
        grid=(b, s // tm),
        in_specs=[pl.BlockSpec((1, tm, d), lambda i, j: (i, j, 0)),
                  pl.BlockSpec((1, d), lambda i, j: (0, 0)),
                  vec, vec,
                  pl.BlockSpec((d, n), lambda i, j: (0, 0))],
        out_specs=[pl.BlockSpec((1, tm, wd), lambda i, j: (i, j, 0)) for wd in widths],
        out_shape=[jax.ShapeDtypeStruct((b, s, wd), BF16) for wd in widths],
        compiler_params=_cparams(("arbitrary", "arbitrary")),
        name="norm_proj",
    )(x, gain, shift, scale, w)


def _bias_table_kernel(rpb_ref, o_ref):
    h = pl.program_id(0)
    d0 = pl.program_id(1)
    q = lax.broadcasted_iota(jnp.int32, (GRID_W, GRID_W), 0)
    kc = lax.broadcasted_iota(jnp.int32, (GRID_W, GRID_W), 1)
    dc = jnp.clip(kc - q + NA_WIN_COLS - 1, 0, 2 * NA_WIN_COLS - 2)
    cs = jnp.clip(q - NA_WIN_COLS // 2, 0, GRID_W - NA_WIN_COLS)
    col_in = (kc >= cs) & (kc < cs + NA_WIN_COLS)
    n_dc = 2 * NA_WIN_COLS - 1
    n_dr = 2 * NA_WIN_ROWS - 1
    for j in range(NA_WIN_ROWS):
        row = h * n_dr + d0 + j
        t = jnp.zeros((GRID_W, GRID_W), F32)
        for c in range(n_dc):
            t = jnp.where(dc == c, rpb_ref[row, c], t)
        o_ref[0, 0, :, j * GRID_W:(j + 1) * GRID_W] = jnp.where(col_in, t, NEG_INF)


def _bias_table(rpb):
    nh, n_dr, n_dc = rpb.shape
    band = NA_WIN_ROWS * GRID_W
    return pl.pallas_call(
        _bias_table_kernel,
        grid=(nh, NA_WIN_ROWS),
        in_specs=[pl.BlockSpec(memory_space=pltpu.SMEM)],
        out_specs=pl.BlockSpec((1, 1, GRID_W, band), lambda h, d: (h, d, 0, 0)),
        out_shape=jax.ShapeDtypeStruct((nh, NA_WIN_ROWS, GRID_W, band), F32),
        compiler_params=_cparams(("arbitrary", "arbitrary")),
        name="bias_table",
    )(rpb.reshape(nh * n_dr, n_dc))


def _attn_kernel(rows, q_ref, k_ref, v_ref, kc_ref, vc_ref, bias_ref, o_ref):
    r = pl.program_id(1)
    rs = jnp.clip(r - NA_WIN_ROWS // 2, 0, rows - NA_WIN_ROWS)
    d0 = rs - r + NA_WIN_ROWS - 1
    band = NA_WIN_ROWS * GRID_W
    start = pl.multiple_of(rs * GRID_W, GRID_W)
    scale = HEAD_DIM ** -0.5
    for h in range(NA_HEADS):
        sl = slice(h * HEAD_DIM, (h + 1) * HEAD_DIM)
        qh = q_ref[0, :, sl]
        kb = k_ref[0, pl.ds(start, band), sl]
        vb = v_ref[0, pl.ds(start, band), sl]
        s_loc = _dot_nt(qh, kb) * scale + bias_ref[h, d0]
        s_ctx = _dot_nt(qh, kc_ref[0, :, sl]) * scale
        m = jnp.maximum(jnp.max(s_loc, axis=-1, keepdims=True),
                        jnp.max(s_ctx, axis=-1, keepdims=True))
        p_loc = jnp.exp(s_loc - m)
        p_ctx = jnp.exp(s_ctx - m)
        den = jnp.sum(p_loc, axis=-1, keepdims=True) + jnp.sum(p_ctx, axis=-1, keepdims=True)
        o = _dot(p_loc.astype(BF16), vb) + _dot(p_ctx.astype(BF16), vc_ref[0, :, sl])
        o_ref[0, :, sl] = (o / den).astype(o_ref.dtype)


def _attention(q, k, v, kc, vc, bias):
    b, s, w = q.shape
    rows = s // GRID_W
    c = kc.shape[1]
    full = lambda n: pl.BlockSpec((1, n, w), lambda i, j: (i, 0, 0))
    return pl.pallas_call(
        functools.partial(_attn_kernel, rows),
        grid=(b, rows),
        in_specs=[pl.BlockSpec((1, GRID_W, w), lambda i, j: (i, j, 0)),
                  full(s), full(s), full(c), full(c),
                  pl.BlockSpec(bias.shape, lambda i, j: (0, 0, 0, 0))],
        out_specs=pl.BlockSpec((1, GRID_W, w), lambda i, j: (i, j, 0)),
        out_shape=jax.ShapeDtypeStruct((b, s, w), BF16),
        compiler_params=_cparams(("arbitrary", "arbitrary")),
        name="nbr_attention",
    )(q, k, v, kc, vc, bias)


def _mix_kernel(x_ref, gu_ref, gv_ref, ga_ref, gb_ref, ya_ref, g1_ref, sh2_ref, sc2_ref,
                lng_ref, ws_ref, bs_ref, wpa_ref, wpb_ref, wout_ref, n2g_ref, wq_ref, keys_ref,
                x1_ref, h2_ref, st_ref):
    tm = x_ref.shape[1]
    gw = gu_ref.shape[2]
    u = jax.nn.gelu(gu_ref[0].astype(F32))
    t = jax.nn.gelu(gv_ref[0].astype(F32))
    tc = t - jnp.mean(t, axis=-1, keepdims=True)
    vn = tc * lax.rsqrt(jnp.mean(tc * tc, axis=-1, keepdims=True) + EPS) * lng_ref[...]
    vnb = vn.astype(BF16)
    lane = lax.broadcasted_iota(jnp.int32, (GM_CHUNK, LANES), 1)
    gd = gw // GM_GROUPS
    chunks = []
    for c in range(tm // GM_CHUNK):
        pairs = []
        for gp in range(gw // LANES):
            vp = vnb[c * GM_CHUNK:(c + 1) * GM_CHUNK, gp * LANES:(gp + 1) * LANES]
            r0 = _dot(ws_ref[2 * gp], vp)
            r1 = _dot(ws_ref[2 * gp + 1], vp)
            pairs.append(jnp.where(lane < gd, r0, r1))
        chunks.append(jnp.concatenate(pairs, axis=1) + bs_ref[...])
    mixed = jnp.concatenate(chunks, axis=0)
    yb = (u * mixed).astype(BF16)
    pa = _dot(ya_ref[0], wpa_ref[...])
    pb = _dot(yb, wpb_ref[...])
    m = jax.nn.sigmoid(ga_ref[0].astype(F32)) * pa + jax.nn.sigmoid(gb_ref[0].astype(F32)) * pb
    out = _dot(m.astype(BF16), wout_ref[...])
    x1 = x_ref[0] + g1_ref[0] * out
    x1_ref[0] = x1
    ms = jnp.mean(x1 * x1, axis=-1, keepdims=True)
    h2 = x1 * lax.rsqrt(ms + EPS) * n2g_ref[...]
    h2 = (h2 * (1.0 + sc2_ref[0]) + sh2_ref[0]).astype(BF16)
    half = h2.shape[1] // 2
    h2_ref[0] = _pack_bf16_pairs(h2[:, :half], h2[:, half:])
    qp = _dot(h2, wq_ref[...]).astype(BF16)
    for hp in range(keys_ref.shape[0]):
        st_ref[hp] = _dot_nt(keys_ref[hp], qp[:, hp * PEER_HALF:(hp + 1) * PEER_HALF])


def _mix(x, gu, gv, ga, gb, ya, g1, sh2, sc2, lng, ws, bs_full, wpa, wpb, wout, n2g, wq, keys, tm):
    b, s, d = x.shape
    gw = gu.shape[2]
    nt = s // tm
    nhp = keys.shape[0]
    act = lambda w: pl.BlockSpec((1, tm, w), lambda i, j: (i, j, 0))
    vec = pl.BlockSpec((1, 1, d), lambda i, j: (i, 0, 0))

    def const(a):
        nd = a.ndim
        return pl.BlockSpec(a.shape, lambda i, j: (0,) * nd)

    return pl.pallas_call(
        _mix_kernel,
        grid=(b, nt),
        in_specs=[act(d), act(gw), act(gw), act(d), act(d), act(gw), vec, vec, vec,
                  const(lng), const(ws), const(bs_full), const(wpa), const(wpb), const(wout),
                  const(n2g), const(wq), const(keys)],
        out_specs=[act(d), act(d // 2),
                   pl.BlockSpec((nhp, PEER_N_KEYS, tm), lambda i, j: (0, 0, i * nt + j))],
        out_shape=[jax.ShapeDtypeStruct((b, s, d), F32),
                   jax.ShapeDtypeStruct((b, s, d // 2), jnp.uint32),
                   jax.ShapeDtypeStruct((nhp, PEER_N_KEYS, b * s), F32)],
        compiler_params=_cparams(("arbitrary", "arbitrary")),
        name="mix_peer_scores",
    )(x, gu, gv, ga, gb, ya, g1, sh2, sc2, lng, ws, bs_full, wpa, wpb, wout, n2g, wq, keys)


def _extract_top(vals, order, payload, k):
    big = jnp.int32(2 ** 30)
    out_v, out_p = [], []
    for _ in range(k):
        m = jnp.max(vals, axis=0, keepdims=True)
        o = jnp.min(jnp.where(vals == m, order, big), axis=0, keepdims=True)
        sel = order == o
        out_v.append(m)
        out_p.append(jnp.max(jnp.where(sel, payload, -1), axis=0, keepdims=True))
        vals = jnp.where(sel, -jnp.inf, vals)
    return jnp.concatenate(out_v, axis=0), jnp.concatenate(out_p, axis=0)


def _topk_kernel(st_ref, idx_ref, gate_ref, sv_ref, si_ref):
    nhp = st_ref.shape[0]
    tm = st_ref.shape[2]
    k = PEER_TOPK
    kidx = lax.broadcasted_iota(jnp.int32, (PEER_N_KEYS, LANES), 0)
    sub = lax.broadcasted_iota(jnp.int32, (SUBLANES, LANES), 0)

    def lane_tile(lt, carry):
        lanes = pl.ds(pl.multiple_of(lt * LANES, LANES), LANES)

        def stage1(hp, c):
            v, i = _extract_top(st_ref[hp, :, lanes], kidx, kidx, k)
            sv_ref[hp] = v
            si_ref[hp] = i
            return c

        lax.fori_loop(0, nhp, stage1, 0)

        experts, gates = [], []
        for h in range(nhp // 2):
            s1, s2 = sv_ref[2 * h], sv_ref[2 * h + 1]
            i1, i2 = si_ref[2 * h], si_ref[2 * h + 1]
            cv, co, ce = [], [], []
            for half in range(2):
                b0 = half * SUBLANES
                cv.append(s1[0:1] + s2[b0:b0 + SUBLANES])
                co.append(sub + b0)
                ce.append(i1[0:1] * PEER_N_KEYS + i2[b0:b0 + SUBLANES])
            for a in range(1, SUBLANES):
                cv.append(s1[a:a + 1] + s2[0:SUBLANES])
                co.append(sub + a * k)
                ce.append(i1[a:a + 1] * PEER_N_KEYS + i2[0:SUBLANES])
            cv.append(s1[SUBLANES:k] + s2[0:1])
            co.append((sub + SUBLANES) * k)
            ce.append(i1[SUBLANES:k] * PEER_N_KEYS + i2[0:1])
            bv, be = _extract_top(jnp.concatenate(cv, axis=0), jnp.concatenate(co, axis=0),
                                  jnp.concatenate(ce, axis=0), k)
            ex = jnp.exp(bv - bv[0:1])
            gates.append(ex / jnp.sum(ex, axis=0, keepdims=True))
            experts.append(be)
        e_all = jnp.concatenate(experts, axis=0)
        g_all = jnp.concatenate(gates, axis=0)
        rows = pl.ds(pl.multiple_of(lt * LANES, LANES), LANES)
        idx_ref[rows, :] = e_all.T
        gate_ref[rows, :] = g_all.T
        return carry

    lax.fori_loop(0, tm // LANES, lane_tile, 0)


def _topk(st, tm):
    nhp, nk, t = st.shape
    ne = (nhp // 2) * PEER_TOPK
    return pl.pallas_call(
        _topk_kernel,
        grid=(t // tm,),
        in_specs=[pl.BlockSpec((nhp, nk, tm), lambda i: (0, 0, i))],
        out_specs=[pl.BlockSpec((tm, ne), lambda i: (i, 0)),
                   pl.BlockSpec((tm, ne), lambda i: (i, 0))],
        out_shape=[jax.ShapeDtypeStruct((t, ne), jnp.int32),
                   jax.ShapeDtypeStruct((t, ne), F32)],
        scratch_shapes=[pltpu.VMEM((nhp, PEER_TOPK, LANES), F32),
                        pltpu.VMEM((nhp, PEER_TOPK, LANES), jnp.int32)],
        compiler_params=_cparams(("arbitrary",)),
        name="peer_topk",
    )(st)


SC_LANES = 16
SC_WINDOW = 32
SC_ROW_BUFFERS = 4
SC_TOKENS_PER_STEP = 8
SC_WSUM_CHUNKS = 8


def _pack_table(t):
    n, d = t.shape
    tb = t.astype(BF16)
    lo = lax.bitcast_convert_type(tb[:, :d // 2], jnp.uint16).astype(jnp.uint32)
    hi = lax.bitcast_convert_type(tb[:, d // 2:], jnp.uint16).astype(jnp.uint32)
    return lo | (hi << 16)


def _sc_bf16(words):
    return plsc.bitcast(words, BF16)


def _sc_peer(table, idx, operand, mode):
    t, ne = idx.shape
    hw = table.shape[1]
    d = 2 * hw
    L = SC_LANES
    win = SC_WINDOW
    nbuf = SC_ROW_BUFFERS
    ahead = nbuf - 1
    tps = SC_TOKENS_PER_STEP
    nq = SC_WSUM_CHUNKS
    info = plsc.get_sparse_core_info()
    nc, ns = info.num_cores, info.num_subcores
    nss = t // (nc * ns * tps)
    wps = tps * ne // win
    wpt = ne // win
    blk = tps * ne
    nwin = nss * wps
    assert nss * nc * ns * tps == t and nss % 2 == 0 and wps % nbuf == 0 and wps > ahead
    assert wpt * win == ne and win % L == 0 and hw % (nq * L) == 0
    dots = mode == "dots"
    assert dots or nbuf % wpt == 0
    mesh = plsc.VectorSubcoreMesh(core_axis_name="c", subcore_axis_name="s")
    op_buf = pltpu.VMEM((2, tps, hw), jnp.uint32) if dots else pltpu.VMEM((2 * blk,), jnp.int32)
    res_buf = pltpu.VMEM((2 * blk,), F32) if dots else pltpu.VMEM((2, tps, d), F32)
    out_type = jax.ShapeDtypeStruct((t * ne,), F32) if dots else jax.ShapeDtypeStruct((t, d), F32)

    @functools.partial(
        pl.kernel, mesh=mesh, out_type=out_type,
        scratch_types=[pltpu.VMEM((2 * blk,), jnp.int32), op_buf,
                       pltpu.VMEM((nbuf, win, hw), jnp.uint32), res_buf,
                       pltpu.SemaphoreType.DMA((2,)), pltpu.SemaphoreType.DMA((2,)),
                       pltpu.SemaphoreType.DMA((nbuf,)), pltpu.SemaphoreType.DMA((2,))],
        compiler_params=pltpu.CompilerParams(needs_layout_passes=False),
        name="peer_" + mode,
    )
    def sc_kernel(table_hbm, idx_hbm, op_hbm, out_hbm, idx_v, op_v, rows_v, res_v, isem, psem, gsem, osem):
        ss0 = (lax.axis_index("s") * nc + lax.axis_index("c")) * nss
        lane = lax.iota(jnp.int32, L)

        def flat(ref, n):
            return ref.at[pl.ds(pl.multiple_of(n * blk, blk), blk)]

        def tok_rows(ref, n):
            return ref.at[pl.ds(pl.multiple_of(n * tps, tps), tps)]

        def idx_copy(ss, slot):
            return pltpu.make_async_copy(flat(idx_hbm, ss0 + ss), flat(idx_v, slot), isem.at[slot])

        def op_copy(ss, slot):
            if dots:
                return pltpu.make_async_copy(tok_rows(op_hbm, ss0 + ss), op_v.at[slot], psem.at[slot])
            return pltpu.make_async_copy(flat(op_hbm, ss0 + ss), flat(op_v, slot), psem.at[slot])

        def out_copy(ss, slot):
            if dots:
                return pltpu.make_async_copy(flat(res_v, slot), flat(out_hbm, ss0 + ss), osem.at[slot])
            return pltpu.make_async_copy(res_v.at[slot], tok_rows(out_hbm, ss0 + ss), osem.at[slot])

        def gather(slot, hs, b):
            ix = idx_v.at[pl.ds(pl.multiple_of(slot * blk + hs * win, win), win)]
            return pltpu.make_async_copy(table_hbm.at[ix], rows_v.at[b], gsem.at[b])

        def reduce_dots(slot, hs, b):
            tok = hs // wpt

            def head(g, c):
                def chunk(jj, accs):
                    off = pl.multiple_of(jj * (2 * L), 2 * L)
                    xa = _sc_bf16(op_v[slot, tok, pl.ds(off, L)])
                    xb = _sc_bf16(op_v[slot, tok, pl.ds(off + L, L)])
                    new = []
                    for r in range(L):
                        p = (_sc_bf16(rows_v[b, g * L + r, pl.ds(off, L)]) * xa
                             + _sc_bf16(rows_v[b, g * L + r, pl.ds(off + L, L)]) * xb)
                        lo, hi = plsc.unpack(p, format=plsc.PackFormat.INTERLEAVED)
                        new.append(accs[r] + lo + hi)
                    return tuple(new)

                accs = lax.fori_loop(0, hw // (2 * L), chunk, tuple(jnp.zeros((L,), F32) for _ in range(L)))
                out = jnp.zeros((L,), F32)
                for r in range(L):
                    out = jnp.where(lane == r, jnp.sum(accs[r]), out)
                res_v[pl.ds(pl.multiple_of(slot * blk + hs * win + g * L, L), L)] = out
                return c

            lax.fori_loop(0, win // L, head, 0)

        def reduce_wsum(slot, hs, b):
            tok = hs // wpt
            wbase = slot * blk + hs * win
            first = b % wpt == 0

            def colgroup(cg, c):
                col = pl.multiple_of(cg * (nq * L), nq * L)
                if first:
                    accs = tuple(jnp.zeros((L,), F32) for _ in range(2 * nq))
                else:
                    accs = tuple(res_v[slot, tok, pl.ds(col + q * L, L)] for q in range(nq)) + \
                           tuple(res_v[slot, tok, pl.ds(hw + col + q * L, L)] for q in range(nq))

                def rowpair(rp, accs):
                    accs = list(accs)
                    r = rp * 2
                    wa = _sc_bf16(plsc.load_gather(op_v, [jnp.full((L,), wbase + r, jnp.int32)]))
                    wb = _sc_bf16(plsc.load_gather(op_v, [jnp.full((L,), wbase + r + 1, jnp.int32)]))
                    for q in range(nq):
                        p = (_sc_bf16(rows_v[b, r, pl.ds(col + q * L, L)]) * wa
                             + _sc_bf16(rows_v[b, r + 1, pl.ds(col + q * L, L)]) * wb)
                        lo, hi = plsc.unpack(p, format=plsc.PackFormat.INTERLEAVED)
                        accs[q] = accs[q] + lo
                        accs[nq + q] = accs[nq + q] + hi
                    return tuple(accs)

                accs = lax.fori_loop(0, win // 2, rowpair, accs)
                for q in range(nq):
                    res_v[slot, tok, pl.ds(col + q * L, L)] = accs[q]
                    res_v[slot, tok, pl.ds(hw + col + q * L, L)] = accs[nq + q]
                return c

            lax.fori_loop(0, hw // (nq * L), colgroup, 0)

        idx_copy(0, 0).start()
        op_copy(0, 0).start()
        idx_copy(1, 1).start()
        op_copy(1, 1).start()
        idx_copy(0, 0).wait()
        op_copy(0, 0).wait()
        for a in range(ahead):
            gather(0, a, a).start()

        @pl.loop(0, nwin, step=nbuf)
        def _(i0):
            for b in range(nbuf):
                i = i0 + b
                ss = i // wps
                hs = i % wps
                slot = ss % 2
                gather(slot, hs, b).wait()
                nb = (b + ahead) % nbuf

                @pl.when(hs + ahead < wps)
                def _():
                    gather(slot, hs + ahead, nb).start()

                @pl.when(jnp.logical_and(hs + ahead >= wps, ss + 1 < nss))
                def _():
                    @pl.when(hs + ahead == wps)
                    def _():
                        idx_copy(ss + 1, 1 - slot).wait()
                        op_copy(ss + 1, 1 - slot).wait()

                    gather(1 - slot, hs + ahead - wps, nb).start()

                @pl.when(jnp.logical_and(hs == 0, ss >= 2))
                def _():
                    out_copy(ss - 2, slot).wait()

                if dots:
                    reduce_dots(slot, hs, b)
                else:
                    reduce_wsum(slot, hs, b)

                @pl.when(hs + 1 == wps)
                def _():
                    out_copy(ss, slot).start()

                    @pl.when(ss + 2 < nss)
                    def _():
                        idx_copy(ss + 2, slot).start()
                        op_copy(ss + 2, slot).start()

        out_copy(nss - 2, 0).wait()
        out_copy(nss - 1, 1).wait()

    op = operand if dots else operand.reshape(t * ne)
    return sc_kernel(table, idx.reshape(t * ne), op)


def _peer_weight_kernel(act_ref, gate_ref, w_ref):
    w = (jax.nn.gelu(act_ref[...]) * gate_ref[...]).astype(BF16)
    w_ref[...] = pltpu.bitcast(_pack_bf16_pairs(w, w), jnp.int32)


def _peer_weight(act, gate, tm):
    t, ne = act.shape
    tm = min(tm, t)
    assert t % tm == 0
    spec = pl.BlockSpec((tm, ne), lambda i: (i, 0))
    return pl.pallas_call(
        _peer_weight_kernel, grid=(t // tm,), in_specs=[spec, spec], out_specs=spec,
        out_shape=jax.ShapeDtypeStruct((t, ne), jnp.int32),
        compiler_params=_cparams(("arbitrary",)), name="peer_weight",
    )(act, gate)


def _final_kernel(x1_ref, p_ref, g2_ref, fg_ref, o_ref):
    x2 = x1_ref[0] + g2_ref[0] * p_ref[0]
    ms = jnp.mean(x2 * x2, axis=-1, keepdims=True)
    o_ref[0] = x2 * lax.rsqrt(ms + EPS) * fg_ref[...]


def _final(x1, peer, g2, fg, tm):
    b, s, d = x1.shape
    act = pl.BlockSpec((1, tm, d), lambda i, j: (i, j, 0))
    return pl.pallas_call(
        _final_kernel, grid=(b, s // tm),
        in_specs=[act, act, pl.BlockSpec((1, 1, d), lambda i, j: (i, 0, 0)),
                  pl.BlockSpec((1, d), lambda i, j: (0, 0))],
        out_specs=act,
        out_shape=jax.ShapeDtypeStruct((b, s, d), F32),
        compiler_params=_cparams(("arbitrary", "arbitrary")), name="final_norm",
    )(x1, peer, g2, fg)


def kernel(x, c, ctx, c_ctx, ada_w, ada_b, norm1_g, norm2_g, w_in, na_rpb, gm_ln_g, gm_ws, gm_bs,
           w_proj_a, w_proj_b, w_out, peer_wq, peer_keys, peer_u, peer_v, final_g):
    b, s, d = x.shape
    naw = NA_HEADS * HEAD_DIM
    gw = gm_ln_g.shape[1]
    layer = 0

    pad = (-(b + 1)) % SUBLANES
    c_all = jnp.concatenate([c, c_ctx[None, :], jnp.zeros((pad, d), F32)], axis=0)
    mod = _adaln(c_all, ada_w[layer], ada_b[layer])
    sh1, sc1, g1, sh2, sc2, g2 = [mod[:b, i * d:(i + 1) * d].reshape(b, 1, d) for i in range(ADA_CHUNKS)]
    csh1 = jnp.broadcast_to(mod[b, 0:d].reshape(1, 1, d), (b, 1, d))
    csc1 = jnp.broadcast_to(mod[b, d:2 * d].reshape(1, 1, d), (b, 1, d))

    w = w_in[layer].astype(BF16)
    w_kv = w[:, naw:3 * naw]
    n1g = norm1_g[layer].reshape(1, d)
    bias = _bias_table(na_rpb[layer])
    bs_full = jnp.repeat(gm_bs[layer].T, gw // GM_GROUPS, axis=1)
    keys = peer_keys[layer].reshape(2 * PEER_HEADS, PEER_N_KEYS, PEER_HALF).astype(BF16)
    lng = gm_ln_g[layer].reshape(1, gw)
    ws = gm_ws[layer].astype(BF16)
    wpa, wpb, wout = w_proj_a[layer].astype(BF16), w_proj_b[layer].astype(BF16), w_out[layer].astype(BF16)
    n2g = norm2_g[layer].reshape(1, d)
    wq = peer_wq[layer].astype(BF16)
    tab_u, tab_v = _pack_table(peer_u[layer]), _pack_table(peer_v[layer])
    fg = final_g.reshape(1, d)
    ne = PEER_HEADS * PEER_TOPK

    bc = b // BATCH_CHUNKS
    t = bc * s
    outs = []
    for ci in range(BATCH_CHUNKS):
        sl = slice(ci * bc, (ci + 1) * bc)
        xc = x[sl]
        q, k, v, gu, gv, ga, gb = _norm_proj(xc, n1g, sh1[sl], sc1[sl], w, (naw, naw, naw, gw, gw, d, d), 512)
        k_c, v_c = _norm_proj(ctx[sl], n1g, csh1[sl], csc1[sl], w_kv, (naw, naw), ctx.shape[1])
        y_a = _attention(q, k, v, k_c, v_c, bias)
        x1, h2, st = _mix(xc, gu, gv, ga, gb, y_a, g1[sl], sh2[sl], sc2[sl], lng, ws, bs_full,
                          wpa, wpb, wout, n2g, wq, keys, 256)
        idx, gate = _topk(st, 512)
        act = _sc_peer(tab_u, idx, h2.reshape(t, d // 2), "dots")
        wgt = _peer_weight(act.reshape(t, ne), gate, 2048)
        peer = _sc_peer(tab_v, idx, wgt, "wsum")
        outs.append(_final(x1, peer.reshape(bc, s, d), g2[sl], fg, 512))
    return jnp.concatenate(outs, axis=0)
```

```python
import functools

import jax
import jax.numpy as jnp
from jax import lax
from jax.experimental import pallas as pl
from jax.experimental.pallas import tpu as pltpu
from jax.experimental.pallas import tpu_sc as plsc

F32 = jnp.float32
BF16 = jnp.bfloat16

GRID_W = 64
NA_HEADS = 8
HEAD_DIM = 64
NA_WIN_ROWS = 8
NA_WIN_COLS = 16
GM_GROUPS = 8
GM_CHUNK = 128
PEER_HEADS = 8
PEER_N_KEYS = 128
PEER_TOPK = 16
PEER_HALF = 128
ADA_CHUNKS = 6
EPS = 1e-6
NEG_INF = -1e30

LANES = 128
SUBLANES = 8
VMEM_LIMIT = 56 * 1024 * 1024
BATCH_CHUNKS = (2, 4, 5, 5)


def _dot(a, b):
    return lax.dot_general(a, b, (((1,), (0,)), ((), ())), preferred_element_type=F32)


def _dot_nt(a, b):
    return lax.dot_general(a, b, (((1,), (1,)), ((), ())), preferred_element_type=F32)


def _cparams(sem):
    return pltpu.CompilerParams(dimension_semantics=sem, vmem_limit_bytes=VMEM_LIMIT)


def _pack_bf16_pairs(lo, hi):
    lo_bits = pltpu.bitcast(lo.astype(F32), jnp.uint32) >> 16
    hi_bits = pltpu.bitcast(hi.astype(F32), jnp.uint32) & jnp.uint32(0xFFFF0000)
    return lo_bits | hi_bits


def _adaln_kernel(c_ref, w_ref, b_ref, o_ref):
    c = c_ref[...]
    s = c * jax.nn.sigmoid(c)
    o_ref[...] = lax.dot_general(s, w_ref[...], (((1,), (0,)), ((), ())),
                                 precision=lax.Precision.HIGHEST,
                                 preferred_element_type=F32) + b_ref[...]


def _adaln(c_all, w, b):
    m, d = c_all.shape
    n = w.shape[1]
    tn = 1024
    return pl.pallas_call(
        _adaln_kernel,
        grid=(n // tn,),
        in_specs=[pl.BlockSpec((m, d), lambda j: (0, 0)),
                  pl.BlockSpec((d, tn), lambda j: (0, j)),
                  pl.BlockSpec((1, tn), lambda j: (0, j))],
        out_specs=pl.BlockSpec((m, tn), lambda j: (0, j)),
        out_shape=jax.ShapeDtypeStruct((m, n), F32),
        compiler_params=_cparams(("arbitrary",)),
        name="adaln",
    )(c_all, w, b.reshape(1, n))


def _norm_proj_kernel(widths, x_ref, g_ref, sh_ref, sc_ref, w_ref, *o_refs):
    x = x_ref[0]
    ms = jnp.mean(x * x, axis=-1, keepdims=True)
    y = x * lax.rsqrt(ms + EPS) * g_ref[...]
    h = (y * (1.0 + sc_ref[0]) + sh_ref[0]).astype(BF16)
    off = 0
    for o_ref, wd in zip(o_refs, widths):
        o_ref[0] = _dot(h, w_ref[:, off:off + wd]).astype(o_ref.dtype)
        off += wd


def _norm_proj(x, gain, shift, scale, w, widths, tm):
    b, s, d = x.shape
    n = w.shape[1]
    assert sum(widths) == n and s % tm == 0
    vec = pl.BlockSpec((1, 1, d), lambda i, j: (i, 0, 0))
    return pl.pallas_call(
        functools.partial(_norm_proj_kernel, widths),
        grid=(b, s // tm),
        in_specs=[pl.BlockSpec((1, tm, d), lambda i, j: (i, j, 0)),
                  pl.BlockSpec((1, d), lambda i, j: (0, 0)),
                  vec, vec,
                  pl.BlockSpec((d, n), lambda i, j: (0, 0))],
        out_specs=[pl.BlockSpec((1, tm, wd), lambda i, j: (i, j, 0)) for wd in widths],
        out_shape=[jax.ShapeDtypeStruct((b, s, wd), BF16) for wd in widths],
        compiler_params=_cparams(("arbitrary", "arbitrary")),
        name="norm_proj",
    )(x, gain, shift, scale, w)


def _bias_table_kernel(rpb_ref, o_ref):
    h = pl.program_id(0)
    d0 = pl.program_id(1)
    q = lax.broadcasted_iota(jnp.int32, (GRID_W, GRID_W), 0)
    kc = lax.broadcasted_iota(jnp.int32, (GRID_W, GRID_W), 1)
    dc = jnp.clip(kc - q + NA_WIN_COLS - 1, 0, 2 * NA_WIN_COLS - 2)
    cs = jnp.clip(q - NA_WIN_COLS // 2, 0, GRID_W - NA_WIN_COLS)
    col_in = (kc >= cs) & (kc < cs + NA_WIN_COLS)
    n_dc = 2 * NA_WIN_COLS - 1
    n_dr = 2 * NA_WIN_ROWS - 1
    for j in range(NA_WIN_ROWS):
        row = h * n_dr + d0 + j
        t = jnp.zeros((GRID_W, GRID_W), F32)
        for c in range(n_dc):
            t = jnp.where(dc == c, rpb_ref[row, c], t)
        o_ref[0, 0, :, j * GRID_W:(j + 1) * GRID_W] = jnp.where(col_in, t, NEG_INF)


def _bias_table(rpb):
    nh, n_dr, n_dc = rpb.shape
    band = NA_WIN_ROWS * GRID_W
    return pl.pallas_call(
        _bias_table_kernel,
        grid=(nh, NA_WIN_ROWS),
        in_specs=[pl.BlockSpec(memory_space=pltpu.SMEM)],
        out_specs=pl.BlockSpec((1, 1, GRID_W, band), lambda h, d: (h, d, 0, 0)),
        out_shape=jax.ShapeDtypeStruct((nh, NA_WIN_ROWS, GRID_W, band), F32),
        compiler_params=_cparams(("arbitrary", "arbitrary")),
        name="bias_table",
    )(rpb.reshape(nh * n_dr, n_dc))


def _attn_kernel(rows, q_ref, k_ref, v_ref, kc_ref, vc_ref, bias_ref, o_ref):
    r = pl.program_id(1)
    rs = jnp.clip(r - NA_WIN_ROWS // 2, 0, rows - NA_WIN_ROWS)
    d0 = rs - r + NA_WIN_ROWS - 1
    band = NA_WIN_ROWS * GRID_W
    start = pl.multiple_of(rs * GRID_W, GRID_W)
    scale = HEAD_DIM ** -0.5
    for h in range(NA_HEADS):
        sl = slice(h * HEAD_DIM, (h + 1) * HEAD_DIM)
        qh = q_ref[0, :, sl]
        kb = k_ref[0, pl.ds(start, band), sl]
        vb = v_ref[0, pl.ds(start, band), sl]
        s_loc = _dot_nt(qh, kb) * scale + bias_ref[h, d0]
        s_ctx = _dot_nt(qh, kc_ref[0, :, sl]) * scale
        m = jnp.maximum(jnp.max(s_loc, axis=-1, keepdims=True),
                        jnp.max(s_ctx, axis=-1, keepdims=True))
        p_loc = jnp.exp(s_loc - m)
        p_ctx = jnp.exp(s_ctx - m)
        den = jnp.sum(p_loc, axis=-1, keepdims=True) + jnp.sum(p_ctx, axis=-1, keepdims=True)
        o = _dot(p_loc.astype(BF16), vb) + _dot(p_ctx.astype(BF16), vc_ref[0, :, sl])
        o_ref[0, :, sl] = (o / den).astype(o_ref.dtype)


def _attention(q, k, v, kc, vc, bias):
    b, s, w = q.shape
    rows = s // GRID_W
    c = kc.shape[1]
    full = lambda n: pl.BlockSpec((1, n, w), lambda i, j: (i, 0, 0))
    return pl.pallas_call(
        functools.partial(_attn_kernel, rows),
        grid=(b, rows),
        in_specs=[pl.BlockSpec((1, GRID_W, w), lambda i, j: (i, j, 0)),
                  full(s), full(s), full(c), full(c),
                  pl.BlockSpec(bias.shape, lambda i, j: (0, 0, 0, 0))],
        out_specs=pl.BlockSpec((1, GRID_W, w), lambda i, j: (i, j, 0)),
        out_shape=jax.ShapeDtypeStruct((b, s, w), BF16),
        compiler_params=_cparams(("arbitrary", "arbitrary")),
        name="nbr_attention",
    )(q, k, v, kc, vc, bias)


def _mix_kernel(x_ref, gu_ref, gv_ref, ga_ref, gb_ref, ya_ref, g1_ref, sh2_ref, sc2_ref,
                lng_ref, ws_ref, bs_ref, wpa_ref, wpb_ref, wout_ref, n2g_ref, wq_ref, keys_ref,
                x1_ref, h2_ref, st_ref):
    tm = x_ref.shape[1]
    gw = gu_ref.shape[2]
    u = jax.nn.gelu(gu_ref[0].astype(F32))
    t = jax.nn.gelu(gv_ref[0].astype(F32))
    tc = t - jnp.mean(t, axis=-1, keepdims=True)
    vn = tc * lax.rsqrt(jnp.mean(tc * tc, axis=-1, keepdims=True) + EPS) * lng_ref[...]
    vnb = vn.astype(BF16)
    lane = lax.broadcasted_iota(jnp.int32, (GM_CHUNK, LANES), 1)
    gd = gw // GM_GROUPS
    chunks = []
    for c in range(tm // GM_CHUNK):
        pairs = []
        for gp in range(gw // LANES):
            vp = vnb[c * GM_CHUNK:(c + 1) * GM_CHUNK, gp * LANES:(gp + 1) * LANES]
            r0 = _dot(ws_ref[2 * gp], vp)
            r1 = _dot(ws_ref[2 * gp + 1], vp)
            pairs.append(jnp.where(lane < gd, r0, r1))
        chunks.append(jnp.concatenate(pairs, axis=1) + bs_ref[...])
    mixed = jnp.concatenate(chunks, axis=0)
    yb = (u * mixed).astype(BF16)
    pa = _dot(ya_ref[0], wpa_ref[...])
    pb = _dot(yb, wpb_ref[...])
    m = jax.nn.sigmoid(ga_ref[0].astype(F32)) * pa + jax.nn.sigmoid(gb_ref[0].astype(F32)) * pb
    out = _dot(m.astype(BF16), wout_ref[...])
    x1 = x_ref[0] + g1_ref[0] * out
    x1_ref[0] = x1
    ms = jnp.mean(x1 * x1, axis=-1, keepdims=True)
    h2 = x1 * lax.rsqrt(ms + EPS) * n2g_ref[...]
    h2 = (h2 * (1.0 + sc2_ref[0]) + sh2_ref[0]).astype(BF16)
    half = h2.shape[1] // 2
    h2_ref[0] = _pack_bf16_pairs(h2[:, :half], h2[:, half:])
    qp = _dot(h2, wq_ref[...]).astype(BF16)
    for hp in range(keys_ref.shape[0]):
        st_ref[hp] = _dot_nt(keys_ref[hp], qp[:, hp * PEER_HALF:(hp + 1) * PEER_HALF])


def _mix(x, gu, gv, ga, gb, ya, g1, sh2, sc2, lng, ws, bs_full, wpa, wpb, wout, n2g, wq, keys, tm):
    b, s, d = x.shape
    gw = gu.shape[2]
    nt = s // tm
    nhp = keys.shape[0]
    act = lambda w: pl.BlockSpec((1, tm, w), lambda i, j: (i, j, 0))
    vec = pl.BlockSpec((1, 1, d), lambda i, j: (i, 0, 0))

    def const(a):
        nd = a.ndim
        return pl.BlockSpec(a.shape, lambda i, j: (0,) * nd)

    return pl.pallas_call(
        _mix_kernel,
        grid=(b, nt),
        in_specs=[act(d), act(gw), act(gw), act(d), act(d), act(gw), vec, vec, vec,
                  const(lng), const(ws), const(bs_full), const(wpa), const(wpb), const(wout),
                  const(n2g), const(wq), const(keys)],
        out_specs=[act(d), act(d // 2),
                   pl.BlockSpec((nhp, PEER_N_KEYS, tm), lambda i, j: (0, 0, i * nt + j))],
        out_shape=[jax.ShapeDtypeStruct((b, s, d), F32),
                   jax.ShapeDtypeStruct((b, s, d // 2), jnp.uint32),
                   jax.ShapeDtypeStruct((nhp, PEER_N_KEYS, b * s), F32)],
        compiler_params=_cparams(("arbitrary", "arbitrary")),
        name="mix_peer_scores",
    )(x, gu, gv, ga, gb, ya, g1, sh2, sc2, lng, ws, bs_full, wpa, wpb, wout, n2g, wq, keys)


def _extract_top(vals, order, payload, k):
    out_v, out_p = [], []
    for _ in range(k):
        m = jnp.max(vals, axis=0, keepdims=True)
        o = jnp.min(jnp.where(vals == m, order, jnp.inf), axis=0, keepdims=True)
        sel = order == o
        out_v.append(m)
        if payload is order:
            out_p.append(o)
        else:
            out_p.append(jnp.max(jnp.where(sel, payload, -1.0), axis=0, keepdims=True))
        vals = jnp.where(sel, -jnp.inf, vals)
    return jnp.concatenate(out_v, axis=0), jnp.concatenate(out_p, axis=0)


def _topk_kernel(st_ref, idx_ref, gate_ref, sv_ref, si_ref, se_ref, sg_ref):
    nhp = st_ref.shape[0]
    tm = st_ref.shape[2]
    k = PEER_TOPK
    kidx = lax.broadcasted_iota(jnp.int32, (PEER_N_KEYS, LANES), 0).astype(F32)
    sub = lax.broadcasted_iota(jnp.int32, (SUBLANES, LANES), 0).astype(F32)

    def lane_tile(lt, carry):
        lanes = pl.ds(pl.multiple_of(lt * LANES, LANES), LANES)

        def stage1(hp, c):
            v, i = _extract_top(st_ref[hp, :, lanes], kidx, kidx, k)
            sv_ref[hp] = v
            si_ref[hp] = i
            return c

        lax.fori_loop(0, nhp, stage1, 0)

        def stage2(h, c):
            s1, s2 = sv_ref[2 * h], sv_ref[2 * h + 1]
            i1, i2 = si_ref[2 * h] * float(PEER_N_KEYS), si_ref[2 * h + 1]
            cv, co, ce = [], [], []
            for half in range(2):
                b0 = half * SUBLANES
                cv.append(s1[0:1] + s2[b0:b0 + SUBLANES])
                co.append(sub + float(b0))
                ce.append(i1[0:1] + i2[b0:b0 + SUBLANES])
            for a in range(1, SUBLANES):
                cv.append(s1[a:a + 1] + s2[0:SUBLANES])
                co.append(sub + float(a * k))
                ce.append(i1[a:a + 1] + i2[0:SUBLANES])
            cv.append(s1[SUBLANES:k] + s2[0:1])
            co.append((sub + float(SUBLANES)) * float(k))
            ce.append(i1[SUBLANES:k] + i2[0:1])
            bv, be = _extract_top(jnp.concatenate(cv, axis=0), jnp.concatenate(co, axis=0),
                                  jnp.concatenate(ce, axis=0), k)
            ex = jnp.exp(bv - bv[0:1])
            rows = pl.ds(pl.multiple_of(h * k, k), k)
            sg_ref[rows, :] = ex / jnp.sum(ex, axis=0, keepdims=True)
            se_ref[rows, :] = be
            return c

        lax.fori_loop(0, nhp // 2, stage2, 0)

        rows = pl.ds(pl.multiple_of(lt * LANES, LANES), LANES)
        idx_ref[rows, :] = se_ref[...].T.astype(jnp.int32)
        gate_ref[rows, :] = sg_ref[...].T
        return carry

    lax.fori_loop(0, tm // LANES, lane_tile, 0)


def _topk(st, tm):
    nhp, nk, t = st.shape
    ne = (nhp // 2) * PEER_TOPK
    return pl.pallas_call(
        _topk_kernel,
        grid=(t // tm,),
        in_specs=[pl.BlockSpec((nhp, nk, tm), lambda i: (0, 0, i))],
        out_specs=[pl.BlockSpec((tm, ne), lambda i: (i, 0)),
                   pl.BlockSpec((tm, ne), lambda i: (i, 0))],
        out_shape=[jax.ShapeDtypeStruct((t, ne), jnp.int32),
                   jax.ShapeDtypeStruct((t, ne), F32)],
        scratch_shapes=[pltpu.VMEM((nhp, PEER_TOPK, LANES), F32),
                        pltpu.VMEM((nhp, PEER_TOPK, LANES), F32),
                        pltpu.VMEM((ne, LANES), F32),
                        pltpu.VMEM((ne, LANES), F32)],
        compiler_params=_cparams(("arbitrary",)),
        name="peer_topk",
    )(st)


SC_LANES = 16
SC_WINDOW = 32
SC_ROW_BUFFERS = 4
SC_TOKENS_PER_STEP = 8
SC_WSUM_CHUNKS = 8


def _pack_table(t):
    n, d = t.shape
    tb = t.astype(BF16)
    lo = lax.bitcast_convert_type(tb[:, :d // 2], jnp.uint16).astype(jnp.uint32)
    hi = lax.bitcast_convert_type(tb[:, d // 2:], jnp.uint16).astype(jnp.uint32)
    return lo | (hi << 16)


def _sc_bf16(words):
    return plsc.bitcast(words, BF16)


def _sc_peer(table, idx, operand, mode):
    t, ne = idx.shape
    hw = table.shape[1]
    d = 2 * hw
    L = SC_LANES
    win = SC_WINDOW
    nbuf = SC_ROW_BUFFERS
    ahead = nbuf - 1
    tps = SC_TOKENS_PER_STEP
    nq = SC_WSUM_CHUNKS
    info = plsc.get_sparse_core_info()
    nc, ns = info.num_cores, info.num_subcores
    nss = t // (nc * ns * tps)
    wps = tps * ne // win
    wpt = ne // win
    blk = tps * ne
    nwin = nss * wps
    assert nss * nc * ns * tps == t and nss % 2 == 0 and wps % nbuf == 0 and wps > ahead
    assert wpt * win == ne and win % L == 0 and hw % (nq * L) == 0
    dots = mode == "dots"
    assert dots or nbuf % wpt == 0
    mesh = plsc.VectorSubcoreMesh(core_axis_name="c", subcore_axis_name="s")
    op_buf = pltpu.VMEM((2, tps, hw), jnp.uint32) if dots else pltpu.VMEM((2 * blk,), jnp.int32)
    res_buf = pltpu.VMEM((2 * blk,), F32) if dots else pltpu.VMEM((2, tps, d), F32)
    out_type = jax.ShapeDtypeStruct((t * ne,), F32) if dots else jax.ShapeDtypeStruct((t, d), F32)

    @functools.partial(
        pl.kernel, mesh=mesh, out_type=out_type,
        scratch_types=[pltpu.VMEM((2 * blk,), jnp.int32), op_buf,
                       pltpu.VMEM((nbuf, win, hw), jnp.uint32), res_buf,
                       pltpu.SemaphoreType.DMA((2,)), pltpu.SemaphoreType.DMA((2,)),
                       pltpu.SemaphoreType.DMA((nbuf,)), pltpu.SemaphoreType.DMA((2,))],
        compiler_params=pltpu.CompilerParams(needs_layout_passes=False),
        name="peer_" + mode,
    )
    def sc_kernel(table_hbm, idx_hbm, op_hbm, out_hbm, idx_v, op_v, rows_v, res_v, isem, psem, gsem, osem):
        ss0 = (lax.axis_index("s") * nc + lax.axis_index("c")) * nss
        lane = lax.iota(jnp.int32, L)

        def flat(ref, n):
            return ref.at[pl.ds(pl.multiple_of(n * blk, blk), blk)]

        def tok_rows(ref, n):
            return ref.at[pl.ds(pl.multiple_of(n * tps, tps), tps)]

        def idx_copy(ss, slot):
            return pltpu.make_async_copy(flat(idx_hbm, ss0 + ss), flat(idx_v, slot), isem.at[slot])

        def op_copy(ss, slot):
            if dots:
                return pltpu.make_async_copy(tok_rows(op_hbm, ss0 + ss), op_v.at[slot], psem.at[slot])
            return pltpu.make_async_copy(flat(op_hbm, ss0 + ss), flat(op_v, slot), psem.at[slot])

        def out_copy(ss, slot):
            if dots:
                return pltpu.make_async_copy(flat(res_v, slot), flat(out_hbm, ss0 + ss), osem.at[slot])
            return pltpu.make_async_copy(res_v.at[slot], tok_rows(out_hbm, ss0 + ss), osem.at[slot])

        def gather(slot, hs, b):
            ix = idx_v.at[pl.ds(pl.multiple_of(slot * blk + hs * win, win), win)]
            return pltpu.make_async_copy(table_hbm.at[ix], rows_v.at[b], gsem.at[b])

        def reduce_dots(slot, hs, b):
            tok = hs // wpt

            def head(g, c):
                def chunk(jj, accs):
                    off = pl.multiple_of(jj * (2 * L), 2 * L)
                    xa = _sc_bf16(op_v[slot, tok, pl.ds(off, L)])
                    xb = _sc_bf16(op_v[slot, tok, pl.ds(off + L, L)])
                    new = []
                    for r in range(L):
                        p = (_sc_bf16(rows_v[b, g * L + r, pl.ds(off, L)]) * xa
                             + _sc_bf16(rows_v[b, g * L + r, pl.ds(off + L, L)]) * xb)
                        lo, hi = plsc.unpack(p, format=plsc.PackFormat.INTERLEAVED)
                        new.append(accs[r] + lo + hi)
                    return tuple(new)

                accs = lax.fori_loop(0, hw // (2 * L), chunk, tuple(jnp.zeros((L,), F32) for _ in range(L)))
                out = jnp.zeros((L,), F32)
                for r in range(L):
                    out = jnp.where(lane == r, jnp.sum(accs[r]), out)
                res_v[pl.ds(pl.multiple_of(slot * blk + hs * win + g * L, L), L)] = out
                return c

            lax.fori_loop(0, win // L, head, 0)

        def reduce_wsum(slot, hs, b):
            tok = hs // wpt
            wbase = slot * blk + hs * win
            first = b % wpt == 0

            def colgroup(cg, c):
                col = pl.multiple_of(cg * (nq * L), nq * L)
                if first:
                    accs = tuple(jnp.zeros((L,), F32) for _ in range(2 * nq))
                else:
                    accs = tuple(res_v[slot, tok, pl.ds(col + q * L, L)] for q in range(nq)) + \
                           tuple(res_v[slot, tok, pl.ds(hw + col + q * L, L)] for q in range(nq))

                def rowpair(rp, accs):
                    accs = list(accs)
                    r = rp * 2
                    wa = _sc_bf16(plsc.load_gather(op_v, [jnp.full((L,), wbase + r, jnp.int32)]))
                    wb = _sc_bf16(plsc.load_gather(op_v, [jnp.full((L,), wbase + r + 1, jnp.int32)]))
                    for q in range(nq):
                        p = (_sc_bf16(rows_v[b, r, pl.ds(col + q * L, L)]) * wa
                             + _sc_bf16(rows_v[b, r + 1, pl.ds(col + q * L, L)]) * wb)
                        lo, hi = plsc.unpack(p, format=plsc.PackFormat.INTERLEAVED)
                        accs[q] = accs[q] + lo
                        accs[nq + q] = accs[nq + q] + hi
                    return tuple(accs)

                accs = lax.fori_loop(0, win // 2, rowpair, accs)
                for q in range(nq):
                    res_v[slot, tok, pl.ds(col + q * L, L)] = accs[q]
                    res_v[slot, tok, pl.ds(hw + col + q * L, L)] = accs[nq + q]
                return c

            lax.fori_loop(0, hw // (nq * L), colgroup, 0)

        idx_copy(0, 0).start()
        op_copy(0, 0).start()
        idx_copy(1, 1).start()
        op_copy(1, 1).start()
        idx_copy(0, 0).wait()
        op_copy(0, 0).wait()
        for a in range(ahead):
            gather(0, a, a).start()

        @pl.loop(0, nwin, step=nbuf)
        def _(i0):
            for b in range(nbuf):
                i = i0 + b
                ss = i // wps
                hs = i % wps
                slot = ss % 2
                gather(slot, hs, b).wait()
                nb = (b + ahead) % nbuf

                @pl.when(hs + ahead < wps)
                def _():
                    gather(slot, hs + ahead, nb).start()

                @pl.when(jnp.logical_and(hs + ahead >= wps, ss + 1 < nss))
                def _():
                    @pl.when(hs + ahead == wps)
                    def _():
                        idx_copy(ss + 1, 1 - slot).wait()
                        op_copy(ss + 1, 1 - slot).wait()

                    gather(1 - slot, hs + ahead - wps, nb).start()

                @pl.when(jnp.logical_and(hs == 0, ss >= 2))
                def _():
                    out_copy(ss - 2, slot).wait()

                if dots:
                    reduce_dots(slot, hs, b)
                else:
                    reduce_wsum(slot, hs, b)

                @pl.when(hs + 1 == wps)
                def _():
                    out_copy(ss, slot).start()

                    @pl.when(ss + 2 < nss)
                    def _():
                        idx_copy(ss + 2, slot).start()
                        op_copy(ss + 2, slot).start()

        out_copy(nss - 2, 0).wait()
        out_copy(nss - 1, 1).wait()

    op = operand if dots else operand.reshape(t * ne)
    return sc_kernel(table, idx.reshape(t * ne), op)


def _peer_weight_kernel(act_ref, gate_ref, w_ref):
    w = (jax.nn.gelu(act_ref[...]) * gate_ref[...]).astype(BF16)
    w_ref[...] = pltpu.bitcast(_pack_bf16_pairs(w, w), jnp.int32)


def _peer_weight(act, gate, tm):
    t, ne = act.shape
    tm = min(tm, t)
    assert t % tm == 0
    spec = pl.BlockSpec((tm, ne), lambda i: (i, 0))
    return pl.pallas_call(
        _peer_weight_kernel, grid=(t // tm,), in_specs=[spec, spec], out_specs=spec,
        out_shape=jax.ShapeDtypeStruct((t, ne), jnp.int32),
        compiler_params=_cparams(("arbitrary",)), name="peer_weight",
    )(act, gate)


def _final_kernel(x1_ref, p_ref, g2_ref, fg_ref, o_ref):
    x2 = x1_ref[0] + g2_ref[0] * p_ref[0]
    ms = jnp.mean(x2 * x2, axis=-1, keepdims=True)
    o_ref[0] = x2 * lax.rsqrt(ms + EPS) * fg_ref[...]


def _final(x1, peer, g2, fg, tm):
    b, s, d = x1.shape
    act = pl.BlockSpec((1, tm, d), lambda i, j: (i, j, 0))
    return pl.pallas_call(
        _final_kernel, grid=(b, s // tm),
        in_specs=[act, act, pl.BlockSpec((1, 1, d), lambda i, j: (i, 0, 0)),
                  pl.BlockSpec((1, d), lambda i, j: (0, 0))],
        out_specs=act,
        out_shape=jax.ShapeDtypeStruct((b, s, d), F32),
        compiler_params=_cparams(("arbitrary", "arbitrary")), name="final_norm",
    )(x1, peer, g2, fg)


def kernel(x, c, ctx, c_ctx, ada_w, ada_b, norm1_g, norm2_g, w_in, na_rpb, gm_ln_g, gm_ws, gm_bs,
           w_proj_a, w_proj_b, w_out, peer_wq, peer_keys, peer_u, peer_v, final_g):
    b, s, d = x.shape
    naw = NA_HEADS * HEAD_DIM
    gw = gm_ln_g.shape[1]
    layer = 0

    pad = (-(b + 1)) % SUBLANES
    c_all = jnp.concatenate([c, c_ctx[None, :], jnp.zeros((pad, d), F32)], axis=0)
    mod = _adaln(c_all, ada_w[layer], ada_b[layer])
    sh1, sc1, g1, sh2, sc2, g2 = [mod[:b, i * d:(i + 1) * d].reshape(b, 1, d) for i in range(ADA_CHUNKS)]
    csh1 = jnp.broadcast_to(mod[b, 0:d].reshape(1, 1, d), (b, 1, d))
    csc1 = jnp.broadcast_to(mod[b, d:2 * d].reshape(1, 1, d), (b, 1, d))

    w = w_in[layer].astype(BF16)
    w_kv = w[:, naw:3 * naw]
    n1g = norm1_g[layer].reshape(1, d)
    bias = _bias_table(na_rpb[layer])
    bs_full = jnp.repeat(gm_bs[layer].T, gw // GM_GROUPS, axis=1)
    keys = peer_keys[layer].reshape(2 * PEER_HEADS, PEER_N_KEYS, PEER_HALF).astype(BF16)
    lng = gm_ln_g[layer].reshape(1, gw)
    ws = gm_ws[layer].astype(BF16)
    wpa, wpb, wout = w_proj_a[layer].astype(BF16), w_proj_b[layer].astype(BF16), w_out[layer].astype(BF16)
    n2g = norm2_g[layer].reshape(1, d)
    wq = peer_wq[layer].astype(BF16)
    tab_u, tab_v = _pack_table(peer_u[layer]), _pack_table(peer_v[layer])
    fg = final_g.reshape(1, d)
    ne = PEER_HEADS * PEER_TOPK

    assert sum(BATCH_CHUNKS) == b
    outs = []
    b0 = 0
    for bc in BATCH_CHUNKS:
        sl = slice(b0, b0 + bc)
        b0 += bc
        t = bc * s
        xc = x[sl]
        q, k, v, gu, gv, ga, gb = _norm_proj(xc, n1g, sh1[sl], sc1[sl], w, (naw, naw, naw, gw, gw, d, d), 512)
        k_c, v_c = _norm_proj(ctx[sl], n1g, csh1[sl], csc1[sl], w_kv, (naw, naw), ctx.shape[1])
        y_a = _attention(q, k, v, k_c, v_c, bias)
        x1, h2, st = _mix(xc, gu, gv, ga, gb, y_a, g1[sl], sh2[sl], sc2[sl], lng, ws, bs_full,
                          wpa, wpb, wout, n2g, wq, keys, 256)
        idx, gate = _topk(st, 512)
        act = _sc_peer(tab_u, idx, h2.reshape(t, d // 2), "dots")
        wgt = _peer_weight(act.reshape(t, ne), gate, 2048)
        peer = _sc_peer(tab_v, idx, wgt, "wsum")
        outs.append(_final(x1, peer.reshape(bc, s, d), g2[sl], fg, 512))
    return jnp.concatenate(outs, axis=0)
```

```python
import functools

import jax
import jax.numpy as jnp
from jax import lax
from jax.experimental import pallas as pl
from jax.experimental.pallas import tpu as pltpu
from jax.experimental.pallas import tpu_sc as plsc

F32 = jnp.float32
BF16 = jnp.bfloat16

GRID_W = 64
NA_HEADS = 8
HEAD_DIM = 64
NA_WIN_ROWS = 8
NA_WIN_COLS = 16
GM_GROUPS = 8
GM_CHUNK = 128
PEER_HEADS = 8
PEER_N_KEYS = 128
PEER_TOPK = 16
PEER_HALF = 128
ADA_CHUNKS = 6
EPS = 1e-6
NEG_INF = -1e30

LANES = 128
SUBLANES = 8
VMEM_LIMIT = 56 * 1024 * 1024
BATCH_CHUNKS = (2, 4, 5, 5)


def _dot(a, b):
    return lax.dot_general(a, b, (((1,), (0,)), ((), ())), preferred_element_type=F32)


def _dot_nt(a, b):
    return lax.dot_general(a, b, (((1,), (1,)), ((), ())), preferred_element_type=F32)


def _cparams(sem):
    return pltpu.CompilerParams(dimension_semantics=sem, vmem_limit_bytes=VMEM_LIMIT)


def _pack_bf16_pairs(lo, hi):
    lo_bits = pltpu.bitcast(lo.astype(F32), jnp.uint32) >> 16
    hi_bits = pltpu.bitcast(hi.astype(F32), jnp.uint32) & jnp.uint32(0xFFFF0000)
    return lo_bits | hi_bits


def _adaln_kernel(c_ref, w_ref, b_ref, o_ref):
    c = c_ref[...]
    s = c * jax.nn.sigmoid(c)
    o_ref[...] = lax.dot_general(s, w_ref[...], (((1,), (0,)), ((), ())),
                                 precision=lax.Precision.HIGHEST,
                                 preferred_element_type=F32) + b_ref[...]


def _adaln(c_all, w, b):
    m, d = c_all.shape
    n = w.shape[1]
    tn = 1024
    return pl.pallas_call(
        _adaln_kernel,
        grid=(n // tn,),
        in_specs=[pl.BlockSpec((m, d), lambda j: (0, 0)),
                  pl.BlockSpec((d, tn), lambda j: (0, j)),
                  pl.BlockSpec((1, tn), lambda j: (0, j))],
        out_specs=pl.BlockSpec((m, tn), lambda j: (0, j)),
        out_shape=jax.ShapeDtypeStruct((m, n), F32),
        compiler_params=_cparams(("arbitrary",)),
        name="adaln",
    )(c_all, w, b.reshape(1, n))


def _norm_proj_kernel(widths, x_ref, g_ref, sh_ref, sc_ref, w_ref, *o_refs):
    x = x_ref[0]
    ms = jnp.mean(x * x, axis=-1, keepdims=True)
    y = x * lax.rsqrt(ms + EPS) * g_ref[...]
    h = (y * (1.0 + sc_ref[0]) + sh_ref[0]).astype(BF16)
    off = 0
    for o_ref, wd in zip(o_refs, widths):
        o_ref[0] = _dot(h, w_ref[:, off:off + wd]).astype(o_ref.dtype)
        off += wd


def _norm_proj(x, gain, shift, scale, w, widths, tm):
    b, s, d = x.shape
    n = w.shape[1]
    assert sum(widths) == n and s % tm == 0
    vec = pl.BlockSpec((1, 1, d), lambda i, j: (i, 0, 0))
    return pl.pallas_call(
        functools.partial(_norm_proj_kernel, widths),
        grid=(b, s // tm),
        in_specs=[pl.BlockSpec((1, tm, d), lambda i, j: (i, j, 0)),
                  pl.BlockSpec((1, d), lambda i, j: (0, 0)),
                  vec, vec,
                  pl.BlockSpec((d, n), lambda i, j: (0, 0))],
        out_specs=[pl.BlockSpec((1, tm, wd), lambda i, j: (i, j, 0)) for wd in widths],
        out_shape=[jax.ShapeDtypeStruct((b, s, wd), BF16) for wd in widths],
        compiler_params=_cparams(("arbitrary", "arbitrary")),
        name="norm_proj",
    )(x, gain, shift, scale, w)


def _bias_table_kernel(rpb_ref, o_ref):
    h = pl.program_id(0)
    d0 = pl.program_id(1)
    q = lax.broadcasted_iota(jnp.int32, (GRID_W, GRID_W), 0)
    kc = lax.broadcasted_iota(jnp.int32, (GRID_W, GRID_W), 1)
    dc = jnp.clip(kc - q + NA_WIN_COLS - 1, 0, 2 * NA_WIN_COLS - 2)
    cs = jnp.clip(q - NA_WIN_COLS // 2, 0, GRID_W - NA_WIN_COLS)
    col_in = (kc >= cs) & (kc < cs + NA_WIN_COLS)
    n_dc = 2 * NA_WIN_COLS - 1
    n_dr = 2 * NA_WIN_ROWS - 1
    for j in range(NA_WIN_ROWS):
        row = h * n_dr + d0 + j
        t = jnp.zeros((GRID_W, GRID_W), F32)
        for c in range(n_dc):
            t = jnp.where(dc == c, rpb_ref[row, c], t)
        o_ref[0, 0, :, j * GRID_W:(j + 1) * GRID_W] = jnp.where(col_in, t, NEG_INF)


def _bias_table(rpb):
    nh, n_dr, n_dc = rpb.shape
    band = NA_WIN_ROWS * GRID_W
    return pl.pallas_call(
        _bias_table_kernel,
        grid=(nh, NA_WIN_ROWS),
        in_specs=[pl.BlockSpec(memory_space=pltpu.SMEM)],
        out_specs=pl.BlockSpec((1, 1, GRID_W, band), lambda h, d: (h, d, 0, 0)),
        out_shape=jax.ShapeDtypeStruct((nh, NA_WIN_ROWS, GRID_W, band), F32),
        compiler_params=_cparams(("arbitrary", "arbitrary")),
        name="bias_table",
    )(rpb.reshape(nh * n_dr, n_dc))


def _attn_kernel(rows, q_ref, k_ref, v_ref, kc_ref, vc_ref, bias_ref, o_ref):
    r = pl.program_id(1)
    rs = jnp.clip(r - NA_WIN_ROWS // 2, 0, rows - NA_WIN_ROWS)
    d0 = rs - r + NA_WIN_ROWS - 1
    band = NA_WIN_ROWS * GRID_W
    start = pl.multiple_of(rs * GRID_W, GRID_W)
    scale = HEAD_DIM ** -0.5
    for h in range(NA_HEADS):
        sl = slice(h * HEAD_DIM, (h + 1) * HEAD_DIM)
        qh = q_ref[0, :, sl]
        kb = k_ref[0, pl.ds(start, band), sl]
        vb = v_ref[0, pl.ds(start, band), sl]
        s_loc = _dot_nt(qh, kb) * scale + bias_ref[h, d0]
        s_ctx = _dot_nt(qh, kc_ref[0, :, sl]) * scale
        m = jnp.maximum(jnp.max(s_loc, axis=-1, keepdims=True),
                        jnp.max(s_ctx, axis=-1, keepdims=True))
        p_loc = jnp.exp(s_loc - m)
        p_ctx = jnp.exp(s_ctx - m)
        den = jnp.sum(p_loc, axis=-1, keepdims=True) + jnp.sum(p_ctx, axis=-1, keepdims=True)
        o = _dot(p_loc.astype(BF16), vb) + _dot(p_ctx.astype(BF16), vc_ref[0, :, sl])
        o_ref[0, :, sl] = (o / den).astype(o_ref.dtype)


def _attention(q, k, v, kc, vc, bias):
    b, s, w = q.shape
    rows = s // GRID_W
    c = kc.shape[1]
    full = lambda n: pl.BlockSpec((1, n, w), lambda i, j: (i, 0, 0))
    return pl.pallas_call(
        functools.partial(_attn_kernel, rows),
        grid=(b, rows),
        in_specs=[pl.BlockSpec((1, GRID_W, w), lambda i, j: (i, j, 0)),
                  full(s), full(s), full(c), full(c),
                  pl.BlockSpec(bias.shape, lambda i, j: (0, 0, 0, 0))],
        out_specs=pl.BlockSpec((1, GRID_W, w), lambda i, j: (i, j, 0)),
        out_shape=jax.ShapeDtypeStruct((b, s, w), BF16),
        compiler_params=_cparams(("arbitrary", "arbitrary")),
        name="nbr_attention",
    )(q, k, v, kc, vc, bias)


def _mix_kernel(x_ref, gu_ref, gv_ref, ga_ref, gb_ref, ya_ref, g1_ref, sh2_ref, sc2_ref,
                lng_ref, ws_ref, bs_ref, wpa_ref, wpb_ref, wout_ref, n2g_ref, wq_ref, keys_ref,
                x1_ref, h2_ref, st_ref):
    tm = x_ref.shape[1]
    gw = gu_ref.shape[2]
    u = jax.nn.gelu(gu_ref[0].astype(F32))
    t = jax.nn.gelu(gv_ref[0].astype(F32))
    tc = t - jnp.mean(t, axis=-1, keepdims=True)
    vn = tc * lax.rsqrt(jnp.mean(tc * tc, axis=-1, keepdims=True) + EPS) * lng_ref[...]
    vnb = vn.astype(BF16)
    lane = lax.broadcasted_iota(jnp.int32, (GM_CHUNK, LANES), 1)
    gd = gw // GM_GROUPS
    chunks = []
    for c in range(tm // GM_CHUNK):
        pairs = []
        for gp in range(gw // LANES):
            vp = vnb[c * GM_CHUNK:(c + 1) * GM_CHUNK, gp * LANES:(gp + 1) * LANES]
            r0 = _dot(ws_ref[2 * gp], vp)
            r1 = _dot(ws_ref[2 * gp + 1], vp)
            pairs.append(jnp.where(lane < gd, r0, r1))
        chunks.append(jnp.concatenate(pairs, axis=1) + bs_ref[...])
    mixed = jnp.concatenate(chunks, axis=0)
    yb = (u * mixed).astype(BF16)
    pa = _dot(ya_ref[0], wpa_ref[...])
    pb = _dot(yb, wpb_ref[...])
    m = jax.nn.sigmoid(ga_ref[0].astype(F32)) * pa + jax.nn.sigmoid(gb_ref[0].astype(F32)) * pb
    out = _dot(m.astype(BF16), wout_ref[...])
    x1 = x_ref[0] + g1_ref[0] * out
    x1_ref[0] = x1
    ms = jnp.mean(x1 * x1, axis=-1, keepdims=True)
    h2 = x1 * lax.rsqrt(ms + EPS) * n2g_ref[...]
    h2 = (h2 * (1.0 + sc2_ref[0]) + sh2_ref[0]).astype(BF16)
    half = h2.shape[1] // 2
    h2_ref[0] = _pack_bf16_pairs(h2[:, :half], h2[:, half:])
    qp = _dot(h2, wq_ref[...]).astype(BF16)
    for hp in range(keys_ref.shape[0]):
        st_ref[hp] = _dot_nt(keys_ref[hp], qp[:, hp * PEER_HALF:(hp + 1) * PEER_HALF])


def _mix(x, gu, gv, ga, gb, ya, g1, sh2, sc2, lng, ws, bs_full, wpa, wpb, wout, n2g, wq, keys, tm):
    b, s, d = x.shape
    gw = gu.shape[2]
    nt = s // tm
    nhp = keys.shape[0]
    act = lambda w: pl.BlockSpec((1, tm, w), lambda i, j: (i, j, 0))
    vec = pl.BlockSpec((1, 1, d), lambda i, j: (i, 0, 0))

    def const(a):
        nd = a.ndim
        return pl.BlockSpec(a.shape, lambda i, j: (0,) * nd)

    return pl.pallas_call(
        _mix_kernel,
        grid=(b, nt),
        in_specs=[act(d), act(gw), act(gw), act(d), act(d), act(gw), vec, vec, vec,
                  const(lng), const(ws), const(bs_full), const(wpa), const(wpb), const(wout),
                  const(n2g), const(wq), const(keys)],
        out_specs=[act(d), act(d // 2),
                   pl.BlockSpec((nhp, PEER_N_KEYS, tm), lambda i, j: (0, 0, i * nt + j))],
        out_shape=[jax.ShapeDtypeStruct((b, s, d), F32),
                   jax.ShapeDtypeStruct((b, s, d // 2), jnp.uint32),
                   jax.ShapeDtypeStruct((nhp, PEER_N_KEYS, b * s), F32)],
        compiler_params=_cparams(("arbitrary", "arbitrary")),
        name="mix_peer_scores",
    )(x, gu, gv, ga, gb, ya, g1, sh2, sc2, lng, ws, bs_full, wpa, wpb, wout, n2g, wq, keys)


def _extract_top(vals, order, payload, k):
    out_v, out_p = [], []
    for _ in range(k):
        m = jnp.max(vals, axis=0, keepdims=True)
        o = jnp.min(jnp.where(vals == m, order, jnp.inf), axis=0, keepdims=True)
        sel = order == o
        out_v.append(m)
        if payload is order:
            out_p.append(o)
        else:
            out_p.append(jnp.max(jnp.where(sel, payload, -1.0), axis=0, keepdims=True))
        vals = jnp.where(sel, -jnp.inf, vals)
    return jnp.concatenate(out_v, axis=0), jnp.concatenate(out_p, axis=0)


def _topk_kernel(st_ref, idx_ref, gate_ref, sv_ref, si_ref, se_ref, sg_ref):
    nhp = st_ref.shape[0]
    tm = st_ref.shape[2]
    k = PEER_TOPK
    kidx = lax.broadcasted_iota(jnp.int32, (PEER_N_KEYS, LANES), 0).astype(F32)
    sub = lax.broadcasted_iota(jnp.int32, (SUBLANES, LANES), 0).astype(F32)

    def lane_tile(lt, carry):
        lanes = pl.ds(pl.multiple_of(lt * LANES, LANES), LANES)

        def stage1(h, c):
            for hp in (2 * h, 2 * h + 1):
                v, i = _extract_top(st_ref[hp, :, lanes], kidx, kidx, k)
                sv_ref[hp] = v
                si_ref[hp] = i
            return c

        lax.fori_loop(0, nhp // 2, stage1, 0)

        def stage2(h2, c):
            for h in (2 * h2, 2 * h2 + 1):
                stage2_head(h)
            return c

        def stage2_head(h):
            s1, s2 = sv_ref[2 * h], sv_ref[2 * h + 1]
            i1, i2 = si_ref[2 * h] * float(PEER_N_KEYS), si_ref[2 * h + 1]
            cv, co, ce = [], [], []
            for half in range(2):
                b0 = half * SUBLANES
                cv.append(s1[0:1] + s2[b0:b0 + SUBLANES])
                co.append(sub + float(b0))
                ce.append(i1[0:1] + i2[b0:b0 + SUBLANES])
            for a in range(1, SUBLANES):
                cv.append(s1[a:a + 1] + s2[0:SUBLANES])
                co.append(sub + float(a * k))
                ce.append(i1[a:a + 1] + i2[0:SUBLANES])
            cv.append(s1[SUBLANES:k] + s2[0:1])
            co.append((sub + float(SUBLANES)) * float(k))
            ce.append(i1[SUBLANES:k] + i2[0:1])
            bv, be = _extract_top(jnp.concatenate(cv, axis=0), jnp.concatenate(co, axis=0),
                                  jnp.concatenate(ce, axis=0), k)
            ex = jnp.exp(bv - bv[0:1])
            rows = pl.ds(pl.multiple_of(h * k, k), k)
            sg_ref[rows, :] = ex / jnp.sum(ex, axis=0, keepdims=True)
            se_ref[rows, :] = be

        lax.fori_loop(0, nhp // 4, stage2, 0)

        rows = pl.ds(pl.multiple_of(lt * LANES, LANES), LANES)
        idx_ref[rows, :] = se_ref[...].T.astype(jnp.int32)
        gate_ref[rows, :] = sg_ref[...].T
        return carry

    lax.fori_loop(0, tm // LANES, lane_tile, 0)


def _topk(st, tm):
    nhp, nk, t = st.shape
    ne = (nhp // 2) * PEER_TOPK
    return pl.pallas_call(
        _topk_kernel,
        grid=(t // tm,),
        in_specs=[pl.BlockSpec((nhp, nk, tm), lambda i: (0, 0, i))],
        out_specs=[pl.BlockSpec((tm, ne), lambda i: (i, 0)),
                   pl.BlockSpec((tm, ne), lambda i: (i, 0))],
        out_shape=[jax.ShapeDtypeStruct((t, ne), jnp.int32),
                   jax.ShapeDtypeStruct((t, ne), F32)],
        scratch_shapes=[pltpu.VMEM((nhp, PEER_TOPK, LANES), F32),
                        pltpu.VMEM((nhp, PEER_TOPK, LANES), F32),
                        pltpu.VMEM((ne, LANES), F32),
                        pltpu.VMEM((ne, LANES), F32)],
        compiler_params=_cparams(("arbitrary",)),
        name="peer_topk",
    )(st)


SC_LANES = 16
SC_WINDOW = 32
SC_ROW_BUFFERS = 4
SC_TOKENS_PER_STEP = 8
SC_WSUM_CHUNKS = 8


def _pack_table(t):
    n, d = t.shape
    tb = t.astype(BF16)
    lo = lax.bitcast_convert_type(tb[:, :d // 2], jnp.uint16).astype(jnp.uint32)
    hi = lax.bitcast_convert_type(tb[:, d // 2:], jnp.uint16).astype(jnp.uint32)
    return lo | (hi << 16)


def _sc_bf16(words):
    return plsc.bitcast(words, BF16)


def _sc_peer(table, idx, operand, mode):
    t, ne = idx.shape
    hw = table.shape[1]
    d = 2 * hw
    L = SC_LANES
    win = SC_WINDOW
    nbuf = SC_ROW_BUFFERS
    ahead = nbuf - 1
    tps = SC_TOKENS_PER_STEP
    nq = SC_WSUM_CHUNKS
    info = plsc.get_sparse_core_info()
    nc, ns = info.num_cores, info.num_subcores
    nss = t // (nc * ns * tps)
    wps = tps * ne // win
    wpt = ne // win
    blk = tps * ne
    nwin = nss * wps
    assert nss * nc * ns * tps == t and nss % 2 == 0 and wps % nbuf == 0 and wps > ahead
    assert wpt * win == ne and win % L == 0 and hw % (nq * L) == 0
    dots = mode == "dots"
    assert dots or nbuf % wpt == 0
    mesh = plsc.VectorSubcoreMesh(core_axis_name="c", subcore_axis_name="s")
    op_buf = pltpu.VMEM((2, tps, hw), jnp.uint32) if dots else pltpu.VMEM((2 * blk,), jnp.int32)
    res_buf = pltpu.VMEM((2 * blk,), F32) if dots else pltpu.VMEM((2, tps, d), F32)
    out_type = jax.ShapeDtypeStruct((t * ne,), F32) if dots else jax.ShapeDtypeStruct((t, d), F32)

    @functools.partial(
        pl.kernel, mesh=mesh, out_type=out_type,
        scratch_types=[pltpu.VMEM((2 * blk,), jnp.int32), op_buf,
                       pltpu.VMEM((nbuf, win, hw), jnp.uint32), res_buf,
                       pltpu.SemaphoreType.DMA((2,)), pltpu.SemaphoreType.DMA((2,)),
                       pltpu.SemaphoreType.DMA((nbuf,)), pltpu.SemaphoreType.DMA((2,))],
        compiler_params=pltpu.CompilerParams(needs_layout_passes=False),
        name="peer_" + mode,
    )
    def sc_kernel(table_hbm, idx_hbm, op_hbm, out_hbm, idx_v, op_v, rows_v, res_v, isem, psem, gsem, osem):
        ss0 = (lax.axis_index("s") * nc + lax.axis_index("c")) * nss
        lane = lax.iota(jnp.int32, L)

        def flat(ref, n):
            return ref.at[pl.ds(pl.multiple_of(n * blk, blk), blk)]

        def tok_rows(ref, n):
            return ref.at[pl.ds(pl.multiple_of(n * tps, tps), tps)]

        def idx_copy(ss, slot):
            return pltpu.make_async_copy(flat(idx_hbm, ss0 + ss), flat(idx_v, slot), isem.at[slot])

        def op_copy(ss, slot):
            if dots:
                return pltpu.make_async_copy(tok_rows(op_hbm, ss0 + ss), op_v.at[slot], psem.at[slot])
            return pltpu.make_async_copy(flat(op_hbm, ss0 + ss), flat(op_v, slot), psem.at[slot])

        def out_copy(ss, slot):
            if dots:
                return pltpu.make_async_copy(flat(res_v, slot), flat(out_hbm, ss0 + ss), osem.at[slot])
            return pltpu.make_async_copy(res_v.at[slot], tok_rows(out_hbm, ss0 + ss), osem.at[slot])

        def gather(slot, hs, b):
            ix = idx_v.at[pl.ds(pl.multiple_of(slot * blk + hs * win, win), win)]
            return pltpu.make_async_copy(table_hbm.at[ix], rows_v.at[b], gsem.at[b])

        def reduce_dots(slot, hs, b):
            tok = hs // wpt

            def head(g, c):
                def chunk(jj, accs):
                    off = pl.multiple_of(jj * (2 * L), 2 * L)
                    xa = _sc_bf16(op_v[slot, tok, pl.ds(off, L)])
                    xb = _sc_bf16(op_v[slot, tok, pl.ds(off + L, L)])
                    new = []
                    for r in range(L):
                        p = (_sc_bf16(rows_v[b, g * L + r, pl.ds(off, L)]) * xa
                             + _sc_bf16(rows_v[b, g * L + r, pl.ds(off + L, L)]) * xb)
                        lo, hi = plsc.unpack(p, format=plsc.PackFormat.INTERLEAVED)
                        new.append(accs[r] + lo + hi)
                    return tuple(new)

                accs = lax.fori_loop(0, hw // (2 * L), chunk, tuple(jnp.zeros((L,), F32) for _ in range(L)))
                out = jnp.zeros((L,), F32)
                for r in range(L):
                    out = jnp.where(lane == r, jnp.sum(accs[r]), out)
                res_v[pl.ds(pl.multiple_of(slot * blk + hs * win + g * L, L), L)] = out
                return c

            lax.fori_loop(0, win // L, head, 0)

        def reduce_wsum(slot, hs, b):
            tok = hs // wpt
            wbase = slot * blk + hs * win
            first = b % wpt == 0

            def colgroup(cg, c):
                col = pl.multiple_of(cg * (nq * L), nq * L)
                if first:
                    accs = tuple(jnp.zeros((L,), F32) for _ in range(2 * nq))
                else:
                    accs = tuple(res_v[slot, tok, pl.ds(col + q * L, L)] for q in range(nq)) + \
                           tuple(res_v[slot, tok, pl.ds(hw + col + q * L, L)] for q in range(nq))

                def rowpair(rp, accs):
                    accs = list(accs)
                    r = rp * 2
                    wa = _sc_bf16(plsc.load_gather(op_v, [jnp.full((L,), wbase + r, jnp.int32)]))
                    wb = _sc_bf16(plsc.load_gather(op_v, [jnp.full((L,), wbase + r + 1, jnp.int32)]))
                    for q in range(nq):
                        p = (_sc_bf16(rows_v[b, r, pl.ds(col + q * L, L)]) * wa
                             + _sc_bf16(rows_v[b, r + 1, pl.ds(col + q * L, L)]) * wb)
                        lo, hi = plsc.unpack(p, format=plsc.PackFormat.INTERLEAVED)
                        accs[q] = accs[q] + lo
                        accs[nq + q] = accs[nq + q] + hi
                    return tuple(accs)

                accs = lax.fori_loop(0, win // 2, rowpair, accs)
                for q in range(nq):
                    res_v[slot, tok, pl.ds(col + q * L, L)] = accs[q]
                    res_v[slot, tok, pl.ds(hw + col + q * L, L)] = accs[nq + q]
                return c

            lax.fori_loop(0, hw // (nq * L), colgroup, 0)

        idx_copy(0, 0).start()
        op_copy(0, 0).start()
        idx_copy(1, 1).start()
        op_copy(1, 1).start()
        idx_copy(0, 0).wait()
        op_copy(0, 0).wait()
        for a in range(ahead):
            gather(0, a, a).start()

        @pl.loop(0, nwin, step=nbuf)
        def _(i0):
            for b in range(nbuf):
                i = i0 + b
                ss = i // wps
                hs = i % wps
                slot = ss % 2
                gather(slot, hs, b).wait()
                nb = (b + ahead) % nbuf

                @pl.when(hs + ahead < wps)
                def _():
                    gather(slot, hs + ahead, nb).start()

                @pl.when(jnp.logical_and(hs + ahead >= wps, ss + 1 < nss))
                def _():
                    @pl.when(hs + ahead == wps)
                    def _():
                        idx_copy(ss + 1, 1 - slot).wait()
                        op_copy(ss + 1, 1 - slot).wait()

                    gather(1 - slot, hs + ahead - wps, nb).start()

                @pl.when(jnp.logical_and(hs == 0, ss >= 2))
                def _():
                    out_copy(ss - 2, slot).wait()

                if dots:
                    reduce_dots(slot, hs, b)
                else:
                    reduce_wsum(slot, hs, b)

                @pl.when(hs + 1 == wps)
                def _():
                    out_copy(ss, slot).start()

                    @pl.when(ss + 2 < nss)
                    def _():
                        idx_copy(ss + 2, slot).start()
                        op_copy(ss + 2, slot).start()

        out_copy(nss - 2, 0).wait()
        out_copy(nss - 1, 1).wait()

    op = operand if dots else operand.reshape(t * ne)
    return sc_kernel(table, idx.reshape(t * ne), op)


def _peer_weight_kernel(act_ref, gate_ref, w_ref):
    w = (jax.nn.gelu(act_ref[...]) * gate_ref[...]).astype(BF16)
    w_ref[...] = pltpu.bitcast(_pack_bf16_pairs(w, w), jnp.int32)


def _peer_weight(act, gate, tm):
    t, ne = act.shape
    tm = min(tm, t)
    assert t % tm == 0
    spec = pl.BlockSpec((tm, ne), lambda i: (i, 0))
    return pl.pallas_call(
        _peer_weight_kernel, grid=(t // tm,), in_specs=[spec, spec], out_specs=spec,
        out_shape=jax.ShapeDtypeStruct((t, ne), jnp.int32),
        compiler_params=_cparams(("arbitrary",)), name="peer_weight",
    )(act, gate)


def _final_kernel(x1_ref, p_ref, g2_ref, fg_ref, o_ref):
    x2 = x1_ref[0] + g2_ref[0] * p_ref[0]
    ms = jnp.mean(x2 * x2, axis=-1, keepdims=True)
    o_ref[0] = x2 * lax.rsqrt(ms + EPS) * fg_ref[...]


def _final(x1, peer, g2, fg, tm):
    b, s, d = x1.shape
    act = pl.BlockSpec((1, tm, d), lambda i, j: (i, j, 0))
    return pl.pallas_call(
        _final_kernel, grid=(b, s // tm),
        in_specs=[act, act, pl.BlockSpec((1, 1, d), lambda i, j: (i, 0, 0)),
                  pl.BlockSpec((1, d), lambda i, j: (0, 0))],
        out_specs=act,
        out_shape=jax.ShapeDtypeStruct((b, s, d), F32),
        compiler_params=_cparams(("arbitrary", "arbitrary")), name="final_norm",
    )(x1, peer, g2, fg)


def kernel(x, c, ctx, c_ctx, ada_w, ada_b, norm1_g, norm2_g, w_in, na_rpb, gm_ln_g, gm_ws, gm_bs,
           w_proj_a, w_proj_b, w_out, peer_wq, peer_keys, peer_u, peer_v, final_g):
    b, s, d = x.shape
    naw = NA_HEADS * HEAD_DIM
    gw = gm_ln_g.shape[1]
    layer = 0

    pad = (-(b + 1)) % SUBLANES
    c_all = jnp.concatenate([c, c_ctx[None, :], jnp.zeros((pad, d), F32)], axis=0)
    mod = _adaln(c_all, ada_w[layer], ada_b[layer])
    sh1, sc1, g1, sh2, sc2, g2 = [mod[:b, i * d:(i + 1) * d].reshape(b, 1, d) for i in range(ADA_CHUNKS)]
    csh1 = jnp.broadcast_to(mod[b, 0:d].reshape(1, 1, d), (b, 1, d))
    csc1 = jnp.broadcast_to(mod[b, d:2 * d].reshape(1, 1, d), (b, 1, d))

    w = w_in[layer].astype(BF16)
    w_kv = w[:, naw:3 * naw]
    n1g = norm1_g[layer].reshape(1, d)
    bias = _bias_table(na_rpb[layer])
    bs_full = jnp.repeat(gm_bs[layer].T, gw // GM_GROUPS, axis=1)
    keys = peer_keys[layer].reshape(2 * PEER_HEADS, PEER_N_KEYS, PEER_HALF).astype(BF16)
    lng = gm_ln_g[layer].reshape(1, gw)
    ws = gm_ws[layer].astype(BF16)
    wpa, wpb, wout = w_proj_a[layer].astype(BF16), w_proj_b[layer].astype(BF16), w_out[layer].astype(BF16)
    n2g = norm2_g[layer].reshape(1, d)
    wq = peer_wq[layer].astype(BF16)
    tab_u, tab_v = _pack_table(peer_u[layer]), _pack_table(peer_v[layer])
    fg = final_g.reshape(1, d)
    ne = PEER_HEADS * PEER_TOPK

    assert sum(BATCH_CHUNKS) == b
    outs = []
    peers = []
    b0 = 0
    for bc in BATCH_CHUNKS:
        sl = slice(b0, b0 + bc)
        b0 += bc
        t = bc * s
        xc = x[sl]
        if len(peers) >= 2:
            xc, _ = lax.optimization_barrier((xc, peers[-2]))
        q, k, v, gu, gv, ga, gb = _norm_proj(xc, n1g, sh1[sl], sc1[sl], w, (naw, naw, naw, gw, gw, d, d), 512)
        k_c, v_c = _norm_proj(ctx[sl], n1g, csh1[sl], csc1[sl], w_kv, (naw, naw), ctx.shape[1])
        y_a = _attention(q, k, v, k_c, v_c, bias)
        x1, h2, st = _mix(xc, gu, gv, ga, gb, y_a, g1[sl], sh2[sl], sc2[sl], lng, ws, bs_full,
                          wpa, wpb, wout, n2g, wq, keys, 256)
        idx, gate = _topk(st, 512)
        act = _sc_peer(tab_u, idx, h2.reshape(t, d // 2), "dots")
        wgt = _peer_weight(act.reshape(t, ne), gate, 2048)
        peer = _sc_peer(tab_v, idx, wgt, "wsum")
        peers.append(peer)
        outs.append(_final(x1, peer.reshape(bc, s, d), g2[sl], fg, 512))
    return jnp.concatenate(outs, axis=0)
```

```python
import functools

import jax
import jax.numpy as jnp
from jax import lax
from jax.experimental import pallas as pl
from jax.experimental.pallas import tpu as pltpu
from jax.experimental.pallas import tpu_sc as plsc

F32 = jnp.float32
BF16 = jnp.bfloat16

GRID_W = 64
NA_HEADS = 8
HEAD_DIM = 64
NA_WIN_ROWS = 8
NA_WIN_COLS = 16
GM_GROUPS = 8
GM_CHUNK = 128
PEER_HEADS = 8
PEER_N_KEYS = 128
PEER_TOPK = 16
PEER_HALF = 128
ADA_CHUNKS = 6
EPS = 1e-6
NEG_INF = -1e30

LANES = 128
SUBLANES = 8
VMEM_LIMIT = 56 * 1024 * 1024
BATCH_CHUNKS = (1, 2, 3, 5, 5)


def _dot(a, b):
    return lax.dot_general(a, b, (((1,), (0,)), ((), ())), preferred_element_type=F32)


def _dot_nt(a, b):
    return lax.dot_general(a, b, (((1,), (1,)), ((), ())), preferred_element_type=F32)


def _cparams(sem):
    return pltpu.CompilerParams(dimension_semantics=sem, vmem_limit_bytes=VMEM_LIMIT)


def _pack_bf16_pairs(lo, hi):
    lo_bits = pltpu.bitcast(lo.astype(F32), jnp.uint32) >> 16
    hi_bits = pltpu.bitcast(hi.astype(F32), jnp.uint32) & jnp.uint32(0xFFFF0000)
    return lo_bits | hi_bits


def _adaln_kernel(c_ref, w_ref, b_ref, o_ref):
    c = c_ref[...]
    s = c * jax.nn.sigmoid(c)
    o_ref[...] = lax.dot_general(s, w_ref[...], (((1,), (0,)), ((), ())),
                                 precision=lax.Precision.HIGHEST,
                                 preferred_element_type=F32) + b_ref[...]


def _adaln(c_all, w, b):
    m, d = c_all.shape
    n = w.shape[1]
    tn = 1024
    return pl.pallas_call(
        _adaln_kernel,
        grid=(n // tn,),
        in_specs=[pl.BlockSpec((m, d), lambda j: (0, 0)),
                  pl.BlockSpec((d, tn), lambda j: (0, j)),
                  pl.BlockSpec((1, tn), lambda j: (0, j))],
        out_specs=pl.BlockSpec((m, tn), lambda j: (0, j)),
        out_shape=jax.ShapeDtypeStruct((m, n), F32),
        compiler_params=_cparams(("arbitrary",)),
        name="adaln",
    )(c_all, w, b.reshape(1, n))


def _norm_proj_kernel(widths, x_ref, g_ref, sh_ref, sc_ref, w_ref, *o_refs):
    x = x_ref[0]
    ms = jnp.mean(x * x, axis=-1, keepdims=True)
    y = x * lax.rsqrt(ms + EPS) * g_ref[...]
    h = (y * (1.0 + sc_ref[0]) + sh_ref[0]).astype(BF16)
    off = 0
    for o_ref, wd in zip(o_refs, widths):
        o_ref[0] = _dot(h, w_ref[:, off:off + wd]).astype(o_ref.dtype)
        off += wd


def _norm_proj(x, gain, shift, scale, w, widths, tm):
    b, s, d = x.shape
    n = w.shape[1]
    assert sum(widths) == n and s % tm == 0
    vec = pl.BlockSpec((1, 1, d), lambda i, j: (i, 0, 0))
    return pl.pallas_call(
        functools.partial(_norm_proj_kernel, widths),
        grid=(b, s // tm),
        in_specs=[pl.BlockSpec((1, tm, d), lambda i, j: (i, j, 0)),
                  pl.BlockSpec((1, d), lambda i, j: (0, 0)),
                  vec, vec,
                  pl.BlockSpec((d, n), lambda i, j: (0, 0))],
        out_specs=[pl.BlockSpec((1, tm, wd), lambda i, j: (i, j, 0)) for wd in widths],
        out_shape=[jax.ShapeDtypeStruct((b, s, wd), BF16) for wd in widths],
        compiler_params=_cparams(("arbitrary", "arbitrary")),
        name="norm_proj",
    )(x, gain, shift, scale, w)


def _bias_table_kernel(rpb_ref, o_ref):
    h = pl.program_id(0)
    d0 = pl.program_id(1)
    q = lax.broadcasted_iota(jnp.int32, (GRID_W, GRID_W), 0)
    kc = lax.broadcasted_iota(jnp.int32, (GRID_W, GRID_W), 1)
    dc = jnp.clip(kc - q + NA_WIN_COLS - 1, 0, 2 * NA_WIN_COLS - 2)
    cs = jnp.clip(q - NA_WIN_COLS // 2, 0, GRID_W - NA_WIN_COLS)
    col_in = (kc >= cs) & (kc < cs + NA_WIN_COLS)
    n_dc = 2 * NA_WIN_COLS - 1
    n_dr = 2 * NA_WIN_ROWS - 1
    for j in range(NA_WIN_ROWS):
        row = h * n_dr + d0 + j
        t = jnp.zeros((GRID_W, GRID_W), F32)
        for c in range(n_dc):
            t = jnp.where(dc == c, rpb_ref[row, c], t)
        o_ref[0, 0, :, j * GRID_W:(j + 1) * GRID_W] = jnp.where(col_in, t, NEG_INF)


def _bias_table(rpb):
    nh, n_dr, n_dc = rpb.shape
    band = NA_WIN_ROWS * GRID_W
    return pl.pallas_call(
        _bias_table_kernel,
        grid=(nh, NA_WIN_ROWS),
        in_specs=[pl.BlockSpec(memory_space=pltpu.SMEM)],
        out_specs=pl.BlockSpec((1, 1, GRID_W, band), lambda h, d: (h, d, 0, 0)),
        out_shape=jax.ShapeDtypeStruct((nh, NA_WIN_ROWS, GRID_W, band), F32),
        compiler_params=_cparams(("arbitrary", "arbitrary")),
        name="bias_table",
    )(rpb.reshape(nh * n_dr, n_dc))


def _attn_kernel(rows, q_ref, k_ref, v_ref, kc_ref, vc_ref, bias_ref, o_ref):
    r = pl.program_id(1)
    rs = jnp.clip(r - NA_WIN_ROWS // 2, 0, rows - NA_WIN_ROWS)
    d0 = rs - r + NA_WIN_ROWS - 1
    band = NA_WIN_ROWS * GRID_W
    start = pl.multiple_of(rs * GRID_W, GRID_W)
    scale = HEAD_DIM ** -0.5
    for h in range(NA_HEADS):
        sl = slice(h * HEAD_DIM, (h + 1) * HEAD_DIM)
        qh = q_ref[0, :, sl]
        kb = k_ref[0, pl.ds(start, band), sl]
        vb = v_ref[0, pl.ds(start, band), sl]
        s_loc = _dot_nt(qh, kb) * scale + bias_ref[h, d0]
        s_ctx = _dot_nt(qh, kc_ref[0, :, sl]) * scale
        m = jnp.maximum(jnp.max(s_loc, axis=-1, keepdims=True),
                        jnp.max(s_ctx, axis=-1, keepdims=True))
        p_loc = jnp.exp(s_loc - m)
        p_ctx = jnp.exp(s_ctx - m)
        den = jnp.sum(p_loc, axis=-1, keepdims=True) + jnp.sum(p_ctx, axis=-1, keepdims=True)
        o = _dot(p_loc.astype(BF16), vb) + _dot(p_ctx.astype(BF16), vc_ref[0, :, sl])
        o_ref[0, :, sl] = (o / den).astype(o_ref.dtype)


def _attention(q, k, v, kc, vc, bias):
    b, s, w = q.shape
    rows = s // GRID_W
    c = kc.shape[1]
    full = lambda n: pl.BlockSpec((1, n, w), lambda i, j: (i, 0, 0))
    return pl.pallas_call(
        functools.partial(_attn_kernel, rows),
        grid=(b, rows),
        in_specs=[pl.BlockSpec((1, GRID_W, w), lambda i, j: (i, j, 0)),
                  full(s), full(s), full(c), full(c),
                  pl.BlockSpec(bias.shape, lambda i, j: (0, 0, 0, 0))],
        out_specs=pl.BlockSpec((1, GRID_W, w), lambda i, j: (i, j, 0)),
        out_shape=jax.ShapeDtypeStruct((b, s, w), BF16),
        compiler_params=_cparams(("arbitrary", "arbitrary")),
        name="nbr_attention",
    )(q, k, v, kc, vc, bias)


def _mix_kernel(x_ref, gu_ref, gv_ref, ga_ref, gb_ref, ya_ref, g1_ref, sh2_ref, sc2_ref,
                lng_ref, ws_ref, bs_ref, wpa_ref, wpb_ref, wout_ref, n2g_ref, wq_ref, keys_ref,
                x1_ref, h2_ref, st_ref):
    tm = x_ref.shape[1]
    gw = gu_ref.shape[2]
    u = jax.nn.gelu(gu_ref[0].astype(F32))
    t = jax.nn.gelu(gv_ref[0].astype(F32))
    tc = t - jnp.mean(t, axis=-1, keepdims=True)
    vn = tc * lax.rsqrt(jnp.mean(tc * tc, axis=-1, keepdims=True) + EPS) * lng_ref[...]
    vnb = vn.astype(BF16)
    lane = lax.broadcasted_iota(jnp.int32, (GM_CHUNK, LANES), 1)
    gd = gw // GM_GROUPS
    chunks = []
    for c in range(tm // GM_CHUNK):
        pairs = []
        for gp in range(gw // LANES):
            vp = vnb[c * GM_CHUNK:(c + 1) * GM_CHUNK, gp * LANES:(gp + 1) * LANES]
            r0 = _dot(ws_ref[2 * gp], vp)
            r1 = _dot(ws_ref[2 * gp + 1], vp)
            pairs.append(jnp.where(lane < gd, r0, r1))
        chunks.append(jnp.concatenate(pairs, axis=1) + bs_ref[...])
    mixed = jnp.concatenate(chunks, axis=0)
    yb = (u * mixed).astype(BF16)
    pa = _dot(ya_ref[0], wpa_ref[...])
    pb = _dot(yb, wpb_ref[...])
    m = jax.nn.sigmoid(ga_ref[0].astype(F32)) * pa + jax.nn.sigmoid(gb_ref[0].astype(F32)) * pb
    out = _dot(m.astype(BF16), wout_ref[...])
    x1 = x_ref[0] + g1_ref[0] * out
    x1_ref[0] = x1
    ms = jnp.mean(x1 * x1, axis=-1, keepdims=True)
    h2 = x1 * lax.rsqrt(ms + EPS) * n2g_ref[...]
    h2 = (h2 * (1.0 + sc2_ref[0]) + sh2_ref[0]).astype(BF16)
    half = h2.shape[1] // 2
    h2_ref[0] = _pack_bf16_pairs(h2[:, :half], h2[:, half:])
    qp = _dot(h2, wq_ref[...]).astype(BF16)
    for hp in range(keys_ref.shape[0]):
        st_ref[hp] = _dot_nt(keys_ref[hp], qp[:, hp * PEER_HALF:(hp + 1) * PEER_HALF])


def _mix(x, gu, gv, ga, gb, ya, g1, sh2, sc2, lng, ws, bs_full, wpa, wpb, wout, n2g, wq, keys, tm):
    b, s, d = x.shape
    gw = gu.shape[2]
    nt = s // tm
    nhp = keys.shape[0]
    act = lambda w: pl.BlockSpec((1, tm, w), lambda i, j: (i, j, 0))
    vec = pl.BlockSpec((1, 1, d), lambda i, j: (i, 0, 0))

    def const(a):
        nd = a.ndim
        return pl.BlockSpec(a.shape, lambda i, j: (0,) * nd)

    return pl.pallas_call(
        _mix_kernel,
        grid=(b, nt),
        in_specs=[act(d), act(gw), act(gw), act(d), act(d), act(gw), vec, vec, vec,
                  const(lng), const(ws), const(bs_full), const(wpa), const(wpb), const(wout),
                  const(n2g), const(wq), const(keys)],
        out_specs=[act(d), act(d // 2),
                   pl.BlockSpec((nhp, PEER_N_KEYS, tm), lambda i, j: (0, 0, i * nt + j))],
        out_shape=[jax.ShapeDtypeStruct((b, s, d), F32),
                   jax.ShapeDtypeStruct((b, s, d // 2), jnp.uint32),
                   jax.ShapeDtypeStruct((nhp, PEER_N_KEYS, b * s), F32)],
        compiler_params=_cparams(("arbitrary", "arbitrary")),
        name="mix_peer_scores",
    )(x, gu, gv, ga, gb, ya, g1, sh2, sc2, lng, ws, bs_full, wpa, wpb, wout, n2g, wq, keys)


def _extract_top(vals, order, payload, k):
    out_v, out_p = [], []
    for _ in range(k):
        m = jnp.max(vals, axis=0, keepdims=True)
        o = jnp.min(jnp.where(vals == m, order, jnp.inf), axis=0, keepdims=True)
        sel = order == o
        out_v.append(m)
        if payload is order:
            out_p.append(o)
        else:
            out_p.append(jnp.max(jnp.where(sel, payload, -1.0), axis=0, keepdims=True))
        vals = jnp.where(sel, -jnp.inf, vals)
    return jnp.concatenate(out_v, axis=0), jnp.concatenate(out_p, axis=0)


def _topk_kernel(st_ref, idx_ref, gate_ref, sv_ref, si_ref, se_ref, sg_ref):
    nhp = st_ref.shape[0]
    tm = st_ref.shape[2]
    k = PEER_TOPK
    kidx = lax.broadcasted_iota(jnp.int32, (PEER_N_KEYS, LANES), 0).astype(F32)
    sub = lax.broadcasted_iota(jnp.int32, (SUBLANES, LANES), 0).astype(F32)

    def lane_tile(lt, carry):
        lanes = pl.ds(pl.multiple_of(lt * LANES, LANES), LANES)

        def stage1(h, c):
            for hp in (2 * h, 2 * h + 1):
                v, i = _extract_top(st_ref[hp, :, lanes], kidx, kidx, k)
                sv_ref[hp] = v
                si_ref[hp] = i
            return c

        lax.fori_loop(0, nhp // 2, stage1, 0)

        def stage2(h2, c):
            for h in (2 * h2, 2 * h2 + 1):
                stage2_head(h)
            return c

        def stage2_head(h):
            s1, s2 = sv_ref[2 * h], sv_ref[2 * h + 1]
            i1, i2 = si_ref[2 * h] * float(PEER_N_KEYS), si_ref[2 * h + 1]
            cv, co, ce = [], [], []
            for half in range(2):
                b0 = half * SUBLANES
                cv.append(s1[0:1] + s2[b0:b0 + SUBLANES])
                co.append(sub + float(b0))
                ce.append(i1[0:1] + i2[b0:b0 + SUBLANES])
            for a in range(1, SUBLANES):
                cv.append(s1[a:a + 1] + s2[0:SUBLANES])
                co.append(sub + float(a * k))
                ce.append(i1[a:a + 1] + i2[0:SUBLANES])
            cv.append(s1[SUBLANES:k] + s2[0:1])
            co.append((sub + float(SUBLANES)) * float(k))
            ce.append(i1[SUBLANES:k] + i2[0:1])
            bv, be = _extract_top(jnp.concatenate(cv, axis=0), jnp.concatenate(co, axis=0),
                                  jnp.concatenate(ce, axis=0), k)
            ex = jnp.exp(bv - bv[0:1])
            rows = pl.ds(pl.multiple_of(h * k, k), k)
            sg_ref[rows, :] = ex / jnp.sum(ex, axis=0, keepdims=True)
            se_ref[rows, :] = be

        lax.fori_loop(0, nhp // 4, stage2, 0)

        rows = pl.ds(pl.multiple_of(lt * LANES, LANES), LANES)
        idx_ref[rows, :] = se_ref[...].T.astype(jnp.int32)
        gate_ref[rows, :] = sg_ref[...].T
        return carry

    lax.fori_loop(0, tm // LANES, lane_tile, 0)


def _topk(st, tm):
    nhp, nk, t = st.shape
    ne = (nhp // 2) * PEER_TOPK
    return pl.pallas_call(
        _topk_kernel,
        grid=(t // tm,),
        in_specs=[pl.BlockSpec((nhp, nk, tm), lambda i: (0, 0, i))],
        out_specs=[pl.BlockSpec((tm, ne), lambda i: (i, 0)),
                   pl.BlockSpec((tm, ne), lambda i: (i, 0))],
        out_shape=[jax.ShapeDtypeStruct((t, ne), jnp.int32),
                   jax.ShapeDtypeStruct((t, ne), F32)],
        scratch_shapes=[pltpu.VMEM((nhp, PEER_TOPK, LANES), F32),
                        pltpu.VMEM((nhp, PEER_TOPK, LANES), F32),
                        pltpu.VMEM((ne, LANES), F32),
                        pltpu.VMEM((ne, LANES), F32)],
        compiler_params=_cparams(("arbitrary",)),
        name="peer_topk",
    )(st)


SC_LANES = 16
SC_WINDOW = 32
SC_ROW_BUFFERS = 4
SC_TOKENS_PER_STEP = 8
SC_WSUM_CHUNKS = 8


def _pack_table(t):
    n, d = t.shape
    tb = t.astype(BF16)
    lo = lax.bitcast_convert_type(tb[:, :d // 2], jnp.uint16).astype(jnp.uint32)
    hi = lax.bitcast_convert_type(tb[:, d // 2:], jnp.uint16).astype(jnp.uint32)
    return lo | (hi << 16)


def _sc_bf16(words):
    return plsc.bitcast(words, BF16)


def _sc_peer(table, idx, operand, mode):
    t, ne = idx.shape
    hw = table.shape[1]
    d = 2 * hw
    L = SC_LANES
    win = SC_WINDOW
    nbuf = SC_ROW_BUFFERS
    ahead = nbuf - 1
    tps = SC_TOKENS_PER_STEP
    nq = SC_WSUM_CHUNKS
    info = plsc.get_sparse_core_info()
    nc, ns = info.num_cores, info.num_subcores
    nss = t // (nc * ns * tps)
    wps = tps * ne // win
    wpt = ne // win
    blk = tps * ne
    nwin = nss * wps
    assert nss * nc * ns * tps == t and nss % 2 == 0 and wps % nbuf == 0 and wps > ahead
    assert wpt * win == ne and win % L == 0 and hw % (nq * L) == 0
    dots = mode == "dots"
    assert dots or nbuf % wpt == 0
    mesh = plsc.VectorSubcoreMesh(core_axis_name="c", subcore_axis_name="s")
    op_buf = pltpu.VMEM((2, tps, hw), jnp.uint32) if dots else pltpu.VMEM((2 * blk,), jnp.int32)
    res_buf = pltpu.VMEM((2 * blk,), F32) if dots else pltpu.VMEM((2, tps, d), F32)
    out_type = jax.ShapeDtypeStruct((t * ne,), F32) if dots else jax.ShapeDtypeStruct((t, d), F32)

    @functools.partial(
        pl.kernel, mesh=mesh, out_type=out_type,
        scratch_types=[pltpu.VMEM((2 * blk,), jnp.int32), op_buf,
                       pltpu.VMEM((nbuf, win, hw), jnp.uint32), res_buf,
                       pltpu.SemaphoreType.DMA((2,)), pltpu.SemaphoreType.DMA((2,)),
                       pltpu.SemaphoreType.DMA((nbuf,)), pltpu.SemaphoreType.DMA((2,))],
        compiler_params=pltpu.CompilerParams(needs_layout_passes=False),
        name="peer_" + mode,
    )
    def sc_kernel(table_hbm, idx_hbm, op_hbm, out_hbm, idx_v, op_v, rows_v, res_v, isem, psem, gsem, osem):
        ss0 = (lax.axis_index("s") * nc + lax.axis_index("c")) * nss
        lane = lax.iota(jnp.int32, L)

        def flat(ref, n):
            return ref.at[pl.ds(pl.multiple_of(n * blk, blk), blk)]

        def tok_rows(ref, n):
            return ref.at[pl.ds(pl.multiple_of(n * tps, tps), tps)]

        def idx_copy(ss, slot):
            return pltpu.make_async_copy(flat(idx_hbm, ss0 + ss), flat(idx_v, slot), isem.at[slot])

        def op_copy(ss, slot):
            if dots:
                return pltpu.make_async_copy(tok_rows(op_hbm, ss0 + ss), op_v.at[slot], psem.at[slot])
            return pltpu.make_async_copy(flat(op_hbm, ss0 + ss), flat(op_v, slot), psem.at[slot])

        def out_copy(ss, slot):
            if dots:
                return pltpu.make_async_copy(flat(res_v, slot), flat(out_hbm, ss0 + ss), osem.at[slot])
            return pltpu.make_async_copy(res_v.at[slot], tok_rows(out_hbm, ss0 + ss), osem.at[slot])

        def gather(slot, hs, b):
            ix = idx_v.at[pl.ds(pl.multiple_of(slot * blk + hs * win, win), win)]
            return pltpu.make_async_copy(table_hbm.at[ix], rows_v.at[b], gsem.at[b])

        def reduce_dots(slot, hs, b):
            tok = hs // wpt

            def head(g, c):
                def chunk(jj, accs):
                    off = pl.multiple_of(jj * (2 * L), 2 * L)
                    xa = _sc_bf16(op_v[slot, tok, pl.ds(off, L)])
                    xb = _sc_bf16(op_v[slot, tok, pl.ds(off + L, L)])
                    new = []
                    for r in range(L):
                        p = (_sc_bf16(rows_v[b, g * L + r, pl.ds(off, L)]) * xa
                             + _sc_bf16(rows_v[b, g * L + r, pl.ds(off + L, L)]) * xb)
                        lo, hi = plsc.unpack(p, format=plsc.PackFormat.INTERLEAVED)
                        new.append(accs[r] + lo + hi)
                    return tuple(new)

                accs = lax.fori_loop(0, hw // (2 * L), chunk, tuple(jnp.zeros((L,), F32) for _ in range(L)))
                out = jnp.zeros((L,), F32)
                for r in range(L):
                    out = jnp.where(lane == r, jnp.sum(accs[r]), out)
                res_v[pl.ds(pl.multiple_of(slot * blk + hs * win + g * L, L), L)] = out
                return c

            lax.fori_loop(0, win // L, head, 0)

        def reduce_wsum(slot, hs, b):
            tok = hs // wpt
            wbase = slot * blk + hs * win
            first = b % wpt == 0

            def colgroup(cg, c):
                col = pl.multiple_of(cg * (nq * L), nq * L)
                if first:
                    accs = tuple(jnp.zeros((L,), F32) for _ in range(2 * nq))
                else:
                    accs = tuple(res_v[slot, tok, pl.ds(col + q * L, L)] for q in range(nq)) + \
                           tuple(res_v[slot, tok, pl.ds(hw + col + q * L, L)] for q in range(nq))

                def rowpair(rp, accs):
                    accs = list(accs)
                    r = rp * 2
                    wa = _sc_bf16(plsc.load_gather(op_v, [jnp.full((L,), wbase + r, jnp.int32)]))
                    wb = _sc_bf16(plsc.load_gather(op_v, [jnp.full((L,), wbase + r + 1, jnp.int32)]))
                    for q in range(nq):
                        p = (_sc_bf16(rows_v[b, r, pl.ds(col + q * L, L)]) * wa
                             + _sc_bf16(rows_v[b, r + 1, pl.ds(col + q * L, L)]) * wb)
                        lo, hi = plsc.unpack(p, format=plsc.PackFormat.INTERLEAVED)
                        accs[q] = accs[q] + lo
                        accs[nq + q] = accs[nq + q] + hi
                    return tuple(accs)

                accs = lax.fori_loop(0, win // 2, rowpair, accs)
                for q in range(nq):
                    res_v[slot, tok, pl.ds(col + q * L, L)] = accs[q]
                    res_v[slot, tok, pl.ds(hw + col + q * L, L)] = accs[nq + q]
                return c

            lax.fori_loop(0, hw // (nq * L), colgroup, 0)

        idx_copy(0, 0).start()
        op_copy(0, 0).start()
        idx_copy(1, 1).start()
        op_copy(1, 1).start()
        idx_copy(0, 0).wait()
        op_copy(0, 0).wait()
        for a in range(ahead):
            gather(0, a, a).start()

        @pl.loop(0, nwin, step=nbuf)
        def _(i0):
            for b in range(nbuf):
                i = i0 + b
                ss = i // wps
                hs = i % wps
                slot = ss % 2
                gather(slot, hs, b).wait()
                nb = (b + ahead) % nbuf

                @pl.when(hs + ahead < wps)
                def _():
                    gather(slot, hs + ahead, nb).start()

                @pl.when(jnp.logical_and(hs + ahead >= wps, ss + 1 < nss))
                def _():
                    @pl.when(hs + ahead == wps)
                    def _():
                        idx_copy(ss + 1, 1 - slot).wait()
                        op_copy(ss + 1, 1 - slot).wait()

                    gather(1 - slot, hs + ahead - wps, nb).start()

                @pl.when(jnp.logical_and(hs == 0, ss >= 2))
                def _():
                    out_copy(ss - 2, slot).wait()

                if dots:
                    reduce_dots(slot, hs, b)
                else:
                    reduce_wsum(slot, hs, b)

                @pl.when(hs + 1 == wps)
                def _():
                    out_copy(ss, slot).start()

                    @pl.when(ss + 2 < nss)
                    def _():
                        idx_copy(ss + 2, slot).start()
                        op_copy(ss + 2, slot).start()

        out_copy(nss - 2, 0).wait()
        out_copy(nss - 1, 1).wait()

    op = operand if dots else operand.reshape(t * ne)
    return sc_kernel(table, idx.reshape(t * ne), op)


def _peer_weight_kernel(act_ref, gate_ref, w_ref):
    w = (jax.nn.gelu(act_ref[...]) * gate_ref[...]).astype(BF16)
    w_ref[...] = pltpu.bitcast(_pack_bf16_pairs(w, w), jnp.int32)


def _peer_weight(act, gate, tm):
    t, ne = act.shape
    tm = min(tm, t)
    assert t % tm == 0
    spec = pl.BlockSpec((tm, ne), lambda i: (i, 0))
    return pl.pallas_call(
        _peer_weight_kernel, grid=(t // tm,), in_specs=[spec, spec], out_specs=spec,
        out_shape=jax.ShapeDtypeStruct((t, ne), jnp.int32),
        compiler_params=_cparams(("arbitrary",)), name="peer_weight",
    )(act, gate)


def _final_kernel(x1_ref, p_ref, g2_ref, fg_ref, o_ref):
    x2 = x1_ref[0] + g2_ref[0] * p_ref[0]
    ms = jnp.mean(x2 * x2, axis=-1, keepdims=True)
    o_ref[0] = x2 * lax.rsqrt(ms + EPS) * fg_ref[...]


def _final(x1, peer, g2, fg, tm):
    b, s, d = x1.shape
    act = pl.BlockSpec((1, tm, d), lambda i, j: (i, j, 0))
    return pl.pallas_call(
        _final_kernel, grid=(b, s // tm),
        in_specs=[act, act, pl.BlockSpec((1, 1, d), lambda i, j: (i, 0, 0)),
                  pl.BlockSpec((1, d), lambda i, j: (0, 0))],
        out_specs=act,
        out_shape=jax.ShapeDtypeStruct((b, s, d), F32),
        compiler_params=_cparams(("arbitrary", "arbitrary")), name="final_norm",
    )(x1, peer, g2, fg)


def kernel(x, c, ctx, c_ctx, ada_w, ada_b, norm1_g, norm2_g, w_in, na_rpb, gm_ln_g, gm_ws, gm_bs,
           w_proj_a, w_proj_b, w_out, peer_wq, peer_keys, peer_u, peer_v, final_g):
    b, s, d = x.shape
    naw = NA_HEADS * HEAD_DIM
    gw = gm_ln_g.shape[1]
    layer = 0

    pad = (-(b + 1)) % SUBLANES
    c_all = jnp.concatenate([c, c_ctx[None, :], jnp.zeros((pad, d), F32)], axis=0)
    mod = _adaln(c_all, ada_w[layer], ada_b[layer])
    sh1, sc1, g1, sh2, sc2, g2 = [mod[:b, i * d:(i + 1) * d].reshape(b, 1, d) for i in range(ADA_CHUNKS)]
    csh1 = jnp.broadcast_to(mod[b, 0:d].reshape(1, 1, d), (b, 1, d))
    csc1 = jnp.broadcast_to(mod[b, d:2 * d].reshape(1, 1, d), (b, 1, d))

    w = w_in[layer].astype(BF16)
    w_kv = w[:, naw:3 * naw]
    n1g = norm1_g[layer].reshape(1, d)
    bias = _bias_table(na_rpb[layer])
    bs_full = jnp.repeat(gm_bs[layer].T, gw // GM_GROUPS, axis=1)
    keys = peer_keys[layer].reshape(2 * PEER_HEADS, PEER_N_KEYS, PEER_HALF).astype(BF16)
    lng = gm_ln_g[layer].reshape(1, gw)
    ws = gm_ws[layer].astype(BF16)
    wpa, wpb, wout = w_proj_a[layer].astype(BF16), w_proj_b[layer].astype(BF16), w_out[layer].astype(BF16)
    n2g = norm2_g[layer].reshape(1, d)
    wq = peer_wq[layer].astype(BF16)
    tab_u, tab_v = _pack_table(peer_u[layer]), _pack_table(peer_v[layer])
    fg = final_g.reshape(1, d)
    ne = PEER_HEADS * PEER_TOPK

    assert sum(BATCH_CHUNKS) == b
    outs = []
    peers = []
    b0 = 0
    for bc in BATCH_CHUNKS:
        sl = slice(b0, b0 + bc)
        b0 += bc
        t = bc * s
        xc = x[sl]
        q, k, v, gu, gv, ga, gb = _norm_proj(xc, n1g, sh1[sl], sc1[sl], w, (naw, naw, naw, gw, gw, d, d), 512)
        k_c, v_c = _norm_proj(ctx[sl], n1g, csh1[sl], csc1[sl], w_kv, (naw, naw), ctx.shape[1])
        y_a = _attention(q, k, v, k_c, v_c, bias)
        x1, h2, st = _mix(xc, gu, gv, ga, gb, y_a, g1[sl], sh2[sl], sc2[sl], lng, ws, bs_full,
                          wpa, wpb, wout, n2g, wq, keys, 256)
        idx, gate = _topk(st, 512)
        if peers:
            idx, _ = lax.optimization_barrier((idx, peers[-1]))
        act = _sc_peer(tab_u, idx, h2.reshape(t, d // 2), "dots")
        wgt = _peer_weight(act.reshape(t, ne), gate, 2048)
        peer = _sc_peer(tab_v, idx, wgt, "wsum")
        peers.append(peer)
        outs.append(_final(x1, peer.reshape(bc, s, d), g2[sl], fg, 512))
    return jnp.concatenate(outs, axis=0)
```

```python
import functools

import jax
import jax.numpy as jnp
from jax import lax
from jax.experimental import pallas as pl
from jax.experimental.pallas import tpu as pltpu
from jax.experimental.pallas import tpu_sc as plsc

F32 = jnp.float32
BF16 = jnp.bfloat16

GRID_W = 64
NA_HEADS = 8
HEAD_DIM = 64
NA_WIN_ROWS = 8
NA_WIN_COLS = 16
GM_GROUPS = 8
GM_CHUNK = 128
PEER_HEADS = 8
PEER_N_KEYS = 128
PEER_TOPK = 16
PEER_HALF = 128
ADA_CHUNKS = 6
EPS = 1e-6
NEG_INF = -1e30

LANES = 128
SUBLANES = 8
VMEM_LIMIT = 56 * 1024 * 1024
BATCH_CHUNKS = (1, 1, 2, 3, 4, 5)


def _dot(a, b):
    return lax.dot_general(a, b, (((1,), (0,)), ((), ())), preferred_element_type=F32)


def _dot_nt(a, b):
    return lax.dot_general(a, b, (((1,), (1,)), ((), ())), preferred_element_type=F32)


def _cparams(sem):
    return pltpu.CompilerParams(dimension_semantics=sem, vmem_limit_bytes=VMEM_LIMIT)


def _pack_bf16_pairs(lo, hi):
    lo_bits = pltpu.bitcast(lo.astype(F32), jnp.uint32) >> 16
    hi_bits = pltpu.bitcast(hi.astype(F32), jnp.uint32) & jnp.uint32(0xFFFF0000)
    return lo_bits | hi_bits


def _adaln_kernel(c_ref, w_ref, b_ref, o_ref):
    c = c_ref[...]
    s = c * jax.nn.sigmoid(c)
    o_ref[...] = lax.dot_general(s, w_ref[...], (((1,), (0,)), ((), ())),
                                 precision=lax.Precision.HIGHEST,
                                 preferred_element_type=F32) + b_ref[...]


def _adaln(c_all, w, b):
    m, d = c_all.shape
    n = w.shape[1]
    tn = 1024
    return pl.pallas_call(
        _adaln_kernel,
        grid=(n // tn,),
        in_specs=[pl.BlockSpec((m, d), lambda j: (0, 0)),
                  pl.BlockSpec((d, tn), lambda j: (0, j)),
                  pl.BlockSpec((1, tn), lambda j: (0, j))],
        out_specs=pl.BlockSpec((m, tn), lambda j: (0, j)),
        out_shape=jax.ShapeDtypeStruct((m, n), F32),
        compiler_params=_cparams(("arbitrary",)),
        name="adaln",
    )(c_all, w, b.reshape(1, n))


def _norm_proj_kernel(widths, x_ref, g_ref, sh_ref, sc_ref, w_ref, *o_refs):
    x = x_ref[0]
    ms = jnp.mean(x * x, axis=-1, keepdims=True)
    y = x * lax.rsqrt(ms + EPS) * g_ref[...]
    h = (y * (1.0 + sc_ref[0]) + sh_ref[0]).astype(BF16)
    off = 0
    for o_ref, wd in zip(o_refs, widths):
        o_ref[0] = _dot(h, w_ref[:, off:off + wd]).astype(o_ref.dtype)
        off += wd


def _norm_proj(x, gain, shift, scale, w, widths, tm):
    b, s, d = x.shape
    n = w.shape[1]
    assert sum(widths) == n and s % tm == 0
    vec = pl.BlockSpec((1, 1, d), lambda i, j: (i, 0, 0))
    return pl.pallas_call(
        functools.partial(_norm_proj_kernel, widths),
        grid=(b, s // tm),
        in_specs=[pl.BlockSpec((1, tm, d), lambda i, j: (i, j, 0)),
                  pl.BlockSpec((1, d), lambda i, j: (0, 0)),
                  vec, vec,
                  pl.BlockSpec((d, n), lambda i, j: (0, 0))],
        out_specs=[pl.BlockSpec((1, tm, wd), lambda i, j: (i, j, 0)) for wd in widths],
        out_shape=[jax.ShapeDtypeStruct((b, s, wd), BF16) for wd in widths],
        compiler_params=_cparams(("arbitrary", "arbitrary")),
        name="norm_proj",
    )(x, gain, shift, scale, w)


def _bias_table_kernel(rpb_ref, o_ref):
    h = pl.program_id(0)
    d0 = pl.program_id(1)
    q = lax.broadcasted_iota(jnp.int32, (GRID_W, GRID_W), 0)
    kc = lax.broadcasted_iota(jnp.int32, (GRID_W, GRID_W), 1)
    dc = jnp.clip(kc - q + NA_WIN_COLS - 1, 0, 2 * NA_WIN_COLS - 2)
    cs = jnp.clip(q - NA_WIN_COLS // 2, 0, GRID_W - NA_WIN_COLS)
    col_in = (kc >= cs) & (kc < cs + NA_WIN_COLS)
    n_dc = 2 * NA_WIN_COLS - 1
    n_dr = 2 * NA_WIN_ROWS - 1
    for j in range(NA_WIN_ROWS):
        row = h * n_dr + d0 + j
        t = jnp.zeros((GRID_W, GRID_W), F32)
        for c in range(n_dc):
            t = jnp.where(dc == c, rpb_ref[row, c], t)
        o_ref[0, 0, :, j * GRID_W:(j + 1) * GRID_W] = jnp.where(col_in, t, NEG_INF)


def _bias_table(rpb):
    nh, n_dr, n_dc = rpb.shape
    band = NA_WIN_ROWS * GRID_W
    return pl.pallas_call(
        _bias_table_kernel,
        grid=(nh, NA_WIN_ROWS),
        in_specs=[pl.BlockSpec(memory_space=pltpu.SMEM)],
        out_specs=pl.BlockSpec((1, 1, GRID_W, band), lambda h, d: (h, d, 0, 0)),
        out_shape=jax.ShapeDtypeStruct((nh, NA_WIN_ROWS, GRID_W, band), F32),
        compiler_params=_cparams(("arbitrary", "arbitrary")),
        name="bias_table",
    )(rpb.reshape(nh * n_dr, n_dc))


def _attn_kernel(rows, q_ref, k_ref, v_ref, kc_ref, vc_ref, bias_ref, o_ref):
    r = pl.program_id(1)
    rs = jnp.clip(r - NA_WIN_ROWS // 2, 0, rows - NA_WIN_ROWS)
    d0 = rs - r + NA_WIN_ROWS - 1
    band = NA_WIN_ROWS * GRID_W
    start = pl.multiple_of(rs * GRID_W, GRID_W)
    scale = HEAD_DIM ** -0.5
    for h in range(NA_HEADS):
        sl = slice(h * HEAD_DIM, (h + 1) * HEAD_DIM)
        qh = q_ref[0, :, sl]
        kb = k_ref[0, pl.ds(start, band), sl]
        vb = v_ref[0, pl.ds(start, band), sl]
        s_loc = _dot_nt(qh, kb) * scale + bias_ref[h, d0]
        s_ctx = _dot_nt(qh, kc_ref[0, :, sl]) * scale
        m = jnp.maximum(jnp.max(s_loc, axis=-1, keepdims=True),
                        jnp.max(s_ctx, axis=-1, keepdims=True))
        p_loc = jnp.exp(s_loc - m)
        p_ctx = jnp.exp(s_ctx - m)
        den = jnp.sum(p_loc, axis=-1, keepdims=True) + jnp.sum(p_ctx, axis=-1, keepdims=True)
        o = _dot(p_loc.astype(BF16), vb) + _dot(p_ctx.astype(BF16), vc_ref[0, :, sl])
        o_ref[0, :, sl] = (o / den).astype(o_ref.dtype)


def _attention(q, k, v, kc, vc, bias):
    b, s, w = q.shape
    rows = s // GRID_W
    c = kc.shape[1]
    full = lambda n: pl.BlockSpec((1, n, w), lambda i, j: (i, 0, 0))
    return pl.pallas_call(
        functools.partial(_attn_kernel, rows),
        grid=(b, rows),
        in_specs=[pl.BlockSpec((1, GRID_W, w), lambda i, j: (i, j, 0)),
                  full(s), full(s), full(c), full(c),
                  pl.BlockSpec(bias.shape, lambda i, j: (0, 0, 0, 0))],
        out_specs=pl.BlockSpec((1, GRID_W, w), lambda i, j: (i, j, 0)),
        out_shape=jax.ShapeDtypeStruct((b, s, w), BF16),
        compiler_params=_cparams(("arbitrary", "arbitrary")),
        name="nbr_attention",
    )(q, k, v, kc, vc, bias)


def _mix_kernel(x_ref, gu_ref, gv_ref, ga_ref, gb_ref, ya_ref, g1_ref, sh2_ref, sc2_ref,
                lng_ref, ws_ref, bs_ref, wpa_ref, wpb_ref, wout_ref, n2g_ref, wq_ref, keys_ref,
                x1_ref, h2_ref, st_ref):
    tm = x_ref.shape[1]
    gw = gu_ref.shape[2]
    u = jax.nn.gelu(gu_ref[0].astype(F32))
    t = jax.nn.gelu(gv_ref[0].astype(F32))
    tc = t - jnp.mean(t, axis=-1, keepdims=True)
    vn = tc * lax.rsqrt(jnp.mean(tc * tc, axis=-1, keepdims=True) + EPS) * lng_ref[...]
    vnb = vn.astype(BF16)
    lane = lax.broadcasted_iota(jnp.int32, (GM_CHUNK, LANES), 1)
    gd = gw // GM_GROUPS
    chunks = []
    for c in range(tm // GM_CHUNK):
        pairs = []
        for gp in range(gw // LANES):
            vp = vnb[c * GM_CHUNK:(c + 1) * GM_CHUNK, gp * LANES:(gp + 1) * LANES]
            r0 = _dot(ws_ref[2 * gp], vp)
            r1 = _dot(ws_ref[2 * gp + 1], vp)
            pairs.append(jnp.where(lane < gd, r0, r1))
        chunks.append(jnp.concatenate(pairs, axis=1) + bs_ref[...])
    mixed = jnp.concatenate(chunks, axis=0)
    yb = (u * mixed).astype(BF16)
    pa = _dot(ya_ref[0], wpa_ref[...])
    pb = _dot(yb, wpb_ref[...])
    m = jax.nn.sigmoid(ga_ref[0].astype(F32)) * pa + jax.nn.sigmoid(gb_ref[0].astype(F32)) * pb
    out = _dot(m.astype(BF16), wout_ref[...])
    x1 = x_ref[0] + g1_ref[0] * out
    x1_ref[0] = x1
    ms = jnp.mean(x1 * x1, axis=-1, keepdims=True)
    h2 = x1 * lax.rsqrt(ms + EPS) * n2g_ref[...]
    h2 = (h2 * (1.0 + sc2_ref[0]) + sh2_ref[0]).astype(BF16)
    half = h2.shape[1] // 2
    h2_ref[0] = _pack_bf16_pairs(h2[:, :half], h2[:, half:])
    qp = _dot(h2, wq_ref[...]).astype(BF16)
    for hp in range(keys_ref.shape[0]):
        st_ref[hp] = _dot_nt(keys_ref[hp], qp[:, hp * PEER_HALF:(hp + 1) * PEER_HALF])


def _mix(x, gu, gv, ga, gb, ya, g1, sh2, sc2, lng, ws, bs_full, wpa, wpb, wout, n2g, wq, keys, tm):
    b, s, d = x.shape
    gw = gu.shape[2]
    nt = s // tm
    nhp = keys.shape[0]
    act = lambda w: pl.BlockSpec((1, tm, w), lambda i, j: (i, j, 0))
    vec = pl.BlockSpec((1, 1, d), lambda i, j: (i, 0, 0))

    def const(a):
        nd = a.ndim
        return pl.BlockSpec(a.shape, lambda i, j: (0,) * nd)

    return pl.pallas_call(
        _mix_kernel,
        grid=(b, nt),
        in_specs=[act(d), act(gw), act(gw), act(d), act(d), act(gw), vec, vec, vec,
                  const(lng), const(ws), const(bs_full), const(wpa), const(wpb), const(wout),
                  const(n2g), const(wq), const(keys)],
        out_specs=[act(d), act(d // 2),
                   pl.BlockSpec((nhp, PEER_N_KEYS, tm), lambda i, j: (0, 0, i * nt + j))],
        out_shape=[jax.ShapeDtypeStruct((b, s, d), F32),
                   jax.ShapeDtypeStruct((b, s, d // 2), jnp.uint32),
                   jax.ShapeDtypeStruct((nhp, PEER_N_KEYS, b * s), F32)],
        compiler_params=_cparams(("arbitrary", "arbitrary")),
        name="mix_peer_scores",
    )(x, gu, gv, ga, gb, ya, g1, sh2, sc2, lng, ws, bs_full, wpa, wpb, wout, n2g, wq, keys)


def _extract_top(vals, order, payload, k):
    out_v, out_p = [], []
    for _ in range(k):
        m = jnp.max(vals, axis=0, keepdims=True)
        o = jnp.min(jnp.where(vals == m, order, jnp.inf), axis=0, keepdims=True)
        sel = order == o
        out_v.append(m)
        if payload is order:
            out_p.append(o)
        else:
            out_p.append(jnp.max(jnp.where(sel, payload, -1.0), axis=0, keepdims=True))
        vals = jnp.where(sel, -jnp.inf, vals)
    return jnp.concatenate(out_v, axis=0), jnp.concatenate(out_p, axis=0)


def _topk_kernel(st_ref, idx_ref, gate_ref, sv_ref, si_ref, se_ref, sg_ref):
    nhp = st_ref.shape[0]
    tm = st_ref.shape[2]
    k = PEER_TOPK
    kidx = lax.broadcasted_iota(jnp.int32, (PEER_N_KEYS, LANES), 0).astype(F32)
    sub = lax.broadcasted_iota(jnp.int32, (SUBLANES, LANES), 0).astype(F32)

    def lane_tile(lt, carry):
        lanes = pl.ds(pl.multiple_of(lt * LANES, LANES), LANES)

        def stage1(h, c):
            for hp in (2 * h, 2 * h + 1):
                v, i = _extract_top(st_ref[hp, :, lanes], kidx, kidx, k)
                sv_ref[hp] = v
                si_ref[hp] = i
            return c

        lax.fori_loop(0, nhp // 2, stage1, 0)

        def stage2(h2, c):
            for h in (2 * h2, 2 * h2 + 1):
                stage2_head(h)
            return c

        def stage2_head(h):
            s1, s2 = sv_ref[2 * h], sv_ref[2 * h + 1]
            i1, i2 = si_ref[2 * h] * float(PEER_N_KEYS), si_ref[2 * h + 1]
            cv, co, ce = [], [], []
            for half in range(2):
                b0 = half * SUBLANES
                cv.append(s1[0:1] + s2[b0:b0 + SUBLANES])
                co.append(sub + float(b0))
                ce.append(i1[0:1] + i2[b0:b0 + SUBLANES])
            for a in range(1, SUBLANES):
                cv.append(s1[a:a + 1] + s2[0:SUBLANES])
                co.append(sub + float(a * k))
                ce.append(i1[a:a + 1] + i2[0:SUBLANES])
            cv.append(s1[SUBLANES:k] + s2[0:1])
            co.append((sub + float(SUBLANES)) * float(k))
            ce.append(i1[SUBLANES:k] + i2[0:1])
            bv, be = _extract_top(jnp.concatenate(cv, axis=0), jnp.concatenate(co, axis=0),
                                  jnp.concatenate(ce, axis=0), k)
            ex = jnp.exp(bv - bv[0:1])
            rows = pl.ds(pl.multiple_of(h * k, k), k)
            sg_ref[rows, :] = ex / jnp.sum(ex, axis=0, keepdims=True)
            se_ref[rows, :] = be

        lax.fori_loop(0, nhp // 4, stage2, 0)

        rows = pl.ds(pl.multiple_of(lt * LANES, LANES), LANES)
        idx_ref[rows, :] = se_ref[...].T.astype(jnp.int32)
        gate_ref[rows, :] = sg_ref[...].T
        return carry

    lax.fori_loop(0, tm // LANES, lane_tile, 0)


def _topk(st, tm):
    nhp, nk, t = st.shape
    ne = (nhp // 2) * PEER_TOPK
    return pl.pallas_call(
        _topk_kernel,
        grid=(t // tm,),
        in_specs=[pl.BlockSpec((nhp, nk, tm), lambda i: (0, 0, i))],
        out_specs=[pl.BlockSpec((tm, ne), lambda i: (i, 0)),
                   pl.BlockSpec((tm, ne), lambda i: (i, 0))],
        out_shape=[jax.ShapeDtypeStruct((t, ne), jnp.int32),
                   jax.ShapeDtypeStruct((t, ne), F32)],
        scratch_shapes=[pltpu.VMEM((nhp, PEER_TOPK, LANES), F32),
                        pltpu.VMEM((nhp, PEER_TOPK, LANES), F32),
                        pltpu.VMEM((ne, LANES), F32),
                        pltpu.VMEM((ne, LANES), F32)],
        compiler_params=_cparams(("arbitrary",)),
        name="peer_topk",
    )(st)


SC_LANES = 16
SC_WINDOW = 32
SC_ROW_BUFFERS = 4
SC_TOKENS_PER_STEP = 8
SC_WSUM_CHUNKS = 8


def _pack_table(t):
    n, d = t.shape
    tb = t.astype(BF16)
    lo = lax.bitcast_convert_type(tb[:, :d // 2], jnp.uint16).astype(jnp.uint32)
    hi = lax.bitcast_convert_type(tb[:, d // 2:], jnp.uint16).astype(jnp.uint32)
    return lo | (hi << 16)


def _sc_bf16(words):
    return plsc.bitcast(words, BF16)


def _sc_phase(table, idx, operand, mode):
    t, ne = idx.shape
    hw = table.shape[1]
    d = 2 * hw
    L = SC_LANES
    win = SC_WINDOW
    nbuf = SC_ROW_BUFFERS
    ahead = nbuf - 1
    tps = SC_TOKENS_PER_STEP
    nq = SC_WSUM_CHUNKS
    info = plsc.get_sparse_core_info()
    nc, ns = info.num_cores, info.num_subcores
    nss = t // (nc * ns * tps)
    wps = tps * ne // win
    wpt = ne // win
    blk = tps * ne
    nwin = nss * wps
    assert nss * nc * ns * tps == t and nss % 2 == 0 and wps % nbuf == 0 and wps > ahead
    assert wpt * win == ne and win % L == 0 and hw % (nq * L) == 0
    dots = mode == "dots"
    assert dots or nbuf % wpt == 0
    op_buf = pltpu.VMEM((2, tps, hw), jnp.uint32) if dots else pltpu.VMEM((2 * blk,), jnp.int32)
    res_buf = pltpu.VMEM((2 * blk,), F32) if dots else pltpu.VMEM((2, tps, d), F32)
    out_type = jax.ShapeDtypeStruct((t * ne,), F32) if dots else jax.ShapeDtypeStruct((t, d), F32)
    scratch = [("idx", pltpu.VMEM((2 * blk,), jnp.int32)), ("op_" + mode, op_buf),
               ("rows", pltpu.VMEM((nbuf, win, hw), jnp.uint32)), ("res_" + mode, res_buf),
               ("isem", pltpu.SemaphoreType.DMA((2,))), ("psem", pltpu.SemaphoreType.DMA((2,))),
               ("gsem", pltpu.SemaphoreType.DMA((nbuf,))), ("osem", pltpu.SemaphoreType.DMA((2,)))]

    def sc_kernel(table_hbm, idx_hbm, op_hbm, out_hbm, idx_v, op_v, rows_v, res_v, isem, psem, gsem, osem):
        ss0 = (lax.axis_index("s") * nc + lax.axis_index("c")) * nss
        lane = lax.iota(jnp.int32, L)

        def flat(ref, n):
            return ref.at[pl.ds(pl.multiple_of(n * blk, blk), blk)]

        def tok_rows(ref, n):
            return ref.at[pl.ds(pl.multiple_of(n * tps, tps), tps)]

        def idx_copy(ss, slot):
            return pltpu.make_async_copy(flat(idx_hbm, ss0 + ss), flat(idx_v, slot), isem.at[slot])

        def op_copy(ss, slot):
            if dots:
                return pltpu.make_async_copy(tok_rows(op_hbm, ss0 + ss), op_v.at[slot], psem.at[slot])
            return pltpu.make_async_copy(flat(op_hbm, ss0 + ss), flat(op_v, slot), psem.at[slot])

        def out_copy(ss, slot):
            if dots:
                return pltpu.make_async_copy(flat(res_v, slot), flat(out_hbm, ss0 + ss), osem.at[slot])
            return pltpu.make_async_copy(res_v.at[slot], tok_rows(out_hbm, ss0 + ss), osem.at[slot])

        def gather(slot, hs, b):
            ix = idx_v.at[pl.ds(pl.multiple_of(slot * blk + hs * win, win), win)]
            return pltpu.make_async_copy(table_hbm.at[ix], rows_v.at[b], gsem.at[b])

        def reduce_dots(slot, hs, b):
            tok = hs // wpt

            def head(g, c):
                def chunk(jj, accs):
                    off = pl.multiple_of(jj * (2 * L), 2 * L)
                    xa = _sc_bf16(op_v[slot, tok, pl.ds(off, L)])
                    xb = _sc_bf16(op_v[slot, tok, pl.ds(off + L, L)])
                    new = []
                    for r in range(L):
                        p = (_sc_bf16(rows_v[b, g * L + r, pl.ds(off, L)]) * xa
                             + _sc_bf16(rows_v[b, g * L + r, pl.ds(off + L, L)]) * xb)
                        lo, hi = plsc.unpack(p, format=plsc.PackFormat.INTERLEAVED)
                        new.append(accs[r] + lo + hi)
                    return tuple(new)

                accs = lax.fori_loop(0, hw // (2 * L), chunk, tuple(jnp.zeros((L,), F32) for _ in range(L)))
                out = jnp.zeros((L,), F32)
                for r in range(L):
                    out = jnp.where(lane == r, jnp.sum(accs[r]), out)
                res_v[pl.ds(pl.multiple_of(slot * blk + hs * win + g * L, L), L)] = out
                return c

            lax.fori_loop(0, win // L, head, 0)

        def reduce_wsum(slot, hs, b):
            tok = hs // wpt
            wbase = slot * blk + hs * win
            first = b % wpt == 0

            def colgroup(cg, c):
                col = pl.multiple_of(cg * (nq * L), nq * L)
                if first:
                    accs = tuple(jnp.zeros((L,), F32) for _ in range(2 * nq))
                else:
                    accs = tuple(res_v[slot, tok, pl.ds(col + q * L, L)] for q in range(nq)) + \
                           tuple(res_v[slot, tok, pl.ds(hw + col + q * L, L)] for q in range(nq))

                def rowpair(rp, accs):
                    accs = list(accs)
                    r = rp * 2
                    wa = _sc_bf16(plsc.load_gather(op_v, [jnp.full((L,), wbase + r, jnp.int32)]))
                    wb = _sc_bf16(plsc.load_gather(op_v, [jnp.full((L,), wbase + r + 1, jnp.int32)]))
                    for q in range(nq):
                        p = (_sc_bf16(rows_v[b, r, pl.ds(col + q * L, L)]) * wa
                             + _sc_bf16(rows_v[b, r + 1, pl.ds(col + q * L, L)]) * wb)
                        lo, hi = plsc.unpack(p, format=plsc.PackFormat.INTERLEAVED)
                        accs[q] = accs[q] + lo
                        accs[nq + q] = accs[nq + q] + hi
                    return tuple(accs)

                accs = lax.fori_loop(0, win // 2, rowpair, accs)
                for q in range(nq):
                    res_v[slot, tok, pl.ds(col + q * L, L)] = accs[q]
                    res_v[slot, tok, pl.ds(hw + col + q * L, L)] = accs[nq + q]
                return c

            lax.fori_loop(0, hw // (nq * L), colgroup, 0)

        idx_copy(0, 0).start()
        op_copy(0, 0).start()
        idx_copy(1, 1).start()
        op_copy(1, 1).start()
        idx_copy(0, 0).wait()
        op_copy(0, 0).wait()
        for a in range(ahead):
            gather(0, a, a).start()

        @pl.loop(0, nwin, step=nbuf)
        def _(i0):
            for b in range(nbuf):
                i = i0 + b
                ss = i // wps
                hs = i % wps
                slot = ss % 2
                gather(slot, hs, b).wait()
                nb = (b + ahead) % nbuf

                @pl.when(hs + ahead < wps)
                def _():
                    gather(slot, hs + ahead, nb).start()

                @pl.when(jnp.logical_and(hs + ahead >= wps, ss + 1 < nss))
                def _():
                    @pl.when(hs + ahead == wps)
                    def _():
                        idx_copy(ss + 1, 1 - slot).wait()
                        op_copy(ss + 1, 1 - slot).wait()

                    gather(1 - slot, hs + ahead - wps, nb).start()

                @pl.when(jnp.logical_and(hs == 0, ss >= 2))
                def _():
                    out_copy(ss - 2, slot).wait()

                if dots:
                    reduce_dots(slot, hs, b)
                else:
                    reduce_wsum(slot, hs, b)

                @pl.when(hs + 1 == wps)
                def _():
                    out_copy(ss, slot).start()

                    @pl.when(ss + 2 < nss)
                    def _():
                        idx_copy(ss + 2, slot).start()
                        op_copy(ss + 2, slot).start()

        out_copy(nss - 2, 0).wait()
        out_copy(nss - 1, 1).wait()

    op = operand if dots else operand.reshape(t * ne)
    return sc_kernel, scratch, out_type, (table, idx.reshape(t * ne), op)


def _sc_call(*phases):
    parts = [_sc_phase(table, idx, operand, mode) for mode, table, idx, operand in phases]
    names, types = [], []
    for _, scratch, _, _ in parts:
        for name, ty in scratch:
            if name not in names:
                names.append(name)
                types.append(ty)
    n = len(parts)

    def fused(*refs):
        ins, outs, scr = refs[:3 * n], refs[3 * n:4 * n], dict(zip(names, refs[4 * n:]))
        for i, (body, scratch, _, _) in enumerate(parts):
            body(*ins[3 * i:3 * i + 3], outs[i], *[scr[name] for name, _ in scratch])

    call = pl.kernel(
        fused, mesh=plsc.VectorSubcoreMesh(core_axis_name="c", subcore_axis_name="s"),
        out_type=tuple(p[2] for p in parts), scratch_types=types,
        compiler_params=pltpu.CompilerParams(needs_layout_passes=False),
        name="peer_" + "_".join(m for m, _, _, _ in phases),
    )
    return call(*[a for p in parts for a in p[3]])


def _peer_weight_kernel(act_ref, gate_ref, w_ref):
    w = (jax.nn.gelu(act_ref[...]) * gate_ref[...]).astype(BF16)
    w_ref[...] = pltpu.bitcast(_pack_bf16_pairs(w, w), jnp.int32)


def _peer_weight(act, gate, tm):
    t, ne = act.shape
    tm = min(tm, t)
    assert t % tm == 0
    spec = pl.BlockSpec((tm, ne), lambda i: (i, 0))
    return pl.pallas_call(
        _peer_weight_kernel, grid=(t // tm,), in_specs=[spec, spec], out_specs=spec,
        out_shape=jax.ShapeDtypeStruct((t, ne), jnp.int32),
        compiler_params=_cparams(("arbitrary",)), name="peer_weight",
    )(act, gate)


def _final_kernel(x1_ref, p_ref, g2_ref, fg_ref, o_ref):
    x2 = x1_ref[0] + g2_ref[0] * p_ref[0]
    ms = jnp.mean(x2 * x2, axis=-1, keepdims=True)
    o_ref[0] = x2 * lax.rsqrt(ms + EPS) * fg_ref[...]


def _final(x1, peer, g2, fg, tm):
    b, s, d = x1.shape
    act = pl.BlockSpec((1, tm, d), lambda i, j: (i, j, 0))
    return pl.pallas_call(
        _final_kernel, grid=(b, s // tm),
        in_specs=[act, act, pl.BlockSpec((1, 1, d), lambda i, j: (i, 0, 0)),
                  pl.BlockSpec((1, d), lambda i, j: (0, 0))],
        out_specs=act,
        out_shape=jax.ShapeDtypeStruct((b, s, d), F32),
        compiler_params=_cparams(("arbitrary", "arbitrary")), name="final_norm",
    )(x1, peer, g2, fg)


def kernel(x, c, ctx, c_ctx, ada_w, ada_b, norm1_g, norm2_g, w_in, na_rpb, gm_ln_g, gm_ws, gm_bs,
           w_proj_a, w_proj_b, w_out, peer_wq, peer_keys, peer_u, peer_v, final_g):
    b, s, d = x.shape
    naw = NA_HEADS * HEAD_DIM
    gw = gm_ln_g.shape[1]
    layer = 0

    pad = (-(b + 1)) % SUBLANES
    c_all = jnp.concatenate([c, c_ctx[None, :], jnp.zeros((pad, d), F32)], axis=0)
    mod = _adaln(c_all, ada_w[layer], ada_b[layer])
    sh1, sc1, g1, sh2, sc2, g2 = [mod[:b, i * d:(i + 1) * d].reshape(b, 1, d) for i in range(ADA_CHUNKS)]
    csh1 = jnp.broadcast_to(mod[b, 0:d].reshape(1, 1, d), (b, 1, d))
    csc1 = jnp.broadcast_to(mod[b, d:2 * d].reshape(1, 1, d), (b, 1, d))

    w = w_in[layer].astype(BF16)
    w_kv = w[:, naw:3 * naw]
    n1g = norm1_g[layer].reshape(1, d)
    bias = _bias_table(na_rpb[layer])
    bs_full = jnp.repeat(gm_bs[layer].T, gw // GM_GROUPS, axis=1)
    keys = peer_keys[layer].reshape(2 * PEER_HEADS, PEER_N_KEYS, PEER_HALF).astype(BF16)
    lng = gm_ln_g[layer].reshape(1, gw)
    ws = gm_ws[layer].astype(BF16)
    wpa, wpb, wout = w_proj_a[layer].astype(BF16), w_proj_b[layer].astype(BF16), w_out[layer].astype(BF16)
    n2g = norm2_g[layer].reshape(1, d)
    wq = peer_wq[layer].astype(BF16)
    tab_u, tab_v = _pack_table(peer_u[layer]), _pack_table(peer_v[layer])
    fg = final_g.reshape(1, d)
    ne = PEER_HEADS * PEER_TOPK

    assert sum(BATCH_CHUNKS) == b
    outs = []
    pending = None
    b0 = 0
    for bc in BATCH_CHUNKS:
        sl = slice(b0, b0 + bc)
        b0 += bc
        t = bc * s
        xc = x[sl]
        q, k, v, gu, gv, ga, gb = _norm_proj(xc, n1g, sh1[sl], sc1[sl], w, (naw, naw, naw, gw, gw, d, d), 512)
        k_c, v_c = _norm_proj(ctx[sl], n1g, csh1[sl], csc1[sl], w_kv, (naw, naw), ctx.shape[1])
        y_a = _attention(q, k, v, k_c, v_c, bias)
        x1, h2, st = _mix(xc, gu, gv, ga, gb, y_a, g1[sl], sh2[sl], sc2[sl], lng, ws, bs_full,
                          wpa, wpb, wout, n2g, wq, keys, 256)
        idx, gate = _topk(st, 512)
        dots = ("dots", tab_u, idx, h2.reshape(t, d // 2))
        if pending is None:
            act, = _sc_call(dots)
        else:
            peer, act = _sc_call(("wsum", tab_v) + pending[:2], dots)
            outs.append(_final(pending[2], peer.reshape(pending[2].shape), pending[3], fg, 512))
        wgt = _peer_weight(act.reshape(t, ne), gate, 2048)
        pending = (idx, wgt, x1, g2[sl])
    peer, = _sc_call(("wsum", tab_v) + pending[:2])
    outs.append(_final(pending[2], peer.reshape(pending[2].shape), pending[3], fg, 512))
    return jnp.concatenate(outs, axis=0)
```

```python
import functools

import jax
import jax.numpy as jnp
from jax import lax
from jax.experimental import pallas as pl
from jax.experimental.pallas import tpu as pltpu
from jax.experimental.pallas import tpu_sc as plsc

F32 = jnp.float32
BF16 = jnp.bfloat16

GRID_W = 64
NA_HEADS = 8
HEAD_DIM = 64
NA_WIN_ROWS = 8
NA_WIN_COLS = 16
GM_GROUPS = 8
GM_CHUNK = 128
PEER_HEADS = 8
PEER_N_KEYS = 128
PEER_TOPK = 16
PEER_HALF = 128
ADA_CHUNKS = 6
EPS = 1e-6
NEG_INF = -1e30

LANES = 128
SUBLANES = 8
VMEM_LIMIT = 56 * 1024 * 1024
BATCH_CHUNKS = (1, 1, 2, 3, 4, 5)


def _dot(a, b):
    return lax.dot_general(a, b, (((1,), (0,)), ((), ())), preferred_element_type=F32)


def _dot_nt(a, b):
    return lax.dot_general(a, b, (((1,), (1,)), ((), ())), preferred_element_type=F32)


def _cparams(sem):
    return pltpu.CompilerParams(dimension_semantics=sem, vmem_limit_bytes=VMEM_LIMIT)


def _pack_bf16_pairs(lo, hi):
    lo_bits = pltpu.bitcast(lo.astype(F32), jnp.uint32) >> 16
    hi_bits = pltpu.bitcast(hi.astype(F32), jnp.uint32) & jnp.uint32(0xFFFF0000)
    return lo_bits | hi_bits


def _adaln_kernel(c_ref, w_ref, b_ref, o_ref):
    c = c_ref[...]
    s = c * jax.nn.sigmoid(c)
    o_ref[...] = lax.dot_general(s, w_ref[...], (((1,), (0,)), ((), ())),
                                 precision=lax.Precision.HIGHEST,
                                 preferred_element_type=F32) + b_ref[...]


def _adaln(c_all, w, b):
    m, d = c_all.shape
    n = w.shape[1]
    tn = 1024
    return pl.pallas_call(
        _adaln_kernel,
        grid=(n // tn,),
        in_specs=[pl.BlockSpec((m, d), lambda j: (0, 0)),
                  pl.BlockSpec((d, tn), lambda j: (0, j)),
                  pl.BlockSpec((1, tn), lambda j: (0, j))],
        out_specs=pl.BlockSpec((m, tn), lambda j: (0, j)),
        out_shape=jax.ShapeDtypeStruct((m, n), F32),
        compiler_params=_cparams(("arbitrary",)),
        name="adaln",
    )(c_all, w, b.reshape(1, n))


def _norm_proj_kernel(widths, x_ref, g_ref, sh_ref, sc_ref, w_ref, *o_refs):
    x = x_ref[0]
    ms = jnp.mean(x * x, axis=-1, keepdims=True)
    y = x * lax.rsqrt(ms + EPS) * g_ref[...]
    h = (y * (1.0 + sc_ref[0]) + sh_ref[0]).astype(BF16)
    off = 0
    for o_ref, wd in zip(o_refs, widths):
        o_ref[0] = _dot(h, w_ref[:, off:off + wd]).astype(o_ref.dtype)
        off += wd


def _norm_proj(x, gain, shift, scale, w, widths, tm):
    b, s, d = x.shape
    n = w.shape[1]
    assert sum(widths) == n and s % tm == 0
    vec = pl.BlockSpec((1, 1, d), lambda i, j: (i, 0, 0))
    return pl.pallas_call(
        functools.partial(_norm_proj_kernel, widths),
        grid=(b, s // tm),
        in_specs=[pl.BlockSpec((1, tm, d), lambda i, j: (i, j, 0)),
                  pl.BlockSpec((1, d), lambda i, j: (0, 0)),
                  vec, vec,
                  pl.BlockSpec((d, n), lambda i, j: (0, 0))],
        out_specs=[pl.BlockSpec((1, tm, wd), lambda i, j: (i, j, 0)) for wd in widths],
        out_shape=[jax.ShapeDtypeStruct((b, s, wd), BF16) for wd in widths],
        compiler_params=_cparams(("arbitrary", "arbitrary")),
        name="norm_proj",
    )(x, gain, shift, scale, w)


def _bias_table_kernel(rpb_ref, o_ref):
    h = pl.program_id(0)
    d0 = pl.program_id(1)
    q = lax.broadcasted_iota(jnp.int32, (GRID_W, GRID_W), 0)
    kc = lax.broadcasted_iota(jnp.int32, (GRID_W, GRID_W), 1)
    dc = jnp.clip(kc - q + NA_WIN_COLS - 1, 0, 2 * NA_WIN_COLS - 2)
    cs = jnp.clip(q - NA_WIN_COLS // 2, 0, GRID_W - NA_WIN_COLS)
    col_in = (kc >= cs) & (kc < cs + NA_WIN_COLS)
    n_dc = 2 * NA_WIN_COLS - 1
    n_dr = 2 * NA_WIN_ROWS - 1
    for j in range(NA_WIN_ROWS):
        row = h * n_dr + d0 + j
        t = jnp.zeros((GRID_W, GRID_W), F32)
        for c in range(n_dc):
            t = jnp.where(dc == c, rpb_ref[row, c], t)
        o_ref[0, 0, :, j * GRID_W:(j + 1) * GRID_W] = jnp.where(col_in, t, NEG_INF)


def _bias_table(rpb):
    nh, n_dr, n_dc = rpb.shape
    band = NA_WIN_ROWS * GRID_W
    return pl.pallas_call(
        _bias_table_kernel,
        grid=(nh, NA_WIN_ROWS),
        in_specs=[pl.BlockSpec(memory_space=pltpu.SMEM)],
        out_specs=pl.BlockSpec((1, 1, GRID_W, band), lambda h, d: (h, d, 0, 0)),
        out_shape=jax.ShapeDtypeStruct((nh, NA_WIN_ROWS, GRID_W, band), F32),
        compiler_params=_cparams(("arbitrary", "arbitrary")),
        name="bias_table",
    )(rpb.reshape(nh * n_dr, n_dc))


def _attn_kernel(rows, q_ref, k_ref, v_ref, kc_ref, vc_ref, bias_ref, o_ref):
    r = pl.program_id(1)
    rs = jnp.clip(r - NA_WIN_ROWS // 2, 0, rows - NA_WIN_ROWS)
    d0 = rs - r + NA_WIN_ROWS - 1
    band = NA_WIN_ROWS * GRID_W
    start = pl.multiple_of(rs * GRID_W, GRID_W)
    scale = HEAD_DIM ** -0.5
    lane = lax.broadcasted_iota(jnp.int32, (GRID_W, LANES), 1)
    for hp in range(NA_HEADS * HEAD_DIM // LANES):
        sl = slice(hp * LANES, (hp + 1) * LANES)
        q2 = q_ref[0, :, sl]
        kb = k_ref[0, pl.ds(start, band), sl]
        vb = v_ref[0, pl.ds(start, band), sl]
        kc = kc_ref[0, :, sl]
        vc = vc_ref[0, :, sl]
        halves = []
        for hh in range(LANES // HEAD_DIM):
            h = hp * (LANES // HEAD_DIM) + hh
            mine = (lane >= hh * HEAD_DIM) & (lane < (hh + 1) * HEAD_DIM)
            qh = jnp.where(mine, q2, jnp.zeros_like(q2))
            s_loc = _dot_nt(qh, kb) * scale + bias_ref[h, d0]
            s_ctx = _dot_nt(qh, kc) * scale
            m = jnp.maximum(jnp.max(s_loc, axis=-1, keepdims=True),
                            jnp.max(s_ctx, axis=-1, keepdims=True))
            p_loc = jnp.exp(s_loc - m)
            p_ctx = jnp.exp(s_ctx - m)
            den = jnp.sum(p_loc, axis=-1, keepdims=True) + jnp.sum(p_ctx, axis=-1, keepdims=True)
            o = _dot(p_loc.astype(BF16), vb) + _dot(p_ctx.astype(BF16), vc)
            halves.append(jnp.where(mine, o / den, 0.0))
        o_ref[0, :, sl] = sum(halves).astype(o_ref.dtype)


def _attention(q, k, v, kc, vc, bias):
    b, s, w = q.shape
    rows = s // GRID_W
    c = kc.shape[1]
    full = lambda n: pl.BlockSpec((1, n, w), lambda i, j: (i, 0, 0))
    return pl.pallas_call(
        functools.partial(_attn_kernel, rows),
        grid=(b, rows),
        in_specs=[pl.BlockSpec((1, GRID_W, w), lambda i, j: (i, j, 0)),
                  full(s), full(s), full(c), full(c),
                  pl.BlockSpec(bias.shape, lambda i, j: (0, 0, 0, 0))],
        out_specs=pl.BlockSpec((1, GRID_W, w), lambda i, j: (i, j, 0)),
        out_shape=jax.ShapeDtypeStruct((b, s, w), BF16),
        compiler_params=_cparams(("arbitrary", "arbitrary")),
        name="nbr_attention",
    )(q, k, v, kc, vc, bias)


def _mix_kernel(x_ref, gu_ref, gv_ref, ga_ref, gb_ref, ya_ref, g1_ref, sh2_ref, sc2_ref,
                lng_ref, ws_ref, bs_ref, wpa_ref, wpb_ref, wout_ref, n2g_ref, wq_ref, keys_ref,
                x1_ref, h2_ref, st_ref):
    tm = x_ref.shape[1]
    gw = gu_ref.shape[2]
    u = jax.nn.gelu(gu_ref[0].astype(F32))
    t = jax.nn.gelu(gv_ref[0].astype(F32))
    tc = t - jnp.mean(t, axis=-1, keepdims=True)
    vn = tc * lax.rsqrt(jnp.mean(tc * tc, axis=-1, keepdims=True) + EPS) * lng_ref[...]
    vnb = vn.astype(BF16)
    lane = lax.broadcasted_iota(jnp.int32, (GM_CHUNK, LANES), 1)
    gd = gw // GM_GROUPS
    chunks = []
    for c in range(tm // GM_CHUNK):
        pairs = []
        for gp in range(gw // LANES):
            vp = vnb[c * GM_CHUNK:(c + 1) * GM_CHUNK, gp * LANES:(gp + 1) * LANES]
            r0 = _dot(ws_ref[2 * gp], vp)
            r1 = _dot(ws_ref[2 * gp + 1], vp)
            pairs.append(jnp.where(lane < gd, r0, r1))
        chunks.append(jnp.concatenate(pairs, axis=1) + bs_ref[...])
    mixed = jnp.concatenate(chunks, axis=0)
    yb = (u * mixed).astype(BF16)
    pa = _dot(ya_ref[0], wpa_ref[...])
    pb = _dot(yb, wpb_ref[...])
    m = jax.nn.sigmoid(ga_ref[0].astype(F32)) * pa + jax.nn.sigmoid(gb_ref[0].astype(F32)) * pb
    out = _dot(m.astype(BF16), wout_ref[...])
    x1 = x_ref[0] + g1_ref[0] * out
    x1_ref[0] = x1
    ms = jnp.mean(x1 * x1, axis=-1, keepdims=True)
    h2 = x1 * lax.rsqrt(ms + EPS) * n2g_ref[...]
    h2 = (h2 * (1.0 + sc2_ref[0]) + sh2_ref[0]).astype(BF16)
    half = h2.shape[1] // 2
    h2_ref[0] = _pack_bf16_pairs(h2[:, :half], h2[:, half:])
    qp = _dot(h2, wq_ref[...]).astype(BF16)
    for hp in range(keys_ref.shape[0]):
        st_ref[hp] = _dot_nt(keys_ref[hp], qp[:, hp * PEER_HALF:(hp + 1) * PEER_HALF])


def _mix(x, gu, gv, ga, gb, ya, g1, sh2, sc2, lng, ws, bs_full, wpa, wpb, wout, n2g, wq, keys, tm):
    b, s, d = x.shape
    gw = gu.shape[2]
    nt = s // tm
    nhp = keys.shape[0]
    act = lambda w: pl.BlockSpec((1, tm, w), lambda i, j: (i, j, 0))
    vec = pl.BlockSpec((1, 1, d), lambda i, j: (i, 0, 0))

    def const(a):
        nd = a.ndim
        return pl.BlockSpec(a.shape, lambda i, j: (0,) * nd)

    return pl.pallas_call(
        _mix_kernel,
        grid=(b, nt),
        in_specs=[act(d), act(gw), act(gw), act(d), act(d), act(gw), vec, vec, vec,
                  const(lng), const(ws), const(bs_full), const(wpa), const(wpb), const(wout),
                  const(n2g), const(wq), const(keys)],
        out_specs=[act(d), act(d // 2),
                   pl.BlockSpec((nhp, PEER_N_KEYS, tm), lambda i, j: (0, 0, i * nt + j))],
        out_shape=[jax.ShapeDtypeStruct((b, s, d), F32),
                   jax.ShapeDtypeStruct((b, s, d // 2), jnp.uint32),
                   jax.ShapeDtypeStruct((nhp, PEER_N_KEYS, b * s), F32)],
        compiler_params=_cparams(("arbitrary", "arbitrary")),
        name="mix_peer_scores",
    )(x, gu, gv, ga, gb, ya, g1, sh2, sc2, lng, ws, bs_full, wpa, wpb, wout, n2g, wq, keys)


def _extract_top(vals, order, payload, k):
    out_v, out_p = [], []
    for _ in range(k):
        m = jnp.max(vals, axis=0, keepdims=True)
        o = jnp.min(jnp.where(vals == m, order, jnp.inf), axis=0, keepdims=True)
        sel = order == o
        out_v.append(m)
        if payload is order:
            out_p.append(o)
        else:
            out_p.append(jnp.max(jnp.where(sel, payload, -1.0), axis=0, keepdims=True))
        vals = jnp.where(sel, -jnp.inf, vals)
    return jnp.concatenate(out_v, axis=0), jnp.concatenate(out_p, axis=0)


def _topk_kernel(st_ref, idx_ref, gate_ref, sv_ref, si_ref, se_ref, sg_ref):
    nhp = st_ref.shape[0]
    tm = st_ref.shape[2]
    k = PEER_TOPK
    kidx = lax.broadcasted_iota(jnp.int32, (PEER_N_KEYS, LANES), 0).astype(F32)
    sub = lax.broadcasted_iota(jnp.int32, (SUBLANES, LANES), 0).astype(F32)

    def lane_tile(lt, carry):
        lanes = pl.ds(pl.multiple_of(lt * LANES, LANES), LANES)

        def stage1(h, c):
            for hp in (2 * h, 2 * h + 1):
                v, i = _extract_top(st_ref[hp, :, lanes], kidx, kidx, k)
                sv_ref[hp] = v
                si_ref[hp] = i
            return c

        lax.fori_loop(0, nhp // 2, stage1, 0)

        def stage2(h2, c):
            for h in (2 * h2, 2 * h2 + 1):
                stage2_head(h)
            return c

        def stage2_head(h):
            s1, s2 = sv_ref[2 * h], sv_ref[2 * h + 1]
            i1, i2 = si_ref[2 * h] * float(PEER_N_KEYS), si_ref[2 * h + 1]
            cv, co, ce = [], [], []
            for half in range(2):
                b0 = half * SUBLANES
                cv.append(s1[0:1] + s2[b0:b0 + SUBLANES])
                co.append(sub + float(b0))
                ce.append(i1[0:1] + i2[b0:b0 + SUBLANES])
            for a in range(1, SUBLANES):
                cv.append(s1[a:a + 1] + s2[0:SUBLANES])
                co.append(sub + float(a * k))
                ce.append(i1[a:a + 1] + i2[0:SUBLANES])
            cv.append(s1[SUBLANES:k] + s2[0:1])
            co.append((sub + float(SUBLANES)) * float(k))
            ce.append(i1[SUBLANES:k] + i2[0:1])
            bv, be = _extract_top(jnp.concatenate(cv, axis=0), jnp.concatenate(co, axis=0),
                                  jnp.concatenate(ce, axis=0), k)
            ex = jnp.exp(bv - bv[0:1])
            rows = pl.ds(pl.multiple_of(h * k, k), k)
            sg_ref[rows, :] = ex / jnp.sum(ex, axis=0, keepdims=True)
            se_ref[rows, :] = be

        lax.fori_loop(0, nhp // 4, stage2, 0)

        rows = pl.ds(pl.multiple_of(lt * LANES, LANES), LANES)
        idx_ref[rows, :] = se_ref[...].T.astype(jnp.int32)
        gate_ref[rows, :] = sg_ref[...].T
        return carry

    lax.fori_loop(0, tm // LANES, lane_tile, 0)


def _topk(st, tm):
    nhp, nk, t = st.shape
    ne = (nhp // 2) * PEER_TOPK
    return pl.pallas_call(
        _topk_kernel,
        grid=(t // tm,),
        in_specs=[pl.BlockSpec((nhp, nk, tm), lambda i: (0, 0, i))],
        out_specs=[pl.BlockSpec((tm, ne), lambda i: (i, 0)),
                   pl.BlockSpec((tm, ne), lambda i: (i, 0))],
        out_shape=[jax.ShapeDtypeStruct((t, ne), jnp.int32),
                   jax.ShapeDtypeStruct((t, ne), F32)],
        scratch_shapes=[pltpu.VMEM((nhp, PEER_TOPK, LANES), F32),
                        pltpu.VMEM((nhp, PEER_TOPK, LANES), F32),
                        pltpu.VMEM((ne, LANES), F32),
                        pltpu.VMEM((ne, LANES), F32)],
        compiler_params=_cparams(("arbitrary",)),
        name="peer_topk",
    )(st)


SC_LANES = 16
SC_WINDOW = 32
SC_ROW_BUFFERS = 4
SC_TOKENS_PER_STEP = 8
SC_WSUM_CHUNKS = 8
SC_BF16_TERMS = 4


def _pack_table(t):
    n, d = t.shape
    tb = t.astype(BF16)
    lo = lax.bitcast_convert_type(tb[:, :d // 2], jnp.uint16).astype(jnp.uint32)
    hi = lax.bitcast_convert_type(tb[:, d // 2:], jnp.uint16).astype(jnp.uint32)
    return lo | (hi << 16)


def _sc_bf16(words):
    return plsc.bitcast(words, BF16)


def _sc_phase(table, idx, operand, mode):
    t, ne = idx.shape
    hw = table.shape[1]
    d = 2 * hw
    L = SC_LANES
    win = SC_WINDOW
    nbuf = SC_ROW_BUFFERS
    ahead = nbuf - 1
    tps = SC_TOKENS_PER_STEP
    nq = SC_WSUM_CHUNKS
    nterm = SC_BF16_TERMS
    info = plsc.get_sparse_core_info()
    nc, ns = info.num_cores, info.num_subcores
    nss = t // (nc * ns * tps)
    wps = tps * ne // win
    wpt = ne // win
    blk = tps * ne
    nwin = nss * wps
    assert nss * nc * ns * tps == t and nss % 2 == 0 and wps % nbuf == 0 and wps > ahead
    assert wpt * win == ne and win % L == 0 and hw % (nq * L) == 0 and hw % (nterm * L) == 0 and win % nterm == 0
    dots = mode == "dots"
    assert dots or nbuf % wpt == 0
    op_buf = pltpu.VMEM((2, tps, hw), jnp.uint32) if dots else pltpu.VMEM((2 * blk,), jnp.int32)
    res_buf = pltpu.VMEM((2 * blk,), F32) if dots else pltpu.VMEM((2, tps, d), F32)
    out_type = jax.ShapeDtypeStruct((t * ne,), F32) if dots else jax.ShapeDtypeStruct((t, d), F32)
    scratch = [("idx", pltpu.VMEM((2 * blk,), jnp.int32)), ("op_" + mode, op_buf),
               ("rows", pltpu.VMEM((nbuf, win, hw), jnp.uint32)), ("res_" + mode, res_buf),
               ("isem", pltpu.SemaphoreType.DMA((2,))), ("psem", pltpu.SemaphoreType.DMA((2,))),
               ("gsem", pltpu.SemaphoreType.DMA((nbuf,))), ("osem", pltpu.SemaphoreType.DMA((2,)))]

    def sc_kernel(table_hbm, idx_hbm, op_hbm, out_hbm, idx_v, op_v, rows_v, res_v, isem, psem, gsem, osem):
        ss0 = (lax.axis_index("s") * nc + lax.axis_index("c")) * nss
        lane = lax.iota(jnp.int32, L)

        def flat(ref, n):
            return ref.at[pl.ds(pl.multiple_of(n * blk, blk), blk)]

        def tok_rows(ref, n):
            return ref.at[pl.ds(pl.multiple_of(n * tps, tps), tps)]

        def idx_copy(ss, slot):
            return pltpu.make_async_copy(flat(idx_hbm, ss0 + ss), flat(idx_v, slot), isem.at[slot])

        def op_copy(ss, slot):
            if dots:
                return pltpu.make_async_copy(tok_rows(op_hbm, ss0 + ss), op_v.at[slot], psem.at[slot])
            return pltpu.make_async_copy(flat(op_hbm, ss0 + ss), flat(op_v, slot), psem.at[slot])

        def out_copy(ss, slot):
            if dots:
                return pltpu.make_async_copy(flat(res_v, slot), flat(out_hbm, ss0 + ss), osem.at[slot])
            return pltpu.make_async_copy(res_v.at[slot], tok_rows(out_hbm, ss0 + ss), osem.at[slot])

        def gather(slot, hs, b):
            ix = idx_v.at[pl.ds(pl.multiple_of(slot * blk + hs * win, win), win)]
            return pltpu.make_async_copy(table_hbm.at[ix], rows_v.at[b], gsem.at[b])

        def reduce_dots(slot, hs, b):
            tok = hs // wpt

            def head(g, c):
                def chunk(jj, accs):
                    off = pl.multiple_of(jj * (nterm * L), nterm * L)
                    xs = [_sc_bf16(op_v[slot, tok, pl.ds(off + q * L, L)]) for q in range(nterm)]
                    new = []
                    for r in range(L):
                        p = _sc_bf16(rows_v[b, g * L + r, pl.ds(off, L)]) * xs[0]
                        for q in range(1, nterm):
                            p = p + _sc_bf16(rows_v[b, g * L + r, pl.ds(off + q * L, L)]) * xs[q]
                        lo, hi = plsc.unpack(p, format=plsc.PackFormat.INTERLEAVED)
                        new.append(accs[r] + lo + hi)
                    return tuple(new)

                accs = lax.fori_loop(0, hw // (nterm * L), chunk, tuple(jnp.zeros((L,), F32) for _ in range(L)))
                out = jnp.zeros((L,), F32)
                for r in range(L):
                    out = jnp.where(lane == r, jnp.sum(accs[r]), out)
                res_v[pl.ds(pl.multiple_of(slot * blk + hs * win + g * L, L), L)] = out
                return c

            lax.fori_loop(0, win // L, head, 0)

        def reduce_wsum(slot, hs, b):
            tok = hs // wpt
            wbase = slot * blk + hs * win
            first = b % wpt == 0

            def colgroup(cg, c):
                col = pl.multiple_of(cg * (nq * L), nq * L)
                if first:
                    accs = tuple(jnp.zeros((L,), F32) for _ in range(2 * nq))
                else:
                    accs = tuple(res_v[slot, tok, pl.ds(col + q * L, L)] for q in range(nq)) + \
                           tuple(res_v[slot, tok, pl.ds(hw + col + q * L, L)] for q in range(nq))

                def rowgroup(rg, accs):
                    accs = list(accs)
                    r = rg * nterm
                    ws = [_sc_bf16(plsc.load_gather(op_v, [jnp.full((L,), wbase + r + k, jnp.int32)]))
                          for k in range(nterm)]
                    for q in range(nq):
                        p = _sc_bf16(rows_v[b, r, pl.ds(col + q * L, L)]) * ws[0]
                        for k in range(1, nterm):
                            p = p + _sc_bf16(rows_v[b, r + k, pl.ds(col + q * L, L)]) * ws[k]
                        lo, hi = plsc.unpack(p, format=plsc.PackFormat.INTERLEAVED)
                        accs[q] = accs[q] + lo
                        accs[nq + q] = accs[nq + q] + hi
                    return tuple(accs)

                accs = lax.fori_loop(0, win // nterm, rowgroup, accs)
                for q in range(nq):
                    res_v[slot, tok, pl.ds(col + q * L, L)] = accs[q]
                    res_v[slot, tok, pl.ds(hw + col + q * L, L)] = accs[nq + q]
                return c

            lax.fori_loop(0, hw // (nq * L), colgroup, 0)

        idx_copy(0, 0).start()
        op_copy(0, 0).start()
        idx_copy(1, 1).start()
        op_copy(1, 1).start()
        idx_copy(0, 0).wait()
        op_copy(0, 0).wait()
        for a in range(ahead):
            gather(0, a, a).start()

        @pl.loop(0, nwin, step=nbuf)
        def _(i0):
            for b in range(nbuf):
                i = i0 + b
                ss = i // wps
                hs = i % wps
                slot = ss % 2
                gather(slot, hs, b).wait()
                nb = (b + ahead) % nbuf

                @pl.when(hs + ahead < wps)
                def _():
                    gather(slot, hs + ahead, nb).start()

                @pl.when(jnp.logical_and(hs + ahead >= wps, ss + 1 < nss))
                def _():
                    @pl.when(hs + ahead == wps)
                    def _():
                        idx_copy(ss + 1, 1 - slot).wait()
                        op_copy(ss + 1, 1 - slot).wait()

                    gather(1 - slot, hs + ahead - wps, nb).start()

                @pl.when(jnp.logical_and(hs == 0, ss >= 2))
                def _():
                    out_copy(ss - 2, slot).wait()

                if dots:
                    reduce_dots(slot, hs, b)
                else:
                    reduce_wsum(slot, hs, b)

                @pl.when(hs + 1 == wps)
                def _():
                    out_copy(ss, slot).start()

                    @pl.when(ss + 2 < nss)
                    def _():
                        idx_copy(ss + 2, slot).start()
                        op_copy(ss + 2, slot).start()

        out_copy(nss - 2, 0).wait()
        out_copy(nss - 1, 1).wait()

    op = operand if dots else operand.reshape(t * ne)
    return sc_kernel, scratch, out_type, (table, idx.reshape(t * ne), op)


def _sc_call(*phases):
    parts = [_sc_phase(table, idx, operand, mode) for mode, table, idx, operand in phases]
    names, types = [], []
    for _, scratch, _, _ in parts:
        for name, ty in scratch:
            if name not in names:
                names.append(name)
                types.append(ty)
    n = len(parts)

    def fused(*refs):
        ins, outs, scr = refs[:3 * n], refs[3 * n:4 * n], dict(zip(names, refs[4 * n:]))
        for i, (body, scratch, _, _) in enumerate(parts):
            body(*ins[3 * i:3 * i + 3], outs[i], *[scr[name] for name, _ in scratch])

    call = pl.kernel(
        fused, mesh=plsc.VectorSubcoreMesh(core_axis_name="c", subcore_axis_name="s"),
        out_type=tuple(p[2] for p in parts), scratch_types=types,
        compiler_params=pltpu.CompilerParams(needs_layout_passes=False),
        name="peer_" + "_".join(m for m, _, _, _ in phases),
    )
    return call(*[a for p in parts for a in p[3]])


def _peer_weight_kernel(act_ref, gate_ref, w_ref):
    w = (jax.nn.gelu(act_ref[...]) * gate_ref[...]).astype(BF16)
    w_ref[...] = pltpu.bitcast(_pack_bf16_pairs(w, w), jnp.int32)


def _peer_weight(act, gate, tm):
    t, ne = act.shape
    tm = min(tm, t)
    assert t % tm == 0
    spec = pl.BlockSpec((tm, ne), lambda i: (i, 0))
    return pl.pallas_call(
        _peer_weight_kernel, grid=(t // tm,), in_specs=[spec, spec], out_specs=spec,
        out_shape=jax.ShapeDtypeStruct((t, ne), jnp.int32),
        compiler_params=_cparams(("arbitrary",)), name="peer_weight",
    )(act, gate)


def _final_kernel(x1_ref, p_ref, g2_ref, fg_ref, prev_ref, o_ref):
    del prev_ref
    x2 = x1_ref[0] + g2_ref[0] * p_ref[0]
    ms = jnp.mean(x2 * x2, axis=-1, keepdims=True)
    o_ref[0] = x2 * lax.rsqrt(ms + EPS) * fg_ref[...]


def _final(x1, peer, g2, fg, tm, out, b0):
    bc, s, d = x1.shape
    act = pl.BlockSpec((1, tm, d), lambda i, j: (i, j, 0))
    return pl.pallas_call(
        _final_kernel, grid=(bc, s // tm),
        in_specs=[act, act, pl.BlockSpec((1, 1, d), lambda i, j: (i, 0, 0)),
                  pl.BlockSpec((1, d), lambda i, j: (0, 0)),
                  pl.BlockSpec(memory_space=pl.ANY)],
        out_specs=pl.BlockSpec((1, tm, d), lambda i, j: (i + b0, j, 0)),
        out_shape=jax.ShapeDtypeStruct(out.shape, F32),
        input_output_aliases={4: 0},
        compiler_params=_cparams(("arbitrary", "arbitrary")), name="final_norm",
    )(x1, peer, g2, fg, out)


def kernel(x, c, ctx, c_ctx, ada_w, ada_b, norm1_g, norm2_g, w_in, na_rpb, gm_ln_g, gm_ws, gm_bs,
           w_proj_a, w_proj_b, w_out, peer_wq, peer_keys, peer_u, peer_v, final_g):
    b, s, d = x.shape
    naw = NA_HEADS * HEAD_DIM
    gw = gm_ln_g.shape[1]
    layer = 0

    pad = (-(b + 1)) % SUBLANES
    c_all = jnp.concatenate([c, c_ctx[None, :], jnp.zeros((pad, d), F32)], axis=0)
    mod = _adaln(c_all, ada_w[layer], ada_b[layer])
    sh1, sc1, g1, sh2, sc2, g2 = [mod[:b, i * d:(i + 1) * d].reshape(b, 1, d) for i in range(ADA_CHUNKS)]
    csh1 = jnp.broadcast_to(mod[b, 0:d].reshape(1, 1, d), (b, 1, d))
    csc1 = jnp.broadcast_to(mod[b, d:2 * d].reshape(1, 1, d), (b, 1, d))

    w = w_in[layer].astype(BF16)
    w_kv = w[:, naw:3 * naw]
    n1g = norm1_g[layer].reshape(1, d)
    bias = _bias_table(na_rpb[layer])
    bs_full = jnp.repeat(gm_bs[layer].T, gw // GM_GROUPS, axis=1)
    keys = peer_keys[layer].reshape(2 * PEER_HEADS, PEER_N_KEYS, PEER_HALF).astype(BF16)
    lng = gm_ln_g[layer].reshape(1, gw)
    ws = gm_ws[layer].astype(BF16)
    wpa, wpb, wout = w_proj_a[layer].astype(BF16), w_proj_b[layer].astype(BF16), w_out[layer].astype(BF16)
    n2g = norm2_g[layer].reshape(1, d)
    wq = peer_wq[layer].astype(BF16)
    tab_u, tab_v = _pack_table(peer_u[layer]), _pack_table(peer_v[layer])
    fg = final_g.reshape(1, d)
    ne = PEER_HEADS * PEER_TOPK

    assert sum(BATCH_CHUNKS) == b
    out = jnp.zeros((b, s, d), F32)

    def finish(out, pending, peer):
        idx, wgt, x1, g2c, boff = pending
        return _final(x1, peer.reshape(x1.shape), g2c, fg, 512, out, boff)

    pending = None
    b0 = 0
    for bc in BATCH_CHUNKS:
        sl = slice(b0, b0 + bc)
        boff = b0
        b0 += bc
        t = bc * s
        xc = x[sl]
        q, k, v, gu, gv, ga, gb = _norm_proj(xc, n1g, sh1[sl], sc1[sl], w, (naw, naw, naw, gw, gw, d, d), 512)
        k_c, v_c = _norm_proj(ctx[sl], n1g, csh1[sl], csc1[sl], w_kv, (naw, naw), ctx.shape[1])
        y_a = _attention(q, k, v, k_c, v_c, bias)
        x1, h2, st = _mix(xc, gu, gv, ga, gb, y_a, g1[sl], sh2[sl], sc2[sl], lng, ws, bs_full,
                          wpa, wpb, wout, n2g, wq, keys, 256)
        idx, gate = _topk(st, 512)
        dots = ("dots", tab_u, idx, h2.reshape(t, d // 2))
        if pending is None:
            act, = _sc_call(dots)
        else:
            peer, act = _sc_call(("wsum", tab_v) + pending[:2], dots)
            out = finish(out, pending, peer)
        wgt = _peer_weight(act.reshape(t, ne), gate, 2048)
        pending = (idx, wgt, x1, g2[sl], boff)
    peer, = _sc_call(("wsum", tab_v) + pending[:2])
    return finish(out, pending, peer)
```

```python
import functools

import jax
import jax.numpy as jnp
from jax import lax
from jax.experimental import pallas as pl
from jax.experimental.pallas import tpu as pltpu
from jax.experimental.pallas import tpu_sc as plsc

F32 = jnp.float32
BF16 = jnp.bfloat16

GRID_W = 64
NA_HEADS = 8
HEAD_DIM = 64
NA_WIN_ROWS = 8
NA_WIN_COLS = 16
GM_GROUPS = 8
GM_CHUNK = 128
PEER_HEADS = 8
PEER_N_KEYS = 128
PEER_TOPK = 16
PEER_HALF = 128
ADA_CHUNKS = 6
EPS = 1e-6
NEG_INF = -1e30

LANES = 128
SUBLANES = 8
VMEM_LIMIT = 56 * 1024 * 1024
BATCH_CHUNKS = (1, 1, 1, 2, 2, 3, 3, 3)


def _dot(a, b):
    return lax.dot_general(a, b, (((1,), (0,)), ((), ())), preferred_element_type=F32)


def _dot_nt(a, b):
    return lax.dot_general(a, b, (((1,), (1,)), ((), ())), preferred_element_type=F32)


def _cparams(sem):
    return pltpu.CompilerParams(dimension_semantics=sem, vmem_limit_bytes=VMEM_LIMIT)


def _pack_bf16_pairs(lo, hi):
    lo_bits = pltpu.bitcast(lo.astype(F32), jnp.uint32) >> 16
    hi_bits = pltpu.bitcast(hi.astype(F32), jnp.uint32) & jnp.uint32(0xFFFF0000)
    return lo_bits | hi_bits


def _adaln_kernel(c_ref, w_ref, b_ref, o_ref):
    c = c_ref[...]
    s = c * jax.nn.sigmoid(c)
    o_ref[...] = lax.dot_general(s, w_ref[...], (((1,), (0,)), ((), ())),
                                 precision=lax.Precision.HIGHEST,
                                 preferred_element_type=F32) + b_ref[...]


def _adaln(c_all, w, b):
    m, d = c_all.shape
    n = w.shape[1]
    tn = 1024
    return pl.pallas_call(
        _adaln_kernel,
        grid=(n // tn,),
        in_specs=[pl.BlockSpec((m, d), lambda j: (0, 0)),
                  pl.BlockSpec((d, tn), lambda j: (0, j)),
                  pl.BlockSpec((1, tn), lambda j: (0, j))],
        out_specs=pl.BlockSpec((m, tn), lambda j: (0, j)),
        out_shape=jax.ShapeDtypeStruct((m, n), F32),
        compiler_params=_cparams(("arbitrary",)),
        name="adaln",
    )(c_all, w, b.reshape(1, n))


def _norm_proj_kernel(widths, x_ref, g_ref, sh_ref, sc_ref, w_ref, *o_refs):
    x = x_ref[0]
    ms = jnp.mean(x * x, axis=-1, keepdims=True)
    y = x * lax.rsqrt(ms + EPS) * g_ref[...]
    h = (y * (1.0 + sc_ref[0]) + sh_ref[0]).astype(BF16)
    off = 0
    for o_ref, wd in zip(o_refs, widths):
        o_ref[0] = _dot(h, w_ref[:, off:off + wd]).astype(o_ref.dtype)
        off += wd


def _norm_proj(x, gain, shift, scale, w, widths, tm):
    b, s, d = x.shape
    n = w.shape[1]
    assert sum(widths) == n and s % tm == 0
    vec = pl.BlockSpec((1, 1, d), lambda i, j: (i, 0, 0))
    return pl.pallas_call(
        functools.partial(_norm_proj_kernel, widths),
        grid=(b, s // tm),
        in_specs=[pl.BlockSpec((1, tm, d), lambda i, j: (i, j, 0)),
                  pl.BlockSpec((1, d), lambda i, j: (0, 0)),
                  vec, vec,
                  pl.BlockSpec((d, n), lambda i, j: (0, 0))],
        out_specs=[pl.BlockSpec((1, tm, wd), lambda i, j: (i, j, 0)) for wd in widths],
        out_shape=[jax.ShapeDtypeStruct((b, s, wd), BF16) for wd in widths],
        compiler_params=_cparams(("arbitrary", "arbitrary")),
        name="norm_proj",
    )(x, gain, shift, scale, w)


def _bias_table_kernel(rpb_ref, o_ref):
    h = pl.program_id(0)
    q = lax.broadcasted_iota(jnp.int32, (GRID_W, GRID_W), 0)
    kc = lax.broadcasted_iota(jnp.int32, (GRID_W, GRID_W), 1)
    dc = jnp.clip(kc - q + NA_WIN_COLS - 1, 0, 2 * NA_WIN_COLS - 2)
    cs = jnp.clip(q - NA_WIN_COLS // 2, 0, GRID_W - NA_WIN_COLS)
    col_in = (kc >= cs) & (kc < cs + NA_WIN_COLS)
    n_dc = 2 * NA_WIN_COLS - 1
    n_dr = 2 * NA_WIN_ROWS - 1
    for dr in range(n_dr):
        t = jnp.zeros((GRID_W, GRID_W), F32)
        for c in range(n_dc):
            t = jnp.where(dc == c, rpb_ref[h * n_dr + dr, c], t)
        t = jnp.where(col_in, t, NEG_INF)
        for d0 in range(NA_WIN_ROWS):
            j = dr - d0
            if 0 <= j < NA_WIN_ROWS:
                o_ref[0, d0, :, j * GRID_W:(j + 1) * GRID_W] = t


def _bias_table(rpb):
    nh, n_dr, n_dc = rpb.shape
    band = NA_WIN_ROWS * GRID_W
    return pl.pallas_call(
        _bias_table_kernel,
        grid=(nh,),
        in_specs=[pl.BlockSpec(memory_space=pltpu.SMEM)],
        out_specs=pl.BlockSpec((1, NA_WIN_ROWS, GRID_W, band), lambda h: (h, 0, 0, 0)),
        out_shape=jax.ShapeDtypeStruct((nh, NA_WIN_ROWS, GRID_W, band), F32),
        compiler_params=_cparams(("arbitrary",)),
        name="bias_table",
    )(rpb.reshape(nh * n_dr, n_dc))


def _attn_kernel(rows, q_ref, k_ref, v_ref, kc_ref, vc_ref, bias_ref, o_ref):
    r = pl.program_id(1)
    rs = jnp.clip(r - NA_WIN_ROWS // 2, 0, rows - NA_WIN_ROWS)
    d0 = rs - r + NA_WIN_ROWS - 1
    band = NA_WIN_ROWS * GRID_W
    start = pl.multiple_of(rs * GRID_W, GRID_W)
    scale = HEAD_DIM ** -0.5
    lane = lax.broadcasted_iota(jnp.int32, (GRID_W, LANES), 1)
    for hp in range(NA_HEADS * HEAD_DIM // LANES):
        sl = slice(hp * LANES, (hp + 1) * LANES)
        q2 = q_ref[0, :, sl]
        kb = k_ref[0, pl.ds(start, band), sl]
        vb = v_ref[0, pl.ds(start, band), sl]
        kc = kc_ref[0, :, sl]
        vc = vc_ref[0, :, sl]
        halves = []
        for hh in range(LANES // HEAD_DIM):
            h = hp * (LANES // HEAD_DIM) + hh
            mine = (lane >= hh * HEAD_DIM) & (lane < (hh + 1) * HEAD_DIM)
            qh = jnp.where(mine, q2, jnp.zeros_like(q2))
            s_loc = _dot_nt(qh, kb) * scale + bias_ref[h, d0]
            s_ctx = _dot_nt(qh, kc) * scale
            m = jnp.maximum(jnp.max(s_loc, axis=-1, keepdims=True),
                            jnp.max(s_ctx, axis=-1, keepdims=True))
            p_loc = jnp.exp(s_loc - m)
            p_ctx = jnp.exp(s_ctx - m)
            den = jnp.sum(p_loc, axis=-1, keepdims=True) + jnp.sum(p_ctx, axis=-1, keepdims=True)
            o = _dot(p_loc.astype(BF16), vb) + _dot(p_ctx.astype(BF16), vc)
            halves.append(jnp.where(mine, o / den, 0.0))
        o_ref[0, :, sl] = sum(halves).astype(o_ref.dtype)


def _attention(q, k, v, kc, vc, bias):
    b, s, w = q.shape
    rows = s // GRID_W
    c = kc.shape[1]
    full = lambda n: pl.BlockSpec((1, n, w), lambda i, j: (i, 0, 0))
    return pl.pallas_call(
        functools.partial(_attn_kernel, rows),
        grid=(b, rows),
        in_specs=[pl.BlockSpec((1, GRID_W, w), lambda i, j: (i, j, 0)),
                  full(s), full(s), full(c), full(c),
                  pl.BlockSpec(bias.shape, lambda i, j: (0, 0, 0, 0))],
        out_specs=pl.BlockSpec((1, GRID_W, w), lambda i, j: (i, j, 0)),
        out_shape=jax.ShapeDtypeStruct((b, s, w), BF16),
        compiler_params=_cparams(("arbitrary", "arbitrary")),
        name="nbr_attention",
    )(q, k, v, kc, vc, bias)


def _mix_kernel(x_ref, gu_ref, gv_ref, ga_ref, gb_ref, ya_ref, g1_ref, sh2_ref, sc2_ref,
                lng_ref, ws_ref, bs_ref, wpa_ref, wpb_ref, wout_ref, n2g_ref, wq_ref, keys_ref,
                x1_ref, h2_ref, st_ref):
    tm = x_ref.shape[1]
    gw = gu_ref.shape[2]
    u = jax.nn.gelu(gu_ref[0].astype(F32))
    t = jax.nn.gelu(gv_ref[0].astype(F32))
    tc = t - jnp.mean(t, axis=-1, keepdims=True)
    vn = tc * lax.rsqrt(jnp.mean(tc * tc, axis=-1, keepdims=True) + EPS) * lng_ref[...]
    vnb = vn.astype(BF16)
    lane = lax.broadcasted_iota(jnp.int32, (GM_CHUNK, LANES), 1)
    gd = gw // GM_GROUPS
    chunks = []
    for c in range(tm // GM_CHUNK):
        pairs = []
        for gp in range(gw // LANES):
            vp = vnb[c * GM_CHUNK:(c + 1) * GM_CHUNK, gp * LANES:(gp + 1) * LANES]
            r0 = _dot(ws_ref[2 * gp], vp)
            r1 = _dot(ws_ref[2 * gp + 1], vp)
            pairs.append(jnp.where(lane < gd, r0, r1))
        chunks.append(jnp.concatenate(pairs, axis=1) + bs_ref[...])
    mixed = jnp.concatenate(chunks, axis=0)
    yb = (u * mixed).astype(BF16)
    pa = _dot(ya_ref[0], wpa_ref[...])
    pb = _dot(yb, wpb_ref[...])
    m = jax.nn.sigmoid(ga_ref[0].astype(F32)) * pa + jax.nn.sigmoid(gb_ref[0].astype(F32)) * pb
    out = _dot(m.astype(BF16), wout_ref[...])
    x1 = x_ref[0] + g1_ref[0] * out
    x1_ref[0] = x1
    ms = jnp.mean(x1 * x1, axis=-1, keepdims=True)
    h2 = x1 * lax.rsqrt(ms + EPS) * n2g_ref[...]
    h2 = (h2 * (1.0 + sc2_ref[0]) + sh2_ref[0]).astype(BF16)
    half = h2.shape[1] // 2
    h2_ref[0] = _pack_bf16_pairs(h2[:, :half], h2[:, half:])
    qp = _dot(h2, wq_ref[...]).astype(BF16)
    for hp in range(keys_ref.shape[0]):
        st_ref[hp] = _dot_nt(keys_ref[hp], qp[:, hp * PEER_HALF:(hp + 1) * PEER_HALF])


def _mix(x, gu, gv, ga, gb, ya, g1, sh2, sc2, lng, ws, bs_full, wpa, wpb, wout, n2g, wq, keys, tm):
    b, s, d = x.shape
    gw = gu.shape[2]
    nt = s // tm
    nhp = keys.shape[0]
    act = lambda w: pl.BlockSpec((1, tm, w), lambda i, j: (i, j, 0))
    vec = pl.BlockSpec((1, 1, d), lambda i, j: (i, 0, 0))

    def const(a):
        nd = a.ndim
        return pl.BlockSpec(a.shape, lambda i, j: (0,) * nd)

    return pl.pallas_call(
        _mix_kernel,
        grid=(b, nt),
        in_specs=[act(d), act(gw), act(gw), act(d), act(d), act(gw), vec, vec, vec,
                  const(lng), const(ws), const(bs_full), const(wpa), const(wpb), const(wout),
                  const(n2g), const(wq), const(keys)],
        out_specs=[act(d), act(d // 2),
                   pl.BlockSpec((nhp, PEER_N_KEYS, tm), lambda i, j: (0, 0, i * nt + j))],
        out_shape=[jax.ShapeDtypeStruct((b, s, d), F32),
                   jax.ShapeDtypeStruct((b, s, d // 2), jnp.uint32),
                   jax.ShapeDtypeStruct((nhp, PEER_N_KEYS, b * s), F32)],
        compiler_params=_cparams(("arbitrary", "arbitrary")),
        name="mix_peer_scores",
    )(x, gu, gv, ga, gb, ya, g1, sh2, sc2, lng, ws, bs_full, wpa, wpb, wout, n2g, wq, keys)


def _extract_top(vals, order, payload, k):
    out_v, out_p = [], []
    for _ in range(k):
        m = jnp.max(vals, axis=0, keepdims=True)
        o = jnp.min(jnp.where(vals == m, order, jnp.inf), axis=0, keepdims=True)
        sel = order == o
        out_v.append(m)
        if payload is order:
            out_p.append(o)
        else:
            out_p.append(jnp.max(jnp.where(sel, payload, -1.0), axis=0, keepdims=True))
        vals = jnp.where(sel, -jnp.inf, vals)
    return jnp.concatenate(out_v, axis=0), jnp.concatenate(out_p, axis=0)


def _topk_kernel(st_ref, idx_ref, gate_ref, sv_ref, si_ref, se_ref, sg_ref):
    nhp = st_ref.shape[0]
    tm = st_ref.shape[2]
    k = PEER_TOPK
    kidx = lax.broadcasted_iota(jnp.int32, (PEER_N_KEYS, LANES), 0).astype(F32)
    sub = lax.broadcasted_iota(jnp.int32, (SUBLANES, LANES), 0).astype(F32)

    def lane_tile(lt, carry):
        lanes = pl.ds(pl.multiple_of(lt * LANES, LANES), LANES)

        def stage1(h, c):
            for hp in (2 * h, 2 * h + 1):
                v, i = _extract_top(st_ref[hp, :, lanes], kidx, kidx, k)
                sv_ref[hp] = v
                si_ref[hp] = i
            return c

        lax.fori_loop(0, nhp // 2, stage1, 0)

        def stage2(h2, c):
            for h in (2 * h2, 2 * h2 + 1):
                stage2_head(h)
            return c

        def stage2_head(h):
            s1, s2 = sv_ref[2 * h], sv_ref[2 * h + 1]
            i1, i2 = si_ref[2 * h] * float(PEER_N_KEYS), si_ref[2 * h + 1]
            cv, co, ce = [], [], []
            for half in range(2):
                b0 = half * SUBLANES
                cv.append(s1[0:1] + s2[b0:b0 + SUBLANES])
                co.append(sub + float(b0))
                ce.append(i1[0:1] + i2[b0:b0 + SUBLANES])
            for a in range(1, SUBLANES):
                cv.append(s1[a:a + 1] + s2[0:SUBLANES])
                co.append(sub + float(a * k))
                ce.append(i1[a:a + 1] + i2[0:SUBLANES])
            cv.append(s1[SUBLANES:k] + s2[0:1])
            co.append((sub + float(SUBLANES)) * float(k))
            ce.append(i1[SUBLANES:k] + i2[0:1])
            bv, be = _extract_top(jnp.concatenate(cv, axis=0), jnp.concatenate(co, axis=0),
                                  jnp.concatenate(ce, axis=0), k)
            ex = jnp.exp(bv - bv[0:1])
            rows = pl.ds(pl.multiple_of(h * k, k), k)
            sg_ref[rows, :] = ex / jnp.sum(ex, axis=0, keepdims=True)
            se_ref[rows, :] = be

        lax.fori_loop(0, nhp // 4, stage2, 0)

        rows = pl.ds(pl.multiple_of(lt * LANES, LANES), LANES)
        idx_ref[rows, :] = se_ref[...].T.astype(jnp.int32)
        gate_ref[rows, :] = sg_ref[...].T
        return carry

    lax.fori_loop(0, tm // LANES, lane_tile, 0)


def _topk(st, tm):
    nhp, nk, t = st.shape
    ne = (nhp // 2) * PEER_TOPK
    return pl.pallas_call(
        _topk_kernel,
        grid=(t // tm,),
        in_specs=[pl.BlockSpec((nhp, nk, tm), lambda i: (0, 0, i))],
        out_specs=[pl.BlockSpec((tm, ne), lambda i: (i, 0)),
                   pl.BlockSpec((tm, ne), lambda i: (i, 0))],
        out_shape=[jax.ShapeDtypeStruct((t, ne), jnp.int32),
                   jax.ShapeDtypeStruct((t, ne), F32)],
        scratch_shapes=[pltpu.VMEM((nhp, PEER_TOPK, LANES), F32),
                        pltpu.VMEM((nhp, PEER_TOPK, LANES), F32),
                        pltpu.VMEM((ne, LANES), F32),
                        pltpu.VMEM((ne, LANES), F32)],
        compiler_params=_cparams(("arbitrary",)),
        name="peer_topk",
    )(st)


SC_LANES = 16
SC_WINDOW = 32
SC_ROW_BUFFERS = 4
SC_TOKENS_PER_STEP = 8
SC_WSUM_CHUNKS = 8
SC_BF16_TERMS = 4


def _pack_table_kernel(t_ref, o_ref):
    half = o_ref.shape[1]
    o_ref[...] = _pack_bf16_pairs(t_ref[:, :half].astype(BF16), t_ref[:, half:].astype(BF16))


def _pack_table(t, tn):
    n, d = t.shape
    assert n % tn == 0
    return pl.pallas_call(
        _pack_table_kernel, grid=(n // tn,),
        in_specs=[pl.BlockSpec((tn, d), lambda i: (i, 0))],
        out_specs=pl.BlockSpec((tn, d // 2), lambda i: (i, 0)),
        out_shape=jax.ShapeDtypeStruct((n, d // 2), jnp.uint32),
        compiler_params=_cparams(("arbitrary",)), name="pack_table",
    )(t)


def _sc_bf16(words):
    return plsc.bitcast(words, BF16)


def _sc_phase(table, idx, operand, mode):
    t, ne = idx.shape
    hw = table.shape[1]
    d = 2 * hw
    L = SC_LANES
    win = SC_WINDOW
    nbuf = SC_ROW_BUFFERS
    ahead = nbuf - 1
    tps = SC_TOKENS_PER_STEP
    nq = SC_WSUM_CHUNKS
    nterm = SC_BF16_TERMS
    info = plsc.get_sparse_core_info()
    nc, ns = info.num_cores, info.num_subcores
    nss = t // (nc * ns * tps)
    wps = tps * ne // win
    wpt = ne // win
    blk = tps * ne
    nwin = nss * wps
    assert nss * nc * ns * tps == t and nss % 2 == 0 and wps % nbuf == 0 and wps > ahead
    assert wpt * win == ne and win % L == 0 and hw % (nq * L) == 0 and hw % (nterm * L) == 0 and win % nterm == 0
    dots = mode == "dots"
    assert dots or nbuf % wpt == 0
    op_buf = pltpu.VMEM((2, tps, hw), jnp.uint32) if dots else pltpu.VMEM((2 * blk,), jnp.int32)
    res_buf = pltpu.VMEM((2 * blk,), F32) if dots else pltpu.VMEM((2, tps, d), F32)
    out_type = jax.ShapeDtypeStruct((t * ne,), F32) if dots else jax.ShapeDtypeStruct((t, d), F32)
    scratch = [("idx", pltpu.VMEM((2 * blk,), jnp.int32)), ("op_" + mode, op_buf),
               ("rows", pltpu.VMEM((nbuf, win, hw), jnp.uint32)), ("res_" + mode, res_buf),
               ("isem", pltpu.SemaphoreType.DMA((2,))), ("psem", pltpu.SemaphoreType.DMA((2,))),
               ("gsem", pltpu.SemaphoreType.DMA((nbuf,))), ("osem", pltpu.SemaphoreType.DMA((2,)))]

    def sc_kernel(table_hbm, idx_hbm, op_hbm, out_hbm, idx_v, op_v, rows_v, res_v, isem, psem, gsem, osem):
        ss0 = (lax.axis_index("s") * nc + lax.axis_index("c")) * nss
        lane = lax.iota(jnp.int32, L)

        def flat(ref, n):
            return ref.at[pl.ds(pl.multiple_of(n * blk, blk), blk)]

        def tok_rows(ref, n):
            return ref.at[pl.ds(pl.multiple_of(n * tps, tps), tps)]

        def idx_copy(ss, slot):
            return pltpu.make_async_copy(flat(idx_hbm, ss0 + ss), flat(idx_v, slot), isem.at[slot])

        def op_copy(ss, slot):
            if dots:
                return pltpu.make_async_copy(tok_rows(op_hbm, ss0 + ss), op_v.at[slot], psem.at[slot])
            return pltpu.make_async_copy(flat(op_hbm, ss0 + ss), flat(op_v, slot), psem.at[slot])

        def out_copy(ss, slot):
            if dots:
                return pltpu.make_async_copy(flat(res_v, slot), flat(out_hbm, ss0 + ss), osem.at[slot])
            return pltpu.make_async_copy(res_v.at[slot], tok_rows(out_hbm, ss0 + ss), osem.at[slot])

        def gather(slot, hs, b):
            ix = idx_v.at[pl.ds(pl.multiple_of(slot * blk + hs * win, win), win)]
            return pltpu.make_async_copy(table_hbm.at[ix], rows_v.at[b], gsem.at[b])

        def reduce_dots(slot, hs, b):
            tok = hs // wpt

            def head(g, c):
                def chunk(jj, accs):
                    off = pl.multiple_of(jj * (nterm * L), nterm * L)
                    xs = [_sc_bf16(op_v[slot, tok, pl.ds(off + q * L, L)]) for q in range(nterm)]
                    new = []
                    for r in range(L):
                        p = _sc_bf16(rows_v[b, g * L + r, pl.ds(off, L)]) * xs[0]
                        for q in range(1, nterm):
                            p = p + _sc_bf16(rows_v[b, g * L + r, pl.ds(off + q * L, L)]) * xs[q]
                        lo, hi = plsc.unpack(p, format=plsc.PackFormat.INTERLEAVED)
                        new.append(accs[r] + lo + hi)
                    return tuple(new)

                accs = lax.fori_loop(0, hw // (nterm * L), chunk, tuple(jnp.zeros((L,), F32) for _ in range(L)))
                out = jnp.zeros((L,), F32)
                for r in range(L):
                    out = jnp.where(lane == r, jnp.sum(accs[r]), out)
                res_v[pl.ds(pl.multiple_of(slot * blk + hs * win + g * L, L), L)] = out
                return c

            lax.fori_loop(0, win // L, head, 0)

        def reduce_wsum(slot, hs, b):
            tok = hs // wpt
            wbase = slot * blk + hs * win
            first = b % wpt == 0

            def colgroup(cg, c):
                col = pl.multiple_of(cg * (nq * L), nq * L)
                if first:
                    accs = tuple(jnp.zeros((L,), F32) for _ in range(2 * nq))
                else:
                    accs = tuple(res_v[slot, tok, pl.ds(col + q * L, L)] for q in range(nq)) + \
                           tuple(res_v[slot, tok, pl.ds(hw + col + q * L, L)] for q in range(nq))

                def rowgroup(rg, accs):
                    accs = list(accs)
                    r = rg * nterm
                    ws = [_sc_bf16(plsc.load_gather(op_v, [jnp.full((L,), wbase + r + k, jnp.int32)]))
                          for k in range(nterm)]
                    for q in range(nq):
                        p = _sc_bf16(rows_v[b, r, pl.ds(col + q * L, L)]) * ws[0]
                        for k in range(1, nterm):
                            p = p + _sc_bf16(rows_v[b, r + k, pl.ds(col + q * L, L)]) * ws[k]
                        lo, hi = plsc.unpack(p, format=plsc.PackFormat.INTERLEAVED)
                        accs[q] = accs[q] + lo
                        accs[nq + q] = accs[nq + q] + hi
                    return tuple(accs)

                accs = lax.fori_loop(0, win // nterm, rowgroup, accs)
                for q in range(nq):
                    res_v[slot, tok, pl.ds(col + q * L, L)] = accs[q]
                    res_v[slot, tok, pl.ds(hw + col + q * L, L)] = accs[nq + q]
                return c

            lax.fori_loop(0, hw // (nq * L), colgroup, 0)

        idx_copy(0, 0).start()
        op_copy(0, 0).start()
        idx_copy(1, 1).start()
        op_copy(1, 1).start()
        idx_copy(0, 0).wait()
        op_copy(0, 0).wait()
        for a in range(ahead):
            gather(0, a, a).start()

        @pl.loop(0, nwin, step=nbuf)
        def _(i0):
            for b in range(nbuf):
                i = i0 + b
                ss = i // wps
                hs = i % wps
                slot = ss % 2
                gather(slot, hs, b).wait()
                nb = (b + ahead) % nbuf

                @pl.when(hs + ahead < wps)
                def _():
                    gather(slot, hs + ahead, nb).start()

                @pl.when(jnp.logical_and(hs + ahead >= wps, ss + 1 < nss))
                def _():
                    @pl.when(hs + ahead == wps)
                    def _():
                        idx_copy(ss + 1, 1 - slot).wait()
                        op_copy(ss + 1, 1 - slot).wait()

                    gather(1 - slot, hs + ahead - wps, nb).start()

                @pl.when(jnp.logical_and(hs == 0, ss >= 2))
                def _():
                    out_copy(ss - 2, slot).wait()

                if dots:
                    reduce_dots(slot, hs, b)
                else:
                    reduce_wsum(slot, hs, b)

                @pl.when(hs + 1 == wps)
                def _():
                    out_copy(ss, slot).start()

                    @pl.when(ss + 2 < nss)
                    def _():
                        idx_copy(ss + 2, slot).start()
                        op_copy(ss + 2, slot).start()

        out_copy(nss - 2, 0).wait()
        out_copy(nss - 1, 1).wait()

    op = operand if dots else operand.reshape(t * ne)
    return sc_kernel, scratch, out_type, (table, idx.reshape(t * ne), op)


def _sc_call(*phases):
    parts = [_sc_phase(table, idx, operand, mode) for mode, table, idx, operand in phases]
    names, types = [], []
    for _, scratch, _, _ in parts:
        for name, ty in scratch:
            if name not in names:
                names.append(name)
                types.append(ty)
    n = len(parts)

    def fused(*refs):
        ins, outs, scr = refs[:3 * n], refs[3 * n:4 * n], dict(zip(names, refs[4 * n:]))
        for i, (body, scratch, _, _) in enumerate(parts):
            body(*ins[3 * i:3 * i + 3], outs[i], *[scr[name] for name, _ in scratch])

    call = pl.kernel(
        fused, mesh=plsc.VectorSubcoreMesh(core_axis_name="c", subcore_axis_name="s"),
        out_type=tuple(p[2] for p in parts), scratch_types=types,
        compiler_params=pltpu.CompilerParams(needs_layout_passes=False),
        name="peer_" + "_".join(m for m, _, _, _ in phases),
    )
    return call(*[a for p in parts for a in p[3]])


def _peer_weight_kernel(act_ref, gate_ref, w_ref):
    w = (jax.nn.gelu(act_ref[...]) * gate_ref[...]).astype(BF16)
    w_ref[...] = pltpu.bitcast(_pack_bf16_pairs(w, w), jnp.int32)


def _peer_weight(act, gate, tm):
    t, ne = act.shape
    tm = min(tm, t)
    assert t % tm == 0
    spec = pl.BlockSpec((tm, ne), lambda i: (i, 0))
    return pl.pallas_call(
        _peer_weight_kernel, grid=(t // tm,), in_specs=[spec, spec], out_specs=spec,
        out_shape=jax.ShapeDtypeStruct((t, ne), jnp.int32),
        compiler_params=_cparams(("arbitrary",)), name="peer_weight",
    )(act, gate)


def _final_kernel(x1_ref, p_ref, g2_ref, fg_ref, prev_ref, o_ref):
    del prev_ref
    x2 = x1_ref[0] + g2_ref[0] * p_ref[0]
    ms = jnp.mean(x2 * x2, axis=-1, keepdims=True)
    o_ref[0] = x2 * lax.rsqrt(ms + EPS) * fg_ref[...]


def _final(x1, peer, g2, fg, tm, out, b0):
    bc, s, d = x1.shape
    act = pl.BlockSpec((1, tm, d), lambda i, j: (i, j, 0))
    return pl.pallas_call(
        _final_kernel, grid=(bc, s // tm),
        in_specs=[act, act, pl.BlockSpec((1, 1, d), lambda i, j: (i, 0, 0)),
                  pl.BlockSpec((1, d), lambda i, j: (0, 0)),
                  pl.BlockSpec(memory_space=pl.ANY)],
        out_specs=pl.BlockSpec((1, tm, d), lambda i, j: (i + b0, j, 0)),
        out_shape=jax.ShapeDtypeStruct(out.shape, F32),
        input_output_aliases={4: 0},
        compiler_params=_cparams(("arbitrary", "arbitrary")), name="final_norm",
    )(x1, peer, g2, fg, out)


def kernel(x, c, ctx, c_ctx, ada_w, ada_b, norm1_g, norm2_g, w_in, na_rpb, gm_ln_g, gm_ws, gm_bs,
           w_proj_a, w_proj_b, w_out, peer_wq, peer_keys, peer_u, peer_v, final_g):
    b, s, d = x.shape
    naw = NA_HEADS * HEAD_DIM
    gw = gm_ln_g.shape[1]
    layer = 0

    pad = (-(b + 1)) % SUBLANES
    c_all = jnp.concatenate([c, c_ctx[None, :], jnp.zeros((pad, d), F32)], axis=0)
    mod = _adaln(c_all, ada_w[layer], ada_b[layer])
    sh1, sc1, g1, sh2, sc2, g2 = [mod[:b, i * d:(i + 1) * d].reshape(b, 1, d) for i in range(ADA_CHUNKS)]
    csh1 = jnp.broadcast_to(mod[b, 0:d].reshape(1, 1, d), (b, 1, d))
    csc1 = jnp.broadcast_to(mod[b, d:2 * d].reshape(1, 1, d), (b, 1, d))

    w = w_in[layer].astype(BF16)
    w_kv = w[:, naw:3 * naw]
    n1g = norm1_g[layer].reshape(1, d)
    bias = _bias_table(na_rpb[layer])
    bs_full = jnp.repeat(gm_bs[layer].T, gw // GM_GROUPS, axis=1)
    keys = peer_keys[layer].reshape(2 * PEER_HEADS, PEER_N_KEYS, PEER_HALF).astype(BF16)
    lng = gm_ln_g[layer].reshape(1, gw)
    ws = gm_ws[layer].astype(BF16)
    wpa, wpb, wout = w_proj_a[layer].astype(BF16), w_proj_b[layer].astype(BF16), w_out[layer].astype(BF16)
    n2g = norm2_g[layer].reshape(1, d)
    wq = peer_wq[layer].astype(BF16)
    tab_u, tab_v = _pack_table(peer_u[layer], 1024), _pack_table(peer_v[layer], 1024)
    fg = final_g.reshape(1, d)
    ne = PEER_HEADS * PEER_TOPK

    assert sum(BATCH_CHUNKS) == b
    out = jnp.zeros((b, s, d), F32)

    def finish(out, pending, peer):
        idx, wgt, x1, g2c, boff = pending
        return _final(x1, peer.reshape(x1.shape), g2c, fg, 512, out, boff)

    pending = None
    b0 = 0
    for bc in BATCH_CHUNKS:
        sl = slice(b0, b0 + bc)
        boff = b0
        b0 += bc
        t = bc * s
        xc = x[sl]
        q, k, v, gu, gv, ga, gb = _norm_proj(xc, n1g, sh1[sl], sc1[sl], w, (naw, naw, naw, gw, gw, d, d), 512)
        k_c, v_c = _norm_proj(ctx[sl], n1g, csh1[sl], csc1[sl], w_kv, (naw, naw), ctx.shape[1])
        y_a = _attention(q, k, v, k_c, v_c, bias)
        x1, h2, st = _mix(xc, gu, gv, ga, gb, y_a, g1[sl], sh2[sl], sc2[sl], lng, ws, bs_full,
                          wpa, wpb, wout, n2g, wq, keys, 256)
        idx, gate = _topk(st, 512)
        dots = ("dots", tab_u, idx, h2.reshape(t, d // 2))
        if pending is None:
            act, = _sc_call(dots)
        else:
            peer, act = _sc_call(("wsum", tab_v) + pending[:2], dots)
            out = finish(out, pending, peer)
        wgt = _peer_weight(act.reshape(t, ne), gate, 2048)
        pending = (idx, wgt, x1, g2[sl], boff)
    peer, = _sc_call(("wsum", tab_v) + pending[:2])
    return finish(out, pending, peer)
```

```python
import functools

import jax
import jax.numpy as jnp
from jax import lax
from jax.experimental import pallas as pl
from jax.experimental.pallas import tpu as pltpu
from jax.experimental.pallas import tpu_sc as plsc

F32 = jnp.float32
BF16 = jnp.bfloat16

GRID_W = 64
NA_HEADS = 8
HEAD_DIM = 64
NA_WIN_ROWS = 8
NA_WIN_COLS = 16
GM_GROUPS = 8
GM_CHUNK = 128
PEER_HEADS = 8
PEER_N_KEYS = 128
PEER_TOPK = 16
PEER_HALF = 128
ADA_CHUNKS = 6
EPS = 1e-6
NEG_INF = -1e30

LANES = 128
SUBLANES = 8
VMEM_LIMIT = 56 * 1024 * 1024
ATTN_ROWS_PER_STEP = 2
BATCH_CHUNKS = (1, 1, 1, 2, 2, 3, 3, 3)


def _dot(a, b):
    return lax.dot_general(a, b, (((1,), (0,)), ((), ())), preferred_element_type=F32)


def _dot_nt(a, b):
    return lax.dot_general(a, b, (((1,), (1,)), ((), ())), preferred_element_type=F32)


def _cparams(sem):
    return pltpu.CompilerParams(dimension_semantics=sem, vmem_limit_bytes=VMEM_LIMIT)


def _pack_bf16_pairs(lo, hi):
    lo_bits = pltpu.bitcast(lo.astype(F32), jnp.uint32) >> 16
    hi_bits = pltpu.bitcast(hi.astype(F32), jnp.uint32) & jnp.uint32(0xFFFF0000)
    return lo_bits | hi_bits


def _adaln_kernel(c_ref, w_ref, b_ref, o_ref):
    c = c_ref[...]
    s = c * jax.nn.sigmoid(c)
    o_ref[...] = lax.dot_general(s, w_ref[...], (((1,), (0,)), ((), ())),
                                 precision=lax.Precision.HIGHEST,
                                 preferred_element_type=F32) + b_ref[...]


def _adaln(c_all, w, b):
    m, d = c_all.shape
    n = w.shape[1]
    tn = 1024
    return pl.pallas_call(
        _adaln_kernel,
        grid=(n // tn,),
        in_specs=[pl.BlockSpec((m, d), lambda j: (0, 0)),
                  pl.BlockSpec((d, tn), lambda j: (0, j)),
                  pl.BlockSpec((1, tn), lambda j: (0, j))],
        out_specs=pl.BlockSpec((m, tn), lambda j: (0, j)),
        out_shape=jax.ShapeDtypeStruct((m, n), F32),
        compiler_params=_cparams(("arbitrary",)),
        name="adaln",
    )(c_all, w, b.reshape(1, n))


def _norm_proj_kernel(widths, x_ref, g_ref, sh_ref, sc_ref, w_ref, *o_refs):
    x = x_ref[0]
    ms = jnp.mean(x * x, axis=-1, keepdims=True)
    y = x * lax.rsqrt(ms + EPS) * g_ref[...]
    h = (y * (1.0 + sc_ref[0]) + sh_ref[0]).astype(BF16)
    off = 0
    for o_ref, wd in zip(o_refs, widths):
        o_ref[0] = _dot(h, w_ref[:, off:off + wd]).astype(o_ref.dtype)
        off += wd


def _norm_proj(x, gain, shift, scale, w, widths, tm):
    b, s, d = x.shape
    n = w.shape[1]
    assert sum(widths) == n and s % tm == 0
    vec = pl.BlockSpec((1, 1, d), lambda i, j: (i, 0, 0))
    return pl.pallas_call(
        functools.partial(_norm_proj_kernel, widths),
        grid=(b, s // tm),
        in_specs=[pl.BlockSpec((1, tm, d), lambda i, j: (i, j, 0)),
                  pl.BlockSpec((1, d), lambda i, j: (0, 0)),
                  vec, vec,
                  pl.BlockSpec((d, n), lambda i, j: (0, 0))],
        out_specs=[pl.BlockSpec((1, tm, wd), lambda i, j: (i, j, 0)) for wd in widths],
        out_shape=[jax.ShapeDtypeStruct((b, s, wd), BF16) for wd in widths],
        compiler_params=_cparams(("arbitrary", "arbitrary")),
        name="norm_proj",
    )(x, gain, shift, scale, w)


def _bias_table_kernel(rpb_ref, o_ref):
    h = pl.program_id(0)
    q = lax.broadcasted_iota(jnp.int32, (GRID_W, GRID_W), 0)
    kc = lax.broadcasted_iota(jnp.int32, (GRID_W, GRID_W), 1)
    dc = jnp.clip(kc - q + NA_WIN_COLS - 1, 0, 2 * NA_WIN_COLS - 2)
    cs = jnp.clip(q - NA_WIN_COLS // 2, 0, GRID_W - NA_WIN_COLS)
    col_in = (kc >= cs) & (kc < cs + NA_WIN_COLS)
    n_dc = 2 * NA_WIN_COLS - 1
    n_dr = 2 * NA_WIN_ROWS - 1
    for dr in range(n_dr):
        t = jnp.zeros((GRID_W, GRID_W), F32)
        for c in range(n_dc):
            t = jnp.where(dc == c, rpb_ref[h * n_dr + dr, c], t)
        t = jnp.where(col_in, t, NEG_INF)
        for d0 in range(NA_WIN_ROWS):
            j = dr - d0
            if 0 <= j < NA_WIN_ROWS:
                o_ref[0, d0, :, j * GRID_W:(j + 1) * GRID_W] = t


def _bias_table(rpb):
    nh, n_dr, n_dc = rpb.shape
    band = NA_WIN_ROWS * GRID_W
    return pl.pallas_call(
        _bias_table_kernel,
        grid=(nh,),
        in_specs=[pl.BlockSpec(memory_space=pltpu.SMEM)],
        out_specs=pl.BlockSpec((1, NA_WIN_ROWS, GRID_W, band), lambda h: (h, 0, 0, 0)),
        out_shape=jax.ShapeDtypeStruct((nh, NA_WIN_ROWS, GRID_W, band), F32),
        compiler_params=_cparams(("arbitrary",)),
        name="bias_table",
    )(rpb.reshape(nh * n_dr, n_dc))


def _attn_kernel(rows, q_ref, k_ref, v_ref, kc_ref, vc_ref, bias_ref, o_ref, s_scr, p_scr):
    for rr in range(ATTN_ROWS_PER_STEP):
        _attn_row(rows, pl.program_id(1) * ATTN_ROWS_PER_STEP + rr, slice(rr * GRID_W, (rr + 1) * GRID_W),
                  q_ref, k_ref, v_ref, kc_ref, vc_ref, bias_ref, o_ref, s_scr, p_scr)


def _attn_row(rows, r, qrows, q_ref, k_ref, v_ref, kc_ref, vc_ref, bias_ref, o_ref, s_scr, p_scr):
    rs = jnp.clip(r - NA_WIN_ROWS // 2, 0, rows - NA_WIN_ROWS)
    d0 = rs - r + NA_WIN_ROWS - 1
    band = NA_WIN_ROWS * GRID_W
    start = pl.multiple_of(rs * GRID_W, GRID_W)
    scale = HEAD_DIM ** -0.5
    lane = lax.broadcasted_iota(jnp.int32, (GRID_W, LANES), 1)
    hpg = LANES // HEAD_DIM
    groups = NA_HEADS // hpg

    def mine(hh):
        return (lane >= hh * HEAD_DIM) & (lane < (hh + 1) * HEAD_DIM)

    for hp in range(groups):
        sl = slice(hp * LANES, (hp + 1) * LANES)
        q2 = q_ref[0, qrows, sl]
        kb = k_ref[0, pl.ds(start, band), sl]
        kc = kc_ref[0, :, sl]
        for hh in range(hpg):
            h = hp * hpg + hh
            qh = jnp.where(mine(hh), q2, jnp.zeros_like(q2))
            s_scr[h, :, :band] = _dot_nt(qh, kb) * scale + bias_ref[h, d0]
            s_scr[h, :, band:] = _dot_nt(qh, kc) * scale
    dens = []
    for h in range(NA_HEADS):
        s = s_scr[h]
        p = jnp.exp(s - jnp.max(s, axis=-1, keepdims=True))
        dens.append(jnp.sum(p, axis=-1, keepdims=True))
        p_scr[h] = p.astype(BF16)
    for hp in range(groups):
        sl = slice(hp * LANES, (hp + 1) * LANES)
        vb = v_ref[0, pl.ds(start, band), sl]
        vc = vc_ref[0, :, sl]
        halves = []
        for hh in range(hpg):
            h = hp * hpg + hh
            o = _dot(p_scr[h, :, :band], vb) + _dot(p_scr[h, :, band:], vc)
            halves.append(jnp.where(mine(hh), o / dens[h], 0.0))
        o_ref[0, qrows, sl] = sum(halves).astype(o_ref.dtype)


def _attention(q, k, v, kc, vc, bias):
    b, s, w = q.shape
    rows = s // GRID_W
    c = kc.shape[1]
    full = lambda n: pl.BlockSpec((1, n, w), lambda i, j: (i, 0, 0))
    return pl.pallas_call(
        functools.partial(_attn_kernel, rows),
        grid=(b, rows // ATTN_ROWS_PER_STEP),
        in_specs=[pl.BlockSpec((1, ATTN_ROWS_PER_STEP * GRID_W, w), lambda i, j: (i, j, 0)),
                  full(s), full(s), full(c), full(c),
                  pl.BlockSpec(bias.shape, lambda i, j: (0, 0, 0, 0))],
        out_specs=pl.BlockSpec((1, ATTN_ROWS_PER_STEP * GRID_W, w), lambda i, j: (i, j, 0)),
        out_shape=jax.ShapeDtypeStruct((b, s, w), BF16),
        scratch_shapes=[pltpu.VMEM((NA_HEADS, GRID_W, NA_WIN_ROWS * GRID_W + c), F32),
                        pltpu.VMEM((NA_HEADS, GRID_W, NA_WIN_ROWS * GRID_W + c), BF16)],
        compiler_params=_cparams(("arbitrary", "arbitrary")),
        name="nbr_attention",
    )(q, k, v, kc, vc, bias)


def _mix_kernel(x_ref, gu_ref, gv_ref, ga_ref, gb_ref, ya_ref, g1_ref, sh2_ref, sc2_ref,
                lng_ref, ws_ref, bs_ref, wpa_ref, wpb_ref, wout_ref, n2g_ref, wq_ref, keys_ref,
                x1_ref, h2_ref, st_ref):
    tm = x_ref.shape[1]
    gw = gu_ref.shape[2]
    u = jax.nn.gelu(gu_ref[0].astype(F32))
    t = jax.nn.gelu(gv_ref[0].astype(F32))
    tc = t - jnp.mean(t, axis=-1, keepdims=True)
    vn = tc * lax.rsqrt(jnp.mean(tc * tc, axis=-1, keepdims=True) + EPS) * lng_ref[...]
    vnb = vn.astype(BF16)
    lane = lax.broadcasted_iota(jnp.int32, (GM_CHUNK, LANES), 1)
    gd = gw // GM_GROUPS
    chunks = []
    for c in range(tm // GM_CHUNK):
        pairs = []
        for gp in range(gw // LANES):
            vp = vnb[c * GM_CHUNK:(c + 1) * GM_CHUNK, gp * LANES:(gp + 1) * LANES]
            r0 = _dot(ws_ref[2 * gp], vp)
            r1 = _dot(ws_ref[2 * gp + 1], vp)
            pairs.append(jnp.where(lane < gd, r0, r1))
        chunks.append(jnp.concatenate(pairs, axis=1) + bs_ref[...])
    mixed = jnp.concatenate(chunks, axis=0)
    yb = (u * mixed).astype(BF16)
    pa = _dot(ya_ref[0], wpa_ref[...])
    pb = _dot(yb, wpb_ref[...])
    m = jax.nn.sigmoid(ga_ref[0].astype(F32)) * pa + jax.nn.sigmoid(gb_ref[0].astype(F32)) * pb
    out = _dot(m.astype(BF16), wout_ref[...])
    x1 = x_ref[0] + g1_ref[0] * out
    x1_ref[0] = x1
    ms = jnp.mean(x1 * x1, axis=-1, keepdims=True)
    h2 = x1 * lax.rsqrt(ms + EPS) * n2g_ref[...]
    h2 = (h2 * (1.0 + sc2_ref[0]) + sh2_ref[0]).astype(BF16)
    half = h2.shape[1] // 2
    h2_ref[0] = _pack_bf16_pairs(h2[:, :half], h2[:, half:])
    qp = _dot(h2, wq_ref[...]).astype(BF16)
    for hp in range(keys_ref.shape[0]):
        st_ref[hp] = _dot_nt(keys_ref[hp], qp[:, hp * PEER_HALF:(hp + 1) * PEER_HALF])


def _mix(x, gu, gv, ga, gb, ya, g1, sh2, sc2, lng, ws, bs_full, wpa, wpb, wout, n2g, wq, keys, tm):
    b, s, d = x.shape
    gw = gu.shape[2]
    nt = s // tm
    nhp = keys.shape[0]
    act = lambda w: pl.BlockSpec((1, tm, w), lambda i, j: (i, j, 0))
    vec = pl.BlockSpec((1, 1, d), lambda i, j: (i, 0, 0))

    def const(a):
        nd = a.ndim
        return pl.BlockSpec(a.shape, lambda i, j: (0,) * nd)

    return pl.pallas_call(
        _mix_kernel,
        grid=(b, nt),
        in_specs=[act(d), act(gw), act(gw), act(d), act(d), act(gw), vec, vec, vec,
                  const(lng), const(ws), const(bs_full), const(wpa), const(wpb), const(wout),
                  const(n2g), const(wq), const(keys)],
        out_specs=[act(d), act(d // 2),
                   pl.BlockSpec((nhp, PEER_N_KEYS, tm), lambda i, j: (0, 0, i * nt + j))],
        out_shape=[jax.ShapeDtypeStruct((b, s, d), F32),
                   jax.ShapeDtypeStruct((b, s, d // 2), jnp.uint32),
                   jax.ShapeDtypeStruct((nhp, PEER_N_KEYS, b * s), F32)],
        compiler_params=_cparams(("arbitrary", "arbitrary")),
        name="mix_peer_scores",
    )(x, gu, gv, ga, gb, ya, g1, sh2, sc2, lng, ws, bs_full, wpa, wpb, wout, n2g, wq, keys)


def _extract_top(vals, order, payload, k):
    out_v, out_p = [], []
    for _ in range(k):
        m = jnp.max(vals, axis=0, keepdims=True)
        o = jnp.min(jnp.where(vals == m, order, jnp.inf), axis=0, keepdims=True)
        sel = order == o
        out_v.append(m)
        if payload is order:
            out_p.append(o)
        else:
            out_p.append(jnp.max(jnp.where(sel, payload, -1.0), axis=0, keepdims=True))
        vals = jnp.where(sel, -jnp.inf, vals)
    return jnp.concatenate(out_v, axis=0), jnp.concatenate(out_p, axis=0)


def _topk_kernel(st_ref, idx_ref, gate_ref, sv_ref, si_ref, se_ref, sg_ref):
    nhp = st_ref.shape[0]
    tm = st_ref.shape[2]
    k = PEER_TOPK
    kidx = lax.broadcasted_iota(jnp.int32, (PEER_N_KEYS, LANES), 0).astype(F32)
    sub = lax.broadcasted_iota(jnp.int32, (SUBLANES, LANES), 0).astype(F32)

    def lane_tile(lt, carry):
        lanes = pl.ds(pl.multiple_of(lt * LANES, LANES), LANES)

        def stage1(h, c):
            for hp in (4 * h, 4 * h + 1, 4 * h + 2, 4 * h + 3):
                v, i = _extract_top(st_ref[hp, :, lanes], kidx, kidx, k)
                sv_ref[hp] = v
                si_ref[hp] = i
            return c

        lax.fori_loop(0, nhp // 4, stage1, 0)

        def stage2(h2, c):
            for h in (2 * h2, 2 * h2 + 1):
                stage2_head(h)
            return c

        def stage2_head(h):
            s1, s2 = sv_ref[2 * h], sv_ref[2 * h + 1]
            i1, i2 = si_ref[2 * h] * float(PEER_N_KEYS), si_ref[2 * h + 1]
            cv, co, ce = [], [], []
            for half in range(2):
                b0 = half * SUBLANES
                cv.append(s1[0:1] + s2[b0:b0 + SUBLANES])
                co.append(sub + float(b0))
                ce.append(i1[0:1] + i2[b0:b0 + SUBLANES])
            for a in range(1, SUBLANES):
                cv.append(s1[a:a + 1] + s2[0:SUBLANES])
                co.append(sub + float(a * k))
                ce.append(i1[a:a + 1] + i2[0:SUBLANES])
            cv.append(s1[SUBLANES:k] + s2[0:1])
            co.append((sub + float(SUBLANES)) * float(k))
            ce.append(i1[SUBLANES:k] + i2[0:1])
            bv, be = _extract_top(jnp.concatenate(cv, axis=0), jnp.concatenate(co, axis=0),
                                  jnp.concatenate(ce, axis=0), k)
            ex = jnp.exp(bv - bv[0:1])
            rows = pl.ds(pl.multiple_of(h * k, k), k)
            sg_ref[rows, :] = ex / jnp.sum(ex, axis=0, keepdims=True)
            se_ref[rows, :] = be

        lax.fori_loop(0, nhp // 4, stage2, 0)

        rows = pl.ds(pl.multiple_of(lt * LANES, LANES), LANES)
        idx_ref[rows, :] = se_ref[...].T.astype(jnp.int32)
        gate_ref[rows, :] = sg_ref[...].T
        return carry

    lax.fori_loop(0, tm // LANES, lane_tile, 0)


def _topk(st, tm):
    nhp, nk, t = st.shape
    ne = (nhp // 2) * PEER_TOPK
    return pl.pallas_call(
        _topk_kernel,
        grid=(t // tm,),
        in_specs=[pl.BlockSpec((nhp, nk, tm), lambda i: (0, 0, i))],
        out_specs=[pl.BlockSpec((tm, ne), lambda i: (i, 0)),
                   pl.BlockSpec((tm, ne), lambda i: (i, 0))],
        out_shape=[jax.ShapeDtypeStruct((t, ne), jnp.int32),
                   jax.ShapeDtypeStruct((t, ne), F32)],
        scratch_shapes=[pltpu.VMEM((nhp, PEER_TOPK, LANES), F32),
                        pltpu.VMEM((nhp, PEER_TOPK, LANES), F32),
                        pltpu.VMEM((ne, LANES), F32),
                        pltpu.VMEM((ne, LANES), F32)],
        compiler_params=_cparams(("arbitrary",)),
        name="peer_topk",
    )(st)


SC_LANES = 16
SC_WINDOW = 32
SC_ROW_BUFFERS = 4
SC_TOKENS_PER_STEP = 8
SC_WSUM_CHUNKS = 8
SC_BF16_TERMS = 4


def _pack_table_kernel(t_ref, o_ref):
    half = o_ref.shape[1]
    o_ref[...] = _pack_bf16_pairs(t_ref[:, :half].astype(BF16), t_ref[:, half:].astype(BF16))


def _pack_table(t, tn):
    n, d = t.shape
    assert n % tn == 0
    return pl.pallas_call(
        _pack_table_kernel, grid=(n // tn,),
        in_specs=[pl.BlockSpec((tn, d), lambda i: (i, 0))],
        out_specs=pl.BlockSpec((tn, d // 2), lambda i: (i, 0)),
        out_shape=jax.ShapeDtypeStruct((n, d // 2), jnp.uint32),
        compiler_params=_cparams(("arbitrary",)), name="pack_table",
    )(t)


def _sc_bf16(words):
    return plsc.bitcast(words, BF16)


def _sc_phase(table, idx, operand, mode):
    t, ne = idx.shape
    hw = table.shape[1]
    d = 2 * hw
    L = SC_LANES
    win = SC_WINDOW
    nbuf = SC_ROW_BUFFERS
    ahead = nbuf - 1
    tps = SC_TOKENS_PER_STEP
    nq = SC_WSUM_CHUNKS
    nterm = SC_BF16_TERMS
    info = plsc.get_sparse_core_info()
    nc, ns = info.num_cores, info.num_subcores
    nss = t // (nc * ns * tps)
    wps = tps * ne // win
    wpt = ne // win
    blk = tps * ne
    nwin = nss * wps
    assert nss * nc * ns * tps == t and nss % 2 == 0 and wps % nbuf == 0 and wps > ahead
    assert wpt * win == ne and win % L == 0 and hw % (nq * L) == 0 and hw % (nterm * L) == 0 and win % nterm == 0
    dots = mode == "dots"
    assert dots or nbuf % wpt == 0
    op_buf = pltpu.VMEM((2, tps, hw), jnp.uint32) if dots else pltpu.VMEM((2 * blk,), jnp.int32)
    res_buf = pltpu.VMEM((2 * blk,), F32) if dots else pltpu.VMEM((2, tps, d), F32)
    out_type = jax.ShapeDtypeStruct((t * ne,), F32) if dots else jax.ShapeDtypeStruct((t, d), F32)
    scratch = [("idx", pltpu.VMEM((2 * blk,), jnp.int32)), ("op_" + mode, op_buf),
               ("rows", pltpu.VMEM((nbuf, win, hw), jnp.uint32)), ("res_" + mode, res_buf),
               ("isem", pltpu.SemaphoreType.DMA((2,))), ("psem", pltpu.SemaphoreType.DMA((2,))),
               ("gsem", pltpu.SemaphoreType.DMA((nbuf,))), ("osem", pltpu.SemaphoreType.DMA((2,)))]

    def sc_kernel(table_hbm, idx_hbm, op_hbm, out_hbm, idx_v, op_v, rows_v, res_v, isem, psem, gsem, osem):
        ss0 = (lax.axis_index("s") * nc + lax.axis_index("c")) * nss
        lane = lax.iota(jnp.int32, L)

        def flat(ref, n):
            return ref.at[pl.ds(pl.multiple_of(n * blk, blk), blk)]

        def tok_rows(ref, n):
            return ref.at[pl.ds(pl.multiple_of(n * tps, tps), tps)]

        def idx_copy(ss, slot):
            return pltpu.make_async_copy(flat(idx_hbm, ss0 + ss), flat(idx_v, slot), isem.at[slot])

        def op_copy(ss, slot):
            if dots:
                return pltpu.make_async_copy(tok_rows(op_hbm, ss0 + ss), op_v.at[slot], psem.at[slot])
            return pltpu.make_async_copy(flat(op_hbm, ss0 + ss), flat(op_v, slot), psem.at[slot])

        def out_copy(ss, slot):
            if dots:
                return pltpu.make_async_copy(flat(res_v, slot), flat(out_hbm, ss0 + ss), osem.at[slot])
            return pltpu.make_async_copy(res_v.at[slot], tok_rows(out_hbm, ss0 + ss), osem.at[slot])

        def gather(slot, hs, b):
            ix = idx_v.at[pl.ds(pl.multiple_of(slot * blk + hs * win, win), win)]
            return pltpu.make_async_copy(table_hbm.at[ix], rows_v.at[b], gsem.at[b])

        def reduce_dots(slot, hs, b):
            tok = hs // wpt

            def head(g, c):
                def chunk(jj, accs):
                    off = pl.multiple_of(jj * (nterm * L), nterm * L)
                    xs = [_sc_bf16(op_v[slot, tok, pl.ds(off + q * L, L)]) for q in range(nterm)]
                    new = []
                    for r in range(L):
                        p = _sc_bf16(rows_v[b, g * L + r, pl.ds(off, L)]) * xs[0]
                        for q in range(1, nterm):
                            p = p + _sc_bf16(rows_v[b, g * L + r, pl.ds(off + q * L, L)]) * xs[q]
                        lo, hi = plsc.unpack(p, format=plsc.PackFormat.INTERLEAVED)
                        new.append(accs[r] + lo + hi)
                    return tuple(new)

                accs = lax.fori_loop(0, hw // (nterm * L), chunk, tuple(jnp.zeros((L,), F32) for _ in range(L)))
                out = jnp.zeros((L,), F32)
                for r in range(L):
                    out = jnp.where(lane == r, jnp.sum(accs[r]), out)
                res_v[pl.ds(pl.multiple_of(slot * blk + hs * win + g * L, L), L)] = out
                return c

            lax.fori_loop(0, win // L, head, 0)

        def reduce_wsum(slot, hs, b):
            tok = hs // wpt
            wbase = slot * blk + hs * win
            first = b % wpt == 0

            def colgroup(cg, c):
                col = pl.multiple_of(cg * (nq * L), nq * L)
                if first:
                    accs = tuple(jnp.zeros((L,), F32) for _ in range(2 * nq))
                else:
                    accs = tuple(res_v[slot, tok, pl.ds(col + q * L, L)] for q in range(nq)) + \
                           tuple(res_v[slot, tok, pl.ds(hw + col + q * L, L)] for q in range(nq))

                def rowgroup(rg, accs):
                    accs = list(accs)
                    r = rg * nterm
                    ws = [_sc_bf16(plsc.load_gather(op_v, [jnp.full((L,), wbase + r + k, jnp.int32)]))
                          for k in range(nterm)]
                    for q in range(nq):
                        p = _sc_bf16(rows_v[b, r, pl.ds(col + q * L, L)]) * ws[0]
                        for k in range(1, nterm):
                            p = p + _sc_bf16(rows_v[b, r + k, pl.ds(col + q * L, L)]) * ws[k]
                        lo, hi = plsc.unpack(p, format=plsc.PackFormat.INTERLEAVED)
                        accs[q] = accs[q] + lo
                        accs[nq + q] = accs[nq + q] + hi
                    return tuple(accs)

                accs = lax.fori_loop(0, win // nterm, rowgroup, accs)
                for q in range(nq):
                    res_v[slot, tok, pl.ds(col + q * L, L)] = accs[q]
                    res_v[slot, tok, pl.ds(hw + col + q * L, L)] = accs[nq + q]
                return c

            lax.fori_loop(0, hw // (nq * L), colgroup, 0)

        idx_copy(0, 0).start()
        op_copy(0, 0).start()
        idx_copy(1, 1).start()
        op_copy(1, 1).start()
        idx_copy(0, 0).wait()
        op_copy(0, 0).wait()
        for a in range(ahead):
            gather(0, a, a).start()

        @pl.loop(0, nwin, step=nbuf)
        def _(i0):
            for b in range(nbuf):
                i = i0 + b
                ss = i // wps
                hs = i % wps
                slot = ss % 2
                gather(slot, hs, b).wait()
                nb = (b + ahead) % nbuf

                @pl.when(hs + ahead < wps)
                def _():
                    gather(slot, hs + ahead, nb).start()

                @pl.when(jnp.logical_and(hs + ahead >= wps, ss + 1 < nss))
                def _():
                    @pl.when(hs + ahead == wps)
                    def _():
                        idx_copy(ss + 1, 1 - slot).wait()
                        op_copy(ss + 1, 1 - slot).wait()

                    gather(1 - slot, hs + ahead - wps, nb).start()

                @pl.when(jnp.logical_and(hs == 0, ss >= 2))
                def _():
                    out_copy(ss - 2, slot).wait()

                if dots:
                    reduce_dots(slot, hs, b)
                else:
                    reduce_wsum(slot, hs, b)

                @pl.when(hs + 1 == wps)
                def _():
                    out_copy(ss, slot).start()

                    @pl.when(ss + 2 < nss)
                    def _():
                        idx_copy(ss + 2, slot).start()
                        op_copy(ss + 2, slot).start()

        out_copy(nss - 2, 0).wait()
        out_copy(nss - 1, 1).wait()

    op = operand if dots else operand.reshape(t * ne)
    return sc_kernel, scratch, out_type, (table, idx.reshape(t * ne), op)


def _sc_call(*phases):
    parts = [_sc_phase(table, idx, operand, mode) for mode, table, idx, operand in phases]
    names, types = [], []
    for _, scratch, _, _ in parts:
        for name, ty in scratch:
            if name not in names:
                names.append(name)
                types.append(ty)
    n = len(parts)

    def fused(*refs):
        ins, outs, scr = refs[:3 * n], refs[3 * n:4 * n], dict(zip(names, refs[4 * n:]))
        for i, (body, scratch, _, _) in enumerate(parts):
            body(*ins[3 * i:3 * i + 3], outs[i], *[scr[name] for name, _ in scratch])

    call = pl.kernel(
        fused, mesh=plsc.VectorSubcoreMesh(core_axis_name="c", subcore_axis_name="s"),
        out_type=tuple(p[2] for p in parts), scratch_types=types,
        compiler_params=pltpu.CompilerParams(needs_layout_passes=False),
        name="peer_" + "_".join(m for m, _, _, _ in phases),
    )
    return call(*[a for p in parts for a in p[3]])


def _peer_weight_kernel(act_ref, gate_ref, w_ref):
    w = (jax.nn.gelu(act_ref[...]) * gate_ref[...]).astype(BF16)
    w_ref[...] = pltpu.bitcast(_pack_bf16_pairs(w, w), jnp.int32)


def _peer_weight(act, gate, tm):
    t, ne = act.shape
    tm = min(tm, t)
    assert t % tm == 0
    spec = pl.BlockSpec((tm, ne), lambda i: (i, 0))
    return pl.pallas_call(
        _peer_weight_kernel, grid=(t // tm,), in_specs=[spec, spec], out_specs=spec,
        out_shape=jax.ShapeDtypeStruct((t, ne), jnp.int32),
        compiler_params=_cparams(("arbitrary",)), name="peer_weight",
    )(act, gate)


def _final_kernel(x1_ref, p_ref, g2_ref, fg_ref, prev_ref, o_ref):
    del prev_ref
    x2 = x1_ref[0] + g2_ref[0] * p_ref[0]
    ms = jnp.mean(x2 * x2, axis=-1, keepdims=True)
    o_ref[0] = x2 * lax.rsqrt(ms + EPS) * fg_ref[...]


def _final(x1, peer, g2, fg, tm, out, b0):
    bc, s, d = x1.shape
    act = pl.BlockSpec((1, tm, d), lambda i, j: (i, j, 0))
    return pl.pallas_call(
        _final_kernel, grid=(bc, s // tm),
        in_specs=[act, act, pl.BlockSpec((1, 1, d), lambda i, j: (i, 0, 0)),
                  pl.BlockSpec((1, d), lambda i, j: (0, 0)),
                  pl.BlockSpec(memory_space=pl.ANY)],
        out_specs=pl.BlockSpec((1, tm, d), lambda i, j: (i + b0, j, 0)),
        out_shape=jax.ShapeDtypeStruct(out.shape, F32),
        input_output_aliases={4: 0},
        compiler_params=_cparams(("arbitrary", "arbitrary")), name="final_norm",
    )(x1, peer, g2, fg, out)


def kernel(x, c, ctx, c_ctx, ada_w, ada_b, norm1_g, norm2_g, w_in, na_rpb, gm_ln_g, gm_ws, gm_bs,
           w_proj_a, w_proj_b, w_out, peer_wq, peer_keys, peer_u, peer_v, final_g):
    b, s, d = x.shape
    naw = NA_HEADS * HEAD_DIM
    gw = gm_ln_g.shape[1]
    layer = 0

    pad = (-(b + 1)) % SUBLANES
    c_all = jnp.concatenate([c, c_ctx[None, :], jnp.zeros((pad, d), F32)], axis=0)
    mod = _adaln(c_all, ada_w[layer], ada_b[layer])
    sh1, sc1, g1, sh2, sc2, g2 = [mod[:b, i * d:(i + 1) * d].reshape(b, 1, d) for i in range(ADA_CHUNKS)]
    csh1 = jnp.broadcast_to(mod[b, 0:d].reshape(1, 1, d), (b, 1, d))
    csc1 = jnp.broadcast_to(mod[b, d:2 * d].reshape(1, 1, d), (b, 1, d))

    w = w_in[layer].astype(BF16)
    w_kv = w[:, naw:3 * naw]
    n1g = norm1_g[layer].reshape(1, d)
    bias = _bias_table(na_rpb[layer])
    bs_full = jnp.repeat(gm_bs[layer].T, gw // GM_GROUPS, axis=1)
    keys = peer_keys[layer].reshape(2 * PEER_HEADS, PEER_N_KEYS, PEER_HALF).astype(BF16)
    lng = gm_ln_g[layer].reshape(1, gw)
    ws = gm_ws[layer].astype(BF16)
    wpa, wpb, wout = w_proj_a[layer].astype(BF16), w_proj_b[layer].astype(BF16), w_out[layer].astype(BF16)
    n2g = norm2_g[layer].reshape(1, d)
    wq = peer_wq[layer].astype(BF16)
    tab_u, tab_v = _pack_table(peer_u[layer], 1024), _pack_table(peer_v[layer], 1024)
    fg = final_g.reshape(1, d)
    ne = PEER_HEADS * PEER_TOPK

    assert sum(BATCH_CHUNKS) == b
    out = jnp.zeros((b, s, d), F32)

    def finish(out, pending, peer):
        idx, wgt, x1, g2c, boff = pending
        return _final(x1, peer.reshape(x1.shape), g2c, fg, 512, out, boff)

    pending = None
    b0 = 0
    for bc in BATCH_CHUNKS:
        sl = slice(b0, b0 + bc)
        boff = b0
        b0 += bc
        t = bc * s
        xc = x[sl]
        q, k, v, gu, gv, ga, gb = _norm_proj(xc, n1g, sh1[sl], sc1[sl], w, (naw, naw, naw, gw, gw, d, d), 512)
        k_c, v_c = _norm_proj(ctx[sl], n1g, csh1[sl], csc1[sl], w_kv, (naw, naw), ctx.shape[1])
        y_a = _attention(q, k, v, k_c, v_c, bias)
        x1, h2, st = _mix(xc, gu, gv, ga, gb, y_a, g1[sl], sh2[sl], sc2[sl], lng, ws, bs_full,
                          wpa, wpb, wout, n2g, wq, keys, 256)
        idx, gate = _topk(st, 512)
        dots = ("dots", tab_u, idx, h2.reshape(t, d // 2))
        if pending is None:
            act, = _sc_call(dots)
        else:
            peer, act = _sc_call(("wsum", tab_v) + pending[:2], dots)
            out = finish(out, pending, peer)
        wgt = _peer_weight(act.reshape(t, ne), gate, 2048)
        pending = (idx, wgt, x1, g2[sl], boff)
    peer, = _sc_call(("wsum", tab_v) + pending[:2])
    return finish(out, pending, peer)
```

```python
import functools

import jax
import jax.numpy as jnp
from jax import lax
from jax.experimental import pallas as pl
from jax.experimental.pallas import tpu as pltpu
from jax.experimental.pallas import tpu_sc as plsc

F32 = jnp.float32
BF16 = jnp.bfloat16

GRID_W = 64
NA_HEADS = 8
HEAD_DIM = 64
NA_WIN_ROWS = 8
NA_WIN_COLS = 16
GM_GROUPS = 8
GM_CHUNK = 128
PEER_HEADS = 8
PEER_N_KEYS = 128
PEER_TOPK = 16
PEER_HALF = 128
ADA_CHUNKS = 6
EPS = 1e-6
NEG_INF = -1e30

LANES = 128
SUBLANES = 8
VMEM_LIMIT = 56 * 1024 * 1024
TILE_ADALN_COLS = 1024
TILE_PROJ = 512
TILE_MIX = 256
TILE_TOPK = 512
TILE_WEIGHT = 2048
TILE_FINAL = 512
TILE_PACK = 1024
ATTN_ROWS_PER_STEP = 2
BATCH_CHUNKS = (1, 1, 1, 2, 2, 3, 3, 3)


def _dot(a, b):
    return lax.dot_general(a, b, (((1,), (0,)), ((), ())), preferred_element_type=F32)


def _dot_nt(a, b):
    return lax.dot_general(a, b, (((1,), (1,)), ((), ())), preferred_element_type=F32)


def _cparams(sem):
    return pltpu.CompilerParams(dimension_semantics=sem, vmem_limit_bytes=VMEM_LIMIT)


def _pack_bf16_pairs(lo, hi):
    lo_bits = pltpu.bitcast(lo.astype(F32), jnp.uint32) >> 16
    hi_bits = pltpu.bitcast(hi.astype(F32), jnp.uint32) & jnp.uint32(0xFFFF0000)
    return lo_bits | hi_bits


def _adaln_kernel(c_ref, w_ref, b_ref, o_ref):
    c = c_ref[...]
    s = c * jax.nn.sigmoid(c)
    o_ref[...] = lax.dot_general(s, w_ref[...], (((1,), (0,)), ((), ())),
                                 precision=lax.Precision.HIGHEST,
                                 preferred_element_type=F32) + b_ref[...]


def _adaln(c_all, w, b):
    m, d = c_all.shape
    n = w.shape[1]
    tn = TILE_ADALN_COLS
    return pl.pallas_call(
        _adaln_kernel,
        grid=(n // tn,),
        in_specs=[pl.BlockSpec((m, d), lambda j: (0, 0)),
                  pl.BlockSpec((d, tn), lambda j: (0, j)),
                  pl.BlockSpec((1, tn), lambda j: (0, j))],
        out_specs=pl.BlockSpec((m, tn), lambda j: (0, j)),
        out_shape=jax.ShapeDtypeStruct((m, n), F32),
        compiler_params=_cparams(("arbitrary",)),
        name="adaln",
    )(c_all, w, b.reshape(1, n))


def _norm_proj_kernel(widths, x_ref, g_ref, sh_ref, sc_ref, w_ref, *o_refs):
    x = x_ref[0]
    ms = jnp.mean(x * x, axis=-1, keepdims=True)
    y = x * lax.rsqrt(ms + EPS) * g_ref[...]
    h = (y * (1.0 + sc_ref[0]) + sh_ref[0]).astype(BF16)
    off = 0
    for o_ref, wd in zip(o_refs, widths):
        o_ref[0] = _dot(h, w_ref[:, off:off + wd]).astype(o_ref.dtype)
        off += wd


def _norm_proj(x, gain, shift, scale, w, widths, tm):
    b, s, d = x.shape
    n = w.shape[1]
    assert sum(widths) == n and s % tm == 0
    vec = pl.BlockSpec((1, 1, d), lambda i, j: (i, 0, 0))
    return pl.pallas_call(
        functools.partial(_norm_proj_kernel, widths),
        grid=(b, s // tm),
        in_specs=[pl.BlockSpec((1, tm, d), lambda i, j: (i, j, 0)),
                  pl.BlockSpec((1, d), lambda i, j: (0, 0)),
                  vec, vec,
                  pl.BlockSpec((d, n), lambda i, j: (0, 0))],
        out_specs=[pl.BlockSpec((1, tm, wd), lambda i, j: (i, j, 0)) for wd in widths],
        out_shape=[jax.ShapeDtypeStruct((b, s, wd), BF16) for wd in widths],
        compiler_params=_cparams(("arbitrary", "arbitrary")),
        name="norm_proj",
    )(x, gain, shift, scale, w)


def _bias_table_kernel(rpb_ref, o_ref):
    h = pl.program_id(0)
    q = lax.broadcasted_iota(jnp.int32, (GRID_W, GRID_W), 0)
    kc = lax.broadcasted_iota(jnp.int32, (GRID_W, GRID_W), 1)
    dc = jnp.clip(kc - q + NA_WIN_COLS - 1, 0, 2 * NA_WIN_COLS - 2)
    cs = jnp.clip(q - NA_WIN_COLS // 2, 0, GRID_W - NA_WIN_COLS)
    col_in = (kc >= cs) & (kc < cs + NA_WIN_COLS)
    n_dc = 2 * NA_WIN_COLS - 1
    n_dr = 2 * NA_WIN_ROWS - 1
    for dr in range(n_dr):
        t = jnp.zeros((GRID_W, GRID_W), F32)
        for c in range(n_dc):
            t = jnp.where(dc == c, rpb_ref[h * n_dr + dr, c], t)
        t = jnp.where(col_in, t, NEG_INF)
        for d0 in range(NA_WIN_ROWS):
            j = dr - d0
            if 0 <= j < NA_WIN_ROWS:
                o_ref[0, d0, :, j * GRID_W:(j + 1) * GRID_W] = t


def _bias_table(rpb):
    nh, n_dr, n_dc = rpb.shape
    band = NA_WIN_ROWS * GRID_W
    return pl.pallas_call(
        _bias_table_kernel,
        grid=(nh,),
        in_specs=[pl.BlockSpec(memory_space=pltpu.SMEM)],
        out_specs=pl.BlockSpec((1, NA_WIN_ROWS, GRID_W, band), lambda h: (h, 0, 0, 0)),
        out_shape=jax.ShapeDtypeStruct((nh, NA_WIN_ROWS, GRID_W, band), F32),
        compiler_params=_cparams(("arbitrary",)),
        name="bias_table",
    )(rpb.reshape(nh * n_dr, n_dc))


def _attn_kernel(rows, q_ref, k_ref, v_ref, kc_ref, vc_ref, bias_ref, o_ref, s_scr, p_scr):
    for rr in range(ATTN_ROWS_PER_STEP):
        _attn_row(rows, pl.program_id(1) * ATTN_ROWS_PER_STEP + rr, slice(rr * GRID_W, (rr + 1) * GRID_W),
                  q_ref, k_ref, v_ref, kc_ref, vc_ref, bias_ref, o_ref, s_scr, p_scr)


def _attn_row(rows, r, qrows, q_ref, k_ref, v_ref, kc_ref, vc_ref, bias_ref, o_ref, s_scr, p_scr):
    rs = jnp.clip(r - NA_WIN_ROWS // 2, 0, rows - NA_WIN_ROWS)
    d0 = rs - r + NA_WIN_ROWS - 1
    band = NA_WIN_ROWS * GRID_W
    start = pl.multiple_of(rs * GRID_W, GRID_W)
    scale = HEAD_DIM ** -0.5
    lane = lax.broadcasted_iota(jnp.int32, (GRID_W, LANES), 1)
    hpg = LANES // HEAD_DIM
    groups = NA_HEADS // hpg

    def mine(hh):
        return (lane >= hh * HEAD_DIM) & (lane < (hh + 1) * HEAD_DIM)

    for hp in range(groups):
        sl = slice(hp * LANES, (hp + 1) * LANES)
        q2 = q_ref[0, qrows, sl]
        kb = k_ref[0, pl.ds(start, band), sl]
        kc = kc_ref[0, :, sl]
        for hh in range(hpg):
            h = hp * hpg + hh
            qh = jnp.where(mine(hh), q2, jnp.zeros_like(q2))
            s_scr[h, :, :band] = _dot_nt(qh, kb) * scale + bias_ref[h, d0]
            s_scr[h, :, band:] = _dot_nt(qh, kc) * scale
    dens = []
    for h in range(NA_HEADS):
        s = s_scr[h]
        p = jnp.exp(s - jnp.max(s, axis=-1, keepdims=True))
        dens.append(jnp.sum(p, axis=-1, keepdims=True))
        p_scr[h] = p.astype(BF16)
    for hp in range(groups):
        sl = slice(hp * LANES, (hp + 1) * LANES)
        vb = v_ref[0, pl.ds(start, band), sl]
        vc = vc_ref[0, :, sl]
        halves = []
        for hh in range(hpg):
            h = hp * hpg + hh
            o = _dot(p_scr[h, :, :band], vb) + _dot(p_scr[h, :, band:], vc)
            halves.append(jnp.where(mine(hh), o / dens[h], 0.0))
        o_ref[0, qrows, sl] = sum(halves).astype(o_ref.dtype)


def _attention(q, k, v, kc, vc, bias):
    b, s, w = q.shape
    rows = s // GRID_W
    c = kc.shape[1]
    full = lambda n: pl.BlockSpec((1, n, w), lambda i, j: (i, 0, 0))
    return pl.pallas_call(
        functools.partial(_attn_kernel, rows),
        grid=(b, rows // ATTN_ROWS_PER_STEP),
        in_specs=[pl.BlockSpec((1, ATTN_ROWS_PER_STEP * GRID_W, w), lambda i, j: (i, j, 0)),
                  full(s), full(s), full(c), full(c),
                  pl.BlockSpec(bias.shape, lambda i, j: (0, 0, 0, 0))],
        out_specs=pl.BlockSpec((1, ATTN_ROWS_PER_STEP * GRID_W, w), lambda i, j: (i, j, 0)),
        out_shape=jax.ShapeDtypeStruct((b, s, w), BF16),
        scratch_shapes=[pltpu.VMEM((NA_HEADS, GRID_W, NA_WIN_ROWS * GRID_W + c), F32),
                        pltpu.VMEM((NA_HEADS, GRID_W, NA_WIN_ROWS * GRID_W + c), BF16)],
        compiler_params=_cparams(("arbitrary", "arbitrary")),
        name="nbr_attention",
    )(q, k, v, kc, vc, bias)


def _mix_kernel(x_ref, gu_ref, gv_ref, ga_ref, gb_ref, ya_ref, g1_ref, sh2_ref, sc2_ref,
                lng_ref, ws_ref, bs_ref, wpa_ref, wpb_ref, wout_ref, n2g_ref, wq_ref, keys_ref,
                x1_ref, h2_ref, st_ref):
    tm = x_ref.shape[1]
    gw = gu_ref.shape[2]
    u = jax.nn.gelu(gu_ref[0].astype(F32))
    t = jax.nn.gelu(gv_ref[0].astype(F32))
    tc = t - jnp.mean(t, axis=-1, keepdims=True)
    vn = tc * lax.rsqrt(jnp.mean(tc * tc, axis=-1, keepdims=True) + EPS) * lng_ref[...]
    vnb = vn.astype(BF16)
    lane = lax.broadcasted_iota(jnp.int32, (GM_CHUNK, LANES), 1)
    gd = gw // GM_GROUPS
    chunks = []
    for c in range(tm // GM_CHUNK):
        pairs = []
        for gp in range(gw // LANES):
            vp = vnb[c * GM_CHUNK:(c + 1) * GM_CHUNK, gp * LANES:(gp + 1) * LANES]
            r0 = _dot(ws_ref[2 * gp], vp)
            r1 = _dot(ws_ref[2 * gp + 1], vp)
            pairs.append(jnp.where(lane < gd, r0, r1))
        chunks.append(jnp.concatenate(pairs, axis=1) + bs_ref[...])
    mixed = jnp.concatenate(chunks, axis=0)
    yb = (u * mixed).astype(BF16)
    pa = _dot(ya_ref[0], wpa_ref[...])
    pb = _dot(yb, wpb_ref[...])
    m = jax.nn.sigmoid(ga_ref[0].astype(F32)) * pa + jax.nn.sigmoid(gb_ref[0].astype(F32)) * pb
    out = _dot(m.astype(BF16), wout_ref[...])
    x1 = x_ref[0] + g1_ref[0] * out
    x1_ref[0] = x1
    ms = jnp.mean(x1 * x1, axis=-1, keepdims=True)
    h2 = x1 * lax.rsqrt(ms + EPS) * n2g_ref[...]
    h2 = (h2 * (1.0 + sc2_ref[0]) + sh2_ref[0]).astype(BF16)
    half = h2.shape[1] // 2
    h2_ref[0] = _pack_bf16_pairs(h2[:, :half], h2[:, half:])
    qp = _dot(h2, wq_ref[...]).astype(BF16)
    for hp in range(keys_ref.shape[0]):
        st_ref[hp] = _dot_nt(keys_ref[hp], qp[:, hp * PEER_HALF:(hp + 1) * PEER_HALF])


def _mix(x, gu, gv, ga, gb, ya, g1, sh2, sc2, lng, ws, bs_full, wpa, wpb, wout, n2g, wq, keys, tm):
    b, s, d = x.shape
    gw = gu.shape[2]
    nt = s // tm
    nhp = keys.shape[0]
    act = lambda w: pl.BlockSpec((1, tm, w), lambda i, j: (i, j, 0))
    vec = pl.BlockSpec((1, 1, d), lambda i, j: (i, 0, 0))

    def const(a):
        nd = a.ndim
        return pl.BlockSpec(a.shape, lambda i, j: (0,) * nd)

    return pl.pallas_call(
        _mix_kernel,
        grid=(b, nt),
        in_specs=[act(d), act(gw), act(gw), act(d), act(d), act(gw), vec, vec, vec,
                  const(lng), const(ws), const(bs_full), const(wpa), const(wpb), const(wout),
                  const(n2g), const(wq), const(keys)],
        out_specs=[act(d), act(d // 2),
                   pl.BlockSpec((nhp, PEER_N_KEYS, tm), lambda i, j: (0, 0, i * nt + j))],
        out_shape=[jax.ShapeDtypeStruct((b, s, d), F32),
                   jax.ShapeDtypeStruct((b, s, d // 2), jnp.uint32),
                   jax.ShapeDtypeStruct((nhp, PEER_N_KEYS, b * s), F32)],
        compiler_params=_cparams(("arbitrary", "arbitrary")),
        name="mix_peer_scores",
    )(x, gu, gv, ga, gb, ya, g1, sh2, sc2, lng, ws, bs_full, wpa, wpb, wout, n2g, wq, keys)


def _extract_top(vals, order, payload, k):
    out_v, out_p = [], []
    for _ in range(k):
        m = jnp.max(vals, axis=0, keepdims=True)
        o = jnp.min(jnp.where(vals == m, order, jnp.inf), axis=0, keepdims=True)
        sel = order == o
        out_v.append(m)
        if payload is order:
            out_p.append(o)
        else:
            out_p.append(jnp.max(jnp.where(sel, payload, -1.0), axis=0, keepdims=True))
        vals = jnp.where(sel, -jnp.inf, vals)
    return jnp.concatenate(out_v, axis=0), jnp.concatenate(out_p, axis=0)


def _topk_kernel(st_ref, idx_ref, gate_ref, sv_ref, si_ref, se_ref, sg_ref):
    nhp = st_ref.shape[0]
    tm = st_ref.shape[2]
    k = PEER_TOPK
    kidx = lax.broadcasted_iota(jnp.int32, (PEER_N_KEYS, LANES), 0).astype(F32)
    sub = lax.broadcasted_iota(jnp.int32, (SUBLANES, LANES), 0).astype(F32)

    def lane_tile(lt, carry):
        lanes = pl.ds(pl.multiple_of(lt * LANES, LANES), LANES)

        def stage1(h, c):
            for hp in (4 * h, 4 * h + 1, 4 * h + 2, 4 * h + 3):
                v, i = _extract_top(st_ref[hp, :, lanes], kidx, kidx, k)
                sv_ref[hp] = v
                si_ref[hp] = i
            return c

        lax.fori_loop(0, nhp // 4, stage1, 0)

        def stage2(h2, c):
            for h in (2 * h2, 2 * h2 + 1):
                stage2_head(h)
            return c

        def stage2_head(h):
            s1, s2 = sv_ref[2 * h], sv_ref[2 * h + 1]
            i1, i2 = si_ref[2 * h] * float(PEER_N_KEYS), si_ref[2 * h + 1]
            cv, co, ce = [], [], []
            for half in range(2):
                b0 = half * SUBLANES
                cv.append(s1[0:1] + s2[b0:b0 + SUBLANES])
                co.append(sub + float(b0))
                ce.append(i1[0:1] + i2[b0:b0 + SUBLANES])
            for a in range(1, SUBLANES):
                cv.append(s1[a:a + 1] + s2[0:SUBLANES])
                co.append(sub + float(a * k))
                ce.append(i1[a:a + 1] + i2[0:SUBLANES])
            cv.append(s1[SUBLANES:k] + s2[0:1])
            co.append((sub + float(SUBLANES)) * float(k))
            ce.append(i1[SUBLANES:k] + i2[0:1])
            bv, be = _extract_top(jnp.concatenate(cv, axis=0), jnp.concatenate(co, axis=0),
                                  jnp.concatenate(ce, axis=0), k)
            ex = jnp.exp(bv - bv[0:1])
            rows = pl.ds(pl.multiple_of(h * k, k), k)
            sg_ref[rows, :] = ex / jnp.sum(ex, axis=0, keepdims=True)
            se_ref[rows, :] = be

        lax.fori_loop(0, nhp // 4, stage2, 0)

        rows = pl.ds(pl.multiple_of(lt * LANES, LANES), LANES)
        idx_ref[rows, :] = se_ref[...].T.astype(jnp.int32)
        gate_ref[rows, :] = sg_ref[...].T
        return carry

    lax.fori_loop(0, tm // LANES, lane_tile, 0)


def _topk(st, tm):
    nhp, nk, t = st.shape
    ne = (nhp // 2) * PEER_TOPK
    return pl.pallas_call(
        _topk_kernel,
        grid=(t // tm,),
        in_specs=[pl.BlockSpec((nhp, nk, tm), lambda i: (0, 0, i))],
        out_specs=[pl.BlockSpec((tm, ne), lambda i: (i, 0)),
                   pl.BlockSpec((tm, ne), lambda i: (i, 0))],
        out_shape=[jax.ShapeDtypeStruct((t, ne), jnp.int32),
                   jax.ShapeDtypeStruct((t, ne), F32)],
        scratch_shapes=[pltpu.VMEM((nhp, PEER_TOPK, LANES), F32),
                        pltpu.VMEM((nhp, PEER_TOPK, LANES), F32),
                        pltpu.VMEM((ne, LANES), F32),
                        pltpu.VMEM((ne, LANES), F32)],
        compiler_params=_cparams(("arbitrary",)),
        name="peer_topk",
    )(st)


SC_LANES = 16
SC_WINDOW = 32
SC_ROW_BUFFERS = 4
SC_TOKENS_PER_STEP = 8
SC_WSUM_CHUNKS = 8
SC_BF16_TERMS = 4


def _pack_table_kernel(t_ref, o_ref):
    half = o_ref.shape[1]
    o_ref[...] = _pack_bf16_pairs(t_ref[:, :half].astype(BF16), t_ref[:, half:].astype(BF16))


def _pack_table(t, tn):
    n, d = t.shape
    assert n % tn == 0
    return pl.pallas_call(
        _pack_table_kernel, grid=(n // tn,),
        in_specs=[pl.BlockSpec((tn, d), lambda i: (i, 0))],
        out_specs=pl.BlockSpec((tn, d // 2), lambda i: (i, 0)),
        out_shape=jax.ShapeDtypeStruct((n, d // 2), jnp.uint32),
        compiler_params=_cparams(("arbitrary",)), name="pack_table",
    )(t)


def _sc_bf16(words):
    return plsc.bitcast(words, BF16)


def _sc_phase(table, idx, operand, mode):
    t, ne = idx.shape
    hw = table.shape[1]
    d = 2 * hw
    L = SC_LANES
    win = SC_WINDOW
    nbuf = SC_ROW_BUFFERS
    ahead = nbuf - 1
    tps = SC_TOKENS_PER_STEP
    nq = SC_WSUM_CHUNKS
    nterm = SC_BF16_TERMS
    info = plsc.get_sparse_core_info()
    nc, ns = info.num_cores, info.num_subcores
    nss = t // (nc * ns * tps)
    wps = tps * ne // win
    wpt = ne // win
    blk = tps * ne
    nwin = nss * wps
    assert nss * nc * ns * tps == t and nss % 2 == 0 and wps % nbuf == 0 and wps > ahead
    assert wpt * win == ne and win % L == 0 and hw % (nq * L) == 0 and hw % (nterm * L) == 0 and win % nterm == 0
    dots = mode == "dots"
    assert dots or nbuf % wpt == 0
    op_buf = pltpu.VMEM((2, tps, hw), jnp.uint32) if dots else pltpu.VMEM((2 * blk,), jnp.int32)
    res_buf = pltpu.VMEM((2 * blk,), F32) if dots else pltpu.VMEM((2, tps, d), F32)
    out_type = jax.ShapeDtypeStruct((t * ne,), F32) if dots else jax.ShapeDtypeStruct((t, d), F32)
    scratch = [("idx", pltpu.VMEM((2 * blk,), jnp.int32)), ("op_" + mode, op_buf),
               ("rows", pltpu.VMEM((nbuf, win, hw), jnp.uint32)), ("res_" + mode, res_buf),
               ("isem", pltpu.SemaphoreType.DMA((2,))), ("psem", pltpu.SemaphoreType.DMA((2,))),
               ("gsem", pltpu.SemaphoreType.DMA((nbuf,))), ("osem", pltpu.SemaphoreType.DMA((2,)))]

    def sc_kernel(table_hbm, idx_hbm, op_hbm, out_hbm, idx_v, op_v, rows_v, res_v, isem, psem, gsem, osem):
        ss0 = (lax.axis_index("s") * nc + lax.axis_index("c")) * nss
        lane = lax.iota(jnp.int32, L)

        def flat(ref, n):
            return ref.at[pl.ds(pl.multiple_of(n * blk, blk), blk)]

        def tok_rows(ref, n):
            return ref.at[pl.ds(pl.multiple_of(n * tps, tps), tps)]

        def idx_copy(ss, slot):
            return pltpu.make_async_copy(flat(idx_hbm, ss0 + ss), flat(idx_v, slot), isem.at[slot])

        def op_copy(ss, slot):
            if dots:
                return pltpu.make_async_copy(tok_rows(op_hbm, ss0 + ss), op_v.at[slot], psem.at[slot])
            return pltpu.make_async_copy(flat(op_hbm, ss0 + ss), flat(op_v, slot), psem.at[slot])

        def out_copy(ss, slot):
            if dots:
                return pltpu.make_async_copy(flat(res_v, slot), flat(out_hbm, ss0 + ss), osem.at[slot])
            return pltpu.make_async_copy(res_v.at[slot], tok_rows(out_hbm, ss0 + ss), osem.at[slot])

        def gather(slot, hs, b):
            ix = idx_v.at[pl.ds(pl.multiple_of(slot * blk + hs * win, win), win)]
            return pltpu.make_async_copy(table_hbm.at[ix], rows_v.at[b], gsem.at[b])

        def reduce_dots(slot, hs, b):
            tok = hs // wpt

            def head(g, c):
                def chunk(jj, accs):
                    off = pl.multiple_of(jj * (nterm * L), nterm * L)
                    xs = [_sc_bf16(op_v[slot, tok, pl.ds(off + q * L, L)]) for q in range(nterm)]
                    new = []
                    for r in range(L):
                        p = _sc_bf16(rows_v[b, g * L + r, pl.ds(off, L)]) * xs[0]
                        for q in range(1, nterm):
                            p = p + _sc_bf16(rows_v[b, g * L + r, pl.ds(off + q * L, L)]) * xs[q]
                        lo, hi = plsc.unpack(p, format=plsc.PackFormat.INTERLEAVED)
                        new.append(accs[r] + lo + hi)
                    return tuple(new)

                accs = lax.fori_loop(0, hw // (nterm * L), chunk, tuple(jnp.zeros((L,), F32) for _ in range(L)))
                out = jnp.zeros((L,), F32)
                for r in range(L):
                    out = jnp.where(lane == r, jnp.sum(accs[r]), out)
                res_v[pl.ds(pl.multiple_of(slot * blk + hs * win + g * L, L), L)] = out
                return c

            lax.fori_loop(0, win // L, head, 0)

        def reduce_wsum(slot, hs, b):
            tok = hs // wpt
            wbase = slot * blk + hs * win
            first = b % wpt == 0

            def colgroup(cg, c):
                col = pl.multiple_of(cg * (nq * L), nq * L)
                if first:
                    accs = tuple(jnp.zeros((L,), F32) for _ in range(2 * nq))
                else:
                    accs = tuple(res_v[slot, tok, pl.ds(col + q * L, L)] for q in range(nq)) + \
                           tuple(res_v[slot, tok, pl.ds(hw + col + q * L, L)] for q in range(nq))

                def rowgroup(rg, accs):
                    accs = list(accs)
                    r = rg * nterm
                    wvec = op_v[pl.ds(pl.multiple_of(wbase + (r // L) * L, L), L)]
                    ws = [_sc_bf16(wvec.at[jnp.full((L,), r % L + k, jnp.int32)].get(mode="promise_in_bounds"))
                          for k in range(nterm)]
                    for q in range(nq):
                        p = _sc_bf16(rows_v[b, r, pl.ds(col + q * L, L)]) * ws[0]
                        for k in range(1, nterm):
                            p = p + _sc_bf16(rows_v[b, r + k, pl.ds(col + q * L, L)]) * ws[k]
                        lo, hi = plsc.unpack(p, format=plsc.PackFormat.INTERLEAVED)
                        accs[q] = accs[q] + lo
                        accs[nq + q] = accs[nq + q] + hi
                    return tuple(accs)

                accs = lax.fori_loop(0, win // nterm, rowgroup, accs)
                for q in range(nq):
                    res_v[slot, tok, pl.ds(col + q * L, L)] = accs[q]
                    res_v[slot, tok, pl.ds(hw + col + q * L, L)] = accs[nq + q]
                return c

            lax.fori_loop(0, hw // (nq * L), colgroup, 0)

        idx_copy(0, 0).start()
        op_copy(0, 0).start()
        idx_copy(1, 1).start()
        op_copy(1, 1).start()
        idx_copy(0, 0).wait()
        op_copy(0, 0).wait()
        for a in range(ahead):
            gather(0, a, a).start()

        @pl.loop(0, nwin, step=nbuf)
        def _(i0):
            for b in range(nbuf):
                i = i0 + b
                ss = i // wps
                hs = i % wps
                slot = ss % 2
                gather(slot, hs, b).wait()
                nb = (b + ahead) % nbuf

                @pl.when(hs + ahead < wps)
                def _():
                    gather(slot, hs + ahead, nb).start()

                @pl.when(jnp.logical_and(hs + ahead >= wps, ss + 1 < nss))
                def _():
                    @pl.when(hs + ahead == wps)
                    def _():
                        idx_copy(ss + 1, 1 - slot).wait()
                        op_copy(ss + 1, 1 - slot).wait()

                    gather(1 - slot, hs + ahead - wps, nb).start()

                @pl.when(jnp.logical_and(hs == 0, ss >= 2))
                def _():
                    out_copy(ss - 2, slot).wait()

                if dots:
                    reduce_dots(slot, hs, b)
                else:
                    reduce_wsum(slot, hs, b)

                @pl.when(hs + 1 == wps)
                def _():
                    out_copy(ss, slot).start()

                    @pl.when(ss + 2 < nss)
                    def _():
                        idx_copy(ss + 2, slot).start()
                        op_copy(ss + 2, slot).start()

        out_copy(nss - 2, 0).wait()
        out_copy(nss - 1, 1).wait()

    op = operand if dots else operand.reshape(t * ne)
    return sc_kernel, scratch, out_type, (table, idx.reshape(t * ne), op)


def _sc_call(*phases):
    parts = [_sc_phase(table, idx, operand, mode) for mode, table, idx, operand in phases]
    names, types = [], []
    for _, scratch, _, _ in parts:
        for name, ty in scratch:
            if name not in names:
                names.append(name)
                types.append(ty)
    n = len(parts)

    def fused(*refs):
        ins, outs, scr = refs[:3 * n], refs[3 * n:4 * n], dict(zip(names, refs[4 * n:]))
        for i, (body, scratch, _, _) in enumerate(parts):
            body(*ins[3 * i:3 * i + 3], outs[i], *[scr[name] for name, _ in scratch])

    call = pl.kernel(
        fused, mesh=plsc.VectorSubcoreMesh(core_axis_name="c", subcore_axis_name="s"),
        out_type=tuple(p[2] for p in parts), scratch_types=types,
        compiler_params=pltpu.CompilerParams(needs_layout_passes=False),
        name="peer_" + "_".join(m for m, _, _, _ in phases),
    )
    return call(*[a for p in parts for a in p[3]])


def _peer_weight_kernel(act_ref, gate_ref, w_ref):
    w = (jax.nn.gelu(act_ref[...]) * gate_ref[...]).astype(BF16)
    w_ref[...] = pltpu.bitcast(_pack_bf16_pairs(w, w), jnp.int32)


def _peer_weight(act, gate, tm):
    t, ne = act.shape
    tm = min(tm, t)
    assert t % tm == 0
    spec = pl.BlockSpec((tm, ne), lambda i: (i, 0))
    return pl.pallas_call(
        _peer_weight_kernel, grid=(t // tm,), in_specs=[spec, spec], out_specs=spec,
        out_shape=jax.ShapeDtypeStruct((t, ne), jnp.int32),
        compiler_params=_cparams(("arbitrary",)), name="peer_weight",
    )(act, gate)


def _final_kernel(x1_ref, p_ref, g2_ref, fg_ref, prev_ref, o_ref):
    del prev_ref
    x2 = x1_ref[0] + g2_ref[0] * p_ref[0]
    ms = jnp.mean(x2 * x2, axis=-1, keepdims=True)
    o_ref[0] = x2 * lax.rsqrt(ms + EPS) * fg_ref[...]


def _final(x1, peer, g2, fg, tm, out, b0):
    bc, s, d = x1.shape
    act = pl.BlockSpec((1, tm, d), lambda i, j: (i, j, 0))
    return pl.pallas_call(
        _final_kernel, grid=(bc, s // tm),
        in_specs=[act, act, pl.BlockSpec((1, 1, d), lambda i, j: (i, 0, 0)),
                  pl.BlockSpec((1, d), lambda i, j: (0, 0)),
                  pl.BlockSpec(memory_space=pl.ANY)],
        out_specs=pl.BlockSpec((1, tm, d), lambda i, j: (i + b0, j, 0)),
        out_shape=jax.ShapeDtypeStruct(out.shape, F32),
        input_output_aliases={4: 0},
        compiler_params=_cparams(("arbitrary", "arbitrary")), name="final_norm",
    )(x1, peer, g2, fg, out)


def kernel(x, c, ctx, c_ctx, ada_w, ada_b, norm1_g, norm2_g, w_in, na_rpb, gm_ln_g, gm_ws, gm_bs,
           w_proj_a, w_proj_b, w_out, peer_wq, peer_keys, peer_u, peer_v, final_g):
    b, s, d = x.shape
    naw = NA_HEADS * HEAD_DIM
    gw = gm_ln_g.shape[1]
    layer = 0

    pad = (-(b + 1)) % SUBLANES
    c_all = jnp.concatenate([c, c_ctx[None, :], jnp.zeros((pad, d), F32)], axis=0)
    mod = _adaln(c_all, ada_w[layer], ada_b[layer])
    sh1, sc1, g1, sh2, sc2, g2 = [mod[:b, i * d:(i + 1) * d].reshape(b, 1, d) for i in range(ADA_CHUNKS)]
    csh1 = jnp.broadcast_to(mod[b, 0:d].reshape(1, 1, d), (b, 1, d))
    csc1 = jnp.broadcast_to(mod[b, d:2 * d].reshape(1, 1, d), (b, 1, d))

    w = w_in[layer].astype(BF16)
    w_kv = w[:, naw:3 * naw]
    n1g = norm1_g[layer].reshape(1, d)
    bias = _bias_table(na_rpb[layer])
    bs_full = jnp.repeat(gm_bs[layer].T, gw // GM_GROUPS, axis=1)
    keys = peer_keys[layer].reshape(2 * PEER_HEADS, PEER_N_KEYS, PEER_HALF).astype(BF16)
    lng = gm_ln_g[layer].reshape(1, gw)
    ws = gm_ws[layer].astype(BF16)
    wpa, wpb, wout = w_proj_a[layer].astype(BF16), w_proj_b[layer].astype(BF16), w_out[layer].astype(BF16)
    n2g = norm2_g[layer].reshape(1, d)
    wq = peer_wq[layer].astype(BF16)
    tab_u, tab_v = _pack_table(peer_u[layer], TILE_PACK), _pack_table(peer_v[layer], TILE_PACK)
    fg = final_g.reshape(1, d)
    ne = PEER_HEADS * PEER_TOPK

    assert sum(BATCH_CHUNKS) == b
    out = jnp.zeros((b, s, d), F32)

    def finish(out, pending, peer):
        idx, wgt, x1, g2c, boff = pending
        return _final(x1, peer.reshape(x1.shape), g2c, fg, TILE_FINAL, out, boff)

    pending = None
    b0 = 0
    for bc in BATCH_CHUNKS:
        sl = slice(b0, b0 + bc)
        boff = b0
        b0 += bc
        t = bc * s
        xc = x[sl]
        q, k, v, gu, gv, ga, gb = _norm_proj(xc, n1g, sh1[sl], sc1[sl], w, (naw, naw, naw, gw, gw, d, d), TILE_PROJ)
        k_c, v_c = _norm_proj(ctx[sl], n1g, csh1[sl], csc1[sl], w_kv, (naw, naw), ctx.shape[1])
        y_a = _attention(q, k, v, k_c, v_c, bias)
        x1, h2, st = _mix(xc, gu, gv, ga, gb, y_a, g1[sl], sh2[sl], sc2[sl], lng, ws, bs_full,
                          wpa, wpb, wout, n2g, wq, keys, TILE_MIX)
        idx, gate = _topk(st, TILE_TOPK)
        dots = ("dots", tab_u, idx, h2.reshape(t, d // 2))
        if pending is None:
            act, = _sc_call(dots)
        else:
            peer, act = _sc_call(("wsum", tab_v) + pending[:2], dots)
            out = finish(out, pending, peer)
        wgt = _peer_weight(act.reshape(t, ne), gate, TILE_WEIGHT)
        pending = (idx, wgt, x1, g2[sl], boff)
    peer, = _sc_call(("wsum", tab_v) + pending[:2])
    return finish(out, pending, peer)
```

```python
import functools

import jax
import jax.numpy as jnp
from jax import lax
from jax.experimental import pallas as pl
from jax.experimental.pallas import tpu as pltpu
from jax.experimental.pallas import tpu_sc as plsc

F32 = jnp.float32
BF16 = jnp.bfloat16

GRID_W = 64
NA_HEADS = 8
HEAD_DIM = 64
NA_WIN_ROWS = 8
NA_WIN_COLS = 16
GM_GROUPS = 8
GM_CHUNK = 128
PEER_HEADS = 8
PEER_N_KEYS = 128
PEER_TOPK = 16
PEER_HALF = 128
ADA_CHUNKS = 6
EPS = 1e-6
NEG_INF = -1e30

LANES = 128
SUBLANES = 8
VMEM_LIMIT = 56 * 1024 * 1024
TILE_ADALN_COLS = 1024
TILE_PROJ = 512
TILE_MIX = 256
TILE_TOPK = 512
TILE_WEIGHT = 2048
TILE_FINAL = 512
TILE_PACK = 1024
ATTN_ROWS_PER_STEP = 2
BATCH_CHUNKS = (1, 1, 2, 3, 4, 5)


def _dot(a, b):
    return lax.dot_general(a, b, (((1,), (0,)), ((), ())), preferred_element_type=F32)


def _dot_nt(a, b):
    return lax.dot_general(a, b, (((1,), (1,)), ((), ())), preferred_element_type=F32)


def _cparams(sem):
    return pltpu.CompilerParams(dimension_semantics=sem, vmem_limit_bytes=VMEM_LIMIT)


def _pack_bf16_pairs(lo, hi):
    lo_bits = pltpu.bitcast(lo.astype(F32), jnp.uint32) >> 16
    hi_bits = pltpu.bitcast(hi.astype(F32), jnp.uint32) & jnp.uint32(0xFFFF0000)
    return lo_bits | hi_bits


def _adaln_kernel(c_ref, w_ref, b_ref, o_ref):
    c = c_ref[...]
    s = c * jax.nn.sigmoid(c)
    o_ref[...] = lax.dot_general(s, w_ref[...], (((1,), (0,)), ((), ())),
                                 precision=lax.Precision.HIGHEST,
                                 preferred_element_type=F32) + b_ref[...]


def _adaln(c_all, w, b):
    m, d = c_all.shape
    n = w.shape[1]
    tn = TILE_ADALN_COLS
    return pl.pallas_call(
        _adaln_kernel,
        grid=(n // tn,),
        in_specs=[pl.BlockSpec((m, d), lambda j: (0, 0)),
                  pl.BlockSpec((d, tn), lambda j: (0, j)),
                  pl.BlockSpec((1, tn), lambda j: (0, j))],
        out_specs=pl.BlockSpec((m, tn), lambda j: (0, j)),
        out_shape=jax.ShapeDtypeStruct((m, n), F32),
        compiler_params=_cparams(("arbitrary",)),
        name="adaln",
    )(c_all, w, b.reshape(1, n))


def _norm_proj_kernel(widths, x_ref, g_ref, sh_ref, sc_ref, w_ref, *o_refs):
    x = x_ref[0]
    ms = jnp.mean(x * x, axis=-1, keepdims=True)
    y = x * lax.rsqrt(ms + EPS) * g_ref[...]
    h = (y * (1.0 + sc_ref[0]) + sh_ref[0]).astype(BF16)
    off = 0
    for o_ref, wd in zip(o_refs, widths):
        o_ref[0] = _dot(h, w_ref[:, off:off + wd]).astype(o_ref.dtype)
        off += wd


def _norm_proj(x, gain, shift, scale, w, widths, tm):
    b, s, d = x.shape
    n = w.shape[1]
    assert sum(widths) == n and s % tm == 0
    vec = pl.BlockSpec((1, 1, d), lambda i, j: (i, 0, 0))
    return pl.pallas_call(
        functools.partial(_norm_proj_kernel, widths),
        grid=(b, s // tm),
        in_specs=[pl.BlockSpec((1, tm, d), lambda i, j: (i, j, 0)),
                  pl.BlockSpec((1, d), lambda i, j: (0, 0)),
                  vec, vec,
                  pl.BlockSpec((d, n), lambda i, j: (0, 0))],
        out_specs=[pl.BlockSpec((1, tm, wd), lambda i, j: (i, j, 0)) for wd in widths],
        out_shape=[jax.ShapeDtypeStruct((b, s, wd), BF16) for wd in widths],
        compiler_params=_cparams(("arbitrary", "arbitrary")),
        name="norm_proj",
    )(x, gain, shift, scale, w)


def _bias_table_kernel(rpb_ref, o_ref):
    h = pl.program_id(0)
    q = lax.broadcasted_iota(jnp.int32, (GRID_W, GRID_W), 0)
    kc = lax.broadcasted_iota(jnp.int32, (GRID_W, GRID_W), 1)
    dc = jnp.clip(kc - q + NA_WIN_COLS - 1, 0, 2 * NA_WIN_COLS - 2)
    cs = jnp.clip(q - NA_WIN_COLS // 2, 0, GRID_W - NA_WIN_COLS)
    col_in = (kc >= cs) & (kc < cs + NA_WIN_COLS)
    n_dc = 2 * NA_WIN_COLS - 1
    n_dr = 2 * NA_WIN_ROWS - 1
    for dr in range(n_dr):
        t = jnp.zeros((GRID_W, GRID_W), F32)
        for c in range(n_dc):
            t = jnp.where(dc == c, rpb_ref[h * n_dr + dr, c], t)
        t = jnp.where(col_in, t, NEG_INF)
        for d0 in range(NA_WIN_ROWS):
            j = dr - d0
            if 0 <= j < NA_WIN_ROWS:
                o_ref[0, d0, :, j * GRID_W:(j + 1) * GRID_W] = t


def _bias_table(rpb):
    nh, n_dr, n_dc = rpb.shape
    band = NA_WIN_ROWS * GRID_W
    return pl.pallas_call(
        _bias_table_kernel,
        grid=(nh,),
        in_specs=[pl.BlockSpec(memory_space=pltpu.SMEM)],
        out_specs=pl.BlockSpec((1, NA_WIN_ROWS, GRID_W, band), lambda h: (h, 0, 0, 0)),
        out_shape=jax.ShapeDtypeStruct((nh, NA_WIN_ROWS, GRID_W, band), F32),
        compiler_params=_cparams(("arbitrary",)),
        name="bias_table",
    )(rpb.reshape(nh * n_dr, n_dc))


def _attn_kernel(rows, q_ref, k_ref, v_ref, kc_ref, vc_ref, bias_ref, o_ref, s_scr, p_scr):
    for rr in range(ATTN_ROWS_PER_STEP):
        _attn_row(rows, pl.program_id(1) * ATTN_ROWS_PER_STEP + rr, slice(rr * GRID_W, (rr + 1) * GRID_W),
                  q_ref, k_ref, v_ref, kc_ref, vc_ref, bias_ref, o_ref, s_scr, p_scr)


def _attn_row(rows, r, qrows, q_ref, k_ref, v_ref, kc_ref, vc_ref, bias_ref, o_ref, s_scr, p_scr):
    rs = jnp.clip(r - NA_WIN_ROWS // 2, 0, rows - NA_WIN_ROWS)
    d0 = rs - r + NA_WIN_ROWS - 1
    band = NA_WIN_ROWS * GRID_W
    start = pl.multiple_of(rs * GRID_W, GRID_W)
    scale = HEAD_DIM ** -0.5
    lane = lax.broadcasted_iota(jnp.int32, (GRID_W, LANES), 1)
    hpg = LANES // HEAD_DIM
    groups = NA_HEADS // hpg

    def mine(hh):
        return (lane >= hh * HEAD_DIM) & (lane < (hh + 1) * HEAD_DIM)

    for hp in range(groups):
        sl = slice(hp * LANES, (hp + 1) * LANES)
        q2 = q_ref[0, qrows, sl]
        kb = k_ref[0, pl.ds(start, band), sl]
        kc = kc_ref[0, :, sl]
        for hh in range(hpg):
            h = hp * hpg + hh
            qh = jnp.where(mine(hh), q2, jnp.zeros_like(q2))
            s_scr[h, :, :band] = _dot_nt(qh, kb) * scale + bias_ref[h, d0]
            s_scr[h, :, band:] = _dot_nt(qh, kc) * scale
    dens = []
    for h in range(NA_HEADS):
        s = s_scr[h]
        p = jnp.exp(s - jnp.max(s, axis=-1, keepdims=True))
        dens.append(jnp.sum(p, axis=-1, keepdims=True))
        p_scr[h] = p.astype(BF16)
    for hp in range(groups):
        sl = slice(hp * LANES, (hp + 1) * LANES)
        vb = v_ref[0, pl.ds(start, band), sl]
        vc = vc_ref[0, :, sl]
        halves = []
        for hh in range(hpg):
            h = hp * hpg + hh
            o = _dot(p_scr[h, :, :band], vb) + _dot(p_scr[h, :, band:], vc)
            halves.append(jnp.where(mine(hh), o / dens[h], 0.0))
        o_ref[0, qrows, sl] = sum(halves).astype(o_ref.dtype)


def _attention(q, k, v, kc, vc, bias):
    b, s, w = q.shape
    rows = s // GRID_W
    c = kc.shape[1]
    full = lambda n: pl.BlockSpec((1, n, w), lambda i, j: (i, 0, 0))
    return pl.pallas_call(
        functools.partial(_attn_kernel, rows),
        grid=(b, rows // ATTN_ROWS_PER_STEP),
        in_specs=[pl.BlockSpec((1, ATTN_ROWS_PER_STEP * GRID_W, w), lambda i, j: (i, j, 0)),
                  full(s), full(s), full(c), full(c),
                  pl.BlockSpec(bias.shape, lambda i, j: (0, 0, 0, 0))],
        out_specs=pl.BlockSpec((1, ATTN_ROWS_PER_STEP * GRID_W, w), lambda i, j: (i, j, 0)),
        out_shape=jax.ShapeDtypeStruct((b, s, w), BF16),
        scratch_shapes=[pltpu.VMEM((NA_HEADS, GRID_W, NA_WIN_ROWS * GRID_W + c), F32),
                        pltpu.VMEM((NA_HEADS, GRID_W, NA_WIN_ROWS * GRID_W + c), BF16)],
        compiler_params=_cparams(("arbitrary", "arbitrary")),
        name="nbr_attention",
    )(q, k, v, kc, vc, bias)


def _mix_kernel(x_ref, gu_ref, gv_ref, ga_ref, gb_ref, ya_ref, g1_ref, sh2_ref, sc2_ref,
                lng_ref, ws_ref, bs_ref, wpa_ref, wpb_ref, wout_ref, n2g_ref, wq_ref, keys_ref,
                x1_ref, h2_ref, st_ref):
    tm = x_ref.shape[1]
    gw = gu_ref.shape[2]
    u = jax.nn.gelu(gu_ref[0].astype(F32))
    t = jax.nn.gelu(gv_ref[0].astype(F32))
    tc = t - jnp.mean(t, axis=-1, keepdims=True)
    vn = tc * lax.rsqrt(jnp.mean(tc * tc, axis=-1, keepdims=True) + EPS) * lng_ref[...]
    vnb = vn.astype(BF16)
    lane = lax.broadcasted_iota(jnp.int32, (GM_CHUNK, LANES), 1)
    gd = gw // GM_GROUPS
    chunks = []
    for c in range(tm // GM_CHUNK):
        pairs = []
        for gp in range(gw // LANES):
            vp = vnb[c * GM_CHUNK:(c + 1) * GM_CHUNK, gp * LANES:(gp + 1) * LANES]
            r0 = _dot(ws_ref[2 * gp], vp)
            r1 = _dot(ws_ref[2 * gp + 1], vp)
            pairs.append(jnp.where(lane < gd, r0, r1))
        chunks.append(jnp.concatenate(pairs, axis=1) + bs_ref[...])
    mixed = jnp.concatenate(chunks, axis=0)
    yb = (u * mixed).astype(BF16)
    pa = _dot(ya_ref[0], wpa_ref[...])
    pb = _dot(yb, wpb_ref[...])
    m = jax.nn.sigmoid(ga_ref[0].astype(F32)) * pa + jax.nn.sigmoid(gb_ref[0].astype(F32)) * pb
    out = _dot(m.astype(BF16), wout_ref[...])
    x1 = x_ref[0] + g1_ref[0] * out
    x1_ref[0] = x1
    ms = jnp.mean(x1 * x1, axis=-1, keepdims=True)
    h2 = x1 * lax.rsqrt(ms + EPS) * n2g_ref[...]
    h2 = (h2 * (1.0 + sc2_ref[0]) + sh2_ref[0]).astype(BF16)
    half = h2.shape[1] // 2
    h2_ref[0] = _pack_bf16_pairs(h2[:, :half], h2[:, half:])
    qp = _dot(h2, wq_ref[...]).astype(BF16)
    for hp in range(keys_ref.shape[0]):
        st_ref[hp] = _dot_nt(keys_ref[hp], qp[:, hp * PEER_HALF:(hp + 1) * PEER_HALF])


def _mix(x, gu, gv, ga, gb, ya, g1, sh2, sc2, lng, ws, bs_full, wpa, wpb, wout, n2g, wq, keys, tm):
    b, s, d = x.shape
    gw = gu.shape[2]
    nt = s // tm
    nhp = keys.shape[0]
    act = lambda w: pl.BlockSpec((1, tm, w), lambda i, j: (i, j, 0))
    vec = pl.BlockSpec((1, 1, d), lambda i, j: (i, 0, 0))

    def const(a):
        nd = a.ndim
        return pl.BlockSpec(a.shape, lambda i, j: (0,) * nd)

    return pl.pallas_call(
        _mix_kernel,
        grid=(b, nt),
        in_specs=[act(d), act(gw), act(gw), act(d), act(d), act(gw), vec, vec, vec,
                  const(lng), const(ws), const(bs_full), const(wpa), const(wpb), const(wout),
                  const(n2g), const(wq), const(keys)],
        out_specs=[act(d), act(d // 2),
                   pl.BlockSpec((nhp, PEER_N_KEYS, tm), lambda i, j: (0, 0, i * nt + j))],
        out_shape=[jax.ShapeDtypeStruct((b, s, d), F32),
                   jax.ShapeDtypeStruct((b, s, d // 2), jnp.uint32),
                   jax.ShapeDtypeStruct((nhp, PEER_N_KEYS, b * s), F32)],
        compiler_params=_cparams(("arbitrary", "arbitrary")),
        name="mix_peer_scores",
    )(x, gu, gv, ga, gb, ya, g1, sh2, sc2, lng, ws, bs_full, wpa, wpb, wout, n2g, wq, keys)


def _extract_top(vals, order, payload, k):
    out_v, out_p = [], []
    for _ in range(k):
        m = jnp.max(vals, axis=0, keepdims=True)
        o = jnp.min(jnp.where(vals == m, order, jnp.inf), axis=0, keepdims=True)
        sel = order == o
        out_v.append(m)
        if payload is order:
            out_p.append(o)
        else:
            out_p.append(jnp.max(jnp.where(sel, payload, -1.0), axis=0, keepdims=True))
        vals = jnp.where(sel, -jnp.inf, vals)
    return jnp.concatenate(out_v, axis=0), jnp.concatenate(out_p, axis=0)


def _topk_kernel(st_ref, idx_ref, gate_ref, sv_ref, si_ref, se_ref, sg_ref):
    nhp = st_ref.shape[0]
    tm = st_ref.shape[2]
    k = PEER_TOPK
    kidx = lax.broadcasted_iota(jnp.int32, (PEER_N_KEYS, LANES), 0).astype(F32)
    sub = lax.broadcasted_iota(jnp.int32, (SUBLANES, LANES), 0).astype(F32)

    def lane_tile(lt, carry):
        lanes = pl.ds(pl.multiple_of(lt * LANES, LANES), LANES)

        def stage1(h, c):
            for hp in (4 * h, 4 * h + 1, 4 * h + 2, 4 * h + 3):
                v, i = _extract_top(st_ref[hp, :, lanes], kidx, kidx, k)
                sv_ref[hp] = v
                si_ref[hp] = i
            return c

        lax.fori_loop(0, nhp // 4, stage1, 0)

        def stage2(h2, c):
            for h in (2 * h2, 2 * h2 + 1):
                stage2_head(h)
            return c

        def stage2_head(h):
            s1, s2 = sv_ref[2 * h], sv_ref[2 * h + 1]
            i1, i2 = si_ref[2 * h] * float(PEER_N_KEYS), si_ref[2 * h + 1]
            cv, co, ce = [], [], []
            for half in range(2):
                b0 = half * SUBLANES
                cv.append(s1[0:1] + s2[b0:b0 + SUBLANES])
                co.append(sub + float(b0))
                ce.append(i1[0:1] + i2[b0:b0 + SUBLANES])
            for a in range(1, SUBLANES):
                cv.append(s1[a:a + 1] + s2[0:SUBLANES])
                co.append(sub + float(a * k))
                ce.append(i1[a:a + 1] + i2[0:SUBLANES])
            cv.append(s1[SUBLANES:k] + s2[0:1])
            co.append((sub + float(SUBLANES)) * float(k))
            ce.append(i1[SUBLANES:k] + i2[0:1])
            bv, be = _extract_top(jnp.concatenate(cv, axis=0), jnp.concatenate(co, axis=0),
                                  jnp.concatenate(ce, axis=0), k)
            ex = jnp.exp(bv - bv[0:1])
            rows = pl.ds(pl.multiple_of(h * k, k), k)
            sg_ref[rows, :] = ex / jnp.sum(ex, axis=0, keepdims=True)
            se_ref[rows, :] = be

        lax.fori_loop(0, nhp // 4, stage2, 0)

        rows = pl.ds(pl.multiple_of(lt * LANES, LANES), LANES)
        idx_ref[rows, :] = se_ref[...].T.astype(jnp.int32)
        gate_ref[rows, :] = sg_ref[...].T
        return carry

    lax.fori_loop(0, tm // LANES, lane_tile, 0)


def _topk(st, tm):
    nhp, nk, t = st.shape
    ne = (nhp // 2) * PEER_TOPK
    return pl.pallas_call(
        _topk_kernel,
        grid=(t // tm,),
        in_specs=[pl.BlockSpec((nhp, nk, tm), lambda i: (0, 0, i))],
        out_specs=[pl.BlockSpec((tm, ne), lambda i: (i, 0)),
                   pl.BlockSpec((tm, ne), lambda i: (i, 0))],
        out_shape=[jax.ShapeDtypeStruct((t, ne), jnp.int32),
                   jax.ShapeDtypeStruct((t, ne), F32)],
        scratch_shapes=[pltpu.VMEM((nhp, PEER_TOPK, LANES), F32),
                        pltpu.VMEM((nhp, PEER_TOPK, LANES), F32),
                        pltpu.VMEM((ne, LANES), F32),
                        pltpu.VMEM((ne, LANES), F32)],
        compiler_params=_cparams(("arbitrary",)),
        name="peer_topk",
    )(st)


SC_LANES = 16
SC_WINDOW = 32
SC_ROW_BUFFERS = 4
SC_TOKENS_PER_STEP = 8
SC_WSUM_CHUNKS = 8
SC_BF16_TERMS = 4


def _pack_table_kernel(t_ref, o_ref):
    half = o_ref.shape[1]
    o_ref[...] = _pack_bf16_pairs(t_ref[:, :half].astype(BF16), t_ref[:, half:].astype(BF16))


def _pack_table(t, tn):
    n, d = t.shape
    assert n % tn == 0
    return pl.pallas_call(
        _pack_table_kernel, grid=(n // tn,),
        in_specs=[pl.BlockSpec((tn, d), lambda i: (i, 0))],
        out_specs=pl.BlockSpec((tn, d // 2), lambda i: (i, 0)),
        out_shape=jax.ShapeDtypeStruct((n, d // 2), jnp.uint32),
        compiler_params=_cparams(("arbitrary",)), name="pack_table",
    )(t)


def _sc_bf16(words):
    return plsc.bitcast(words, BF16)


def _sc_phase(table, idx, operand, mode):
    t, ne = idx.shape
    hw = table.shape[1]
    d = 2 * hw
    L = SC_LANES
    win = SC_WINDOW
    nbuf = SC_ROW_BUFFERS
    ahead = nbuf - 1
    tps = SC_TOKENS_PER_STEP
    nq = SC_WSUM_CHUNKS
    nterm = SC_BF16_TERMS
    info = plsc.get_sparse_core_info()
    nc, ns = info.num_cores, info.num_subcores
    nss = t // (nc * ns * tps)
    wps = tps * ne // win
    wpt = ne // win
    blk = tps * ne
    nwin = nss * wps
    assert nss * nc * ns * tps == t and nss % 2 == 0 and wps % nbuf == 0 and wps > ahead
    assert wpt * win == ne and win % L == 0 and hw % (nq * L) == 0 and hw % (nterm * L) == 0 and win % nterm == 0
    dots = mode == "dots"
    assert dots or nbuf % wpt == 0
    op_buf = pltpu.VMEM((2, tps, hw), jnp.uint32) if dots else pltpu.VMEM((2 * blk,), jnp.int32)
    res_buf = pltpu.VMEM((2 * blk,), F32) if dots else pltpu.VMEM((2, tps, d), F32)
    out_type = jax.ShapeDtypeStruct((t * ne,), F32) if dots else jax.ShapeDtypeStruct((t, d), F32)
    scratch = [("idx", pltpu.VMEM((2 * blk,), jnp.int32)), ("op_" + mode, op_buf),
               ("rows", pltpu.VMEM((nbuf, win, hw), jnp.uint32)), ("res_" + mode, res_buf),
               ("isem", pltpu.SemaphoreType.DMA((2,))), ("psem", pltpu.SemaphoreType.DMA((2,))),
               ("gsem", pltpu.SemaphoreType.DMA((nbuf,))), ("osem", pltpu.SemaphoreType.DMA((2,)))]

    def sc_kernel(table_hbm, idx_hbm, op_hbm, out_hbm, idx_v, op_v, rows_v, res_v, isem, psem, gsem, osem):
        ss0 = (lax.axis_index("s") * nc + lax.axis_index("c")) * nss
        lane = lax.iota(jnp.int32, L)

        def flat(ref, n):
            return ref.at[pl.ds(pl.multiple_of(n * blk, blk), blk)]

        def tok_rows(ref, n):
            return ref.at[pl.ds(pl.multiple_of(n * tps, tps), tps)]

        def idx_copy(ss, slot):
            return pltpu.make_async_copy(flat(idx_hbm, ss0 + ss), flat(idx_v, slot), isem.at[slot])

        def op_copy(ss, slot):
            if dots:
                return pltpu.make_async_copy(tok_rows(op_hbm, ss0 + ss), op_v.at[slot], psem.at[slot])
            return pltpu.make_async_copy(flat(op_hbm, ss0 + ss), flat(op_v, slot), psem.at[slot])

        def out_copy(ss, slot):
            if dots:
                return pltpu.make_async_copy(flat(res_v, slot), flat(out_hbm, ss0 + ss), osem.at[slot])
            return pltpu.make_async_copy(res_v.at[slot], tok_rows(out_hbm, ss0 + ss), osem.at[slot])

        def gather(slot, hs, b):
            ix = idx_v.at[pl.ds(pl.multiple_of(slot * blk + hs * win, win), win)]
            return pltpu.make_async_copy(table_hbm.at[ix], rows_v.at[b], gsem.at[b])

        def reduce_dots(slot, hs, b):
            tok = hs // wpt

            def head(g, c):
                def chunk(jj, accs):
                    off = pl.multiple_of(jj * (nterm * L), nterm * L)
                    xs = [_sc_bf16(op_v[slot, tok, pl.ds(off + q * L, L)]) for q in range(nterm)]
                    new = []
                    for r in range(L):
                        p = _sc_bf16(rows_v[b, g * L + r, pl.ds(off, L)]) * xs[0]
                        for q in range(1, nterm):
                            p = p + _sc_bf16(rows_v[b, g * L + r, pl.ds(off + q * L, L)]) * xs[q]
                        lo, hi = plsc.unpack(p, format=plsc.PackFormat.INTERLEAVED)
                        new.append(accs[r] + lo + hi)
                    return tuple(new)

                accs = lax.fori_loop(0, hw // (nterm * L), chunk, tuple(jnp.zeros((L,), F32) for _ in range(L)))
                out = jnp.zeros((L,), F32)
                for r in range(L):
                    out = jnp.where(lane == r, jnp.sum(accs[r]), out)
                res_v[pl.ds(pl.multiple_of(slot * blk + hs * win + g * L, L), L)] = out
                return c

            lax.fori_loop(0, win // L, head, 0)

        def reduce_wsum(slot, hs, b):
            tok = hs // wpt
            wbase = slot * blk + hs * win
            first = b % wpt == 0

            def colgroup(cg, c):
                col = pl.multiple_of(cg * (nq * L), nq * L)
                if first:
                    accs = tuple(jnp.zeros((L,), F32) for _ in range(2 * nq))
                else:
                    accs = tuple(res_v[slot, tok, pl.ds(col + q * L, L)] for q in range(nq)) + \
                           tuple(res_v[slot, tok, pl.ds(hw + col + q * L, L)] for q in range(nq))

                def rowgroup(rg, accs):
                    accs = list(accs)
                    r = rg * nterm
                    wvec = op_v[pl.ds(pl.multiple_of(wbase + (r // L) * L, L), L)]
                    ws = [_sc_bf16(wvec.at[jnp.full((L,), r % L + k, jnp.int32)].get(mode="promise_in_bounds"))
                          for k in range(nterm)]
                    for q in range(nq):
                        p = _sc_bf16(rows_v[b, r, pl.ds(col + q * L, L)]) * ws[0]
                        for k in range(1, nterm):
                            p = p + _sc_bf16(rows_v[b, r + k, pl.ds(col + q * L, L)]) * ws[k]
                        lo, hi = plsc.unpack(p, format=plsc.PackFormat.INTERLEAVED)
                        accs[q] = accs[q] + lo
                        accs[nq + q] = accs[nq + q] + hi
                    return tuple(accs)

                accs = lax.fori_loop(0, win // nterm, rowgroup, accs)
                for q in range(nq):
                    res_v[slot, tok, pl.ds(col + q * L, L)] = accs[q]
                    res_v[slot, tok, pl.ds(hw + col + q * L, L)] = accs[nq + q]
                return c

            lax.fori_loop(0, hw // (nq * L), colgroup, 0)

        idx_copy(0, 0).start()
        op_copy(0, 0).start()
        idx_copy(1, 1).start()
        op_copy(1, 1).start()
        idx_copy(0, 0).wait()
        op_copy(0, 0).wait()
        for a in range(ahead):
            gather(0, a, a).start()

        @pl.loop(0, nwin, step=nbuf)
        def _(i0):
            for b in range(nbuf):
                i = i0 + b
                ss = i // wps
                hs = i % wps
                slot = ss % 2
                gather(slot, hs, b).wait()
                nb = (b + ahead) % nbuf

                @pl.when(hs + ahead < wps)
                def _():
                    gather(slot, hs + ahead, nb).start()

                @pl.when(jnp.logical_and(hs + ahead >= wps, ss + 1 < nss))
                def _():
                    @pl.when(hs + ahead == wps)
                    def _():
                        idx_copy(ss + 1, 1 - slot).wait()
                        op_copy(ss + 1, 1 - slot).wait()

                    gather(1 - slot, hs + ahead - wps, nb).start()

                @pl.when(jnp.logical_and(hs == 0, ss >= 2))
                def _():
                    out_copy(ss - 2, slot).wait()

                if dots:
                    reduce_dots(slot, hs, b)
                else:
                    reduce_wsum(slot, hs, b)

                @pl.when(hs + 1 == wps)
                def _():
                    out_copy(ss, slot).start()

                    @pl.when(ss + 2 < nss)
                    def _():
                        idx_copy(ss + 2, slot).start()
                        op_copy(ss + 2, slot).start()

        out_copy(nss - 2, 0).wait()
        out_copy(nss - 1, 1).wait()

    op = operand if dots else operand.reshape(t * ne)
    return sc_kernel, scratch, out_type, (table, idx.reshape(t * ne), op)


def _sc_call(*phases):
    parts = [_sc_phase(table, idx, operand, mode) for mode, table, idx, operand in phases]
    names, types = [], []
    for _, scratch, _, _ in parts:
        for name, ty in scratch:
            if name not in names:
                names.append(name)
                types.append(ty)
    n = len(parts)

    def fused(*refs):
        ins, outs, scr = refs[:3 * n], refs[3 * n:4 * n], dict(zip(names, refs[4 * n:]))
        for i, (body, scratch, _, _) in enumerate(parts):
            body(*ins[3 * i:3 * i + 3], outs[i], *[scr[name] for name, _ in scratch])

    call = pl.kernel(
        fused, mesh=plsc.VectorSubcoreMesh(core_axis_name="c", subcore_axis_name="s"),
        out_type=tuple(p[2] for p in parts), scratch_types=types,
        compiler_params=pltpu.CompilerParams(needs_layout_passes=False),
        name="peer_" + "_".join(m for m, _, _, _ in phases),
    )
    return call(*[a for p in parts for a in p[3]])


def _peer_weight_kernel(act_ref, gate_ref, w_ref):
    w = (jax.nn.gelu(act_ref[...]) * gate_ref[...]).astype(BF16)
    w_ref[...] = pltpu.bitcast(_pack_bf16_pairs(w, w), jnp.int32)


def _peer_weight(act, gate, tm):
    t, ne = act.shape
    tm = min(tm, t)
    assert t % tm == 0
    spec = pl.BlockSpec((tm, ne), lambda i: (i, 0))
    return pl.pallas_call(
        _peer_weight_kernel, grid=(t // tm,), in_specs=[spec, spec], out_specs=spec,
        out_shape=jax.ShapeDtypeStruct((t, ne), jnp.int32),
        compiler_params=_cparams(("arbitrary",)), name="peer_weight",
    )(act, gate)


def _final_kernel(x1_ref, p_ref, g2_ref, fg_ref, prev_ref, o_ref):
    del prev_ref
    x2 = x1_ref[0] + g2_ref[0] * p_ref[0]
    ms = jnp.mean(x2 * x2, axis=-1, keepdims=True)
    o_ref[0] = x2 * lax.rsqrt(ms + EPS) * fg_ref[...]


def _final(x1, peer, g2, fg, tm, out, b0):
    bc, s, d = x1.shape
    act = pl.BlockSpec((1, tm, d), lambda i, j: (i, j, 0))
    return pl.pallas_call(
        _final_kernel, grid=(bc, s // tm),
        in_specs=[act, act, pl.BlockSpec((1, 1, d), lambda i, j: (i, 0, 0)),
                  pl.BlockSpec((1, d), lambda i, j: (0, 0)),
                  pl.BlockSpec(memory_space=pl.ANY)],
        out_specs=pl.BlockSpec((1, tm, d), lambda i, j: (i + b0, j, 0)),
        out_shape=jax.ShapeDtypeStruct(out.shape, F32),
        input_output_aliases={4: 0},
        compiler_params=_cparams(("arbitrary", "arbitrary")), name="final_norm",
    )(x1, peer, g2, fg, out)


def kernel(x, c, ctx, c_ctx, ada_w, ada_b, norm1_g, norm2_g, w_in, na_rpb, gm_ln_g, gm_ws, gm_bs,
           w_proj_a, w_proj_b, w_out, peer_wq, peer_keys, peer_u, peer_v, final_g):
    b, s, d = x.shape
    naw = NA_HEADS * HEAD_DIM
    gw = gm_ln_g.shape[1]
    layer = 0

    pad = (-(b + 1)) % SUBLANES
    c_all = jnp.concatenate([c, c_ctx[None, :], jnp.zeros((pad, d), F32)], axis=0)
    mod = _adaln(c_all, ada_w[layer], ada_b[layer])
    sh1, sc1, g1, sh2, sc2, g2 = [mod[:b, i * d:(i + 1) * d].reshape(b, 1, d) for i in range(ADA_CHUNKS)]
    csh1 = jnp.broadcast_to(mod[b, 0:d].reshape(1, 1, d), (b, 1, d))
    csc1 = jnp.broadcast_to(mod[b, d:2 * d].reshape(1, 1, d), (b, 1, d))

    w = w_in[layer].astype(BF16)
    w_kv = w[:, naw:3 * naw]
    n1g = norm1_g[layer].reshape(1, d)
    bias = _bias_table(na_rpb[layer])
    bs_full = jnp.repeat(gm_bs[layer].T, gw // GM_GROUPS, axis=1)
    keys = peer_keys[layer].reshape(2 * PEER_HEADS, PEER_N_KEYS, PEER_HALF).astype(BF16)
    lng = gm_ln_g[layer].reshape(1, gw)
    ws = gm_ws[layer].astype(BF16)
    wpa, wpb, wout = w_proj_a[layer].astype(BF16), w_proj_b[layer].astype(BF16), w_out[layer].astype(BF16)
    n2g = norm2_g[layer].reshape(1, d)
    wq = peer_wq[layer].astype(BF16)
    tab_u, tab_v = _pack_table(peer_u[layer], TILE_PACK), _pack_table(peer_v[layer], TILE_PACK)
    fg = final_g.reshape(1, d)
    ne = PEER_HEADS * PEER_TOPK

    assert sum(BATCH_CHUNKS) == b
    out = jnp.zeros((b, s, d), F32)

    def finish(out, pending, peer):
        idx, wgt, x1, g2c, boff = pending
        return _final(x1, peer.reshape(x1.shape), g2c, fg, TILE_FINAL, out, boff)

    pending = None
    b0 = 0
    for bc in BATCH_CHUNKS:
        sl = slice(b0, b0 + bc)
        boff = b0
        b0 += bc
        t = bc * s
        xc = x[sl]
        q, k, v, gu, gv, ga, gb = _norm_proj(xc, n1g, sh1[sl], sc1[sl], w, (naw, naw, naw, gw, gw, d, d), TILE_PROJ)
        k_c, v_c = _norm_proj(ctx[sl], n1g, csh1[sl], csc1[sl], w_kv, (naw, naw), ctx.shape[1])
        y_a = _attention(q, k, v, k_c, v_c, bias)
        x1, h2, st = _mix(xc, gu, gv, ga, gb, y_a, g1[sl], sh2[sl], sc2[sl], lng, ws, bs_full,
                          wpa, wpb, wout, n2g, wq, keys, TILE_MIX)
        idx, gate = _topk(st, TILE_TOPK)
        dots = ("dots", tab_u, idx, h2.reshape(t, d // 2))
        if pending is None:
            act, = _sc_call(dots)
        else:
            peer, act = _sc_call(("wsum", tab_v) + pending[:2], dots)
            out = finish(out, pending, peer)
        wgt = _peer_weight(act.reshape(t, ne), gate, TILE_WEIGHT)
        pending = (idx, wgt, x1, g2[sl], boff)
    peer, = _sc_call(("wsum", tab_v) + pending[:2])
    return finish(out, pending, peer)
```

```python
import functools

import jax
import jax.numpy as jnp
from jax import lax
from jax.experimental import pallas as pl
from jax.experimental.pallas import tpu as pltpu
from jax.experimental.pallas import tpu_sc as plsc

F32 = jnp.float32
BF16 = jnp.bfloat16

GRID_W = 64
NA_HEADS = 8
HEAD_DIM = 64
NA_WIN_ROWS = 8
NA_WIN_COLS = 16
GM_GROUPS = 8
GM_CHUNK = 128
PEER_HEADS = 8
PEER_N_KEYS = 128
PEER_TOPK = 16
PEER_HALF = 128
ADA_CHUNKS = 6
EPS = 1e-6
NEG_INF = -1e30

LANES = 128
SUBLANES = 8
VMEM_LIMIT = 56 * 1024 * 1024
TILE_ADALN_COLS = 1024
TILE_PROJ = 512
TILE_MIX = 512
TILE_TOPK = 512
TILE_WEIGHT = 2048
TILE_FINAL = 512
TILE_PACK = 1024
ATTN_ROWS_PER_STEP = 2
BATCH_CHUNKS = (1, 1, 2, 3, 4, 5)


def _dot(a, b):
    return lax.dot_general(a, b, (((1,), (0,)), ((), ())), preferred_element_type=F32)


def _dot_nt(a, b):
    return lax.dot_general(a, b, (((1,), (1,)), ((), ())), preferred_element_type=F32)


def _cparams(sem):
    return pltpu.CompilerParams(dimension_semantics=sem, vmem_limit_bytes=VMEM_LIMIT)


def _pack_bf16_pairs(lo, hi):
    lo_bits = pltpu.bitcast(lo.astype(F32), jnp.uint32) >> 16
    hi_bits = pltpu.bitcast(hi.astype(F32), jnp.uint32) & jnp.uint32(0xFFFF0000)
    return lo_bits | hi_bits


def _adaln_kernel(c_ref, w_ref, b_ref, o_ref):
    c = c_ref[...]
    s = c * jax.nn.sigmoid(c)
    o_ref[...] = lax.dot_general(s, w_ref[...], (((1,), (0,)), ((), ())),
                                 precision=lax.Precision.HIGHEST,
                                 preferred_element_type=F32) + b_ref[...]


def _adaln(c_all, w, b):
    m, d = c_all.shape
    n = w.shape[1]
    tn = TILE_ADALN_COLS
    return pl.pallas_call(
        _adaln_kernel,
        grid=(n // tn,),
        in_specs=[pl.BlockSpec((m, d), lambda j: (0, 0)),
                  pl.BlockSpec((d, tn), lambda j: (0, j)),
                  pl.BlockSpec((1, tn), lambda j: (0, j))],
        out_specs=pl.BlockSpec((m, tn), lambda j: (0, j)),
        out_shape=jax.ShapeDtypeStruct((m, n), F32),
        compiler_params=_cparams(("arbitrary",)),
        name="adaln",
    )(c_all, w, b.reshape(1, n))


def _norm_proj_kernel(widths, x_ref, g_ref, sh_ref, sc_ref, w_ref, *o_refs):
    x = x_ref[0]
    ms = jnp.mean(x * x, axis=-1, keepdims=True)
    y = x * lax.rsqrt(ms + EPS) * g_ref[...]
    h = (y * (1.0 + sc_ref[0]) + sh_ref[0]).astype(BF16)
    off = 0
    for o_ref, wd in zip(o_refs, widths):
        o_ref[0] = _dot(h, w_ref[:, off:off + wd]).astype(o_ref.dtype)
        off += wd


def _norm_proj(x, gain, shift, scale, w, widths, tm):
    b, s, d = x.shape
    n = w.shape[1]
    assert sum(widths) == n and s % tm == 0
    vec = pl.BlockSpec((1, 1, d), lambda i, j: (i, 0, 0))
    return pl.pallas_call(
        functools.partial(_norm_proj_kernel, widths),
        grid=(b, s // tm),
        in_specs=[pl.BlockSpec((1, tm, d), lambda i, j: (i, j, 0)),
                  pl.BlockSpec((1, d), lambda i, j: (0, 0)),
                  vec, vec,
                  pl.BlockSpec((d, n), lambda i, j: (0, 0))],
        out_specs=[pl.BlockSpec((1, tm, wd), lambda i, j: (i, j, 0)) for wd in widths],
        out_shape=[jax.ShapeDtypeStruct((b, s, wd), BF16) for wd in widths],
        compiler_params=_cparams(("arbitrary", "arbitrary")),
        name="norm_proj",
    )(x, gain, shift, scale, w)


def _bias_table_kernel(rpb_ref, o_ref):
    h = pl.program_id(0)
    q = lax.broadcasted_iota(jnp.int32, (GRID_W, GRID_W), 0)
    kc = lax.broadcasted_iota(jnp.int32, (GRID_W, GRID_W), 1)
    dc = jnp.clip(kc - q + NA_WIN_COLS - 1, 0, 2 * NA_WIN_COLS - 2)
    cs = jnp.clip(q - NA_WIN_COLS // 2, 0, GRID_W - NA_WIN_COLS)
    col_in = (kc >= cs) & (kc < cs + NA_WIN_COLS)
    n_dc = 2 * NA_WIN_COLS - 1
    n_dr = 2 * NA_WIN_ROWS - 1
    for dr in range(n_dr):
        t = jnp.zeros((GRID_W, GRID_W), F32)
        for c in range(n_dc):
            t = jnp.where(dc == c, rpb_ref[h * n_dr + dr, c], t)
        t = jnp.where(col_in, t, NEG_INF)
        for d0 in range(NA_WIN_ROWS):
            j = dr - d0
            if 0 <= j < NA_WIN_ROWS:
                o_ref[0, d0, :, j * GRID_W:(j + 1) * GRID_W] = t


def _bias_table(rpb):
    nh, n_dr, n_dc = rpb.shape
    band = NA_WIN_ROWS * GRID_W
    return pl.pallas_call(
        _bias_table_kernel,
        grid=(nh,),
        in_specs=[pl.BlockSpec(memory_space=pltpu.SMEM)],
        out_specs=pl.BlockSpec((1, NA_WIN_ROWS, GRID_W, band), lambda h: (h, 0, 0, 0)),
        out_shape=jax.ShapeDtypeStruct((nh, NA_WIN_ROWS, GRID_W, band), F32),
        compiler_params=_cparams(("arbitrary",)),
        name="bias_table",
    )(rpb.reshape(nh * n_dr, n_dc))


def _attn_kernel(rows, q_ref, k_ref, v_ref, kc_ref, vc_ref, bias_ref, o_ref, s_scr, p_scr):
    for rr in range(ATTN_ROWS_PER_STEP):
        _attn_row(rows, pl.program_id(1) * ATTN_ROWS_PER_STEP + rr, slice(rr * GRID_W, (rr + 1) * GRID_W),
                  q_ref, k_ref, v_ref, kc_ref, vc_ref, bias_ref, o_ref, s_scr, p_scr)


def _attn_row(rows, r, qrows, q_ref, k_ref, v_ref, kc_ref, vc_ref, bias_ref, o_ref, s_scr, p_scr):
    rs = jnp.clip(r - NA_WIN_ROWS // 2, 0, rows - NA_WIN_ROWS)
    d0 = rs - r + NA_WIN_ROWS - 1
    band = NA_WIN_ROWS * GRID_W
    start = pl.multiple_of(rs * GRID_W, GRID_W)
    scale = HEAD_DIM ** -0.5
    lane = lax.broadcasted_iota(jnp.int32, (GRID_W, LANES), 1)
    hpg = LANES // HEAD_DIM
    groups = NA_HEADS // hpg

    def mine(hh):
        return (lane >= hh * HEAD_DIM) & (lane < (hh + 1) * HEAD_DIM)

    for hp in range(groups):
        sl = slice(hp * LANES, (hp + 1) * LANES)
        q2 = q_ref[0, qrows, sl]
        kb = k_ref[0, pl.ds(start, band), sl]
        kc = kc_ref[0, :, sl]
        for hh in range(hpg):
            h = hp * hpg + hh
            qh = jnp.where(mine(hh), q2, jnp.zeros_like(q2))
            s_scr[h, :, :band] = _dot_nt(qh, kb) * scale + bias_ref[h, d0]
            s_scr[h, :, band:] = _dot_nt(qh, kc) * scale
    dens = []
    for h in range(NA_HEADS):
        s = s_scr[h]
        p = jnp.exp(s - jnp.max(s, axis=-1, keepdims=True))
        dens.append(jnp.sum(p, axis=-1, keepdims=True))
        p_scr[h] = p.astype(BF16)
    for hp in range(groups):
        sl = slice(hp * LANES, (hp + 1) * LANES)
        vb = v_ref[0, pl.ds(start, band), sl]
        vc = vc_ref[0, :, sl]
        halves = []
        for hh in range(hpg):
            h = hp * hpg + hh
            o = _dot(p_scr[h, :, :band], vb) + _dot(p_scr[h, :, band:], vc)
            halves.append(jnp.where(mine(hh), o / dens[h], 0.0))
        o_ref[0, qrows, sl] = sum(halves).astype(o_ref.dtype)


def _attention(q, k, v, kc, vc, bias):
    b, s, w = q.shape
    rows = s // GRID_W
    c = kc.shape[1]
    full = lambda n: pl.BlockSpec((1, n, w), lambda i, j: (i, 0, 0))
    return pl.pallas_call(
        functools.partial(_attn_kernel, rows),
        grid=(b, rows // ATTN_ROWS_PER_STEP),
        in_specs=[pl.BlockSpec((1, ATTN_ROWS_PER_STEP * GRID_W, w), lambda i, j: (i, j, 0)),
                  full(s), full(s), full(c), full(c),
                  pl.BlockSpec(bias.shape, lambda i, j: (0, 0, 0, 0))],
        out_specs=pl.BlockSpec((1, ATTN_ROWS_PER_STEP * GRID_W, w), lambda i, j: (i, j, 0)),
        out_shape=jax.ShapeDtypeStruct((b, s, w), BF16),
        scratch_shapes=[pltpu.VMEM((NA_HEADS, GRID_W, NA_WIN_ROWS * GRID_W + c), F32),
                        pltpu.VMEM((NA_HEADS, GRID_W, NA_WIN_ROWS * GRID_W + c), BF16)],
        compiler_params=_cparams(("arbitrary", "arbitrary")),
        name="nbr_attention",
    )(q, k, v, kc, vc, bias)


def _mix_kernel(x_ref, gu_ref, gv_ref, ga_ref, gb_ref, ya_ref, g1_ref, sh2_ref, sc2_ref,
                lng_ref, ws_ref, bs_ref, wpa_ref, wpb_ref, wout_ref, n2g_ref, wq_ref, keys_ref,
                x1_ref, h2_ref, st_ref):
    tm = x_ref.shape[1]
    gw = gu_ref.shape[2]
    u = jax.nn.gelu(gu_ref[0].astype(F32))
    t = jax.nn.gelu(gv_ref[0].astype(F32))
    tc = t - jnp.mean(t, axis=-1, keepdims=True)
    vn = tc * lax.rsqrt(jnp.mean(tc * tc, axis=-1, keepdims=True) + EPS) * lng_ref[...]
    vnb = vn.astype(BF16)
    lane = lax.broadcasted_iota(jnp.int32, (GM_CHUNK, LANES), 1)
    gd = gw // GM_GROUPS
    chunks = []
    for c in range(tm // GM_CHUNK):
        pairs = []
        for gp in range(gw // LANES):
            vp = vnb[c * GM_CHUNK:(c + 1) * GM_CHUNK, gp * LANES:(gp + 1) * LANES]
            r0 = _dot(ws_ref[2 * gp], vp)
            r1 = _dot(ws_ref[2 * gp + 1], vp)
            pairs.append(jnp.where(lane < gd, r0, r1))
        chunks.append(jnp.concatenate(pairs, axis=1) + bs_ref[...])
    mixed = jnp.concatenate(chunks, axis=0)
    yb = (u * mixed).astype(BF16)
    pa = _dot(ya_ref[0], wpa_ref[...])
    pb = _dot(yb, wpb_ref[...])
    m = jax.nn.sigmoid(ga_ref[0].astype(F32)) * pa + jax.nn.sigmoid(gb_ref[0].astype(F32)) * pb
    out = _dot(m.astype(BF16), wout_ref[...])
    x1 = x_ref[0] + g1_ref[0] * out
    x1_ref[0] = x1
    ms = jnp.mean(x1 * x1, axis=-1, keepdims=True)
    h2 = x1 * lax.rsqrt(ms + EPS) * n2g_ref[...]
    h2 = (h2 * (1.0 + sc2_ref[0]) + sh2_ref[0]).astype(BF16)
    half = h2.shape[1] // 2
    h2_ref[0] = _pack_bf16_pairs(h2[:, :half], h2[:, half:])
    qp = _dot(h2, wq_ref[...]).astype(BF16)
    for hp in range(keys_ref.shape[0]):
        st_ref[hp] = _dot_nt(keys_ref[hp], qp[:, hp * PEER_HALF:(hp + 1) * PEER_HALF])


def _mix(x, gu, gv, ga, gb, ya, g1, sh2, sc2, lng, ws, bs_full, wpa, wpb, wout, n2g, wq, keys, tm):
    b, s, d = x.shape
    gw = gu.shape[2]
    nt = s // tm
    nhp = keys.shape[0]
    act = lambda w: pl.BlockSpec((1, tm, w), lambda i, j: (i, j, 0))
    vec = pl.BlockSpec((1, 1, d), lambda i, j: (i, 0, 0))

    def const(a):
        nd = a.ndim
        return pl.BlockSpec(a.shape, lambda i, j: (0,) * nd)

    return pl.pallas_call(
        _mix_kernel,
        grid=(b, nt),
        in_specs=[act(d), act(gw), act(gw), act(d), act(d), act(gw), vec, vec, vec,
                  const(lng), const(ws), const(bs_full), const(wpa), const(wpb), const(wout),
                  const(n2g), const(wq), const(keys)],
        out_specs=[act(d), act(d // 2),
                   pl.BlockSpec((nhp, PEER_N_KEYS, tm), lambda i, j: (0, 0, i * nt + j))],
        out_shape=[jax.ShapeDtypeStruct((b, s, d), F32),
                   jax.ShapeDtypeStruct((b, s, d // 2), jnp.uint32),
                   jax.ShapeDtypeStruct((nhp, PEER_N_KEYS, b * s), F32)],
        compiler_params=_cparams(("arbitrary", "arbitrary")),
        name="mix_peer_scores",
    )(x, gu, gv, ga, gb, ya, g1, sh2, sc2, lng, ws, bs_full, wpa, wpb, wout, n2g, wq, keys)


def _extract_top(vals, order, payload, k):
    out_v, out_p = [], []
    for _ in range(k):
        m = jnp.max(vals, axis=0, keepdims=True)
        o = jnp.min(jnp.where(vals == m, order, jnp.inf), axis=0, keepdims=True)
        sel = order == o
        out_v.append(m)
        if payload is order:
            out_p.append(o)
        else:
            out_p.append(jnp.max(jnp.where(sel, payload, -1.0), axis=0, keepdims=True))
        vals = jnp.where(sel, -jnp.inf, vals)
    return jnp.concatenate(out_v, axis=0), jnp.concatenate(out_p, axis=0)


def _topk_kernel(st_ref, idx_ref, gate_ref, sv_ref, si_ref, se_ref, sg_ref):
    nhp = st_ref.shape[0]
    tm = st_ref.shape[2]
    k = PEER_TOPK
    kidx = lax.broadcasted_iota(jnp.int32, (PEER_N_KEYS, LANES), 0).astype(F32)
    sub = lax.broadcasted_iota(jnp.int32, (SUBLANES, LANES), 0).astype(F32)

    def lane_tile(lt, carry):
        lanes = pl.ds(pl.multiple_of(lt * LANES, LANES), LANES)

        def stage1(h, c):
            for hp in (4 * h, 4 * h + 1, 4 * h + 2, 4 * h + 3):
                v, i = _extract_top(st_ref[hp, :, lanes], kidx, kidx, k)
                sv_ref[hp] = v
                si_ref[hp] = i
            return c

        lax.fori_loop(0, nhp // 4, stage1, 0)

        def stage2(h2, c):
            for h in (2 * h2, 2 * h2 + 1):
                stage2_head(h)
            return c

        def stage2_head(h):
            s1, s2 = sv_ref[2 * h], sv_ref[2 * h + 1]
            i1, i2 = si_ref[2 * h] * float(PEER_N_KEYS), si_ref[2 * h + 1]
            cv, co, ce = [], [], []
            for half in range(2):
                b0 = half * SUBLANES
                cv.append(s1[0:1] + s2[b0:b0 + SUBLANES])
                co.append(sub + float(b0))
                ce.append(i1[0:1] + i2[b0:b0 + SUBLANES])
            for a in range(1, SUBLANES):
                cv.append(s1[a:a + 1] + s2[0:SUBLANES])
                co.append(sub + float(a * k))
                ce.append(i1[a:a + 1] + i2[0:SUBLANES])
            cv.append(s1[SUBLANES:k] + s2[0:1])
            co.append((sub + float(SUBLANES)) * float(k))
            ce.append(i1[SUBLANES:k] + i2[0:1])
            bv, be = _extract_top(jnp.concatenate(cv, axis=0), jnp.concatenate(co, axis=0),
                                  jnp.concatenate(ce, axis=0), k)
            ex = jnp.exp(bv - bv[0:1])
            rows = pl.ds(pl.multiple_of(h * k, k), k)
            sg_ref[rows, :] = ex / jnp.sum(ex, axis=0, keepdims=True)
            se_ref[rows, :] = be

        lax.fori_loop(0, nhp // 4, stage2, 0)

        rows = pl.ds(pl.multiple_of(lt * LANES, LANES), LANES)
        idx_ref[rows, :] = se_ref[...].T.astype(jnp.int32)
        gate_ref[rows, :] = sg_ref[...].T
        return carry

    lax.fori_loop(0, tm // LANES, lane_tile, 0)


def _topk(st, tm):
    nhp, nk, t = st.shape
    ne = (nhp // 2) * PEER_TOPK
    return pl.pallas_call(
        _topk_kernel,
        grid=(t // tm,),
        in_specs=[pl.BlockSpec((nhp, nk, tm), lambda i: (0, 0, i))],
        out_specs=[pl.BlockSpec((tm, ne), lambda i: (i, 0)),
                   pl.BlockSpec((tm, ne), lambda i: (i, 0))],
        out_shape=[jax.ShapeDtypeStruct((t, ne), jnp.int32),
                   jax.ShapeDtypeStruct((t, ne), F32)],
        scratch_shapes=[pltpu.VMEM((nhp, PEER_TOPK, LANES), F32),
                        pltpu.VMEM((nhp, PEER_TOPK, LANES), F32),
                        pltpu.VMEM((ne, LANES), F32),
                        pltpu.VMEM((ne, LANES), F32)],
        compiler_params=_cparams(("arbitrary",)),
        name="peer_topk",
    )(st)


SC_LANES = 16
SC_WINDOW = 32
SC_ROW_BUFFERS = 4
SC_TOKENS_PER_STEP = 8
SC_WSUM_CHUNKS = 8
SC_BF16_TERMS = 4


def _pack_table_kernel(t_ref, o_ref):
    half = o_ref.shape[1]
    o_ref[...] = _pack_bf16_pairs(t_ref[:, :half].astype(BF16), t_ref[:, half:].astype(BF16))


def _pack_table(t, tn):
    n, d = t.shape
    assert n % tn == 0
    return pl.pallas_call(
        _pack_table_kernel, grid=(n // tn,),
        in_specs=[pl.BlockSpec((tn, d), lambda i: (i, 0))],
        out_specs=pl.BlockSpec((tn, d // 2), lambda i: (i, 0)),
        out_shape=jax.ShapeDtypeStruct((n, d // 2), jnp.uint32),
        compiler_params=_cparams(("arbitrary",)), name="pack_table",
    )(t)


def _sc_bf16(words):
    return plsc.bitcast(words, BF16)


def _sc_phase(table, idx, operand, mode):
    t, ne = idx.shape
    hw = table.shape[1]
    d = 2 * hw
    L = SC_LANES
    win = SC_WINDOW
    nbuf = SC_ROW_BUFFERS
    ahead = nbuf - 1
    tps = SC_TOKENS_PER_STEP
    nq = SC_WSUM_CHUNKS
    nterm = SC_BF16_TERMS
    info = plsc.get_sparse_core_info()
    nc, ns = info.num_cores, info.num_subcores
    nss = t // (nc * ns * tps)
    wps = tps * ne // win
    wpt = ne // win
    blk = tps * ne
    nwin = nss * wps
    assert nss * nc * ns * tps == t and nss % 2 == 0 and wps % nbuf == 0 and wps > ahead
    assert wpt * win == ne and win % L == 0 and hw % (nq * L) == 0 and hw % (nterm * L) == 0 and win % nterm == 0
    dots = mode == "dots"
    assert dots or nbuf % wpt == 0
    op_buf = pltpu.VMEM((2, tps, hw), jnp.uint32) if dots else pltpu.VMEM((2 * blk,), jnp.int32)
    res_buf = pltpu.VMEM((2 * blk,), F32) if dots else pltpu.VMEM((2, tps, d), F32)
    out_type = jax.ShapeDtypeStruct((t * ne,), F32) if dots else jax.ShapeDtypeStruct((t, d), F32)
    scratch = [("idx", pltpu.VMEM((2 * blk,), jnp.int32)), ("op_" + mode, op_buf),
               ("rows", pltpu.VMEM((nbuf, win, hw), jnp.uint32)), ("res_" + mode, res_buf),
               ("isem", pltpu.SemaphoreType.DMA((2,))), ("psem", pltpu.SemaphoreType.DMA((2,))),
               ("gsem", pltpu.SemaphoreType.DMA((nbuf,))), ("osem", pltpu.SemaphoreType.DMA((2,)))]

    def sc_kernel(table_hbm, idx_hbm, op_hbm, out_hbm, idx_v, op_v, rows_v, res_v, isem, psem, gsem, osem):
        ss0 = (lax.axis_index("s") * nc + lax.axis_index("c")) * nss
        lane = lax.iota(jnp.int32, L)

        def flat(ref, n):
            return ref.at[pl.ds(pl.multiple_of(n * blk, blk), blk)]

        def tok_rows(ref, n):
            return ref.at[pl.ds(pl.multiple_of(n * tps, tps), tps)]

        def idx_copy(ss, slot):
            return pltpu.make_async_copy(flat(idx_hbm, ss0 + ss), flat(idx_v, slot), isem.at[slot])

        def op_copy(ss, slot):
            if dots:
                return pltpu.make_async_copy(tok_rows(op_hbm, ss0 + ss), op_v.at[slot], psem.at[slot])
            return pltpu.make_async_copy(flat(op_hbm, ss0 + ss), flat(op_v, slot), psem.at[slot])

        def out_copy(ss, slot):
            if dots:
                return pltpu.make_async_copy(flat(res_v, slot), flat(out_hbm, ss0 + ss), osem.at[slot])
            return pltpu.make_async_copy(res_v.at[slot], tok_rows(out_hbm, ss0 + ss), osem.at[slot])

        def gather(slot, hs, b):
            ix = idx_v.at[pl.ds(pl.multiple_of(slot * blk + hs * win, win), win)]
            return pltpu.make_async_copy(table_hbm.at[ix], rows_v.at[b], gsem.at[b])

        def reduce_dots(slot, hs, b):
            tok = hs // wpt

            def head(g, c):
                def chunk(jj, accs):
                    off = pl.multiple_of(jj * (nterm * L), nterm * L)
                    xs = [_sc_bf16(op_v[slot, tok, pl.ds(off + q * L, L)]) for q in range(nterm)]
                    new = []
                    for r in range(L):
                        p = _sc_bf16(rows_v[b, g * L + r, pl.ds(off, L)]) * xs[0]
                        for q in range(1, nterm):
                            p = p + _sc_bf16(rows_v[b, g * L + r, pl.ds(off + q * L, L)]) * xs[q]
                        lo, hi = plsc.unpack(p, format=plsc.PackFormat.INTERLEAVED)
                        new.append(accs[r] + lo + hi)
                    return tuple(new)

                accs = lax.fori_loop(0, hw // (nterm * L), chunk, tuple(jnp.zeros((L,), F32) for _ in range(L)))
                out = jnp.zeros((L,), F32)
                for r in range(L):
                    out = jnp.where(lane == r, jnp.sum(accs[r]), out)
                res_v[pl.ds(pl.multiple_of(slot * blk + hs * win + g * L, L), L)] = out
                return c

            lax.fori_loop(0, win // L, head, 0)

        def reduce_wsum(slot, hs, b):
            tok = hs // wpt
            wbase = slot * blk + hs * win
            first = b % wpt == 0

            def colgroup(cg, c):
                col = pl.multiple_of(cg * (nq * L), nq * L)
                if first:
                    accs = tuple(jnp.zeros((L,), F32) for _ in range(2 * nq))
                else:
                    accs = tuple(res_v[slot, tok, pl.ds(col + q * L, L)] for q in range(nq)) + \
                           tuple(res_v[slot, tok, pl.ds(hw + col + q * L, L)] for q in range(nq))

                def rowgroup(rg, accs):
                    accs = list(accs)
                    r = rg * nterm
                    wvec = op_v[pl.ds(pl.multiple_of(wbase + (r // L) * L, L), L)]
                    ws = [_sc_bf16(wvec.at[jnp.full((L,), r % L + k, jnp.int32)].get(mode="promise_in_bounds"))
                          for k in range(nterm)]
                    for q in range(nq):
                        p = _sc_bf16(rows_v[b, r, pl.ds(col + q * L, L)]) * ws[0]
                        for k in range(1, nterm):
                            p = p + _sc_bf16(rows_v[b, r + k, pl.ds(col + q * L, L)]) * ws[k]
                        lo, hi = plsc.unpack(p, format=plsc.PackFormat.INTERLEAVED)
                        accs[q] = accs[q] + lo
                        accs[nq + q] = accs[nq + q] + hi
                    return tuple(accs)

                accs = lax.fori_loop(0, win // nterm, rowgroup, accs)
                for q in range(nq):
                    res_v[slot, tok, pl.ds(col + q * L, L)] = accs[q]
                    res_v[slot, tok, pl.ds(hw + col + q * L, L)] = accs[nq + q]
                return c

            lax.fori_loop(0, hw // (nq * L), colgroup, 0)

        idx_copy(0, 0).start()
        op_copy(0, 0).start()
        idx_copy(1, 1).start()
        op_copy(1, 1).start()
        idx_copy(0, 0).wait()
        op_copy(0, 0).wait()
        for a in range(ahead):
            gather(0, a, a).start()

        @pl.loop(0, nwin, step=nbuf)
        def _(i0):
            for b in range(nbuf):
                i = i0 + b
                ss = i // wps
                hs = i % wps
                slot = ss % 2
                gather(slot, hs, b).wait()
                nb = (b + ahead) % nbuf

                @pl.when(hs + ahead < wps)
                def _():
                    gather(slot, hs + ahead, nb).start()

                @pl.when(jnp.logical_and(hs + ahead >= wps, ss + 1 < nss))
                def _():
                    @pl.when(hs + ahead == wps)
                    def _():
                        idx_copy(ss + 1, 1 - slot).wait()
                        op_copy(ss + 1, 1 - slot).wait()

                    gather(1 - slot, hs + ahead - wps, nb).start()

                @pl.when(jnp.logical_and(hs == 0, ss >= 2))
                def _():
                    out_copy(ss - 2, slot).wait()

                if dots:
                    reduce_dots(slot, hs, b)
                else:
                    reduce_wsum(slot, hs, b)

                @pl.when(hs + 1 == wps)
                def _():
                    out_copy(ss, slot).start()

                    @pl.when(ss + 2 < nss)
                    def _():
                        idx_copy(ss + 2, slot).start()
                        op_copy(ss + 2, slot).start()

        out_copy(nss - 2, 0).wait()
        out_copy(nss - 1, 1).wait()

    op = operand if dots else operand.reshape(t * ne)
    return sc_kernel, scratch, out_type, (table, idx.reshape(t * ne), op)


def _sc_call(*phases):
    parts = [_sc_phase(table, idx, operand, mode) for mode, table, idx, operand in phases]
    names, types = [], []
    for _, scratch, _, _ in parts:
        for name, ty in scratch:
            if name not in names:
                names.append(name)
                types.append(ty)
    n = len(parts)

    def fused(*refs):
        ins, outs, scr = refs[:3 * n], refs[3 * n:4 * n], dict(zip(names, refs[4 * n:]))
        for i, (body, scratch, _, _) in enumerate(parts):
            body(*ins[3 * i:3 * i + 3], outs[i], *[scr[name] for name, _ in scratch])

    call = pl.kernel(
        fused, mesh=plsc.VectorSubcoreMesh(core_axis_name="c", subcore_axis_name="s"),
        out_type=tuple(p[2] for p in parts), scratch_types=types,
        compiler_params=pltpu.CompilerParams(needs_layout_passes=False),
        name="peer_" + "_".join(m for m, _, _, _ in phases),
    )
    return call(*[a for p in parts for a in p[3]])


def _peer_weight_kernel(act_ref, gate_ref, w_ref):
    w = (jax.nn.gelu(act_ref[...]) * gate_ref[...]).astype(BF16)
    w_ref[...] = pltpu.bitcast(_pack_bf16_pairs(w, w), jnp.int32)


def _peer_weight(act, gate, tm):
    t, ne = act.shape
    tm = min(tm, t)
    assert t % tm == 0
    spec = pl.BlockSpec((tm, ne), lambda i: (i, 0))
    return pl.pallas_call(
        _peer_weight_kernel, grid=(t // tm,), in_specs=[spec, spec], out_specs=spec,
        out_shape=jax.ShapeDtypeStruct((t, ne), jnp.int32),
        compiler_params=_cparams(("arbitrary",)), name="peer_weight",
    )(act, gate)


def _final_kernel(x1_ref, p_ref, g2_ref, fg_ref, prev_ref, o_ref):
    del prev_ref
    x2 = x1_ref[0] + g2_ref[0] * p_ref[0]
    ms = jnp.mean(x2 * x2, axis=-1, keepdims=True)
    o_ref[0] = x2 * lax.rsqrt(ms + EPS) * fg_ref[...]


def _final(x1, peer, g2, fg, tm, out, b0):
    bc, s, d = x1.shape
    act = pl.BlockSpec((1, tm, d), lambda i, j: (i, j, 0))
    return pl.pallas_call(
        _final_kernel, grid=(bc, s // tm),
        in_specs=[act, act, pl.BlockSpec((1, 1, d), lambda i, j: (i, 0, 0)),
                  pl.BlockSpec((1, d), lambda i, j: (0, 0)),
                  pl.BlockSpec(memory_space=pl.ANY)],
        out_specs=pl.BlockSpec((1, tm, d), lambda i, j: (i + b0, j, 0)),
        out_shape=jax.ShapeDtypeStruct(out.shape, F32),
        input_output_aliases={4: 0},
        compiler_params=_cparams(("arbitrary", "arbitrary")), name="final_norm",
    )(x1, peer, g2, fg, out)


def kernel(x, c, ctx, c_ctx, ada_w, ada_b, norm1_g, norm2_g, w_in, na_rpb, gm_ln_g, gm_ws, gm_bs,
           w_proj_a, w_proj_b, w_out, peer_wq, peer_keys, peer_u, peer_v, final_g):
    b, s, d = x.shape
    naw = NA_HEADS * HEAD_DIM
    gw = gm_ln_g.shape[1]
    layer = 0

    pad = (-(b + 1)) % SUBLANES
    c_all = jnp.concatenate([c, c_ctx[None, :], jnp.zeros((pad, d), F32)], axis=0)
    mod = _adaln(c_all, ada_w[layer], ada_b[layer])
    sh1, sc1, g1, sh2, sc2, g2 = [mod[:b, i * d:(i + 1) * d].reshape(b, 1, d) for i in range(ADA_CHUNKS)]
    csh1 = jnp.broadcast_to(mod[b, 0:d].reshape(1, 1, d), (b, 1, d))
    csc1 = jnp.broadcast_to(mod[b, d:2 * d].reshape(1, 1, d), (b, 1, d))

    w = w_in[layer].astype(BF16)
    w_kv = w[:, naw:3 * naw]
    n1g = norm1_g[layer].reshape(1, d)
    bias = _bias_table(na_rpb[layer])
    bs_full = jnp.repeat(gm_bs[layer].T, gw // GM_GROUPS, axis=1)
    keys = peer_keys[layer].reshape(2 * PEER_HEADS, PEER_N_KEYS, PEER_HALF).astype(BF16)
    lng = gm_ln_g[layer].reshape(1, gw)
    ws = gm_ws[layer].astype(BF16)
    wpa, wpb, wout = w_proj_a[layer].astype(BF16), w_proj_b[layer].astype(BF16), w_out[layer].astype(BF16)
    n2g = norm2_g[layer].reshape(1, d)
    wq = peer_wq[layer].astype(BF16)
    tab_u, tab_v = _pack_table(peer_u[layer], TILE_PACK), _pack_table(peer_v[layer], TILE_PACK)
    fg = final_g.reshape(1, d)
    ne = PEER_HEADS * PEER_TOPK

    assert sum(BATCH_CHUNKS) == b
    out = jnp.zeros((b, s, d), F32)

    def finish(out, pending, peer):
        idx, wgt, x1, g2c, boff = pending
        return _final(x1, peer.reshape(x1.shape), g2c, fg, TILE_FINAL, out, boff)

    pending = None
    b0 = 0
    for bc in BATCH_CHUNKS:
        sl = slice(b0, b0 + bc)
        boff = b0
        b0 += bc
        t = bc * s
        xc = x[sl]
        q, k, v, gu, gv, ga, gb = _norm_proj(xc, n1g, sh1[sl], sc1[sl], w, (naw, naw, naw, gw, gw, d, d), TILE_PROJ)
        k_c, v_c = _norm_proj(ctx[sl], n1g, csh1[sl], csc1[sl], w_kv, (naw, naw), ctx.shape[1])
        y_a = _attention(q, k, v, k_c, v_c, bias)
        x1, h2, st = _mix(xc, gu, gv, ga, gb, y_a, g1[sl], sh2[sl], sc2[sl], lng, ws, bs_full,
                          wpa, wpb, wout, n2g, wq, keys, TILE_MIX)
        idx, gate = _topk(st, TILE_TOPK)
        dots = ("dots", tab_u, idx, h2.reshape(t, d // 2))
        if pending is None:
            act, = _sc_call(dots)
        else:
            peer, act = _sc_call(("wsum", tab_v) + pending[:2], dots)
            out = finish(out, pending, peer)
        wgt = _peer_weight(act.reshape(t, ne), gate, TILE_WEIGHT)
        pending = (idx, wgt, x1, g2[sl], boff)
    peer, = _sc_call(("wsum", tab_v) + pending[:2])
    return finish(out, pending, peer)
```

```python
import functools

import jax
import jax.numpy as jnp
from jax import lax
from jax.experimental import pallas as pl
from jax.experimental.pallas import tpu as pltpu
from jax.experimental.pallas import tpu_sc as plsc

F32 = jnp.float32
BF16 = jnp.bfloat16

GRID_W = 64
NA_HEADS = 8
HEAD_DIM = 64
NA_WIN_ROWS = 8
NA_WIN_COLS = 16
GM_GROUPS = 8
GM_CHUNK = 128
PEER_HEADS = 8
PEER_N_KEYS = 128
PEER_TOPK = 16
PEER_HALF = 128
ADA_CHUNKS = 6
EPS = 1e-6
NEG_INF = -1e30

LANES = 128
SUBLANES = 8
VMEM_LIMIT = 56 * 1024 * 1024
TILE_ADALN_COLS = 1024
TILE_PROJ = 512
TILE_MIX = 512
TILE_WEIGHT = 2048
TILE_FINAL = 512
TILE_PACK = 1024
ATTN_ROWS_PER_STEP = 2
BATCH_CHUNKS = (1, 1, 2, 3, 4, 5)


def _dot(a, b):
    return lax.dot_general(a, b, (((1,), (0,)), ((), ())), preferred_element_type=F32)


def _dot_nt(a, b):
    return lax.dot_general(a, b, (((1,), (1,)), ((), ())), preferred_element_type=F32)


def _cparams(sem):
    return pltpu.CompilerParams(dimension_semantics=sem, vmem_limit_bytes=VMEM_LIMIT)


def _pack_bf16_pairs(lo, hi):
    lo_bits = pltpu.bitcast(lo.astype(F32), jnp.uint32) >> 16
    hi_bits = pltpu.bitcast(hi.astype(F32), jnp.uint32) & jnp.uint32(0xFFFF0000)
    return lo_bits | hi_bits


def _adaln_kernel(c_ref, w_ref, b_ref, o_ref):
    c = c_ref[...]
    s = c * jax.nn.sigmoid(c)
    o_ref[...] = lax.dot_general(s, w_ref[...], (((1,), (0,)), ((), ())),
                                 precision=lax.Precision.HIGHEST,
                                 preferred_element_type=F32) + b_ref[...]


def _adaln(c_all, w, b):
    m, d = c_all.shape
    n = w.shape[1]
    tn = TILE_ADALN_COLS
    return pl.pallas_call(
        _adaln_kernel,
        grid=(n // tn,),
        in_specs=[pl.BlockSpec((m, d), lambda j: (0, 0)),
                  pl.BlockSpec((d, tn), lambda j: (0, j)),
                  pl.BlockSpec((1, tn), lambda j: (0, j))],
        out_specs=pl.BlockSpec((m, tn), lambda j: (0, j)),
        out_shape=jax.ShapeDtypeStruct((m, n), F32),
        compiler_params=_cparams(("arbitrary",)),
        name="adaln",
    )(c_all, w, b.reshape(1, n))


def _norm_proj_kernel(widths, x_ref, g_ref, sh_ref, sc_ref, w_ref, *o_refs):
    x = x_ref[0]
    ms = jnp.mean(x * x, axis=-1, keepdims=True)
    y = x * lax.rsqrt(ms + EPS) * g_ref[...]
    h = (y * (1.0 + sc_ref[0]) + sh_ref[0]).astype(BF16)
    off = 0
    for o_ref, wd in zip(o_refs, widths):
        o_ref[0] = _dot(h, w_ref[:, off:off + wd]).astype(o_ref.dtype)
        off += wd


def _norm_proj(x, gain, shift, scale, w, widths, tm):
    b, s, d = x.shape
    n = w.shape[1]
    assert sum(widths) == n and s % tm == 0
    vec = pl.BlockSpec((1, 1, d), lambda i, j: (i, 0, 0))
    return pl.pallas_call(
        functools.partial(_norm_proj_kernel, widths),
        grid=(b, s // tm),
        in_specs=[pl.BlockSpec((1, tm, d), lambda i, j: (i, j, 0)),
                  pl.BlockSpec((1, d), lambda i, j: (0, 0)),
                  vec, vec,
                  pl.BlockSpec((d, n), lambda i, j: (0, 0))],
        out_specs=[pl.BlockSpec((1, tm, wd), lambda i, j: (i, j, 0)) for wd in widths],
        out_shape=[jax.ShapeDtypeStruct((b, s, wd), BF16) for wd in widths],
        compiler_params=_cparams(("arbitrary", "arbitrary")),
        name="norm_proj",
    )(x, gain, shift, scale, w)


def _bias_table_kernel(rpb_ref, o_ref):
    h = pl.program_id(0)
    q = lax.broadcasted_iota(jnp.int32, (GRID_W, GRID_W), 0)
    kc = lax.broadcasted_iota(jnp.int32, (GRID_W, GRID_W), 1)
    dc = jnp.clip(kc - q + NA_WIN_COLS - 1, 0, 2 * NA_WIN_COLS - 2)
    cs = jnp.clip(q - NA_WIN_COLS // 2, 0, GRID_W - NA_WIN_COLS)
    col_in = (kc >= cs) & (kc < cs + NA_WIN_COLS)
    n_dc = 2 * NA_WIN_COLS - 1
    n_dr = 2 * NA_WIN_ROWS - 1
    for dr in range(n_dr):
        t = jnp.zeros((GRID_W, GRID_W), F32)
        for c in range(n_dc):
            t = jnp.where(dc == c, rpb_ref[h * n_dr + dr, c], t)
        t = jnp.where(col_in, t, NEG_INF)
        for d0 in range(NA_WIN_ROWS):
            j = dr - d0
            if 0 <= j < NA_WIN_ROWS:
                o_ref[0, d0, :, j * GRID_W:(j + 1) * GRID_W] = t


def _bias_table(rpb):
    nh, n_dr, n_dc = rpb.shape
    band = NA_WIN_ROWS * GRID_W
    return pl.pallas_call(
        _bias_table_kernel,
        grid=(nh,),
        in_specs=[pl.BlockSpec(memory_space=pltpu.SMEM)],
        out_specs=pl.BlockSpec((1, NA_WIN_ROWS, GRID_W, band), lambda h: (h, 0, 0, 0)),
        out_shape=jax.ShapeDtypeStruct((nh, NA_WIN_ROWS, GRID_W, band), F32),
        compiler_params=_cparams(("arbitrary",)),
        name="bias_table",
    )(rpb.reshape(nh * n_dr, n_dc))


def _attn_kernel(rows, q_ref, k_ref, v_ref, kc_ref, vc_ref, bias_ref, o_ref, s_scr, p_scr):
    for rr in range(ATTN_ROWS_PER_STEP):
        _attn_row(rows, pl.program_id(1) * ATTN_ROWS_PER_STEP + rr, slice(rr * GRID_W, (rr + 1) * GRID_W),
                  q_ref, k_ref, v_ref, kc_ref, vc_ref, bias_ref, o_ref, s_scr, p_scr)


def _attn_row(rows, r, qrows, q_ref, k_ref, v_ref, kc_ref, vc_ref, bias_ref, o_ref, s_scr, p_scr):
    rs = jnp.clip(r - NA_WIN_ROWS // 2, 0, rows - NA_WIN_ROWS)
    d0 = rs - r + NA_WIN_ROWS - 1
    band = NA_WIN_ROWS * GRID_W
    start = pl.multiple_of(rs * GRID_W, GRID_W)
    scale = HEAD_DIM ** -0.5
    lane = lax.broadcasted_iota(jnp.int32, (GRID_W, LANES), 1)
    hpg = LANES // HEAD_DIM
    groups = NA_HEADS // hpg

    def mine(hh):
        return (lane >= hh * HEAD_DIM) & (lane < (hh + 1) * HEAD_DIM)

    for hp in range(groups):
        sl = slice(hp * LANES, (hp + 1) * LANES)
        q2 = q_ref[0, qrows, sl]
        kb = k_ref[0, pl.ds(start, band), sl]
        kc = kc_ref[0, :, sl]
        for hh in range(hpg):
            h = hp * hpg + hh
            qh = jnp.where(mine(hh), q2, jnp.zeros_like(q2))
            s_scr[h, :, :band] = _dot_nt(qh, kb) * scale + bias_ref[h, d0]
            s_scr[h, :, band:] = _dot_nt(qh, kc) * scale
    dens = []
    for h in range(NA_HEADS):
        s = s_scr[h]
        p = jnp.exp(s - jnp.max(s, axis=-1, keepdims=True))
        dens.append(jnp.sum(p, axis=-1, keepdims=True))
        p_scr[h] = p.astype(BF16)
    for hp in range(groups):
        sl = slice(hp * LANES, (hp + 1) * LANES)
        vb = v_ref[0, pl.ds(start, band), sl]
        vc = vc_ref[0, :, sl]
        halves = []
        for hh in range(hpg):
            h = hp * hpg + hh
            o = _dot(p_scr[h, :, :band], vb) + _dot(p_scr[h, :, band:], vc)
            halves.append(jnp.where(mine(hh), o / dens[h], 0.0))
        o_ref[0, qrows, sl] = sum(halves).astype(o_ref.dtype)


def _attention(q, k, v, kc, vc, bias):
    b, s, w = q.shape
    rows = s // GRID_W
    c = kc.shape[1]
    full = lambda n: pl.BlockSpec((1, n, w), lambda i, j: (i, 0, 0))
    return pl.pallas_call(
        functools.partial(_attn_kernel, rows),
        grid=(b, rows // ATTN_ROWS_PER_STEP),
        in_specs=[pl.BlockSpec((1, ATTN_ROWS_PER_STEP * GRID_W, w), lambda i, j: (i, j, 0)),
                  full(s), full(s), full(c), full(c),
                  pl.BlockSpec(bias.shape, lambda i, j: (0, 0, 0, 0))],
        out_specs=pl.BlockSpec((1, ATTN_ROWS_PER_STEP * GRID_W, w), lambda i, j: (i, j, 0)),
        out_shape=jax.ShapeDtypeStruct((b, s, w), BF16),
        scratch_shapes=[pltpu.VMEM((NA_HEADS, GRID_W, NA_WIN_ROWS * GRID_W + c), F32),
                        pltpu.VMEM((NA_HEADS, GRID_W, NA_WIN_ROWS * GRID_W + c), BF16)],
        compiler_params=_cparams(("arbitrary", "arbitrary")),
        name="nbr_attention",
    )(q, k, v, kc, vc, bias)


def _mix_kernel(x_ref, gu_ref, gv_ref, ga_ref, gb_ref, ya_ref, g1_ref, sh2_ref, sc2_ref,
                lng_ref, ws_ref, bs_ref, wpa_ref, wpb_ref, wout_ref, n2g_ref, wq_ref, keys_ref,
                x1_ref, h2_ref, idx_ref, gate_ref, st_ref, sv_ref, si_ref, se_ref, sg_ref):
    tm = x_ref.shape[1]
    gw = gu_ref.shape[2]
    u = jax.nn.gelu(gu_ref[0].astype(F32))
    t = jax.nn.gelu(gv_ref[0].astype(F32))
    tc = t - jnp.mean(t, axis=-1, keepdims=True)
    vn = tc * lax.rsqrt(jnp.mean(tc * tc, axis=-1, keepdims=True) + EPS) * lng_ref[...]
    vnb = vn.astype(BF16)
    lane = lax.broadcasted_iota(jnp.int32, (GM_CHUNK, LANES), 1)
    gd = gw // GM_GROUPS
    chunks = []
    for c in range(tm // GM_CHUNK):
        pairs = []
        for gp in range(gw // LANES):
            vp = vnb[c * GM_CHUNK:(c + 1) * GM_CHUNK, gp * LANES:(gp + 1) * LANES]
            r0 = _dot(ws_ref[2 * gp], vp)
            r1 = _dot(ws_ref[2 * gp + 1], vp)
            pairs.append(jnp.where(lane < gd, r0, r1))
        chunks.append(jnp.concatenate(pairs, axis=1) + bs_ref[...])
    mixed = jnp.concatenate(chunks, axis=0)
    yb = (u * mixed).astype(BF16)
    pa = _dot(ya_ref[0], wpa_ref[...])
    pb = _dot(yb, wpb_ref[...])
    m = jax.nn.sigmoid(ga_ref[0].astype(F32)) * pa + jax.nn.sigmoid(gb_ref[0].astype(F32)) * pb
    out = _dot(m.astype(BF16), wout_ref[...])
    x1 = x_ref[0] + g1_ref[0] * out
    x1_ref[0] = x1
    ms = jnp.mean(x1 * x1, axis=-1, keepdims=True)
    h2 = x1 * lax.rsqrt(ms + EPS) * n2g_ref[...]
    h2 = (h2 * (1.0 + sc2_ref[0]) + sh2_ref[0]).astype(BF16)
    half = h2.shape[1] // 2
    h2_ref[0] = _pack_bf16_pairs(h2[:, :half], h2[:, half:])
    qp = _dot(h2, wq_ref[...]).astype(BF16)
    for hp in range(keys_ref.shape[0]):
        st_ref[hp] = _dot_nt(keys_ref[hp], qp[:, hp * PEER_HALF:(hp + 1) * PEER_HALF])
    _topk_body(st_ref, idx_ref.at[0], gate_ref.at[0], sv_ref, si_ref, se_ref, sg_ref)


def _mix(x, gu, gv, ga, gb, ya, g1, sh2, sc2, lng, ws, bs_full, wpa, wpb, wout, n2g, wq, keys, tm):
    b, s, d = x.shape
    gw = gu.shape[2]
    nt = s // tm
    nhp = keys.shape[0]
    ne = (nhp // 2) * PEER_TOPK
    assert s % tm == 0 and tm % LANES == 0
    act = lambda w: pl.BlockSpec((1, tm, w), lambda i, j: (i, j, 0))
    vec = pl.BlockSpec((1, 1, d), lambda i, j: (i, 0, 0))

    def const(a):
        nd = a.ndim
        return pl.BlockSpec(a.shape, lambda i, j: (0,) * nd)

    return pl.pallas_call(
        _mix_kernel,
        grid=(b, nt),
        in_specs=[act(d), act(gw), act(gw), act(d), act(d), act(gw), vec, vec, vec,
                  const(lng), const(ws), const(bs_full), const(wpa), const(wpb), const(wout),
                  const(n2g), const(wq), const(keys)],
        out_specs=[act(d), act(d // 2), act(ne), act(ne)],
        out_shape=[jax.ShapeDtypeStruct((b, s, d), F32),
                   jax.ShapeDtypeStruct((b, s, d // 2), jnp.uint32),
                   jax.ShapeDtypeStruct((b, s, ne), jnp.int32),
                   jax.ShapeDtypeStruct((b, s, ne), F32)],
        scratch_shapes=[pltpu.VMEM((nhp, PEER_N_KEYS, tm), F32),
                        pltpu.VMEM((nhp, PEER_TOPK, LANES), F32),
                        pltpu.VMEM((nhp, PEER_TOPK, LANES), F32),
                        pltpu.VMEM((ne, LANES), F32),
                        pltpu.VMEM((ne, LANES), F32)],
        compiler_params=_cparams(("arbitrary", "arbitrary")),
        name="mix_peer_topk",
    )(x, gu, gv, ga, gb, ya, g1, sh2, sc2, lng, ws, bs_full, wpa, wpb, wout, n2g, wq, keys)


def _extract_top(vals, order, payload, k):
    out_v, out_p = [], []
    for _ in range(k):
        m = jnp.max(vals, axis=0, keepdims=True)
        o = jnp.min(jnp.where(vals == m, order, jnp.inf), axis=0, keepdims=True)
        sel = order == o
        out_v.append(m)
        if payload is order:
            out_p.append(o)
        else:
            out_p.append(jnp.max(jnp.where(sel, payload, -1.0), axis=0, keepdims=True))
        vals = jnp.where(sel, -jnp.inf, vals)
    return jnp.concatenate(out_v, axis=0), jnp.concatenate(out_p, axis=0)


def _topk_body(st_ref, idx_ref, gate_ref, sv_ref, si_ref, se_ref, sg_ref):
    nhp = st_ref.shape[0]
    tm = st_ref.shape[2]
    k = PEER_TOPK
    kidx = lax.broadcasted_iota(jnp.int32, (PEER_N_KEYS, LANES), 0).astype(F32)
    sub = lax.broadcasted_iota(jnp.int32, (SUBLANES, LANES), 0).astype(F32)

    def lane_tile(lt, carry):
        lanes = pl.ds(pl.multiple_of(lt * LANES, LANES), LANES)

        def stage1(h, c):
            for hp in (4 * h, 4 * h + 1, 4 * h + 2, 4 * h + 3):
                v, i = _extract_top(st_ref[hp, :, lanes], kidx, kidx, k)
                sv_ref[hp] = v
                si_ref[hp] = i
            return c

        lax.fori_loop(0, nhp // 4, stage1, 0)

        def stage2(h2, c):
            for h in (2 * h2, 2 * h2 + 1):
                stage2_head(h)
            return c

        def stage2_head(h):
            s1, s2 = sv_ref[2 * h], sv_ref[2 * h + 1]
            i1, i2 = si_ref[2 * h] * float(PEER_N_KEYS), si_ref[2 * h + 1]
            cv, co, ce = [], [], []
            for half in range(2):
                b0 = half * SUBLANES
                cv.append(s1[0:1] + s2[b0:b0 + SUBLANES])
                co.append(sub + float(b0))
                ce.append(i1[0:1] + i2[b0:b0 + SUBLANES])
            for a in range(1, SUBLANES):
                cv.append(s1[a:a + 1] + s2[0:SUBLANES])
                co.append(sub + float(a * k))
                ce.append(i1[a:a + 1] + i2[0:SUBLANES])
            cv.append(s1[SUBLANES:k] + s2[0:1])
            co.append((sub + float(SUBLANES)) * float(k))
            ce.append(i1[SUBLANES:k] + i2[0:1])
            bv, be = _extract_top(jnp.concatenate(cv, axis=0), jnp.concatenate(co, axis=0),
                                  jnp.concatenate(ce, axis=0), k)
            ex = jnp.exp(bv - bv[0:1])
            rows = pl.ds(pl.multiple_of(h * k, k), k)
            sg_ref[rows, :] = ex / jnp.sum(ex, axis=0, keepdims=True)
            se_ref[rows, :] = be

        lax.fori_loop(0, nhp // 4, stage2, 0)

        rows = pl.ds(pl.multiple_of(lt * LANES, LANES), LANES)
        idx_ref[rows, :] = se_ref[...].T.astype(jnp.int32)
        gate_ref[rows, :] = sg_ref[...].T
        return carry

    lax.fori_loop(0, tm // LANES, lane_tile, 0)


SC_LANES = 16
SC_WINDOW = 32
SC_ROW_BUFFERS = 4
SC_TOKENS_PER_STEP = 8
SC_WSUM_CHUNKS = 8
SC_BF16_TERMS = 4


def _pack_table_kernel(t_ref, o_ref):
    half = o_ref.shape[1]
    o_ref[...] = _pack_bf16_pairs(t_ref[:, :half].astype(BF16), t_ref[:, half:].astype(BF16))


def _pack_table(t, tn):
    n, d = t.shape
    assert n % tn == 0
    return pl.pallas_call(
        _pack_table_kernel, grid=(n // tn,),
        in_specs=[pl.BlockSpec((tn, d), lambda i: (i, 0))],
        out_specs=pl.BlockSpec((tn, d // 2), lambda i: (i, 0)),
        out_shape=jax.ShapeDtypeStruct((n, d // 2), jnp.uint32),
        compiler_params=_cparams(("arbitrary",)), name="pack_table",
    )(t)


def _sc_bf16(words):
    return plsc.bitcast(words, BF16)


def _sc_phase(table, idx, operand, mode):
    t, ne = idx.shape
    hw = table.shape[1]
    d = 2 * hw
    L = SC_LANES
    win = SC_WINDOW
    nbuf = SC_ROW_BUFFERS
    ahead = nbuf - 1
    tps = SC_TOKENS_PER_STEP
    nq = SC_WSUM_CHUNKS
    nterm = SC_BF16_TERMS
    info = plsc.get_sparse_core_info()
    nc, ns = info.num_cores, info.num_subcores
    nss = t // (nc * ns * tps)
    wps = tps * ne // win
    wpt = ne // win
    blk = tps * ne
    nwin = nss * wps
    assert nss * nc * ns * tps == t and nss % 2 == 0 and wps % nbuf == 0 and wps > ahead
    assert wpt * win == ne and win % L == 0 and hw % (nq * L) == 0 and hw % (nterm * L) == 0 and win % nterm == 0
    dots = mode == "dots"
    assert dots or nbuf % wpt == 0
    op_buf = pltpu.VMEM((2, tps, hw), jnp.uint32) if dots else pltpu.VMEM((2 * blk,), jnp.int32)
    res_buf = pltpu.VMEM((2 * blk,), F32) if dots else pltpu.VMEM((2, tps, d), F32)
    out_type = jax.ShapeDtypeStruct((t * ne,), F32) if dots else jax.ShapeDtypeStruct((t, d), F32)
    scratch = [("idx", pltpu.VMEM((2 * blk,), jnp.int32)), ("op_" + mode, op_buf),
               ("rows", pltpu.VMEM((nbuf, win, hw), jnp.uint32)), ("res_" + mode, res_buf),
               ("isem", pltpu.SemaphoreType.DMA((2,))), ("psem", pltpu.SemaphoreType.DMA((2,))),
               ("gsem", pltpu.SemaphoreType.DMA((nbuf,))), ("osem", pltpu.SemaphoreType.DMA((2,)))]

    def sc_kernel(table_hbm, idx_hbm, op_hbm, out_hbm, idx_v, op_v, rows_v, res_v, isem, psem, gsem, osem):
        ss0 = (lax.axis_index("s") * nc + lax.axis_index("c")) * nss
        lane = lax.iota(jnp.int32, L)

        def flat(ref, n):
            return ref.at[pl.ds(pl.multiple_of(n * blk, blk), blk)]

        def tok_rows(ref, n):
            return ref.at[pl.ds(pl.multiple_of(n * tps, tps), tps)]

        def idx_copy(ss, slot):
            return pltpu.make_async_copy(flat(idx_hbm, ss0 + ss), flat(idx_v, slot), isem.at[slot])

        def op_copy(ss, slot):
            if dots:
                return pltpu.make_async_copy(tok_rows(op_hbm, ss0 + ss), op_v.at[slot], psem.at[slot])
            return pltpu.make_async_copy(flat(op_hbm, ss0 + ss), flat(op_v, slot), psem.at[slot])

        def out_copy(ss, slot):
            if dots:
                return pltpu.make_async_copy(flat(res_v, slot), flat(out_hbm, ss0 + ss), osem.at[slot])
            return pltpu.make_async_copy(res_v.at[slot], tok_rows(out_hbm, ss0 + ss), osem.at[slot])

        def gather(slot, hs, b):
            ix = idx_v.at[pl.ds(pl.multiple_of(slot * blk + hs * win, win), win)]
            return pltpu.make_async_copy(table_hbm.at[ix], rows_v.at[b], gsem.at[b])

        def reduce_dots(slot, hs, b):
            tok = hs // wpt

            def head(g, c):
                def chunk(jj, accs):
                    off = pl.multiple_of(jj * (nterm * L), nterm * L)
                    xs = [_sc_bf16(op_v[slot, tok, pl.ds(off + q * L, L)]) for q in range(nterm)]
                    new = []
                    for r in range(L):
                        p = _sc_bf16(rows_v[b, g * L + r, pl.ds(off, L)]) * xs[0]
                        for q in range(1, nterm):
                            p = p + _sc_bf16(rows_v[b, g * L + r, pl.ds(off + q * L, L)]) * xs[q]
                        lo, hi = plsc.unpack(p, format=plsc.PackFormat.INTERLEAVED)
                        new.append(accs[r] + lo + hi)
                    return tuple(new)

                accs = lax.fori_loop(0, hw // (nterm * L), chunk, tuple(jnp.zeros((L,), F32) for _ in range(L)))
                out = jnp.zeros((L,), F32)
                for r in range(L):
                    out = jnp.where(lane == r, jnp.sum(accs[r]), out)
                res_v[pl.ds(pl.multiple_of(slot * blk + hs * win + g * L, L), L)] = out
                return c

            lax.fori_loop(0, win // L, head, 0)

        def reduce_wsum(slot, hs, b):
            tok = hs // wpt
            wbase = slot * blk + hs * win
            first = b % wpt == 0

            def colgroup(cg, c):
                col = pl.multiple_of(cg * (nq * L), nq * L)
                if first:
                    accs = tuple(jnp.zeros((L,), F32) for _ in range(2 * nq))
                else:
                    accs = tuple(res_v[slot, tok, pl.ds(col + q * L, L)] for q in range(nq)) + \
                           tuple(res_v[slot, tok, pl.ds(hw + col + q * L, L)] for q in range(nq))

                def rowgroup(rg, accs):
                    accs = list(accs)
                    r = rg * nterm
                    wvec = op_v[pl.ds(pl.multiple_of(wbase + (r // L) * L, L), L)]
                    ws = [_sc_bf16(wvec.at[jnp.full((L,), r % L + k, jnp.int32)].get(mode="promise_in_bounds"))
                          for k in range(nterm)]
                    for q in range(nq):
                        p = _sc_bf16(rows_v[b, r, pl.ds(col + q * L, L)]) * ws[0]
                        for k in range(1, nterm):
                            p = p + _sc_bf16(rows_v[b, r + k, pl.ds(col + q * L, L)]) * ws[k]
                        lo, hi = plsc.unpack(p, format=plsc.PackFormat.INTERLEAVED)
                        accs[q] = accs[q] + lo
                        accs[nq + q] = accs[nq + q] + hi
                    return tuple(accs)

                accs = lax.fori_loop(0, win // nterm, rowgroup, accs)
                for q in range(nq):
                    res_v[slot, tok, pl.ds(col + q * L, L)] = accs[q]
                    res_v[slot, tok, pl.ds(hw + col + q * L, L)] = accs[nq + q]
                return c

            lax.fori_loop(0, hw // (nq * L), colgroup, 0)

        idx_copy(0, 0).start()
        op_copy(0, 0).start()
        idx_copy(1, 1).start()
        op_copy(1, 1).start()
        idx_copy(0, 0).wait()
        op_copy(0, 0).wait()
        for a in range(ahead):
            gather(0, a, a).start()

        @pl.loop(0, nwin, step=nbuf)
        def _(i0):
            for b in range(nbuf):
                i = i0 + b
                ss = i // wps
                hs = i % wps
                slot = ss % 2
                gather(slot, hs, b).wait()
                nb = (b + ahead) % nbuf

                @pl.when(hs + ahead < wps)
                def _():
                    gather(slot, hs + ahead, nb).start()

                @pl.when(jnp.logical_and(hs + ahead >= wps, ss + 1 < nss))
                def _():
                    @pl.when(hs + ahead == wps)
                    def _():
                        idx_copy(ss + 1, 1 - slot).wait()
                        op_copy(ss + 1, 1 - slot).wait()

                    gather(1 - slot, hs + ahead - wps, nb).start()

                @pl.when(jnp.logical_and(hs == 0, ss >= 2))
                def _():
                    out_copy(ss - 2, slot).wait()

                if dots:
                    reduce_dots(slot, hs, b)
                else:
                    reduce_wsum(slot, hs, b)

                @pl.when(hs + 1 == wps)
                def _():
                    out_copy(ss, slot).start()

                    @pl.when(ss + 2 < nss)
                    def _():
                        idx_copy(ss + 2, slot).start()
                        op_copy(ss + 2, slot).start()

        out_copy(nss - 2, 0).wait()
        out_copy(nss - 1, 1).wait()

    op = operand if dots else operand.reshape(t * ne)
    return sc_kernel, scratch, out_type, (table, idx.reshape(t * ne), op)


def _sc_call(*phases):
    parts = [_sc_phase(table, idx, operand, mode) for mode, table, idx, operand in phases]
    names, types = [], []
    for _, scratch, _, _ in parts:
        for name, ty in scratch:
            if name not in names:
                names.append(name)
                types.append(ty)
    n = len(parts)

    def fused(*refs):
        ins, outs, scr = refs[:3 * n], refs[3 * n:4 * n], dict(zip(names, refs[4 * n:]))
        for i, (body, scratch, _, _) in enumerate(parts):
            body(*ins[3 * i:3 * i + 3], outs[i], *[scr[name] for name, _ in scratch])

    call = pl.kernel(
        fused, mesh=plsc.VectorSubcoreMesh(core_axis_name="c", subcore_axis_name="s"),
        out_type=tuple(p[2] for p in parts), scratch_types=types,
        compiler_params=pltpu.CompilerParams(needs_layout_passes=False),
        name="peer_" + "_".join(m for m, _, _, _ in phases),
    )
    return call(*[a for p in parts for a in p[3]])


def _peer_weight_kernel(act_ref, gate_ref, w_ref):
    w = (jax.nn.gelu(act_ref[...]) * gate_ref[...]).astype(BF16)
    w_ref[...] = pltpu.bitcast(_pack_bf16_pairs(w, w), jnp.int32)


def _peer_weight(act, gate, tm):
    t, ne = act.shape
    tm = min(tm, t)
    assert t % tm == 0
    spec = pl.BlockSpec((tm, ne), lambda i: (i, 0))
    return pl.pallas_call(
        _peer_weight_kernel, grid=(t // tm,), in_specs=[spec, spec], out_specs=spec,
        out_shape=jax.ShapeDtypeStruct((t, ne), jnp.int32),
        compiler_params=_cparams(("arbitrary",)), name="peer_weight",
    )(act, gate)


def _final_kernel(x1_ref, p_ref, g2_ref, fg_ref, prev_ref, o_ref):
    del prev_ref
    x2 = x1_ref[0] + g2_ref[0] * p_ref[0]
    ms = jnp.mean(x2 * x2, axis=-1, keepdims=True)
    o_ref[0] = x2 * lax.rsqrt(ms + EPS) * fg_ref[...]


def _final(x1, peer, g2, fg, tm, out, b0):
    bc, s, d = x1.shape
    act = pl.BlockSpec((1, tm, d), lambda i, j: (i, j, 0))
    return pl.pallas_call(
        _final_kernel, grid=(bc, s // tm),
        in_specs=[act, act, pl.BlockSpec((1, 1, d), lambda i, j: (i, 0, 0)),
                  pl.BlockSpec((1, d), lambda i, j: (0, 0)),
                  pl.BlockSpec(memory_space=pl.ANY)],
        out_specs=pl.BlockSpec((1, tm, d), lambda i, j: (i + b0, j, 0)),
        out_shape=jax.ShapeDtypeStruct(out.shape, F32),
        input_output_aliases={4: 0},
        compiler_params=_cparams(("arbitrary", "arbitrary")), name="final_norm",
    )(x1, peer, g2, fg, out)


def kernel(x, c, ctx, c_ctx, ada_w, ada_b, norm1_g, norm2_g, w_in, na_rpb, gm_ln_g, gm_ws, gm_bs,
           w_proj_a, w_proj_b, w_out, peer_wq, peer_keys, peer_u, peer_v, final_g):
    b, s, d = x.shape
    naw = NA_HEADS * HEAD_DIM
    gw = gm_ln_g.shape[1]
    layer = 0

    pad = (-(b + 1)) % SUBLANES
    c_all = jnp.concatenate([c, c_ctx[None, :], jnp.zeros((pad, d), F32)], axis=0)
    mod = _adaln(c_all, ada_w[layer], ada_b[layer])
    sh1, sc1, g1, sh2, sc2, g2 = [mod[:b, i * d:(i + 1) * d].reshape(b, 1, d) for i in range(ADA_CHUNKS)]
    csh1 = jnp.broadcast_to(mod[b, 0:d].reshape(1, 1, d), (b, 1, d))
    csc1 = jnp.broadcast_to(mod[b, d:2 * d].reshape(1, 1, d), (b, 1, d))

    w = w_in[layer].astype(BF16)
    w_kv = w[:, naw:3 * naw]
    n1g = norm1_g[layer].reshape(1, d)
    bias = _bias_table(na_rpb[layer])
    bs_full = jnp.repeat(gm_bs[layer].T, gw // GM_GROUPS, axis=1)
    keys = peer_keys[layer].reshape(2 * PEER_HEADS, PEER_N_KEYS, PEER_HALF).astype(BF16)
    lng = gm_ln_g[layer].reshape(1, gw)
    ws = gm_ws[layer].astype(BF16)
    wpa, wpb, wout = w_proj_a[layer].astype(BF16), w_proj_b[layer].astype(BF16), w_out[layer].astype(BF16)
    n2g = norm2_g[layer].reshape(1, d)
    wq = peer_wq[layer].astype(BF16)
    tab_u, tab_v = _pack_table(peer_u[layer], TILE_PACK), _pack_table(peer_v[layer], TILE_PACK)
    fg = final_g.reshape(1, d)
    ne = PEER_HEADS * PEER_TOPK

    assert sum(BATCH_CHUNKS) == b
    out = jnp.zeros((b, s, d), F32)

    def finish(out, pending, peer):
        idx, wgt, x1, g2c, boff = pending
        return _final(x1, peer.reshape(x1.shape), g2c, fg, TILE_FINAL, out, boff)

    pending = None
    b0 = 0
    for bc in BATCH_CHUNKS:
        sl = slice(b0, b0 + bc)
        boff = b0
        b0 += bc
        t = bc * s
        xc = x[sl]
        q, k, v, gu, gv, ga, gb = _norm_proj(xc, n1g, sh1[sl], sc1[sl], w, (naw, naw, naw, gw, gw, d, d), TILE_PROJ)
        k_c, v_c = _norm_proj(ctx[sl], n1g, csh1[sl], csc1[sl], w_kv, (naw, naw), ctx.shape[1])
        y_a = _attention(q, k, v, k_c, v_c, bias)
        x1, h2, idx, gate = _mix(xc, gu, gv, ga, gb, y_a, g1[sl], sh2[sl], sc2[sl], lng, ws, bs_full,
                                 wpa, wpb, wout, n2g, wq, keys, TILE_MIX)
        idx, gate = idx.reshape(t, ne), gate.reshape(t, ne)
        dots = ("dots", tab_u, idx, h2.reshape(t, d // 2))
        if pending is None:
            act, = _sc_call(dots)
        else:
            peer, act = _sc_call(("wsum", tab_v) + pending[:2], dots)
            out = finish(out, pending, peer)
        wgt = _peer_weight(act.reshape(t, ne), gate, TILE_WEIGHT)
        pending = (idx, wgt, x1, g2[sl], boff)
    peer, = _sc_call(("wsum", tab_v) + pending[:2])
    return finish(out, pending, peer)
```

```python
import functools

import jax
import jax.numpy as jnp
from jax import lax
from jax.experimental import pallas as pl
from jax.experimental.pallas import tpu as pltpu
from jax.experimental.pallas import tpu_sc as plsc

F32 = jnp.float32
BF16 = jnp.bfloat16

GRID_W = 64
NA_HEADS = 8
HEAD_DIM = 64
NA_WIN_ROWS = 8
NA_WIN_COLS = 16
GM_GROUPS = 8
GM_CHUNK = 128
PEER_HEADS = 8
PEER_N_KEYS = 128
PEER_TOPK = 16
PEER_HALF = 128
ADA_CHUNKS = 6
EPS = 1e-6
NEG_INF = -1e30

LANES = 128
SUBLANES = 8
VMEM_LIMIT = 56 * 1024 * 1024
TILE_ADALN_COLS = 1024
TILE_PROJ = 512
TILE_MIX = 256
TILE_WEIGHT = 2048
TILE_FINAL = 512
TILE_PACK = 1024
ATTN_ROWS_PER_STEP = 2
BATCH_CHUNKS = (1, 1, 2, 3, 4, 5)


def _dot(a, b):
    return lax.dot_general(a, b, (((1,), (0,)), ((), ())), preferred_element_type=F32)


def _dot_nt(a, b):
    return lax.dot_general(a, b, (((1,), (1,)), ((), ())), preferred_element_type=F32)


def _cparams(sem):
    return pltpu.CompilerParams(dimension_semantics=sem, vmem_limit_bytes=VMEM_LIMIT)


def _pack_bf16_pairs(lo, hi):
    lo_bits = pltpu.bitcast(lo.astype(F32), jnp.uint32) >> 16
    hi_bits = pltpu.bitcast(hi.astype(F32), jnp.uint32) & jnp.uint32(0xFFFF0000)
    return lo_bits | hi_bits


def _adaln_kernel(c_ref, w_ref, b_ref, o_ref):
    c = c_ref[...]
    s = c * jax.nn.sigmoid(c)
    o_ref[...] = lax.dot_general(s, w_ref[...], (((1,), (0,)), ((), ())),
                                 precision=lax.Precision.HIGHEST,
                                 preferred_element_type=F32) + b_ref[...]


def _adaln(c_all, w, b):
    m, d = c_all.shape
    n = w.shape[1]
    tn = TILE_ADALN_COLS
    return pl.pallas_call(
        _adaln_kernel,
        grid=(n // tn,),
        in_specs=[pl.BlockSpec((m, d), lambda j: (0, 0)),
                  pl.BlockSpec((d, tn), lambda j: (0, j)),
                  pl.BlockSpec((1, tn), lambda j: (0, j))],
        out_specs=pl.BlockSpec((m, tn), lambda j: (0, j)),
        out_shape=jax.ShapeDtypeStruct((m, n), F32),
        compiler_params=_cparams(("arbitrary",)),
        name="adaln",
    )(c_all, w, b.reshape(1, n))


def _norm_proj_kernel(widths, x_ref, g_ref, sh_ref, sc_ref, w_ref, *o_refs):
    x = x_ref[0]
    ms = jnp.mean(x * x, axis=-1, keepdims=True)
    y = x * lax.rsqrt(ms + EPS) * g_ref[...]
    h = (y * (1.0 + sc_ref[0]) + sh_ref[0]).astype(BF16)
    off = 0
    for o_ref, wd in zip(o_refs, widths):
        o_ref[0] = _dot(h, w_ref[:, off:off + wd]).astype(o_ref.dtype)
        off += wd


def _norm_proj(x, gain, shift, scale, w, widths, tm):
    b, s, d = x.shape
    n = w.shape[1]
    assert sum(widths) == n and s % tm == 0
    vec = pl.BlockSpec((1, 1, d), lambda i, j: (i, 0, 0))
    return pl.pallas_call(
        functools.partial(_norm_proj_kernel, widths),
        grid=(b, s // tm),
        in_specs=[pl.BlockSpec((1, tm, d), lambda i, j: (i, j, 0)),
                  pl.BlockSpec((1, d), lambda i, j: (0, 0)),
                  vec, vec,
                  pl.BlockSpec((d, n), lambda i, j: (0, 0))],
        out_specs=[pl.BlockSpec((1, tm, wd), lambda i, j: (i, j, 0)) for wd in widths],
        out_shape=[jax.ShapeDtypeStruct((b, s, wd), BF16) for wd in widths],
        compiler_params=_cparams(("arbitrary", "arbitrary")),
        name="norm_proj",
    )(x, gain, shift, scale, w)


def _bias_table_kernel(rpb_ref, o_ref):
    h = pl.program_id(0)
    q = lax.broadcasted_iota(jnp.int32, (GRID_W, GRID_W), 0)
    kc = lax.broadcasted_iota(jnp.int32, (GRID_W, GRID_W), 1)
    dc = jnp.clip(kc - q + NA_WIN_COLS - 1, 0, 2 * NA_WIN_COLS - 2)
    cs = jnp.clip(q - NA_WIN_COLS // 2, 0, GRID_W - NA_WIN_COLS)
    col_in = (kc >= cs) & (kc < cs + NA_WIN_COLS)
    n_dc = 2 * NA_WIN_COLS - 1
    n_dr = 2 * NA_WIN_ROWS - 1
    for dr in range(n_dr):
        t = jnp.zeros((GRID_W, GRID_W), F32)
        for c in range(n_dc):
            t = jnp.where(dc == c, rpb_ref[h * n_dr + dr, c], t)
        t = jnp.where(col_in, t, NEG_INF)
        for d0 in range(NA_WIN_ROWS):
            j = dr - d0
            if 0 <= j < NA_WIN_ROWS:
                o_ref[0, d0, :, j * GRID_W:(j + 1) * GRID_W] = t


def _bias_table(rpb):
    nh, n_dr, n_dc = rpb.shape
    band = NA_WIN_ROWS * GRID_W
    return pl.pallas_call(
        _bias_table_kernel,
        grid=(nh,),
        in_specs=[pl.BlockSpec(memory_space=pltpu.SMEM)],
        out_specs=pl.BlockSpec((1, NA_WIN_ROWS, GRID_W, band), lambda h: (h, 0, 0, 0)),
        out_shape=jax.ShapeDtypeStruct((nh, NA_WIN_ROWS, GRID_W, band), F32),
        compiler_params=_cparams(("arbitrary",)),
        name="bias_table",
    )(rpb.reshape(nh * n_dr, n_dc))


def _attn_kernel(rows, q_ref, k_ref, v_ref, kc_ref, vc_ref, bias_ref, o_ref, s_scr, p_scr):
    for rr in range(ATTN_ROWS_PER_STEP):
        _attn_row(rows, pl.program_id(1) * ATTN_ROWS_PER_STEP + rr, slice(rr * GRID_W, (rr + 1) * GRID_W),
                  q_ref, k_ref, v_ref, kc_ref, vc_ref, bias_ref, o_ref, s_scr, p_scr)


def _attn_row(rows, r, qrows, q_ref, k_ref, v_ref, kc_ref, vc_ref, bias_ref, o_ref, s_scr, p_scr):
    rs = jnp.clip(r - NA_WIN_ROWS // 2, 0, rows - NA_WIN_ROWS)
    d0 = rs - r + NA_WIN_ROWS - 1
    band = NA_WIN_ROWS * GRID_W
    start = pl.multiple_of(rs * GRID_W, GRID_W)
    scale = HEAD_DIM ** -0.5
    lane = lax.broadcasted_iota(jnp.int32, (GRID_W, LANES), 1)
    hpg = LANES // HEAD_DIM
    groups = NA_HEADS // hpg

    def mine(hh):
        return (lane >= hh * HEAD_DIM) & (lane < (hh + 1) * HEAD_DIM)

    for hp in range(groups):
        sl = slice(hp * LANES, (hp + 1) * LANES)
        q2 = q_ref[0, qrows, sl]
        kb = k_ref[0, pl.ds(start, band), sl]
        kc = kc_ref[0, :, sl]
        for hh in range(hpg):
            h = hp * hpg + hh
            qh = jnp.where(mine(hh), q2, jnp.zeros_like(q2))
            s_scr[h, :, :band] = _dot_nt(qh, kb) * scale + bias_ref[h, d0]
            s_scr[h, :, band:] = _dot_nt(qh, kc) * scale
    dens = []
    for h in range(NA_HEADS):
        s = s_scr[h]
        p = jnp.exp(s - jnp.max(s, axis=-1, keepdims=True))
        dens.append(jnp.sum(p, axis=-1, keepdims=True))
        p_scr[h] = p.astype(BF16)
    for hp in range(groups):
        sl = slice(hp * LANES, (hp + 1) * LANES)
        vb = v_ref[0, pl.ds(start, band), sl]
        vc = vc_ref[0, :, sl]
        halves = []
        for hh in range(hpg):
            h = hp * hpg + hh
            o = _dot(p_scr[h, :, :band], vb) + _dot(p_scr[h, :, band:], vc)
            halves.append(jnp.where(mine(hh), o / dens[h], 0.0))
        o_ref[0, qrows, sl] = sum(halves).astype(o_ref.dtype)


def _attention(q, k, v, kc, vc, bias):
    b, s, w = q.shape
    rows = s // GRID_W
    c = kc.shape[1]
    full = lambda n: pl.BlockSpec((1, n, w), lambda i, j: (i, 0, 0))
    return pl.pallas_call(
        functools.partial(_attn_kernel, rows),
        grid=(b, rows // ATTN_ROWS_PER_STEP),
        in_specs=[pl.BlockSpec((1, ATTN_ROWS_PER_STEP * GRID_W, w), lambda i, j: (i, j, 0)),
                  full(s), full(s), full(c), full(c),
                  pl.BlockSpec(bias.shape, lambda i, j: (0, 0, 0, 0))],
        out_specs=pl.BlockSpec((1, ATTN_ROWS_PER_STEP * GRID_W, w), lambda i, j: (i, j, 0)),
        out_shape=jax.ShapeDtypeStruct((b, s, w), BF16),
        scratch_shapes=[pltpu.VMEM((NA_HEADS, GRID_W, NA_WIN_ROWS * GRID_W + c), F32),
                        pltpu.VMEM((NA_HEADS, GRID_W, NA_WIN_ROWS * GRID_W + c), BF16)],
        compiler_params=_cparams(("arbitrary", "arbitrary")),
        name="nbr_attention",
    )(q, k, v, kc, vc, bias)


def _mix_kernel(x_ref, ya_ref, n1g_ref, sh1_ref, sc1_ref, g1_ref, sh2_ref, sc2_ref, wg_ref,
                lng_ref, ws_ref, bs_ref, wpa_ref, wpb_ref, wout_ref, n2g_ref, wq_ref, keys_ref,
                x1_ref, h2_ref, idx_ref, gate_ref, st_ref, sv_ref, si_ref, se_ref, sg_ref):
    tm = x_ref.shape[1]
    d = x_ref.shape[2]
    gw = lng_ref.shape[1]
    x = x_ref[0]
    h = x * lax.rsqrt(jnp.mean(x * x, axis=-1, keepdims=True) + EPS) * n1g_ref[...]
    g4 = _dot((h * (1.0 + sc1_ref[0]) + sh1_ref[0]).astype(BF16), wg_ref[...])
    ga = g4[:, 2 * gw:2 * gw + d]
    gb = g4[:, 2 * gw + d:]
    u = jax.nn.gelu(g4[:, :gw])
    t = jax.nn.gelu(g4[:, gw:2 * gw])
    tc = t - jnp.mean(t, axis=-1, keepdims=True)
    vn = tc * lax.rsqrt(jnp.mean(tc * tc, axis=-1, keepdims=True) + EPS) * lng_ref[...]
    vnb = vn.astype(BF16)
    lane = lax.broadcasted_iota(jnp.int32, (GM_CHUNK, LANES), 1)
    gd = gw // GM_GROUPS
    chunks = []
    for c in range(tm // GM_CHUNK):
        pairs = []
        for gp in range(gw // LANES):
            vp = vnb[c * GM_CHUNK:(c + 1) * GM_CHUNK, gp * LANES:(gp + 1) * LANES]
            r0 = _dot(ws_ref[2 * gp], vp)
            r1 = _dot(ws_ref[2 * gp + 1], vp)
            pairs.append(jnp.where(lane < gd, r0, r1))
        chunks.append(jnp.concatenate(pairs, axis=1) + bs_ref[...])
    mixed = jnp.concatenate(chunks, axis=0)
    yb = (u * mixed).astype(BF16)
    pa = _dot(ya_ref[0], wpa_ref[...])
    pb = _dot(yb, wpb_ref[...])
    m = jax.nn.sigmoid(ga) * pa + jax.nn.sigmoid(gb) * pb
    out = _dot(m.astype(BF16), wout_ref[...])
    x1 = x + g1_ref[0] * out
    x1_ref[0] = x1
    ms = jnp.mean(x1 * x1, axis=-1, keepdims=True)
    h2 = x1 * lax.rsqrt(ms + EPS) * n2g_ref[...]
    h2 = (h2 * (1.0 + sc2_ref[0]) + sh2_ref[0]).astype(BF16)
    half = h2.shape[1] // 2
    h2_ref[0] = _pack_bf16_pairs(h2[:, :half], h2[:, half:])
    qp = _dot(h2, wq_ref[...]).astype(BF16)
    for hp in range(keys_ref.shape[0]):
        st_ref[hp] = _dot_nt(keys_ref[hp], qp[:, hp * PEER_HALF:(hp + 1) * PEER_HALF])
    _topk_body(st_ref, idx_ref.at[0], gate_ref.at[0], sv_ref, si_ref, se_ref, sg_ref)


def _mix(x, ya, n1g, sh1, sc1, g1, sh2, sc2, wg, lng, ws, bs_full, wpa, wpb, wout, n2g, wq, keys, tm):
    b, s, d = x.shape
    gw = lng.shape[1]
    nt = s // tm
    nhp = keys.shape[0]
    ne = (nhp // 2) * PEER_TOPK
    assert s % tm == 0 and tm % LANES == 0
    act = lambda w: pl.BlockSpec((1, tm, w), lambda i, j: (i, j, 0))
    vec = pl.BlockSpec((1, 1, d), lambda i, j: (i, 0, 0))

    def const(a):
        nd = a.ndim
        return pl.BlockSpec(a.shape, lambda i, j: (0,) * nd)

    return pl.pallas_call(
        _mix_kernel,
        grid=(b, nt),
        in_specs=[act(d), act(ya.shape[2]), const(n1g), vec, vec, vec, vec, vec, const(wg),
                  const(lng), const(ws), const(bs_full), const(wpa), const(wpb), const(wout),
                  const(n2g), const(wq), const(keys)],
        out_specs=[act(d), act(d // 2), act(ne), act(ne)],
        out_shape=[jax.ShapeDtypeStruct((b, s, d), F32),
                   jax.ShapeDtypeStruct((b, s, d // 2), jnp.uint32),
                   jax.ShapeDtypeStruct((b, s, ne), jnp.int32),
                   jax.ShapeDtypeStruct((b, s, ne), F32)],
        scratch_shapes=[pltpu.VMEM((nhp, PEER_N_KEYS, tm), F32),
                        pltpu.VMEM((nhp, PEER_TOPK, LANES), F32),
                        pltpu.VMEM((nhp, PEER_TOPK, LANES), F32),
                        pltpu.VMEM((ne, LANES), F32),
                        pltpu.VMEM((ne, LANES), F32)],
        compiler_params=_cparams(("arbitrary", "arbitrary")),
        name="mix_peer_topk",
    )(x, ya, n1g, sh1, sc1, g1, sh2, sc2, wg, lng, ws, bs_full, wpa, wpb, wout, n2g, wq, keys)


def _extract_top(vals, order, payload, k):
    out_v, out_p = [], []
    for _ in range(k):
        m = jnp.max(vals, axis=0, keepdims=True)
        o = jnp.min(jnp.where(vals == m, order, jnp.inf), axis=0, keepdims=True)
        sel = order == o
        out_v.append(m)
        if payload is order:
            out_p.append(o)
        else:
            out_p.append(jnp.max(jnp.where(sel, payload, -1.0), axis=0, keepdims=True))
        vals = jnp.where(sel, -jnp.inf, vals)
    return jnp.concatenate(out_v, axis=0), jnp.concatenate(out_p, axis=0)


def _topk_body(st_ref, idx_ref, gate_ref, sv_ref, si_ref, se_ref, sg_ref):
    nhp = st_ref.shape[0]
    tm = st_ref.shape[2]
    k = PEER_TOPK
    kidx = lax.broadcasted_iota(jnp.int32, (PEER_N_KEYS, LANES), 0).astype(F32)
    sub = lax.broadcasted_iota(jnp.int32, (SUBLANES, LANES), 0).astype(F32)

    def lane_tile(lt, carry):
        lanes = pl.ds(pl.multiple_of(lt * LANES, LANES), LANES)

        def stage1(h, c):
            for hp in (4 * h, 4 * h + 1, 4 * h + 2, 4 * h + 3):
                v, i = _extract_top(st_ref[hp, :, lanes], kidx, kidx, k)
                sv_ref[hp] = v
                si_ref[hp] = i
            return c

        lax.fori_loop(0, nhp // 4, stage1, 0)

        def stage2(h2, c):
            for h in (2 * h2, 2 * h2 + 1):
                stage2_head(h)
            return c

        def stage2_head(h):
            s1, s2 = sv_ref[2 * h], sv_ref[2 * h + 1]
            i1, i2 = si_ref[2 * h] * float(PEER_N_KEYS), si_ref[2 * h + 1]
            cv, co, ce = [], [], []
            for half in range(2):
                b0 = half * SUBLANES
                cv.append(s1[0:1] + s2[b0:b0 + SUBLANES])
                co.append(sub + float(b0))
                ce.append(i1[0:1] + i2[b0:b0 + SUBLANES])
            for a in range(1, SUBLANES):
                cv.append(s1[a:a + 1] + s2[0:SUBLANES])
                co.append(sub + float(a * k))
                ce.append(i1[a:a + 1] + i2[0:SUBLANES])
            cv.append(s1[SUBLANES:k] + s2[0:1])
            co.append((sub + float(SUBLANES)) * float(k))
            ce.append(i1[SUBLANES:k] + i2[0:1])
            bv, be = _extract_top(jnp.concatenate(cv, axis=0), jnp.concatenate(co, axis=0),
                                  jnp.concatenate(ce, axis=0), k)
            ex = jnp.exp(bv - bv[0:1])
            rows = pl.ds(pl.multiple_of(h * k, k), k)
            sg_ref[rows, :] = ex / jnp.sum(ex, axis=0, keepdims=True)
            se_ref[rows, :] = be

        lax.fori_loop(0, nhp // 4, stage2, 0)

        rows = pl.ds(pl.multiple_of(lt * LANES, LANES), LANES)
        idx_ref[rows, :] = se_ref[...].T.astype(jnp.int32)
        gate_ref[rows, :] = sg_ref[...].T
        return carry

    lax.fori_loop(0, tm // LANES, lane_tile, 0)


SC_LANES = 16
SC_WINDOW = 32
SC_ROW_BUFFERS = 4
SC_TOKENS_PER_STEP = 8
SC_WSUM_CHUNKS = 8
SC_BF16_TERMS = 4


def _pack_table_kernel(t_ref, o_ref):
    half = o_ref.shape[1]
    o_ref[...] = _pack_bf16_pairs(t_ref[:, :half].astype(BF16), t_ref[:, half:].astype(BF16))


def _pack_table(t, tn):
    n, d = t.shape
    assert n % tn == 0
    return pl.pallas_call(
        _pack_table_kernel, grid=(n // tn,),
        in_specs=[pl.BlockSpec((tn, d), lambda i: (i, 0))],
        out_specs=pl.BlockSpec((tn, d // 2), lambda i: (i, 0)),
        out_shape=jax.ShapeDtypeStruct((n, d // 2), jnp.uint32),
        compiler_params=_cparams(("arbitrary",)), name="pack_table",
    )(t)


def _sc_bf16(words):
    return plsc.bitcast(words, BF16)


def _sc_phase(table, idx, operand, mode):
    t, ne = idx.shape
    hw = table.shape[1]
    d = 2 * hw
    L = SC_LANES
    win = SC_WINDOW
    nbuf = SC_ROW_BUFFERS
    ahead = nbuf - 1
    tps = SC_TOKENS_PER_STEP
    nq = SC_WSUM_CHUNKS
    nterm = SC_BF16_TERMS
    info = plsc.get_sparse_core_info()
    nc, ns = info.num_cores, info.num_subcores
    nss = t // (nc * ns * tps)
    wps = tps * ne // win
    wpt = ne // win
    blk = tps * ne
    nwin = nss * wps
    assert nss * nc * ns * tps == t and nss % 2 == 0 and wps % nbuf == 0 and wps > ahead
    assert wpt * win == ne and win % L == 0 and hw % (nq * L) == 0 and hw % (nterm * L) == 0 and win % nterm == 0
    dots = mode == "dots"
    assert dots or nbuf % wpt == 0
    op_buf = pltpu.VMEM((2, tps, hw), jnp.uint32) if dots else pltpu.VMEM((2 * blk,), jnp.int32)
    res_buf = pltpu.VMEM((2 * blk,), F32) if dots else pltpu.VMEM((2, tps, d), F32)
    out_type = jax.ShapeDtypeStruct((t * ne,), F32) if dots else jax.ShapeDtypeStruct((t, d), F32)
    scratch = [("idx", pltpu.VMEM((2 * blk,), jnp.int32)), ("op_" + mode, op_buf),
               ("rows", pltpu.VMEM((nbuf, win, hw), jnp.uint32)), ("res_" + mode, res_buf),
               ("isem", pltpu.SemaphoreType.DMA((2,))), ("psem", pltpu.SemaphoreType.DMA((2,))),
               ("gsem", pltpu.SemaphoreType.DMA((nbuf,))), ("osem", pltpu.SemaphoreType.DMA((2,)))]

    def sc_kernel(table_hbm, idx_hbm, op_hbm, out_hbm, idx_v, op_v, rows_v, res_v, isem, psem, gsem, osem):
        ss0 = (lax.axis_index("s") * nc + lax.axis_index("c")) * nss
        lane = lax.iota(jnp.int32, L)

        def flat(ref, n):
            return ref.at[pl.ds(pl.multiple_of(n * blk, blk), blk)]

        def tok_rows(ref, n):
            return ref.at[pl.ds(pl.multiple_of(n * tps, tps), tps)]

        def idx_copy(ss, slot):
            return pltpu.make_async_copy(flat(idx_hbm, ss0 + ss), flat(idx_v, slot), isem.at[slot])

        def op_copy(ss, slot):
            if dots:
                return pltpu.make_async_copy(tok_rows(op_hbm, ss0 + ss), op_v.at[slot], psem.at[slot])
            return pltpu.make_async_copy(flat(op_hbm, ss0 + ss), flat(op_v, slot), psem.at[slot])

        def out_copy(ss, slot):
            if dots:
                return pltpu.make_async_copy(flat(res_v, slot), flat(out_hbm, ss0 + ss), osem.at[slot])
            return pltpu.make_async_copy(res_v.at[slot], tok_rows(out_hbm, ss0 + ss), osem.at[slot])

        def gather(slot, hs, b):
            ix = idx_v.at[pl.ds(pl.multiple_of(slot * blk + hs * win, win), win)]
            return pltpu.make_async_copy(table_hbm.at[ix], rows_v.at[b], gsem.at[b])

        def reduce_dots(slot, hs, b):
            tok = hs // wpt

            def head(g, c):
                def chunk(jj, accs):
                    off = pl.multiple_of(jj * (nterm * L), nterm * L)
                    xs = [_sc_bf16(op_v[slot, tok, pl.ds(off + q * L, L)]) for q in range(nterm)]
                    new = []
                    for r in range(L):
                        p = _sc_bf16(rows_v[b, g * L + r, pl.ds(off, L)]) * xs[0]
                        for q in range(1, nterm):
                            p = p + _sc_bf16(rows_v[b, g * L + r, pl.ds(off + q * L, L)]) * xs[q]
                        lo, hi = plsc.unpack(p, format=plsc.PackFormat.INTERLEAVED)
                        new.append(accs[r] + lo + hi)
                    return tuple(new)

                accs = lax.fori_loop(0, hw // (nterm * L), chunk, tuple(jnp.zeros((L,), F32) for _ in range(L)))
                out = jnp.zeros((L,), F32)
                for r in range(L):
                    out = jnp.where(lane == r, jnp.sum(accs[r]), out)
                res_v[pl.ds(pl.multiple_of(slot * blk + hs * win + g * L, L), L)] = out
                return c

            lax.fori_loop(0, win // L, head, 0)

        def reduce_wsum(slot, hs, b):
            tok = hs // wpt
            wbase = slot * blk + hs * win
            first = b % wpt == 0

            def colgroup(cg, c):
                col = pl.multiple_of(cg * (nq * L), nq * L)
                if first:
                    accs = tuple(jnp.zeros((L,), F32) for _ in range(2 * nq))
                else:
                    accs = tuple(res_v[slot, tok, pl.ds(col + q * L, L)] for q in range(nq)) + \
                           tuple(res_v[slot, tok, pl.ds(hw + col + q * L, L)] for q in range(nq))

                def rowgroup(rg, accs):
                    accs = list(accs)
                    r = rg * nterm
                    wvec = op_v[pl.ds(pl.multiple_of(wbase + (r // L) * L, L), L)]
                    ws = [_sc_bf16(wvec.at[jnp.full((L,), r % L + k, jnp.int32)].get(mode="promise_in_bounds"))
                          for k in range(nterm)]
                    for q in range(nq):
                        p = _sc_bf16(rows_v[b, r, pl.ds(col + q * L, L)]) * ws[0]
                        for k in range(1, nterm):
                            p = p + _sc_bf16(rows_v[b, r + k, pl.ds(col + q * L, L)]) * ws[k]
                        lo, hi = plsc.unpack(p, format=plsc.PackFormat.INTERLEAVED)
                        accs[q] = accs[q] + lo
                        accs[nq + q] = accs[nq + q] + hi
                    return tuple(accs)

                accs = lax.fori_loop(0, win // nterm, rowgroup, accs)
                for q in range(nq):
                    res_v[slot, tok, pl.ds(col + q * L, L)] = accs[q]
                    res_v[slot, tok, pl.ds(hw + col + q * L, L)] = accs[nq + q]
                return c

            lax.fori_loop(0, hw // (nq * L), colgroup, 0)

        idx_copy(0, 0).start()
        op_copy(0, 0).start()
        idx_copy(1, 1).start()
        op_copy(1, 1).start()
        idx_copy(0, 0).wait()
        op_copy(0, 0).wait()
        for a in range(ahead):
            gather(0, a, a).start()

        @pl.loop(0, nwin, step=nbuf)
        def _(i0):
            for b in range(nbuf):
                i = i0 + b
                ss = i // wps
                hs = i % wps
                slot = ss % 2
                gather(slot, hs, b).wait()
                nb = (b + ahead) % nbuf

                @pl.when(hs + ahead < wps)
                def _():
                    gather(slot, hs + ahead, nb).start()

                @pl.when(jnp.logical_and(hs + ahead >= wps, ss + 1 < nss))
                def _():
                    @pl.when(hs + ahead == wps)
                    def _():
                        idx_copy(ss + 1, 1 - slot).wait()
                        op_copy(ss + 1, 1 - slot).wait()

                    gather(1 - slot, hs + ahead - wps, nb).start()

                @pl.when(jnp.logical_and(hs == 0, ss >= 2))
                def _():
                    out_copy(ss - 2, slot).wait()

                if dots:
                    reduce_dots(slot, hs, b)
                else:
                    reduce_wsum(slot, hs, b)

                @pl.when(hs + 1 == wps)
                def _():
                    out_copy(ss, slot).start()

                    @pl.when(ss + 2 < nss)
                    def _():
                        idx_copy(ss + 2, slot).start()
                        op_copy(ss + 2, slot).start()

        out_copy(nss - 2, 0).wait()
        out_copy(nss - 1, 1).wait()

    op = operand if dots else operand.reshape(t * ne)
    return sc_kernel, scratch, out_type, (table, idx.reshape(t * ne), op)


def _sc_call(*phases):
    parts = [_sc_phase(table, idx, operand, mode) for mode, table, idx, operand in phases]
    names, types = [], []
    for _, scratch, _, _ in parts:
        for name, ty in scratch:
            if name not in names:
                names.append(name)
                types.append(ty)
    n = len(parts)

    def fused(*refs):
        ins, outs, scr = refs[:3 * n], refs[3 * n:4 * n], dict(zip(names, refs[4 * n:]))
        for i, (body, scratch, _, _) in enumerate(parts):
            body(*ins[3 * i:3 * i + 3], outs[i], *[scr[name] for name, _ in scratch])

    call = pl.kernel(
        fused, mesh=plsc.VectorSubcoreMesh(core_axis_name="c", subcore_axis_name="s"),
        out_type=tuple(p[2] for p in parts), scratch_types=types,
        compiler_params=pltpu.CompilerParams(needs_layout_passes=False),
        name="peer_" + "_".join(m for m, _, _, _ in phases),
    )
    return call(*[a for p in parts for a in p[3]])


def _peer_weight_kernel(act_ref, gate_ref, w_ref):
    w = (jax.nn.gelu(act_ref[...]) * gate_ref[...]).astype(BF16)
    w_ref[...] = pltpu.bitcast(_pack_bf16_pairs(w, w), jnp.int32)


def _peer_weight(act, gate, tm):
    t, ne = act.shape
    tm = min(tm, t)
    assert t % tm == 0
    spec = pl.BlockSpec((tm, ne), lambda i: (i, 0))
    return pl.pallas_call(
        _peer_weight_kernel, grid=(t // tm,), in_specs=[spec, spec], out_specs=spec,
        out_shape=jax.ShapeDtypeStruct((t, ne), jnp.int32),
        compiler_params=_cparams(("arbitrary",)), name="peer_weight",
    )(act, gate)


def _final_kernel(x1_ref, p_ref, g2_ref, fg_ref, prev_ref, o_ref):
    del prev_ref
    x2 = x1_ref[0] + g2_ref[0] * p_ref[0]
    ms = jnp.mean(x2 * x2, axis=-1, keepdims=True)
    o_ref[0] = x2 * lax.rsqrt(ms + EPS) * fg_ref[...]


def _final(x1, peer, g2, fg, tm, out, b0):
    bc, s, d = x1.shape
    act = pl.BlockSpec((1, tm, d), lambda i, j: (i, j, 0))
    return pl.pallas_call(
        _final_kernel, grid=(bc, s // tm),
        in_specs=[act, act, pl.BlockSpec((1, 1, d), lambda i, j: (i, 0, 0)),
                  pl.BlockSpec((1, d), lambda i, j: (0, 0)),
                  pl.BlockSpec(memory_space=pl.ANY)],
        out_specs=pl.BlockSpec((1, tm, d), lambda i, j: (i + b0, j, 0)),
        out_shape=jax.ShapeDtypeStruct(out.shape, F32),
        input_output_aliases={4: 0},
        compiler_params=_cparams(("arbitrary", "arbitrary")), name="final_norm",
    )(x1, peer, g2, fg, out)


def kernel(x, c, ctx, c_ctx, ada_w, ada_b, norm1_g, norm2_g, w_in, na_rpb, gm_ln_g, gm_ws, gm_bs,
           w_proj_a, w_proj_b, w_out, peer_wq, peer_keys, peer_u, peer_v, final_g):
    b, s, d = x.shape
    naw = NA_HEADS * HEAD_DIM
    gw = gm_ln_g.shape[1]
    layer = 0

    pad = (-(b + 1)) % SUBLANES
    c_all = jnp.concatenate([c, c_ctx[None, :], jnp.zeros((pad, d), F32)], axis=0)
    mod = _adaln(c_all, ada_w[layer], ada_b[layer])
    sh1, sc1, g1, sh2, sc2, g2 = [mod[:b, i * d:(i + 1) * d].reshape(b, 1, d) for i in range(ADA_CHUNKS)]
    csh1 = jnp.broadcast_to(mod[b, 0:d].reshape(1, 1, d), (b, 1, d))
    csc1 = jnp.broadcast_to(mod[b, d:2 * d].reshape(1, 1, d), (b, 1, d))

    w = w_in[layer].astype(BF16)
    w_qkv, w_kv, w_gates = w[:, :3 * naw], w[:, naw:3 * naw], w[:, 3 * naw:]
    n1g = norm1_g[layer].reshape(1, d)
    bias = _bias_table(na_rpb[layer])
    bs_full = jnp.repeat(gm_bs[layer].T, gw // GM_GROUPS, axis=1)
    keys = peer_keys[layer].reshape(2 * PEER_HEADS, PEER_N_KEYS, PEER_HALF).astype(BF16)
    lng = gm_ln_g[layer].reshape(1, gw)
    ws = gm_ws[layer].astype(BF16)
    wpa, wpb, wout = w_proj_a[layer].astype(BF16), w_proj_b[layer].astype(BF16), w_out[layer].astype(BF16)
    n2g = norm2_g[layer].reshape(1, d)
    wq = peer_wq[layer].astype(BF16)
    tab_u, tab_v = _pack_table(peer_u[layer], TILE_PACK), _pack_table(peer_v[layer], TILE_PACK)
    fg = final_g.reshape(1, d)
    ne = PEER_HEADS * PEER_TOPK

    assert sum(BATCH_CHUNKS) == b
    out = jnp.zeros((b, s, d), F32)

    def finish(out, pending, peer):
        idx, wgt, x1, g2c, boff = pending
        return _final(x1, peer.reshape(x1.shape), g2c, fg, TILE_FINAL, out, boff)

    pending = None
    b0 = 0
    for bc in BATCH_CHUNKS:
        sl = slice(b0, b0 + bc)
        boff = b0
        b0 += bc
        t = bc * s
        xc = x[sl]
        q, k, v = _norm_proj(xc, n1g, sh1[sl], sc1[sl], w_qkv, (naw, naw, naw), TILE_PROJ)
        k_c, v_c = _norm_proj(ctx[sl], n1g, csh1[sl], csc1[sl], w_kv, (naw, naw), ctx.shape[1])
        y_a = _attention(q, k, v, k_c, v_c, bias)
        x1, h2, idx, gate = _mix(xc, y_a, n1g, sh1[sl], sc1[sl], g1[sl], sh2[sl], sc2[sl], w_gates, lng, ws, bs_full,
                                 wpa, wpb, wout, n2g, wq, keys, TILE_MIX)
        idx, gate = idx.reshape(t, ne), gate.reshape(t, ne)
        dots = ("dots", tab_u, idx, h2.reshape(t, d // 2))
        if pending is None:
            act, = _sc_call(dots)
        else:
            peer, act = _sc_call(("wsum", tab_v) + pending[:2], dots)
            out = finish(out, pending, peer)
        wgt = _peer_weight(act.reshape(t, ne), gate, TILE_WEIGHT)
        pending = (idx, wgt, x1, g2[sl], boff)
    peer, = _sc_call(("wsum", tab_v) + pending[:2])
    return finish(out, pending, peer)
```

```python
import functools

import jax
import jax.numpy as jnp
from jax import lax
from jax.experimental import pallas as pl
from jax.experimental.pallas import tpu as pltpu
from jax.experimental.pallas import tpu_sc as plsc

F32 = jnp.float32
BF16 = jnp.bfloat16

GRID_W = 64
NA_HEADS = 8
HEAD_DIM = 64
NA_WIN_ROWS = 8
NA_WIN_COLS = 16
GM_GROUPS = 8
GM_CHUNK = 128
PEER_HEADS = 8
PEER_N_KEYS = 128
PEER_TOPK = 16
PEER_HALF = 128
ADA_CHUNKS = 6
EPS = 1e-6
NEG_INF = -1e30

LANES = 128
SUBLANES = 8
VMEM_LIMIT = 56 * 1024 * 1024
TILE_ADALN_COLS = 1024
TILE_PROJ = 512
TILE_MIX = 256
TILE_FINAL = 512
TILE_PACK = 1024
ATTN_ROWS_PER_STEP = 2
BATCH_CHUNKS = (1, 2, 3, 4, 6)


def _dot(a, b):
    return lax.dot_general(a, b, (((1,), (0,)), ((), ())), preferred_element_type=F32)


def _dot_nt(a, b):
    return lax.dot_general(a, b, (((1,), (1,)), ((), ())), preferred_element_type=F32)


def _cparams(sem):
    return pltpu.CompilerParams(dimension_semantics=sem, vmem_limit_bytes=VMEM_LIMIT)


def _pack_bf16_pairs(lo, hi):
    lo_bits = pltpu.bitcast(lo.astype(F32), jnp.uint32) >> 16
    hi_bits = pltpu.bitcast(hi.astype(F32), jnp.uint32) & jnp.uint32(0xFFFF0000)
    return lo_bits | hi_bits


def _adaln_kernel(c_ref, w_ref, b_ref, o_ref):
    c = c_ref[...]
    s = c * jax.nn.sigmoid(c)
    o_ref[...] = lax.dot_general(s, w_ref[...], (((1,), (0,)), ((), ())),
                                 precision=lax.Precision.HIGHEST,
                                 preferred_element_type=F32) + b_ref[...]


def _adaln(c_all, w, b):
    m, d = c_all.shape
    n = w.shape[1]
    tn = TILE_ADALN_COLS
    return pl.pallas_call(
        _adaln_kernel,
        grid=(n // tn,),
        in_specs=[pl.BlockSpec((m, d), lambda j: (0, 0)),
                  pl.BlockSpec((d, tn), lambda j: (0, j)),
                  pl.BlockSpec((1, tn), lambda j: (0, j))],
        out_specs=pl.BlockSpec((m, tn), lambda j: (0, j)),
        out_shape=jax.ShapeDtypeStruct((m, n), F32),
        compiler_params=_cparams(("arbitrary",)),
        name="adaln",
    )(c_all, w, b.reshape(1, n))


def _norm_proj_kernel(widths, x_ref, g_ref, sh_ref, sc_ref, w_ref, *o_refs):
    x = x_ref[0]
    ms = jnp.mean(x * x, axis=-1, keepdims=True)
    y = x * lax.rsqrt(ms + EPS) * g_ref[...]
    h = (y * (1.0 + sc_ref[0]) + sh_ref[0]).astype(BF16)
    off = 0
    for o_ref, wd in zip(o_refs, widths):
        o_ref[0] = _dot(h, w_ref[:, off:off + wd]).astype(o_ref.dtype)
        off += wd


def _norm_proj(x, gain, shift, scale, w, widths, tm):
    b, s, d = x.shape
    n = w.shape[1]
    assert sum(widths) == n and s % tm == 0
    vec = pl.BlockSpec((1, 1, d), lambda i, j: (i, 0, 0))
    return pl.pallas_call(
        functools.partial(_norm_proj_kernel, widths),
        grid=(b, s // tm),
        in_specs=[pl.BlockSpec((1, tm, d), lambda i, j: (i, j, 0)),
                  pl.BlockSpec((1, d), lambda i, j: (0, 0)),
                  vec, vec,
                  pl.BlockSpec((d, n), lambda i, j: (0, 0))],
        out_specs=[pl.BlockSpec((1, tm, wd), lambda i, j: (i, j, 0)) for wd in widths],
        out_shape=[jax.ShapeDtypeStruct((b, s, wd), BF16) for wd in widths],
        compiler_params=_cparams(("arbitrary", "arbitrary")),
        name="norm_proj",
    )(x, gain, shift, scale, w)


def _bias_table_kernel(rpb_ref, o_ref):
    h = pl.program_id(0)
    q = lax.broadcasted_iota(jnp.int32, (GRID_W, GRID_W), 0)
    kc = lax.broadcasted_iota(jnp.int32, (GRID_W, GRID_W), 1)
    dc = jnp.clip(kc - q + NA_WIN_COLS - 1, 0, 2 * NA_WIN_COLS - 2)
    cs = jnp.clip(q - NA_WIN_COLS // 2, 0, GRID_W - NA_WIN_COLS)
    col_in = (kc >= cs) & (kc < cs + NA_WIN_COLS)
    n_dc = 2 * NA_WIN_COLS - 1
    n_dr = 2 * NA_WIN_ROWS - 1
    for dr in range(n_dr):
        t = jnp.zeros((GRID_W, GRID_W), F32)
        for c in range(n_dc):
            t = jnp.where(dc == c, rpb_ref[h * n_dr + dr, c], t)
        t = jnp.where(col_in, t, NEG_INF)
        for d0 in range(NA_WIN_ROWS):
            j = dr - d0
            if 0 <= j < NA_WIN_ROWS:
                o_ref[0, d0, :, j * GRID_W:(j + 1) * GRID_W] = t


def _bias_table(rpb):
    nh, n_dr, n_dc = rpb.shape
    band = NA_WIN_ROWS * GRID_W
    return pl.pallas_call(
        _bias_table_kernel,
        grid=(nh,),
        in_specs=[pl.BlockSpec(memory_space=pltpu.SMEM)],
        out_specs=pl.BlockSpec((1, NA_WIN_ROWS, GRID_W, band), lambda h: (h, 0, 0, 0)),
        out_shape=jax.ShapeDtypeStruct((nh, NA_WIN_ROWS, GRID_W, band), F32),
        compiler_params=_cparams(("arbitrary",)),
        name="bias_table",
    )(rpb.reshape(nh * n_dr, n_dc))


def _attn_kernel(rows, q_ref, k_ref, v_ref, kc_ref, vc_ref, bias_ref, o_ref, s_scr, p_scr):
    for rr in range(ATTN_ROWS_PER_STEP):
        _attn_row(rows, pl.program_id(1) * ATTN_ROWS_PER_STEP + rr, slice(rr * GRID_W, (rr + 1) * GRID_W),
                  q_ref, k_ref, v_ref, kc_ref, vc_ref, bias_ref, o_ref, s_scr, p_scr)


def _attn_row(rows, r, qrows, q_ref, k_ref, v_ref, kc_ref, vc_ref, bias_ref, o_ref, s_scr, p_scr):
    rs = jnp.clip(r - NA_WIN_ROWS // 2, 0, rows - NA_WIN_ROWS)
    d0 = rs - r + NA_WIN_ROWS - 1
    band = NA_WIN_ROWS * GRID_W
    start = pl.multiple_of(rs * GRID_W, GRID_W)
    scale = HEAD_DIM ** -0.5
    lane = lax.broadcasted_iota(jnp.int32, (GRID_W, LANES), 1)
    hpg = LANES // HEAD_DIM
    groups = NA_HEADS // hpg

    def mine(hh):
        return (lane >= hh * HEAD_DIM) & (lane < (hh + 1) * HEAD_DIM)

    for hp in range(groups):
        sl = slice(hp * LANES, (hp + 1) * LANES)
        q2 = q_ref[0, qrows, sl]
        kb = k_ref[0, pl.ds(start, band), sl]
        kc = kc_ref[0, :, sl]
        for hh in range(hpg):
            h = hp * hpg + hh
            qh = jnp.where(mine(hh), q2, jnp.zeros_like(q2))
            s_scr[h, :, :band] = _dot_nt(qh, kb) * scale + bias_ref[h, d0]
            s_scr[h, :, band:] = _dot_nt(qh, kc) * scale
    dens = []
    for h in range(NA_HEADS):
        s = s_scr[h]
        p = jnp.exp(s - jnp.max(s, axis=-1, keepdims=True))
        dens.append(jnp.sum(p, axis=-1, keepdims=True))
        p_scr[h] = p.astype(BF16)
    for hp in range(groups):
        sl = slice(hp * LANES, (hp + 1) * LANES)
        vb = v_ref[0, pl.ds(start, band), sl]
        vc = vc_ref[0, :, sl]
        halves = []
        for hh in range(hpg):
            h = hp * hpg + hh
            o = _dot(p_scr[h, :, :band], vb) + _dot(p_scr[h, :, band:], vc)
            halves.append(jnp.where(mine(hh), o / dens[h], 0.0))
        o_ref[0, qrows, sl] = sum(halves).astype(o_ref.dtype)


def _attention(q, k, v, kc, vc, bias):
    b, s, w = q.shape
    rows = s // GRID_W
    c = kc.shape[1]
    full = lambda n: pl.BlockSpec((1, n, w), lambda i, j: (i, 0, 0))
    return pl.pallas_call(
        functools.partial(_attn_kernel, rows),
        grid=(b, rows // ATTN_ROWS_PER_STEP),
        in_specs=[pl.BlockSpec((1, ATTN_ROWS_PER_STEP * GRID_W, w), lambda i, j: (i, j, 0)),
                  full(s), full(s), full(c), full(c),
                  pl.BlockSpec(bias.shape, lambda i, j: (0, 0, 0, 0))],
        out_specs=pl.BlockSpec((1, ATTN_ROWS_PER_STEP * GRID_W, w), lambda i, j: (i, j, 0)),
        out_shape=jax.ShapeDtypeStruct((b, s, w), BF16),
        scratch_shapes=[pltpu.VMEM((NA_HEADS, GRID_W, NA_WIN_ROWS * GRID_W + c), F32),
                        pltpu.VMEM((NA_HEADS, GRID_W, NA_WIN_ROWS * GRID_W + c), BF16)],
        compiler_params=_cparams(("arbitrary", "arbitrary")),
        name="nbr_attention",
    )(q, k, v, kc, vc, bias)


def _mix_kernel(x_ref, ya_ref, n1g_ref, sh1_ref, sc1_ref, g1_ref, sh2_ref, sc2_ref, wg_ref,
                lng_ref, ws_ref, bs_ref, wpa_ref, wpb_ref, wout_ref, n2g_ref, wq_ref, keys_ref,
                x1_ref, h2_ref, idx_ref, gate_ref, st_ref, sv_ref, si_ref, se_ref, sg_ref):
    tm = x_ref.shape[1]
    d = x_ref.shape[2]
    gw = lng_ref.shape[1]
    x = x_ref[0]
    h = x * lax.rsqrt(jnp.mean(x * x, axis=-1, keepdims=True) + EPS) * n1g_ref[...]
    g4 = _dot((h * (1.0 + sc1_ref[0]) + sh1_ref[0]).astype(BF16), wg_ref[...])
    ga = g4[:, 2 * gw:2 * gw + d]
    gb = g4[:, 2 * gw + d:]
    u = jax.nn.gelu(g4[:, :gw])
    t = jax.nn.gelu(g4[:, gw:2 * gw])
    tc = t - jnp.mean(t, axis=-1, keepdims=True)
    vn = tc * lax.rsqrt(jnp.mean(tc * tc, axis=-1, keepdims=True) + EPS) * lng_ref[...]
    vnb = vn.astype(BF16)
    lane = lax.broadcasted_iota(jnp.int32, (GM_CHUNK, LANES), 1)
    gd = gw // GM_GROUPS
    chunks = []
    for c in range(tm // GM_CHUNK):
        pairs = []
        for gp in range(gw // LANES):
            vp = vnb[c * GM_CHUNK:(c + 1) * GM_CHUNK, gp * LANES:(gp + 1) * LANES]
            r0 = _dot(ws_ref[2 * gp], vp)
            r1 = _dot(ws_ref[2 * gp + 1], vp)
            pairs.append(jnp.where(lane < gd, r0, r1))
        chunks.append(jnp.concatenate(pairs, axis=1) + bs_ref[...])
    mixed = jnp.concatenate(chunks, axis=0)
    yb = (u * mixed).astype(BF16)
    pa = _dot(ya_ref[0], wpa_ref[...])
    pb = _dot(yb, wpb_ref[...])
    m = jax.nn.sigmoid(ga) * pa + jax.nn.sigmoid(gb) * pb
    out = _dot(m.astype(BF16), wout_ref[...])
    x1 = x + g1_ref[0] * out
    x1_ref[0] = x1
    ms = jnp.mean(x1 * x1, axis=-1, keepdims=True)
    h2 = x1 * lax.rsqrt(ms + EPS) * n2g_ref[...]
    h2 = (h2 * (1.0 + sc2_ref[0]) + sh2_ref[0]).astype(BF16)
    half = h2.shape[1] // 2
    h2_ref[0] = _pack_bf16_pairs(h2[:, :half], h2[:, half:])
    qp = _dot(h2, wq_ref[...]).astype(BF16)
    for hp in range(keys_ref.shape[0]):
        st_ref[hp] = _dot_nt(keys_ref[hp], qp[:, hp * PEER_HALF:(hp + 1) * PEER_HALF])
    _topk_body(st_ref, idx_ref.at[0], gate_ref.at[0], sv_ref, si_ref, se_ref, sg_ref)


def _mix(x, ya, n1g, sh1, sc1, g1, sh2, sc2, wg, lng, ws, bs_full, wpa, wpb, wout, n2g, wq, keys, tm):
    b, s, d = x.shape
    gw = lng.shape[1]
    nt = s // tm
    nhp = keys.shape[0]
    ne = (nhp // 2) * PEER_TOPK
    assert s % tm == 0 and tm % LANES == 0
    act = lambda w: pl.BlockSpec((1, tm, w), lambda i, j: (i, j, 0))
    vec = pl.BlockSpec((1, 1, d), lambda i, j: (i, 0, 0))

    def const(a):
        nd = a.ndim
        return pl.BlockSpec(a.shape, lambda i, j: (0,) * nd)

    return pl.pallas_call(
        _mix_kernel,
        grid=(b, nt),
        in_specs=[act(d), act(ya.shape[2]), const(n1g), vec, vec, vec, vec, vec, const(wg),
                  const(lng), const(ws), const(bs_full), const(wpa), const(wpb), const(wout),
                  const(n2g), const(wq), const(keys)],
        out_specs=[act(d), act(d // 2), act(ne), act(ne)],
        out_shape=[jax.ShapeDtypeStruct((b, s, d), F32),
                   jax.ShapeDtypeStruct((b, s, d // 2), jnp.uint32),
                   jax.ShapeDtypeStruct((b, s, ne), jnp.int32),
                   jax.ShapeDtypeStruct((b, s, ne), F32)],
        scratch_shapes=[pltpu.VMEM((nhp, PEER_N_KEYS, tm), F32),
                        pltpu.VMEM((nhp, PEER_TOPK, LANES), F32),
                        pltpu.VMEM((nhp, PEER_TOPK, LANES), F32),
                        pltpu.VMEM((ne, LANES), F32),
                        pltpu.VMEM((ne, LANES), F32)],
        compiler_params=_cparams(("arbitrary", "arbitrary")),
        name="mix_peer_topk",
    )(x, ya, n1g, sh1, sc1, g1, sh2, sc2, wg, lng, ws, bs_full, wpa, wpb, wout, n2g, wq, keys)


def _extract_top(vals, order, payload, k):
    out_v, out_p = [], []
    for _ in range(k):
        m = jnp.max(vals, axis=0, keepdims=True)
        o = jnp.min(jnp.where(vals == m, order, jnp.inf), axis=0, keepdims=True)
        sel = order == o
        out_v.append(m)
        if payload is order:
            out_p.append(o)
        else:
            out_p.append(jnp.max(jnp.where(sel, payload, -1.0), axis=0, keepdims=True))
        vals = jnp.where(sel, -jnp.inf, vals)
    return jnp.concatenate(out_v, axis=0), jnp.concatenate(out_p, axis=0)


def _topk_body(st_ref, idx_ref, gate_ref, sv_ref, si_ref, se_ref, sg_ref):
    nhp = st_ref.shape[0]
    tm = st_ref.shape[2]
    k = PEER_TOPK
    kidx = lax.broadcasted_iota(jnp.int32, (PEER_N_KEYS, LANES), 0).astype(F32)
    sub = lax.broadcasted_iota(jnp.int32, (SUBLANES, LANES), 0).astype(F32)

    def lane_tile(lt, carry):
        lanes = pl.ds(pl.multiple_of(lt * LANES, LANES), LANES)

        def stage1(h, c):
            for hp in (4 * h, 4 * h + 1, 4 * h + 2, 4 * h + 3):
                v, i = _extract_top(st_ref[hp, :, lanes], kidx, kidx, k)
                sv_ref[hp] = v
                si_ref[hp] = i
            return c

        lax.fori_loop(0, nhp // 4, stage1, 0)

        def stage2(h2, c):
            for h in (2 * h2, 2 * h2 + 1):
                stage2_head(h)
            return c

        def stage2_head(h):
            s1, s2 = sv_ref[2 * h], sv_ref[2 * h + 1]
            i1, i2 = si_ref[2 * h] * float(PEER_N_KEYS), si_ref[2 * h + 1]
            cv, co, ce = [], [], []
            for half in range(2):
                b0 = half * SUBLANES
                cv.append(s1[0:1] + s2[b0:b0 + SUBLANES])
                co.append(sub + float(b0))
                ce.append(i1[0:1] + i2[b0:b0 + SUBLANES])
            for a in range(1, SUBLANES):
                cv.append(s1[a:a + 1] + s2[0:SUBLANES])
                co.append(sub + float(a * k))
                ce.append(i1[a:a + 1] + i2[0:SUBLANES])
            cv.append(s1[SUBLANES:k] + s2[0:1])
            co.append((sub + float(SUBLANES)) * float(k))
            ce.append(i1[SUBLANES:k] + i2[0:1])
            bv, be = _extract_top(jnp.concatenate(cv, axis=0), jnp.concatenate(co, axis=0),
                                  jnp.concatenate(ce, axis=0), k)
            ex = jnp.exp(bv - bv[0:1])
            rows = pl.ds(pl.multiple_of(h * k, k), k)
            sg_ref[rows, :] = ex / jnp.sum(ex, axis=0, keepdims=True)
            se_ref[rows, :] = be

        lax.fori_loop(0, nhp // 4, stage2, 0)

        rows = pl.ds(pl.multiple_of(lt * LANES, LANES), LANES)
        idx_ref[rows, :] = se_ref[...].T.astype(jnp.int32)
        gate_ref[rows, :] = sg_ref[...].T
        return carry

    lax.fori_loop(0, tm // LANES, lane_tile, 0)


SC_LANES = 16
SC_WINDOW = 32
SC_ROW_BUFFERS = 4
SC_TOKENS_PER_STEP = 8
SC_WSUM_CHUNKS = 8
SC_BF16_TERMS = 4


def _pack_table_kernel(t_ref, o_ref):
    half = o_ref.shape[1]
    o_ref[...] = _pack_bf16_pairs(t_ref[:, :half].astype(BF16), t_ref[:, half:].astype(BF16))


def _pack_table(t, tn):
    n, d = t.shape
    assert n % tn == 0
    return pl.pallas_call(
        _pack_table_kernel, grid=(n // tn,),
        in_specs=[pl.BlockSpec((tn, d), lambda i: (i, 0))],
        out_specs=pl.BlockSpec((tn, d // 2), lambda i: (i, 0)),
        out_shape=jax.ShapeDtypeStruct((n, d // 2), jnp.uint32),
        compiler_params=_cparams(("arbitrary",)), name="pack_table",
    )(t)


def _sc_bf16(words):
    return plsc.bitcast(words, BF16)


def _sc_phase(table, idx, operand, mode):
    t, ne = idx.shape
    hw = table.shape[1]
    d = 2 * hw
    L = SC_LANES
    win = SC_WINDOW
    nbuf = SC_ROW_BUFFERS
    ahead = nbuf - 1
    tps = SC_TOKENS_PER_STEP
    nq = SC_WSUM_CHUNKS
    nterm = SC_BF16_TERMS
    info = plsc.get_sparse_core_info()
    nc, ns = info.num_cores, info.num_subcores
    nss = t // (nc * ns * tps)
    wps = tps * ne // win
    wpt = ne // win
    blk = tps * ne
    nwin = nss * wps
    assert nss * nc * ns * tps == t and nss % 2 == 0 and wps % nbuf == 0 and wps > ahead
    assert wpt * win == ne and win % L == 0 and hw % (nq * L) == 0 and hw % (nterm * L) == 0 and win % nterm == 0
    dots = mode == "dots"
    assert dots or nbuf % wpt == 0
    op_buf = pltpu.VMEM((2, tps, hw), jnp.uint32) if dots else pltpu.VMEM((2 * blk,), jnp.int32)
    res_buf = pltpu.VMEM((2 * blk,), F32) if dots else pltpu.VMEM((2, tps, d), F32)
    out_type = jax.ShapeDtypeStruct((t * ne,), F32) if dots else jax.ShapeDtypeStruct((t, d), F32)
    scratch = [("idx", pltpu.VMEM((2 * blk,), jnp.int32)), ("op_" + mode, op_buf),
               ("rows", pltpu.VMEM((nbuf, win, hw), jnp.uint32)), ("res_" + mode, res_buf),
               ("isem", pltpu.SemaphoreType.DMA((2,))), ("psem", pltpu.SemaphoreType.DMA((2,))),
               ("gsem", pltpu.SemaphoreType.DMA((nbuf,))), ("osem", pltpu.SemaphoreType.DMA((2,)))]
    if not dots:
        scratch += [("act_stage", pltpu.VMEM((2 * blk,), F32)), ("gate_stage", pltpu.VMEM((2 * blk,), F32)),
                    ("asem", pltpu.SemaphoreType.DMA((2,)))]

    def sc_kernel(table_hbm, idx_hbm, op_hbm, out_hbm, idx_v, op_v, rows_v, res_v, isem, psem, gsem, osem,
                  act_v=None, gate_v=None, asem=None, act_hbm=None):
        ss0 = (lax.axis_index("s") * nc + lax.axis_index("c")) * nss
        lane = lax.iota(jnp.int32, L)

        def flat(ref, n):
            return ref.at[pl.ds(pl.multiple_of(n * blk, blk), blk)]

        def tok_rows(ref, n):
            return ref.at[pl.ds(pl.multiple_of(n * tps, tps), tps)]

        def idx_copy(ss, slot):
            return pltpu.make_async_copy(flat(idx_hbm, ss0 + ss), flat(idx_v, slot), isem.at[slot])

        def op_copies(ss, slot):
            if dots:
                return [pltpu.make_async_copy(tok_rows(op_hbm, ss0 + ss), op_v.at[slot], psem.at[slot])]
            return [pltpu.make_async_copy(flat(act_hbm, ss0 + ss), flat(act_v, slot), asem.at[slot]),
                    pltpu.make_async_copy(flat(op_hbm, ss0 + ss), flat(gate_v, slot), psem.at[slot])]

        def op_start(ss, slot):
            for cp in op_copies(ss, slot):
                cp.start()

        def op_wait(ss, slot):
            for cp in op_copies(ss, slot):
                cp.wait()

        def make_weights(slot):
            def vec(i, c):
                at = pl.ds(pl.multiple_of(slot * blk + i * L, L), L)
                a = act_v[at]
                z = 0.7978845608028654 * (a + 0.044715 * (a * a * a))
                tanh = 1.0 - 2.0 / (jnp.exp(2.0 * z) + 1.0)
                w = 0.5 * a * (1.0 + tanh) * gate_v[at]
                op_v[at] = plsc.bitcast(plsc.pack(w, w, format=plsc.PackFormat.INTERLEAVED), jnp.int32)
                return c

            lax.fori_loop(0, blk // L, vec, 0)

        def out_copy(ss, slot):
            if dots:
                return pltpu.make_async_copy(flat(res_v, slot), flat(out_hbm, ss0 + ss), osem.at[slot])
            return pltpu.make_async_copy(res_v.at[slot], tok_rows(out_hbm, ss0 + ss), osem.at[slot])

        def gather(slot, hs, b):
            ix = idx_v.at[pl.ds(pl.multiple_of(slot * blk + hs * win, win), win)]
            return pltpu.make_async_copy(table_hbm.at[ix], rows_v.at[b], gsem.at[b])

        def reduce_dots(slot, hs, b):
            tok = hs // wpt

            def head(g, c):
                def chunk(jj, accs):
                    off = pl.multiple_of(jj * (nterm * L), nterm * L)
                    xs = [_sc_bf16(op_v[slot, tok, pl.ds(off + q * L, L)]) for q in range(nterm)]
                    new = []
                    for r in range(L):
                        p = _sc_bf16(rows_v[b, g * L + r, pl.ds(off, L)]) * xs[0]
                        for q in range(1, nterm):
                            p = p + _sc_bf16(rows_v[b, g * L + r, pl.ds(off + q * L, L)]) * xs[q]
                        lo, hi = plsc.unpack(p, format=plsc.PackFormat.INTERLEAVED)
                        new.append(accs[r] + lo + hi)
                    return tuple(new)

                accs = lax.fori_loop(0, hw // (nterm * L), chunk, tuple(jnp.zeros((L,), F32) for _ in range(L)))
                out = jnp.zeros((L,), F32)
                for r in range(L):
                    out = jnp.where(lane == r, jnp.sum(accs[r]), out)
                res_v[pl.ds(pl.multiple_of(slot * blk + hs * win + g * L, L), L)] = out
                return c

            lax.fori_loop(0, win // L, head, 0)

        def reduce_wsum(slot, hs, b):
            tok = hs // wpt
            wbase = slot * blk + hs * win
            first = b % wpt == 0

            def colgroup(cg, c):
                col = pl.multiple_of(cg * (nq * L), nq * L)
                if first:
                    accs = tuple(jnp.zeros((L,), F32) for _ in range(2 * nq))
                else:
                    accs = tuple(res_v[slot, tok, pl.ds(col + q * L, L)] for q in range(nq)) + \
                           tuple(res_v[slot, tok, pl.ds(hw + col + q * L, L)] for q in range(nq))

                def rowgroup(rg, accs):
                    accs = list(accs)
                    r = rg * nterm
                    wvec = op_v[pl.ds(pl.multiple_of(wbase + (r // L) * L, L), L)]
                    ws = [_sc_bf16(wvec.at[jnp.full((L,), r % L + k, jnp.int32)].get(mode="promise_in_bounds"))
                          for k in range(nterm)]
                    for q in range(nq):
                        p = _sc_bf16(rows_v[b, r, pl.ds(col + q * L, L)]) * ws[0]
                        for k in range(1, nterm):
                            p = p + _sc_bf16(rows_v[b, r + k, pl.ds(col + q * L, L)]) * ws[k]
                        lo, hi = plsc.unpack(p, format=plsc.PackFormat.INTERLEAVED)
                        accs[q] = accs[q] + lo
                        accs[nq + q] = accs[nq + q] + hi
                    return tuple(accs)

                accs = lax.fori_loop(0, win // nterm, rowgroup, accs)
                for q in range(nq):
                    res_v[slot, tok, pl.ds(col + q * L, L)] = accs[q]
                    res_v[slot, tok, pl.ds(hw + col + q * L, L)] = accs[nq + q]
                return c

            lax.fori_loop(0, hw // (nq * L), colgroup, 0)

        idx_copy(0, 0).start()
        op_start(0, 0)
        idx_copy(1, 1).start()
        op_start(1, 1)
        idx_copy(0, 0).wait()
        op_wait(0, 0)
        for a in range(ahead):
            gather(0, a, a).start()

        @pl.loop(0, nwin, step=nbuf)
        def _(i0):
            for b in range(nbuf):
                i = i0 + b
                ss = i // wps
                hs = i % wps
                slot = ss % 2
                gather(slot, hs, b).wait()
                nb = (b + ahead) % nbuf

                @pl.when(hs + ahead < wps)
                def _():
                    gather(slot, hs + ahead, nb).start()

                @pl.when(jnp.logical_and(hs + ahead >= wps, ss + 1 < nss))
                def _():
                    @pl.when(hs + ahead == wps)
                    def _():
                        idx_copy(ss + 1, 1 - slot).wait()
                        op_wait(ss + 1, 1 - slot)

                    gather(1 - slot, hs + ahead - wps, nb).start()

                @pl.when(jnp.logical_and(hs == 0, ss >= 2))
                def _():
                    out_copy(ss - 2, slot).wait()

                if dots:
                    reduce_dots(slot, hs, b)
                else:
                    @pl.when(hs == 0)
                    def _():
                        make_weights(slot)

                    reduce_wsum(slot, hs, b)

                @pl.when(hs + 1 == wps)
                def _():
                    out_copy(ss, slot).start()

                    @pl.when(ss + 2 < nss)
                    def _():
                        idx_copy(ss + 2, slot).start()
                        op_start(ss + 2, slot)

        out_copy(nss - 2, 0).wait()
        out_copy(nss - 1, 1).wait()

    op = operand if dots else operand.reshape(t * ne)
    return sc_kernel, scratch, out_type, (table, idx.reshape(t * ne), op)


def _sc_peer(tab_u, tab_v, idx, x_packed, gate):
    parts = [_sc_phase(tab_u, idx, x_packed, "dots"), _sc_phase(tab_v, idx, gate, "wsum")]
    names, types = [], []
    for _, scratch, _, _ in parts:
        for name, ty in scratch:
            if name not in names:
                names.append(name)
                types.append(ty)
    n = len(parts)

    def fused(*refs):
        ins, outs, scr = refs[:3 * n], refs[3 * n:4 * n], dict(zip(names, refs[4 * n:]))
        (dots_body, dots_scratch, _, _), (wsum_body, wsum_scratch, _, _) = parts
        dots_body(*ins[0:3], outs[0], *[scr[name] for name, _ in dots_scratch])
        wsum_body(*ins[3:6], outs[1], *[scr[name] for name, _ in wsum_scratch], outs[0])

    call = pl.kernel(
        fused, mesh=plsc.VectorSubcoreMesh(core_axis_name="c", subcore_axis_name="s"),
        out_type=tuple(p[2] for p in parts), scratch_types=types,
        compiler_params=pltpu.CompilerParams(needs_layout_passes=False),
        name="peer_dots_wsum",
    )
    return call(*[a for p in parts for a in p[3]])


def _final_kernel(x1_ref, p_ref, g2_ref, fg_ref, prev_ref, o_ref):
    del prev_ref
    x2 = x1_ref[0] + g2_ref[0] * p_ref[0]
    ms = jnp.mean(x2 * x2, axis=-1, keepdims=True)
    o_ref[0] = x2 * lax.rsqrt(ms + EPS) * fg_ref[...]


def _final(x1, peer, g2, fg, tm, out, b0):
    bc, s, d = x1.shape
    act = pl.BlockSpec((1, tm, d), lambda i, j: (i, j, 0))
    return pl.pallas_call(
        _final_kernel, grid=(bc, s // tm),
        in_specs=[act, act, pl.BlockSpec((1, 1, d), lambda i, j: (i, 0, 0)),
                  pl.BlockSpec((1, d), lambda i, j: (0, 0)),
                  pl.BlockSpec(memory_space=pl.ANY)],
        out_specs=pl.BlockSpec((1, tm, d), lambda i, j: (i + b0, j, 0)),
        out_shape=jax.ShapeDtypeStruct(out.shape, F32),
        input_output_aliases={4: 0},
        compiler_params=_cparams(("arbitrary", "arbitrary")), name="final_norm",
    )(x1, peer, g2, fg, out)


def kernel(x, c, ctx, c_ctx, ada_w, ada_b, norm1_g, norm2_g, w_in, na_rpb, gm_ln_g, gm_ws, gm_bs,
           w_proj_a, w_proj_b, w_out, peer_wq, peer_keys, peer_u, peer_v, final_g):
    b, s, d = x.shape
    naw = NA_HEADS * HEAD_DIM
    gw = gm_ln_g.shape[1]
    layer = 0

    pad = (-(b + 1)) % SUBLANES
    c_all = jnp.concatenate([c, c_ctx[None, :], jnp.zeros((pad, d), F32)], axis=0)
    mod = _adaln(c_all, ada_w[layer], ada_b[layer])
    sh1, sc1, g1, sh2, sc2, g2 = [mod[:b, i * d:(i + 1) * d].reshape(b, 1, d) for i in range(ADA_CHUNKS)]
    csh1 = jnp.broadcast_to(mod[b, 0:d].reshape(1, 1, d), (b, 1, d))
    csc1 = jnp.broadcast_to(mod[b, d:2 * d].reshape(1, 1, d), (b, 1, d))

    w = w_in[layer].astype(BF16)
    w_qkv, w_kv, w_gates = w[:, :3 * naw], w[:, naw:3 * naw], w[:, 3 * naw:]
    n1g = norm1_g[layer].reshape(1, d)
    bias = _bias_table(na_rpb[layer])
    bs_full = jnp.repeat(gm_bs[layer].T, gw // GM_GROUPS, axis=1)
    keys = peer_keys[layer].reshape(2 * PEER_HEADS, PEER_N_KEYS, PEER_HALF).astype(BF16)
    lng = gm_ln_g[layer].reshape(1, gw)
    ws = gm_ws[layer].astype(BF16)
    wpa, wpb, wout = w_proj_a[layer].astype(BF16), w_proj_b[layer].astype(BF16), w_out[layer].astype(BF16)
    n2g = norm2_g[layer].reshape(1, d)
    wq = peer_wq[layer].astype(BF16)
    tab_u, tab_v = _pack_table(peer_u[layer], TILE_PACK), _pack_table(peer_v[layer], TILE_PACK)
    fg = final_g.reshape(1, d)
    ne = PEER_HEADS * PEER_TOPK

    assert sum(BATCH_CHUNKS) == b
    out = jnp.zeros((b, s, d), F32)

    b0 = 0
    for bc in BATCH_CHUNKS:
        sl = slice(b0, b0 + bc)
        boff = b0
        b0 += bc
        t = bc * s
        xc = x[sl]
        q, k, v = _norm_proj(xc, n1g, sh1[sl], sc1[sl], w_qkv, (naw, naw, naw), TILE_PROJ)
        k_c, v_c = _norm_proj(ctx[sl], n1g, csh1[sl], csc1[sl], w_kv, (naw, naw), ctx.shape[1])
        y_a = _attention(q, k, v, k_c, v_c, bias)
        x1, h2, idx, gate = _mix(xc, y_a, n1g, sh1[sl], sc1[sl], g1[sl], sh2[sl], sc2[sl], w_gates, lng, ws, bs_full,
                                 wpa, wpb, wout, n2g, wq, keys, TILE_MIX)
        _, peer = _sc_peer(tab_u, tab_v, idx.reshape(t, ne), h2.reshape(t, d // 2), gate.reshape(t, ne))
        out = _final(x1, peer.reshape(x1.shape), g2[sl], fg, TILE_FINAL, out, boff)
    return out
```

```python
import functools

import jax
import jax.numpy as jnp
from jax import lax
from jax.experimental import pallas as pl
from jax.experimental.pallas import tpu as pltpu
from jax.experimental.pallas import tpu_sc as plsc

F32 = jnp.float32
BF16 = jnp.bfloat16

GRID_W = 64
NA_HEADS = 8
HEAD_DIM = 64
NA_WIN_ROWS = 8
NA_WIN_COLS = 16
GM_GROUPS = 8
GM_CHUNK = 128
PEER_HEADS = 8
PEER_N_KEYS = 128
PEER_TOPK = 16
PEER_HALF = 128
ADA_CHUNKS = 6
EPS = 1e-6
NEG_INF = -1e30

LANES = 128
SUBLANES = 8
VMEM_LIMIT = 56 * 1024 * 1024
TILE_ADALN_COLS = 1024
TILE_PROJ = 512
TILE_MIX = 256
TILE_FINAL = 512
TILE_PACK = 1024
ATTN_ROWS_PER_STEP = 2
BATCH_CHUNKS = (1, 2, 4, 5, 4)


def _dot(a, b):
    return lax.dot_general(a, b, (((1,), (0,)), ((), ())), preferred_element_type=F32)


def _dot_nt(a, b):
    return lax.dot_general(a, b, (((1,), (1,)), ((), ())), preferred_element_type=F32)


def _cparams(sem):
    return pltpu.CompilerParams(dimension_semantics=sem, vmem_limit_bytes=VMEM_LIMIT)


def _pack_bf16_pairs(lo, hi):
    lo_bits = pltpu.bitcast(lo.astype(F32), jnp.uint32) >> 16
    hi_bits = pltpu.bitcast(hi.astype(F32), jnp.uint32) & jnp.uint32(0xFFFF0000)
    return lo_bits | hi_bits


def _adaln_kernel(c_ref, w_ref, b_ref, o_ref):
    c = c_ref[...]
    s = c * jax.nn.sigmoid(c)
    o_ref[...] = lax.dot_general(s, w_ref[...], (((1,), (0,)), ((), ())),
                                 precision=lax.Precision.HIGHEST,
                                 preferred_element_type=F32) + b_ref[...]


def _adaln(c_all, w, b):
    m, d = c_all.shape
    n = w.shape[1]
    tn = TILE_ADALN_COLS
    return pl.pallas_call(
        _adaln_kernel,
        grid=(n // tn,),
        in_specs=[pl.BlockSpec((m, d), lambda j: (0, 0)),
                  pl.BlockSpec((d, tn), lambda j: (0, j)),
                  pl.BlockSpec((1, tn), lambda j: (0, j))],
        out_specs=pl.BlockSpec((m, tn), lambda j: (0, j)),
        out_shape=jax.ShapeDtypeStruct((m, n), F32),
        compiler_params=_cparams(("arbitrary",)),
        name="adaln",
    )(c_all, w, b.reshape(1, n))


def _norm_proj_kernel(widths, x_ref, g_ref, sh_ref, sc_ref, w_ref, *o_refs):
    x = x_ref[0]
    ms = jnp.mean(x * x, axis=-1, keepdims=True)
    y = x * lax.rsqrt(ms + EPS) * g_ref[...]
    h = (y * (1.0 + sc_ref[0]) + sh_ref[0]).astype(BF16)
    off = 0
    for o_ref, wd in zip(o_refs, widths):
        o_ref[0] = _dot(h, w_ref[:, off:off + wd]).astype(o_ref.dtype)
        off += wd


def _norm_proj(x, gain, shift, scale, w, widths, tm):
    b, s, d = x.shape
    n = w.shape[1]
    assert sum(widths) == n and s % tm == 0
    vec = pl.BlockSpec((1, 1, d), lambda i, j: (i, 0, 0))
    return pl.pallas_call(
        functools.partial(_norm_proj_kernel, widths),
        grid=(b, s // tm),
        in_specs=[pl.BlockSpec((1, tm, d), lambda i, j: (i, j, 0)),
                  pl.BlockSpec((1, d), lambda i, j: (0, 0)),
                  vec, vec,
                  pl.BlockSpec((d, n), lambda i, j: (0, 0))],
        out_specs=[pl.BlockSpec((1, tm, wd), lambda i, j: (i, j, 0)) for wd in widths],
        out_shape=[jax.ShapeDtypeStruct((b, s, wd), BF16) for wd in widths],
        compiler_params=_cparams(("arbitrary", "arbitrary")),
        name="norm_proj",
    )(x, gain, shift, scale, w)


def _bias_table_kernel(rpb_ref, o_ref):
    h = pl.program_id(0)
    q = lax.broadcasted_iota(jnp.int32, (GRID_W, GRID_W), 0)
    kc = lax.broadcasted_iota(jnp.int32, (GRID_W, GRID_W), 1)
    dc = jnp.clip(kc - q + NA_WIN_COLS - 1, 0, 2 * NA_WIN_COLS - 2)
    cs = jnp.clip(q - NA_WIN_COLS // 2, 0, GRID_W - NA_WIN_COLS)
    col_in = (kc >= cs) & (kc < cs + NA_WIN_COLS)
    n_dc = 2 * NA_WIN_COLS - 1
    n_dr = 2 * NA_WIN_ROWS - 1
    for dr in range(n_dr):
        t = jnp.zeros((GRID_W, GRID_W), F32)
        for c in range(n_dc):
            t = jnp.where(dc == c, rpb_ref[h * n_dr + dr, c], t)
        t = jnp.where(col_in, t, NEG_INF)
        for d0 in range(NA_WIN_ROWS):
            j = dr - d0
            if 0 <= j < NA_WIN_ROWS:
                o_ref[0, d0, :, j * GRID_W:(j + 1) * GRID_W] = t


def _bias_table(rpb):
    nh, n_dr, n_dc = rpb.shape
    band = NA_WIN_ROWS * GRID_W
    return pl.pallas_call(
        _bias_table_kernel,
        grid=(nh,),
        in_specs=[pl.BlockSpec(memory_space=pltpu.SMEM)],
        out_specs=pl.BlockSpec((1, NA_WIN_ROWS, GRID_W, band), lambda h: (h, 0, 0, 0)),
        out_shape=jax.ShapeDtypeStruct((nh, NA_WIN_ROWS, GRID_W, band), F32),
        compiler_params=_cparams(("arbitrary",)),
        name="bias_table",
    )(rpb.reshape(nh * n_dr, n_dc))


def _attn_kernel(rows, q_ref, k_ref, v_ref, kc_ref, vc_ref, bias_ref, o_ref, s_scr, p_scr):
    for rr in range(ATTN_ROWS_PER_STEP):
        _attn_row(rows, pl.program_id(1) * ATTN_ROWS_PER_STEP + rr, slice(rr * GRID_W, (rr + 1) * GRID_W),
                  q_ref, k_ref, v_ref, kc_ref, vc_ref, bias_ref, o_ref, s_scr, p_scr)


def _attn_row(rows, r, qrows, q_ref, k_ref, v_ref, kc_ref, vc_ref, bias_ref, o_ref, s_scr, p_scr):
    rs = jnp.clip(r - NA_WIN_ROWS // 2, 0, rows - NA_WIN_ROWS)
    d0 = rs - r + NA_WIN_ROWS - 1
    band = NA_WIN_ROWS * GRID_W
    start = pl.multiple_of(rs * GRID_W, GRID_W)
    scale = HEAD_DIM ** -0.5
    lane = lax.broadcasted_iota(jnp.int32, (GRID_W, LANES), 1)
    hpg = LANES // HEAD_DIM
    groups = NA_HEADS // hpg

    def mine(hh):
        return (lane >= hh * HEAD_DIM) & (lane < (hh + 1) * HEAD_DIM)

    for hp in range(groups):
        sl = slice(hp * LANES, (hp + 1) * LANES)
        q2 = q_ref[0, qrows, sl]
        kb = k_ref[0, pl.ds(start, band), sl]
        kc = kc_ref[0, :, sl]
        for hh in range(hpg):
            h = hp * hpg + hh
            qh = jnp.where(mine(hh), q2, jnp.zeros_like(q2))
            s_scr[h, :, :band] = _dot_nt(qh, kb) * scale + bias_ref[h, d0]
            s_scr[h, :, band:] = _dot_nt(qh, kc) * scale
    dens = []
    for h in range(NA_HEADS):
        s = s_scr[h]
        p = jnp.exp(s - jnp.max(s, axis=-1, keepdims=True))
        dens.append(jnp.sum(p, axis=-1, keepdims=True))
        p_scr[h] = p.astype(BF16)
    for hp in range(groups):
        sl = slice(hp * LANES, (hp + 1) * LANES)
        vb = v_ref[0, pl.ds(start, band), sl]
        vc = vc_ref[0, :, sl]
        halves = []
        for hh in range(hpg):
            h = hp * hpg + hh
            o = _dot(p_scr[h, :, :band], vb) + _dot(p_scr[h, :, band:], vc)
            halves.append(jnp.where(mine(hh), o / dens[h], 0.0))
        o_ref[0, qrows, sl] = sum(halves).astype(o_ref.dtype)


def _attention(q, k, v, kc, vc, bias):
    b, s, w = q.shape
    rows = s // GRID_W
    c = kc.shape[1]
    full = lambda n: pl.BlockSpec((1, n, w), lambda i, j: (i, 0, 0))
    return pl.pallas_call(
        functools.partial(_attn_kernel, rows),
        grid=(b, rows // ATTN_ROWS_PER_STEP),
        in_specs=[pl.BlockSpec((1, ATTN_ROWS_PER_STEP * GRID_W, w), lambda i, j: (i, j, 0)),
                  full(s), full(s), full(c), full(c),
                  pl.BlockSpec(bias.shape, lambda i, j: (0, 0, 0, 0))],
        out_specs=pl.BlockSpec((1, ATTN_ROWS_PER_STEP * GRID_W, w), lambda i, j: (i, j, 0)),
        out_shape=jax.ShapeDtypeStruct((b, s, w), BF16),
        scratch_shapes=[pltpu.VMEM((NA_HEADS, GRID_W, NA_WIN_ROWS * GRID_W + c), F32),
                        pltpu.VMEM((NA_HEADS, GRID_W, NA_WIN_ROWS * GRID_W + c), BF16)],
        compiler_params=_cparams(("arbitrary", "arbitrary")),
        name="nbr_attention",
    )(q, k, v, kc, vc, bias)


def _mix_kernel(x_ref, ya_ref, n1g_ref, sh1_ref, sc1_ref, g1_ref, sh2_ref, sc2_ref, wg_ref,
                lng_ref, ws_ref, bs_ref, wpa_ref, wpb_ref, wout_ref, n2g_ref, wq_ref, keys_ref,
                x1_ref, h2_ref, idx_ref, gate_ref, st_ref, sv_ref, si_ref, se_ref, sg_ref):
    tm = x_ref.shape[1]
    d = x_ref.shape[2]
    gw = lng_ref.shape[1]
    x = x_ref[0]
    h = x * lax.rsqrt(jnp.mean(x * x, axis=-1, keepdims=True) + EPS) * n1g_ref[...]
    g4 = _dot((h * (1.0 + sc1_ref[0]) + sh1_ref[0]).astype(BF16), wg_ref[...])
    ga = g4[:, 2 * gw:2 * gw + d]
    gb = g4[:, 2 * gw + d:]
    u = jax.nn.gelu(g4[:, :gw])
    t = jax.nn.gelu(g4[:, gw:2 * gw])
    tc = t - jnp.mean(t, axis=-1, keepdims=True)
    vn = tc * lax.rsqrt(jnp.mean(tc * tc, axis=-1, keepdims=True) + EPS) * lng_ref[...]
    vnb = vn.astype(BF16)
    lane = lax.broadcasted_iota(jnp.int32, (GM_CHUNK, LANES), 1)
    gd = gw // GM_GROUPS
    chunks = []
    for c in range(tm // GM_CHUNK):
        pairs = []
        for gp in range(gw // LANES):
            vp = vnb[c * GM_CHUNK:(c + 1) * GM_CHUNK, gp * LANES:(gp + 1) * LANES]
            r0 = _dot(ws_ref[2 * gp], vp)
            r1 = _dot(ws_ref[2 * gp + 1], vp)
            pairs.append(jnp.where(lane < gd, r0, r1))
        chunks.append(jnp.concatenate(pairs, axis=1) + bs_ref[...])
    mixed = jnp.concatenate(chunks, axis=0)
    yb = (u * mixed).astype(BF16)
    pa = _dot(ya_ref[0], wpa_ref[...])
    pb = _dot(yb, wpb_ref[...])
    m = jax.nn.sigmoid(ga) * pa + jax.nn.sigmoid(gb) * pb
    out = _dot(m.astype(BF16), wout_ref[...])
    x1 = x + g1_ref[0] * out
    x1_ref[0] = x1
    ms = jnp.mean(x1 * x1, axis=-1, keepdims=True)
    h2 = x1 * lax.rsqrt(ms + EPS) * n2g_ref[...]
    h2 = (h2 * (1.0 + sc2_ref[0]) + sh2_ref[0]).astype(BF16)
    half = h2.shape[1] // 2
    h2_ref[0] = _pack_bf16_pairs(h2[:, :half], h2[:, half:])
    qp = _dot(h2, wq_ref[...]).astype(BF16)
    for hp in range(keys_ref.shape[0]):
        st_ref[hp] = _dot_nt(keys_ref[hp], qp[:, hp * PEER_HALF:(hp + 1) * PEER_HALF])
    _topk_body(st_ref, idx_ref.at[0], gate_ref.at[0], sv_ref, si_ref, se_ref, sg_ref)


def _mix(x, ya, n1g, sh1, sc1, g1, sh2, sc2, wg, lng, ws, bs_full, wpa, wpb, wout, n2g, wq, keys, tm):
    b, s, d = x.shape
    gw = lng.shape[1]
    nt = s // tm
    nhp = keys.shape[0]
    ne = (nhp // 2) * PEER_TOPK
    assert s % tm == 0 and tm % LANES == 0
    act = lambda w: pl.BlockSpec((1, tm, w), lambda i, j: (i, j, 0))
    vec = pl.BlockSpec((1, 1, d), lambda i, j: (i, 0, 0))

    def const(a):
        nd = a.ndim
        return pl.BlockSpec(a.shape, lambda i, j: (0,) * nd)

    return pl.pallas_call(
        _mix_kernel,
        grid=(b, nt),
        in_specs=[act(d), act(ya.shape[2]), const(n1g), vec, vec, vec, vec, vec, const(wg),
                  const(lng), const(ws), const(bs_full), const(wpa), const(wpb), const(wout),
                  const(n2g), const(wq), const(keys)],
        out_specs=[act(d), act(d // 2), act(ne), act(ne)],
        out_shape=[jax.ShapeDtypeStruct((b, s, d), F32),
                   jax.ShapeDtypeStruct((b, s, d // 2), jnp.uint32),
                   jax.ShapeDtypeStruct((b, s, ne), jnp.int32),
                   jax.ShapeDtypeStruct((b, s, ne), F32)],
        scratch_shapes=[pltpu.VMEM((nhp, PEER_N_KEYS, tm), F32),
                        pltpu.VMEM((nhp, PEER_TOPK, LANES), F32),
                        pltpu.VMEM((nhp, PEER_TOPK, LANES), F32),
                        pltpu.VMEM((ne, LANES), F32),
                        pltpu.VMEM((ne, LANES), F32)],
        compiler_params=_cparams(("arbitrary", "arbitrary")),
        name="mix_peer_topk",
    )(x, ya, n1g, sh1, sc1, g1, sh2, sc2, wg, lng, ws, bs_full, wpa, wpb, wout, n2g, wq, keys)


def _extract_top(vals, order, payload, k):
    out_v, out_p = [], []
    for _ in range(k):
        m = jnp.max(vals, axis=0, keepdims=True)
        o = jnp.min(jnp.where(vals == m, order, jnp.inf), axis=0, keepdims=True)
        sel = order == o
        out_v.append(m)
        if payload is order:
            out_p.append(o)
        else:
            out_p.append(jnp.max(jnp.where(sel, payload, -1.0), axis=0, keepdims=True))
        vals = jnp.where(sel, -jnp.inf, vals)
    return jnp.concatenate(out_v, axis=0), jnp.concatenate(out_p, axis=0)


def _topk_body(st_ref, idx_ref, gate_ref, sv_ref, si_ref, se_ref, sg_ref):
    nhp = st_ref.shape[0]
    tm = st_ref.shape[2]
    k = PEER_TOPK
    kidx = lax.broadcasted_iota(jnp.int32, (PEER_N_KEYS, LANES), 0).astype(F32)
    sub = lax.broadcasted_iota(jnp.int32, (SUBLANES, LANES), 0).astype(F32)

    def lane_tile(lt, carry):
        lanes = pl.ds(pl.multiple_of(lt * LANES, LANES), LANES)

        def stage1(h, c):
            for hp in (4 * h, 4 * h + 1, 4 * h + 2, 4 * h + 3):
                v, i = _extract_top(st_ref[hp, :, lanes], kidx, kidx, k)
                sv_ref[hp] = v
                si_ref[hp] = i
            return c

        lax.fori_loop(0, nhp // 4, stage1, 0)

        def stage2(h2, c):
            for h in (2 * h2, 2 * h2 + 1):
                stage2_head(h)
            return c

        def stage2_head(h):
            s1, s2 = sv_ref[2 * h], sv_ref[2 * h + 1]
            i1, i2 = si_ref[2 * h] * float(PEER_N_KEYS), si_ref[2 * h + 1]
            cv, co, ce = [], [], []
            for half in range(2):
                b0 = half * SUBLANES
                cv.append(s1[0:1] + s2[b0:b0 + SUBLANES])
                co.append(sub + float(b0))
                ce.append(i1[0:1] + i2[b0:b0 + SUBLANES])
            for a in range(1, SUBLANES):
                cv.append(s1[a:a + 1] + s2[0:SUBLANES])
                co.append(sub + float(a * k))
                ce.append(i1[a:a + 1] + i2[0:SUBLANES])
            cv.append(s1[SUBLANES:k] + s2[0:1])
            co.append((sub + float(SUBLANES)) * float(k))
            ce.append(i1[SUBLANES:k] + i2[0:1])
            bv, be = _extract_top(jnp.concatenate(cv, axis=0), jnp.concatenate(co, axis=0),
                                  jnp.concatenate(ce, axis=0), k)
            ex = jnp.exp(bv - bv[0:1])
            rows = pl.ds(pl.multiple_of(h * k, k), k)
            sg_ref[rows, :] = ex / jnp.sum(ex, axis=0, keepdims=True)
            se_ref[rows, :] = be

        lax.fori_loop(0, nhp // 4, stage2, 0)

        rows = pl.ds(pl.multiple_of(lt * LANES, LANES), LANES)
        idx_ref[rows, :] = se_ref[...].T.astype(jnp.int32)
        gate_ref[rows, :] = sg_ref[...].T
        return carry

    lax.fori_loop(0, tm // LANES, lane_tile, 0)


SC_LANES = 16
SC_WINDOW = 32
SC_ROW_BUFFERS = 4
SC_TOKENS_PER_STEP = 8
SC_WSUM_CHUNKS = 8
SC_BF16_TERMS = 4


def _pack_table_kernel(t_ref, o_ref):
    half = o_ref.shape[1]
    o_ref[...] = _pack_bf16_pairs(t_ref[:, :half].astype(BF16), t_ref[:, half:].astype(BF16))


def _pack_table(t, tn):
    n, d = t.shape
    assert n % tn == 0
    return pl.pallas_call(
        _pack_table_kernel, grid=(n // tn,),
        in_specs=[pl.BlockSpec((tn, d), lambda i: (i, 0))],
        out_specs=pl.BlockSpec((tn, d // 2), lambda i: (i, 0)),
        out_shape=jax.ShapeDtypeStruct((n, d // 2), jnp.uint32),
        compiler_params=_cparams(("arbitrary",)), name="pack_table",
    )(t)


def _sc_bf16(words):
    return plsc.bitcast(words, BF16)


def _sc_phase(table, idx, operand, mode):
    t, ne = idx.shape
    hw = table.shape[1]
    d = 2 * hw
    L = SC_LANES
    win = SC_WINDOW
    nbuf = SC_ROW_BUFFERS
    ahead = nbuf - 1
    tps = SC_TOKENS_PER_STEP
    nq = SC_WSUM_CHUNKS
    nterm = SC_BF16_TERMS
    info = plsc.get_sparse_core_info()
    nc, ns = info.num_cores, info.num_subcores
    nss = t // (nc * ns * tps)
    wps = tps * ne // win
    wpt = ne // win
    blk = tps * ne
    nwin = nss * wps
    assert nss * nc * ns * tps == t and nss % 2 == 0 and wps % nbuf == 0 and wps > ahead
    assert wpt * win == ne and win % L == 0 and hw % (nq * L) == 0 and hw % (nterm * L) == 0 and win % nterm == 0
    dots = mode == "dots"
    assert dots or nbuf % wpt == 0
    op_buf = pltpu.VMEM((2, tps, hw), jnp.uint32) if dots else pltpu.VMEM((2 * blk,), jnp.int32)
    res_buf = pltpu.VMEM((2 * blk,), F32) if dots else pltpu.VMEM((2, tps, d), F32)
    out_type = jax.ShapeDtypeStruct((t * ne,), F32) if dots else jax.ShapeDtypeStruct((t, d), F32)
    scratch = [("idx", pltpu.VMEM((2 * blk,), jnp.int32)), ("op_" + mode, op_buf),
               ("rows", pltpu.VMEM((nbuf, win, hw), jnp.uint32)), ("res_" + mode, res_buf),
               ("isem", pltpu.SemaphoreType.DMA((2,))), ("psem", pltpu.SemaphoreType.DMA((2,))),
               ("gsem", pltpu.SemaphoreType.DMA((nbuf,))), ("osem", pltpu.SemaphoreType.DMA((2,)))]
    if not dots:
        scratch += [("act_stage", pltpu.VMEM((2 * blk,), F32)), ("gate_stage", pltpu.VMEM((2 * blk,), F32)),
                    ("asem", pltpu.SemaphoreType.DMA((2,)))]

    def sc_kernel(table_hbm, idx_hbm, op_hbm, out_hbm, idx_v, op_v, rows_v, res_v, isem, psem, gsem, osem,
                  act_v=None, gate_v=None, asem=None, act_hbm=None):
        ss0 = (lax.axis_index("s") * nc + lax.axis_index("c")) * nss
        lane = lax.iota(jnp.int32, L)

        def flat(ref, n):
            return ref.at[pl.ds(pl.multiple_of(n * blk, blk), blk)]

        def tok_rows(ref, n):
            return ref.at[pl.ds(pl.multiple_of(n * tps, tps), tps)]

        def idx_copy(ss, slot):
            return pltpu.make_async_copy(flat(idx_hbm, ss0 + ss), flat(idx_v, slot), isem.at[slot])

        def op_copies(ss, slot):
            if dots:
                return [pltpu.make_async_copy(tok_rows(op_hbm, ss0 + ss), op_v.at[slot], psem.at[slot])]
            return [pltpu.make_async_copy(flat(act_hbm, ss0 + ss), flat(act_v, slot), asem.at[slot]),
                    pltpu.make_async_copy(flat(op_hbm, ss0 + ss), flat(gate_v, slot), psem.at[slot])]

        def op_start(ss, slot):
            for cp in op_copies(ss, slot):
                cp.start()

        def op_wait(ss, slot):
            for cp in op_copies(ss, slot):
                cp.wait()

        def make_weights(slot):
            def vec(i, c):
                at = pl.ds(pl.multiple_of(slot * blk + i * L, L), L)
                a = act_v[at]
                z = 0.7978845608028654 * (a + 0.044715 * (a * a * a))
                tanh = 1.0 - 2.0 / (jnp.exp(2.0 * z) + 1.0)
                w = 0.5 * a * (1.0 + tanh) * gate_v[at]
                op_v[at] = plsc.bitcast(plsc.pack(w, w, format=plsc.PackFormat.INTERLEAVED), jnp.int32)
                return c

            lax.fori_loop(0, blk // L, vec, 0)

        def out_copy(ss, slot):
            if dots:
                return pltpu.make_async_copy(flat(res_v, slot), flat(out_hbm, ss0 + ss), osem.at[slot])
            return pltpu.make_async_copy(res_v.at[slot], tok_rows(out_hbm, ss0 + ss), osem.at[slot])

        def gather(slot, hs, b):
            ix = idx_v.at[pl.ds(pl.multiple_of(slot * blk + hs * win, win), win)]
            return pltpu.make_async_copy(table_hbm.at[ix], rows_v.at[b], gsem.at[b])

        def reduce_dots(slot, hs, b):
            tok = hs // wpt

            def head(g, c):
                def chunk(jj, accs):
                    off = pl.multiple_of(jj * (nterm * L), nterm * L)
                    xs = [_sc_bf16(op_v[slot, tok, pl.ds(off + q * L, L)]) for q in range(nterm)]
                    new = []
                    for r in range(L):
                        p = _sc_bf16(rows_v[b, g * L + r, pl.ds(off, L)]) * xs[0]
                        for q in range(1, nterm):
                            p = p + _sc_bf16(rows_v[b, g * L + r, pl.ds(off + q * L, L)]) * xs[q]
                        lo, hi = plsc.unpack(p, format=plsc.PackFormat.INTERLEAVED)
                        new.append(accs[r] + lo + hi)
                    return tuple(new)

                accs = lax.fori_loop(0, hw // (nterm * L), chunk, tuple(jnp.zeros((L,), F32) for _ in range(L)))
                out = jnp.zeros((L,), F32)
                for r in range(L):
                    out = jnp.where(lane == r, jnp.sum(accs[r]), out)
                res_v[pl.ds(pl.multiple_of(slot * blk + hs * win + g * L, L), L)] = out
                return c

            lax.fori_loop(0, win // L, head, 0)

        def reduce_wsum(slot, hs, b):
            tok = hs // wpt
            wbase = slot * blk + hs * win
            first = b % wpt == 0

            def colgroup(cg, c):
                col = pl.multiple_of(cg * (nq * L), nq * L)
                if first:
                    accs = tuple(jnp.zeros((L,), F32) for _ in range(2 * nq))
                else:
                    accs = tuple(res_v[slot, tok, pl.ds(col + q * L, L)] for q in range(nq)) + \
                           tuple(res_v[slot, tok, pl.ds(hw + col + q * L, L)] for q in range(nq))

                def rowgroup(rg, accs):
                    accs = list(accs)
                    r = rg * nterm
                    wvec = op_v[pl.ds(pl.multiple_of(wbase + (r // L) * L, L), L)]
                    ws = [_sc_bf16(wvec.at[jnp.full((L,), r % L + k, jnp.int32)].get(mode="promise_in_bounds"))
                          for k in range(nterm)]
                    for q in range(nq):
                        p = _sc_bf16(rows_v[b, r, pl.ds(col + q * L, L)]) * ws[0]
                        for k in range(1, nterm):
                            p = p + _sc_bf16(rows_v[b, r + k, pl.ds(col + q * L, L)]) * ws[k]
                        lo, hi = plsc.unpack(p, format=plsc.PackFormat.INTERLEAVED)
                        accs[q] = accs[q] + lo
                        accs[nq + q] = accs[nq + q] + hi
                    return tuple(accs)

                accs = lax.fori_loop(0, win // nterm, rowgroup, accs)
                for q in range(nq):
                    res_v[slot, tok, pl.ds(col + q * L, L)] = accs[q]
                    res_v[slot, tok, pl.ds(hw + col + q * L, L)] = accs[nq + q]
                return c

            lax.fori_loop(0, hw // (nq * L), colgroup, 0)

        idx_copy(0, 0).start()
        op_start(0, 0)
        idx_copy(1, 1).start()
        op_start(1, 1)
        idx_copy(0, 0).wait()
        op_wait(0, 0)
        for a in range(ahead):
            gather(0, a, a).start()

        @pl.loop(0, nwin, step=nbuf)
        def _(i0):
            for b in range(nbuf):
                i = i0 + b
                ss = i // wps
                hs = i % wps
                slot = ss % 2
                gather(slot, hs, b).wait()
                nb = (b + ahead) % nbuf

                @pl.when(hs + ahead < wps)
                def _():
                    gather(slot, hs + ahead, nb).start()

                @pl.when(jnp.logical_and(hs + ahead >= wps, ss + 1 < nss))
                def _():
                    @pl.when(hs + ahead == wps)
                    def _():
                        idx_copy(ss + 1, 1 - slot).wait()
                        op_wait(ss + 1, 1 - slot)

                    gather(1 - slot, hs + ahead - wps, nb).start()

                @pl.when(jnp.logical_and(hs == 0, ss >= 2))
                def _():
                    out_copy(ss - 2, slot).wait()

                if dots:
                    reduce_dots(slot, hs, b)
                else:
                    @pl.when(hs == 0)
                    def _():
                        make_weights(slot)

                    reduce_wsum(slot, hs, b)

                @pl.when(hs + 1 == wps)
                def _():
                    out_copy(ss, slot).start()

                    @pl.when(ss + 2 < nss)
                    def _():
                        idx_copy(ss + 2, slot).start()
                        op_start(ss + 2, slot)

        out_copy(nss - 2, 0).wait()
        out_copy(nss - 1, 1).wait()

    op = operand if dots else operand.reshape(t * ne)
    return sc_kernel, scratch, out_type, (table, idx.reshape(t * ne), op)


def _sc_peer(tab_u, tab_v, idx, x_packed, gate):
    parts = [_sc_phase(tab_u, idx, x_packed, "dots"), _sc_phase(tab_v, idx, gate, "wsum")]
    names, types = [], []
    for _, scratch, _, _ in parts:
        for name, ty in scratch:
            if name not in names:
                names.append(name)
                types.append(ty)
    n = len(parts)

    def fused(*refs):
        ins, outs, scr = refs[:3 * n], refs[3 * n:4 * n], dict(zip(names, refs[4 * n:]))
        (dots_body, dots_scratch, _, _), (wsum_body, wsum_scratch, _, _) = parts
        dots_body(*ins[0:3], outs[0], *[scr[name] for name, _ in dots_scratch])
        wsum_body(*ins[3:6], outs[1], *[scr[name] for name, _ in wsum_scratch], outs[0])

    call = pl.kernel(
        fused, mesh=plsc.VectorSubcoreMesh(core_axis_name="c", subcore_axis_name="s"),
        out_type=tuple(p[2] for p in parts), scratch_types=types,
        compiler_params=pltpu.CompilerParams(needs_layout_passes=False),
        name="peer_dots_wsum",
    )
    return call(*[a for p in parts for a in p[3]])


def _final_kernel(x1_ref, p_ref, g2_ref, fg_ref, prev_ref, o_ref):
    del prev_ref
    x2 = x1_ref[0] + g2_ref[0] * p_ref[0]
    ms = jnp.mean(x2 * x2, axis=-1, keepdims=True)
    o_ref[0] = x2 * lax.rsqrt(ms + EPS) * fg_ref[...]


def _final(x1, peer, g2, fg, tm, out, b0):
    bc, s, d = x1.shape
    act = pl.BlockSpec((1, tm, d), lambda i, j: (i, j, 0))
    return pl.pallas_call(
        _final_kernel, grid=(bc, s // tm),
        in_specs=[act, act, pl.BlockSpec((1, 1, d), lambda i, j: (i, 0, 0)),
                  pl.BlockSpec((1, d), lambda i, j: (0, 0)),
                  pl.BlockSpec(memory_space=pl.ANY)],
        out_specs=pl.BlockSpec((1, tm, d), lambda i, j: (i + b0, j, 0)),
        out_shape=jax.ShapeDtypeStruct(out.shape, F32),
        input_output_aliases={4: 0},
        compiler_params=_cparams(("arbitrary", "arbitrary")), name="final_norm",
    )(x1, peer, g2, fg, out)


def kernel(x, c, ctx, c_ctx, ada_w, ada_b, norm1_g, norm2_g, w_in, na_rpb, gm_ln_g, gm_ws, gm_bs,
           w_proj_a, w_proj_b, w_out, peer_wq, peer_keys, peer_u, peer_v, final_g):
    b, s, d = x.shape
    naw = NA_HEADS * HEAD_DIM
    gw = gm_ln_g.shape[1]
    layer = 0

    pad = (-(b + 1)) % SUBLANES
    c_all = jnp.concatenate([c, c_ctx[None, :], jnp.zeros((pad, d), F32)], axis=0)
    mod = _adaln(c_all, ada_w[layer], ada_b[layer])
    sh1, sc1, g1, sh2, sc2, g2 = [mod[:b, i * d:(i + 1) * d].reshape(b, 1, d) for i in range(ADA_CHUNKS)]
    csh1 = jnp.broadcast_to(mod[b, 0:d].reshape(1, 1, d), (b, 1, d))
    csc1 = jnp.broadcast_to(mod[b, d:2 * d].reshape(1, 1, d), (b, 1, d))

    w = w_in[layer].astype(BF16)
    w_qkv, w_kv, w_gates = w[:, :3 * naw], w[:, naw:3 * naw], w[:, 3 * naw:]
    n1g = norm1_g[layer].reshape(1, d)
    bias = _bias_table(na_rpb[layer])
    bs_full = jnp.repeat(gm_bs[layer].T, gw // GM_GROUPS, axis=1)
    keys = peer_keys[layer].reshape(2 * PEER_HEADS, PEER_N_KEYS, PEER_HALF).astype(BF16)
    lng = gm_ln_g[layer].reshape(1, gw)
    ws = gm_ws[layer].astype(BF16)
    wpa, wpb, wout = w_proj_a[layer].astype(BF16), w_proj_b[layer].astype(BF16), w_out[layer].astype(BF16)
    n2g = norm2_g[layer].reshape(1, d)
    wq = peer_wq[layer].astype(BF16)
    tab_u, tab_v = _pack_table(peer_u[layer], TILE_PACK), _pack_table(peer_v[layer], TILE_PACK)
    fg = final_g.reshape(1, d)
    ne = PEER_HEADS * PEER_TOPK

    assert sum(BATCH_CHUNKS) == b
    out = jnp.zeros((b, s, d), F32)

    b0 = 0
    for bc in BATCH_CHUNKS:
        sl = slice(b0, b0 + bc)
        boff = b0
        b0 += bc
        t = bc * s
        xc = x[sl]
        q, k, v = _norm_proj(xc, n1g, sh1[sl], sc1[sl], w_qkv, (naw, naw, naw), TILE_PROJ)
        k_c, v_c = _norm_proj(ctx[sl], n1g, csh1[sl], csc1[sl], w_kv, (naw, naw), ctx.shape[1])
        y_a = _attention(q, k, v, k_c, v_c, bias)
        x1, h2, idx, gate = _mix(xc, y_a, n1g, sh1[sl], sc1[sl], g1[sl], sh2[sl], sc2[sl], w_gates, lng, ws, bs_full,
                                 wpa, wpb, wout, n2g, wq, keys, TILE_MIX)
        _, peer = _sc_peer(tab_u, tab_v, idx.reshape(t, ne), h2.reshape(t, d // 2), gate.reshape(t, ne))
        out = _final(x1, peer.reshape(x1.shape), g2[sl], fg, TILE_FINAL, out, boff)
    return out
```

```python
import functools

import jax
import jax.numpy as jnp
from jax import lax
from jax.experimental import pallas as pl
from jax.experimental.pallas import tpu as pltpu
from jax.experimental.pallas import tpu_sc as plsc

F32 = jnp.float32
BF16 = jnp.bfloat16

GRID_W = 64
NA_HEADS = 8
HEAD_DIM = 64
NA_WIN_ROWS = 8
NA_WIN_COLS = 16
GM_GROUPS = 8
GM_CHUNK = 128
PEER_HEADS = 8
PEER_N_KEYS = 128
PEER_TOPK = 16
PEER_HALF = 128
ADA_CHUNKS = 6
EPS = 1e-6
NEG_INF = -1e30

LANES = 128
SUBLANES = 8
VMEM_LIMIT = 56 * 1024 * 1024
TILE_ADALN_COLS = 1024
TILE_PROJ = 512
TILE_MIX = 256
TILE_FINAL = 512
TILE_PACK = 1024
ATTN_ROWS_PER_STEP = 2
FIRST_CHUNK_SPLITS = 2
BATCH_CHUNKS = (1, 2, 3, 4, 6)


def _dot(a, b):
    return lax.dot_general(a, b, (((1,), (0,)), ((), ())), preferred_element_type=F32)


def _dot_nt(a, b):
    return lax.dot_general(a, b, (((1,), (1,)), ((), ())), preferred_element_type=F32)


def _cparams(sem):
    return pltpu.CompilerParams(dimension_semantics=sem, vmem_limit_bytes=VMEM_LIMIT)


def _pack_bf16_pairs(lo, hi):
    lo_bits = pltpu.bitcast(lo.astype(F32), jnp.uint32) >> 16
    hi_bits = pltpu.bitcast(hi.astype(F32), jnp.uint32) & jnp.uint32(0xFFFF0000)
    return lo_bits | hi_bits


def _adaln_kernel(c_ref, w_ref, b_ref, o_ref):
    c = c_ref[...]
    s = c * jax.nn.sigmoid(c)
    o_ref[...] = lax.dot_general(s, w_ref[...], (((1,), (0,)), ((), ())),
                                 precision=lax.Precision.HIGHEST,
                                 preferred_element_type=F32) + b_ref[...]


def _adaln(c_all, w, b):
    m, d = c_all.shape
    n = w.shape[1]
    tn = TILE_ADALN_COLS
    return pl.pallas_call(
        _adaln_kernel,
        grid=(n // tn,),
        in_specs=[pl.BlockSpec((m, d), lambda j: (0, 0)),
                  pl.BlockSpec((d, tn), lambda j: (0, j)),
                  pl.BlockSpec((1, tn), lambda j: (0, j))],
        out_specs=pl.BlockSpec((m, tn), lambda j: (0, j)),
        out_shape=jax.ShapeDtypeStruct((m, n), F32),
        compiler_params=_cparams(("arbitrary",)),
        name="adaln",
    )(c_all, w, b.reshape(1, n))


def _norm_proj_kernel(widths, x_ref, g_ref, sh_ref, sc_ref, w_ref, *o_refs):
    x = x_ref[0]
    ms = jnp.mean(x * x, axis=-1, keepdims=True)
    y = x * lax.rsqrt(ms + EPS) * g_ref[...]
    h = (y * (1.0 + sc_ref[0]) + sh_ref[0]).astype(BF16)
    off = 0
    for o_ref, wd in zip(o_refs, widths):
        o_ref[0] = _dot(h, w_ref[:, off:off + wd]).astype(o_ref.dtype)
        off += wd


def _norm_proj(x, gain, shift, scale, w, widths, tm):
    b, s, d = x.shape
    n = w.shape[1]
    assert sum(widths) == n and s % tm == 0
    vec = pl.BlockSpec((1, 1, d), lambda i, j: (i, 0, 0))
    return pl.pallas_call(
        functools.partial(_norm_proj_kernel, widths),
        grid=(b, s // tm),
        in_specs=[pl.BlockSpec((1, tm, d), lambda i, j: (i, j, 0)),
                  pl.BlockSpec((1, d), lambda i, j: (0, 0)),
                  vec, vec,
                  pl.BlockSpec((d, n), lambda i, j: (0, 0))],
        out_specs=[pl.BlockSpec((1, tm, wd), lambda i, j: (i, j, 0)) for wd in widths],
        out_shape=[jax.ShapeDtypeStruct((b, s, wd), BF16) for wd in widths],
        compiler_params=_cparams(("arbitrary", "arbitrary")),
        name="norm_proj",
    )(x, gain, shift, scale, w)


def _bias_table_kernel(rpb_ref, o_ref):
    h = pl.program_id(0)
    q = lax.broadcasted_iota(jnp.int32, (GRID_W, GRID_W), 0)
    kc = lax.broadcasted_iota(jnp.int32, (GRID_W, GRID_W), 1)
    dc = jnp.clip(kc - q + NA_WIN_COLS - 1, 0, 2 * NA_WIN_COLS - 2)
    cs = jnp.clip(q - NA_WIN_COLS // 2, 0, GRID_W - NA_WIN_COLS)
    col_in = (kc >= cs) & (kc < cs + NA_WIN_COLS)
    n_dc = 2 * NA_WIN_COLS - 1
    n_dr = 2 * NA_WIN_ROWS - 1
    for dr in range(n_dr):
        t = jnp.zeros((GRID_W, GRID_W), F32)
        for c in range(n_dc):
            t = jnp.where(dc == c, rpb_ref[h * n_dr + dr, c], t)
        t = jnp.where(col_in, t, NEG_INF)
        for d0 in range(NA_WIN_ROWS):
            j = dr - d0
            if 0 <= j < NA_WIN_ROWS:
                o_ref[0, d0, :, j * GRID_W:(j + 1) * GRID_W] = t


def _bias_table(rpb):
    nh, n_dr, n_dc = rpb.shape
    band = NA_WIN_ROWS * GRID_W
    return pl.pallas_call(
        _bias_table_kernel,
        grid=(nh,),
        in_specs=[pl.BlockSpec(memory_space=pltpu.SMEM)],
        out_specs=pl.BlockSpec((1, NA_WIN_ROWS, GRID_W, band), lambda h: (h, 0, 0, 0)),
        out_shape=jax.ShapeDtypeStruct((nh, NA_WIN_ROWS, GRID_W, band), F32),
        compiler_params=_cparams(("arbitrary",)),
        name="bias_table",
    )(rpb.reshape(nh * n_dr, n_dc))


def _attn_kernel(rows, q_ref, k_ref, v_ref, kc_ref, vc_ref, bias_ref, o_ref, s_scr, p_scr):
    for rr in range(ATTN_ROWS_PER_STEP):
        _attn_row(rows, pl.program_id(1) * ATTN_ROWS_PER_STEP + rr, slice(rr * GRID_W, (rr + 1) * GRID_W),
                  q_ref, k_ref, v_ref, kc_ref, vc_ref, bias_ref, o_ref, s_scr, p_scr)


def _attn_row(rows, r, qrows, q_ref, k_ref, v_ref, kc_ref, vc_ref, bias_ref, o_ref, s_scr, p_scr):
    rs = jnp.clip(r - NA_WIN_ROWS // 2, 0, rows - NA_WIN_ROWS)
    d0 = rs - r + NA_WIN_ROWS - 1
    band = NA_WIN_ROWS * GRID_W
    start = pl.multiple_of(rs * GRID_W, GRID_W)
    scale = HEAD_DIM ** -0.5
    lane = lax.broadcasted_iota(jnp.int32, (GRID_W, LANES), 1)
    hpg = LANES // HEAD_DIM
    groups = NA_HEADS // hpg

    def mine(hh):
        return (lane >= hh * HEAD_DIM) & (lane < (hh + 1) * HEAD_DIM)

    for hp in range(groups):
        sl = slice(hp * LANES, (hp + 1) * LANES)
        q2 = q_ref[0, qrows, sl]
        kb = k_ref[0, pl.ds(start, band), sl]
        kc = kc_ref[0, :, sl]
        for hh in range(hpg):
            h = hp * hpg + hh
            qh = jnp.where(mine(hh), q2, jnp.zeros_like(q2))
            s_scr[h, :, :band] = _dot_nt(qh, kb) * scale + bias_ref[h, d0]
            s_scr[h, :, band:] = _dot_nt(qh, kc) * scale
    dens = []
    for h in range(NA_HEADS):
        s = s_scr[h]
        p = jnp.exp(s - jnp.max(s, axis=-1, keepdims=True))
        dens.append(jnp.sum(p, axis=-1, keepdims=True))
        p_scr[h] = p.astype(BF16)
    for hp in range(groups):
        sl = slice(hp * LANES, (hp + 1) * LANES)
        vb = v_ref[0, pl.ds(start, band), sl]
        vc = vc_ref[0, :, sl]
        halves = []
        for hh in range(hpg):
            h = hp * hpg + hh
            o = _dot(p_scr[h, :, :band], vb) + _dot(p_scr[h, :, band:], vc)
            halves.append(jnp.where(mine(hh), o / dens[h], 0.0))
        o_ref[0, qrows, sl] = sum(halves).astype(o_ref.dtype)


def _attention(q, k, v, kc, vc, bias):
    b, s, w = q.shape
    rows = s // GRID_W
    c = kc.shape[1]
    full = lambda n: pl.BlockSpec((1, n, w), lambda i, j: (i, 0, 0))
    return pl.pallas_call(
        functools.partial(_attn_kernel, rows),
        grid=(b, rows // ATTN_ROWS_PER_STEP),
        in_specs=[pl.BlockSpec((1, ATTN_ROWS_PER_STEP * GRID_W, w), lambda i, j: (i, j, 0)),
                  full(s), full(s), full(c), full(c),
                  pl.BlockSpec(bias.shape, lambda i, j: (0, 0, 0, 0))],
        out_specs=pl.BlockSpec((1, ATTN_ROWS_PER_STEP * GRID_W, w), lambda i, j: (i, j, 0)),
        out_shape=jax.ShapeDtypeStruct((b, s, w), BF16),
        scratch_shapes=[pltpu.VMEM((NA_HEADS, GRID_W, NA_WIN_ROWS * GRID_W + c), F32),
                        pltpu.VMEM((NA_HEADS, GRID_W, NA_WIN_ROWS * GRID_W + c), BF16)],
        compiler_params=_cparams(("arbitrary", "arbitrary")),
        name="nbr_attention",
    )(q, k, v, kc, vc, bias)


def _mix_kernel(x_ref, ya_ref, n1g_ref, sh1_ref, sc1_ref, g1_ref, sh2_ref, sc2_ref, wg_ref,
                lng_ref, ws_ref, bs_ref, wpa_ref, wpb_ref, wout_ref, n2g_ref, wq_ref, keys_ref,
                x1_ref, h2_ref, idx_ref, gate_ref, st_ref, sv_ref, si_ref, se_ref, sg_ref):
    tm = x_ref.shape[1]
    d = x_ref.shape[2]
    gw = lng_ref.shape[1]
    x = x_ref[0]
    h = x * lax.rsqrt(jnp.mean(x * x, axis=-1, keepdims=True) + EPS) * n1g_ref[...]
    g4 = _dot((h * (1.0 + sc1_ref[0]) + sh1_ref[0]).astype(BF16), wg_ref[...])
    ga = g4[:, 2 * gw:2 * gw + d]
    gb = g4[:, 2 * gw + d:]
    u = jax.nn.gelu(g4[:, :gw])
    t = jax.nn.gelu(g4[:, gw:2 * gw])
    tc = t - jnp.mean(t, axis=-1, keepdims=True)
    vn = tc * lax.rsqrt(jnp.mean(tc * tc, axis=-1, keepdims=True) + EPS) * lng_ref[...]
    vnb = vn.astype(BF16)
    lane = lax.broadcasted_iota(jnp.int32, (GM_CHUNK, LANES), 1)
    gd = gw // GM_GROUPS
    chunks = []
    for c in range(tm // GM_CHUNK):
        pairs = []
        for gp in range(gw // LANES):
            vp = vnb[c * GM_CHUNK:(c + 1) * GM_CHUNK, gp * LANES:(gp + 1) * LANES]
            r0 = _dot(ws_ref[2 * gp], vp)
            r1 = _dot(ws_ref[2 * gp + 1], vp)
            pairs.append(jnp.where(lane < gd, r0, r1))
        chunks.append(jnp.concatenate(pairs, axis=1) + bs_ref[...])
    mixed = jnp.concatenate(chunks, axis=0)
    yb = (u * mixed).astype(BF16)
    pa = _dot(ya_ref[0], wpa_ref[...])
    pb = _dot(yb, wpb_ref[...])
    m = jax.nn.sigmoid(ga) * pa + jax.nn.sigmoid(gb) * pb
    out = _dot(m.astype(BF16), wout_ref[...])
    x1 = x + g1_ref[0] * out
    x1_ref[0] = x1
    ms = jnp.mean(x1 * x1, axis=-1, keepdims=True)
    h2 = x1 * lax.rsqrt(ms + EPS) * n2g_ref[...]
    h2 = (h2 * (1.0 + sc2_ref[0]) + sh2_ref[0]).astype(BF16)
    half = h2.shape[1] // 2
    h2_ref[0] = _pack_bf16_pairs(h2[:, :half], h2[:, half:])
    qp = _dot(h2, wq_ref[...]).astype(BF16)
    for hp in range(keys_ref.shape[0]):
        st_ref[hp] = _dot_nt(keys_ref[hp], qp[:, hp * PEER_HALF:(hp + 1) * PEER_HALF])
    _topk_body(st_ref, idx_ref.at[0], gate_ref.at[0], sv_ref, si_ref, se_ref, sg_ref)


def _mix(x, ya, n1g, sh1, sc1, g1, sh2, sc2, wg, lng, ws, bs_full, wpa, wpb, wout, n2g, wq, keys, tm):
    b, s, d = x.shape
    gw = lng.shape[1]
    nt = s // tm
    nhp = keys.shape[0]
    ne = (nhp // 2) * PEER_TOPK
    assert s % tm == 0 and tm % LANES == 0
    act = lambda w: pl.BlockSpec((1, tm, w), lambda i, j: (i, j, 0))
    vec = pl.BlockSpec((1, 1, d), lambda i, j: (i, 0, 0))

    def const(a):
        nd = a.ndim
        return pl.BlockSpec(a.shape, lambda i, j: (0,) * nd)

    return pl.pallas_call(
        _mix_kernel,
        grid=(b, nt),
        in_specs=[act(d), act(ya.shape[2]), const(n1g), vec, vec, vec, vec, vec, const(wg),
                  const(lng), const(ws), const(bs_full), const(wpa), const(wpb), const(wout),
                  const(n2g), const(wq), const(keys)],
        out_specs=[act(d), act(d // 2), act(ne), act(ne)],
        out_shape=[jax.ShapeDtypeStruct((b, s, d), F32),
                   jax.ShapeDtypeStruct((b, s, d // 2), jnp.uint32),
                   jax.ShapeDtypeStruct((b, s, ne), jnp.int32),
                   jax.ShapeDtypeStruct((b, s, ne), F32)],
        scratch_shapes=[pltpu.VMEM((nhp, PEER_N_KEYS, tm), F32),
                        pltpu.VMEM((nhp, PEER_TOPK, LANES), F32),
                        pltpu.VMEM((nhp, PEER_TOPK, LANES), F32),
                        pltpu.VMEM((ne, LANES), F32),
                        pltpu.VMEM((ne, LANES), F32)],
        compiler_params=_cparams(("arbitrary", "arbitrary")),
        name="mix_peer_topk",
    )(x, ya, n1g, sh1, sc1, g1, sh2, sc2, wg, lng, ws, bs_full, wpa, wpb, wout, n2g, wq, keys)


def _extract_top(vals, order, payload, k):
    out_v, out_p = [], []
    for _ in range(k):
        m = jnp.max(vals, axis=0, keepdims=True)
        o = jnp.min(jnp.where(vals == m, order, jnp.inf), axis=0, keepdims=True)
        sel = order == o
        out_v.append(m)
        if payload is order:
            out_p.append(o)
        else:
            out_p.append(jnp.max(jnp.where(sel, payload, -1.0), axis=0, keepdims=True))
        vals = jnp.where(sel, -jnp.inf, vals)
    return jnp.concatenate(out_v, axis=0), jnp.concatenate(out_p, axis=0)


def _topk_body(st_ref, idx_ref, gate_ref, sv_ref, si_ref, se_ref, sg_ref):
    nhp = st_ref.shape[0]
    tm = st_ref.shape[2]
    k = PEER_TOPK
    kidx = lax.broadcasted_iota(jnp.int32, (PEER_N_KEYS, LANES), 0).astype(F32)
    sub = lax.broadcasted_iota(jnp.int32, (SUBLANES, LANES), 0).astype(F32)

    def lane_tile(lt, carry):
        lanes = pl.ds(pl.multiple_of(lt * LANES, LANES), LANES)

        def stage1(h, c):
            for hp in (4 * h, 4 * h + 1, 4 * h + 2, 4 * h + 3):
                v, i = _extract_top(st_ref[hp, :, lanes], kidx, kidx, k)
                sv_ref[hp] = v
                si_ref[hp] = i
            return c

        lax.fori_loop(0, nhp // 4, stage1, 0)

        def stage2(h2, c):
            for h in (2 * h2, 2 * h2 + 1):
                stage2_head(h)
            return c

        def stage2_head(h):
            s1, s2 = sv_ref[2 * h], sv_ref[2 * h + 1]
            i1, i2 = si_ref[2 * h] * float(PEER_N_KEYS), si_ref[2 * h + 1]
            cv, co, ce = [], [], []
            for half in range(2):
                b0 = half * SUBLANES
                cv.append(s1[0:1] + s2[b0:b0 + SUBLANES])
                co.append(sub + float(b0))
                ce.append(i1[0:1] + i2[b0:b0 + SUBLANES])
            for a in range(1, SUBLANES):
                cv.append(s1[a:a + 1] + s2[0:SUBLANES])
                co.append(sub + float(a * k))
                ce.append(i1[a:a + 1] + i2[0:SUBLANES])
            cv.append(s1[SUBLANES:k] + s2[0:1])
            co.append((sub + float(SUBLANES)) * float(k))
            ce.append(i1[SUBLANES:k] + i2[0:1])
            bv, be = _extract_top(jnp.concatenate(cv, axis=0), jnp.concatenate(co, axis=0),
                                  jnp.concatenate(ce, axis=0), k)
            ex = jnp.exp(bv - bv[0:1])
            rows = pl.ds(pl.multiple_of(h * k, k), k)
            sg_ref[rows, :] = ex / jnp.sum(ex, axis=0, keepdims=True)
            se_ref[rows, :] = be

        lax.fori_loop(0, nhp // 4, stage2, 0)

        rows = pl.ds(pl.multiple_of(lt * LANES, LANES), LANES)
        idx_ref[rows, :] = se_ref[...].T.astype(jnp.int32)
        gate_ref[rows, :] = sg_ref[...].T
        return carry

    lax.fori_loop(0, tm // LANES, lane_tile, 0)


SC_LANES = 16
SC_WINDOW = 32
SC_ROW_BUFFERS = 4
SC_TOKENS_PER_STEP = 8
SC_WSUM_CHUNKS = 8
SC_BF16_TERMS = 4


def _pack_table_kernel(t_ref, o_ref):
    half = o_ref.shape[1]
    o_ref[...] = _pack_bf16_pairs(t_ref[:, :half].astype(BF16), t_ref[:, half:].astype(BF16))


def _pack_table(t, tn):
    n, d = t.shape
    assert n % tn == 0
    return pl.pallas_call(
        _pack_table_kernel, grid=(n // tn,),
        in_specs=[pl.BlockSpec((tn, d), lambda i: (i, 0))],
        out_specs=pl.BlockSpec((tn, d // 2), lambda i: (i, 0)),
        out_shape=jax.ShapeDtypeStruct((n, d // 2), jnp.uint32),
        compiler_params=_cparams(("arbitrary",)), name="pack_table",
    )(t)


def _sc_bf16(words):
    return plsc.bitcast(words, BF16)


def _sc_phase(table, idx, operand, mode):
    t, ne = idx.shape
    hw = table.shape[1]
    d = 2 * hw
    L = SC_LANES
    win = SC_WINDOW
    nbuf = SC_ROW_BUFFERS
    ahead = nbuf - 1
    tps = SC_TOKENS_PER_STEP
    nq = SC_WSUM_CHUNKS
    nterm = SC_BF16_TERMS
    info = plsc.get_sparse_core_info()
    nc, ns = info.num_cores, info.num_subcores
    nss = t // (nc * ns * tps)
    wps = tps * ne // win
    wpt = ne // win
    blk = tps * ne
    nwin = nss * wps
    assert nss * nc * ns * tps == t and nss % 2 == 0 and wps % nbuf == 0 and wps > ahead
    assert wpt * win == ne and win % L == 0 and hw % (nq * L) == 0 and hw % (nterm * L) == 0 and win % nterm == 0
    dots = mode == "dots"
    assert dots or nbuf % wpt == 0
    op_buf = pltpu.VMEM((2, tps, hw), jnp.uint32) if dots else pltpu.VMEM((2 * blk,), jnp.int32)
    res_buf = pltpu.VMEM((2 * blk,), F32) if dots else pltpu.VMEM((2, tps, d), F32)
    out_type = jax.ShapeDtypeStruct((t * ne,), F32) if dots else jax.ShapeDtypeStruct((t, d), F32)
    scratch = [("idx", pltpu.VMEM((2 * blk,), jnp.int32)), ("op_" + mode, op_buf),
               ("rows", pltpu.VMEM((nbuf, win, hw), jnp.uint32)), ("res_" + mode, res_buf),
               ("isem", pltpu.SemaphoreType.DMA((2,))), ("psem", pltpu.SemaphoreType.DMA((2,))),
               ("gsem", pltpu.SemaphoreType.DMA((nbuf,))), ("osem", pltpu.SemaphoreType.DMA((2,)))]
    if not dots:
        scratch += [("act_stage", pltpu.VMEM((2 * blk,), F32)), ("gate_stage", pltpu.VMEM((2 * blk,), F32)),
                    ("asem", pltpu.SemaphoreType.DMA((2,)))]

    def sc_kernel(table_hbm, idx_hbm, op_hbm, out_hbm, idx_v, op_v, rows_v, res_v, isem, psem, gsem, osem,
                  act_v=None, gate_v=None, asem=None, act_hbm=None):
        ss0 = (lax.axis_index("s") * nc + lax.axis_index("c")) * nss
        lane = lax.iota(jnp.int32, L)

        def flat(ref, n):
            return ref.at[pl.ds(pl.multiple_of(n * blk, blk), blk)]

        def tok_rows(ref, n):
            return ref.at[pl.ds(pl.multiple_of(n * tps, tps), tps)]

        def idx_copy(ss, slot):
            return pltpu.make_async_copy(flat(idx_hbm, ss0 + ss), flat(idx_v, slot), isem.at[slot])

        def op_copies(ss, slot):
            if dots:
                return [pltpu.make_async_copy(tok_rows(op_hbm, ss0 + ss), op_v.at[slot], psem.at[slot])]
            return [pltpu.make_async_copy(flat(act_hbm, ss0 + ss), flat(act_v, slot), asem.at[slot]),
                    pltpu.make_async_copy(flat(op_hbm, ss0 + ss), flat(gate_v, slot), psem.at[slot])]

        def op_start(ss, slot):
            for cp in op_copies(ss, slot):
                cp.start()

        def op_wait(ss, slot):
            for cp in op_copies(ss, slot):
                cp.wait()

        def make_weights(slot):
            def vec(i, c):
                at = pl.ds(pl.multiple_of(slot * blk + i * L, L), L)
                a = act_v[at]
                z = 0.7978845608028654 * (a + 0.044715 * (a * a * a))
                tanh = 1.0 - 2.0 / (jnp.exp(2.0 * z) + 1.0)
                w = 0.5 * a * (1.0 + tanh) * gate_v[at]
                op_v[at] = plsc.bitcast(plsc.pack(w, w, format=plsc.PackFormat.INTERLEAVED), jnp.int32)
                return c

            lax.fori_loop(0, blk // L, vec, 0)

        def out_copy(ss, slot):
            if dots:
                return pltpu.make_async_copy(flat(res_v, slot), flat(out_hbm, ss0 + ss), osem.at[slot])
            return pltpu.make_async_copy(res_v.at[slot], tok_rows(out_hbm, ss0 + ss), osem.at[slot])

        def gather(slot, hs, b):
            ix = idx_v.at[pl.ds(pl.multiple_of(slot * blk + hs * win, win), win)]
            return pltpu.make_async_copy(table_hbm.at[ix], rows_v.at[b], gsem.at[b])

        def reduce_dots(slot, hs, b):
            tok = hs // wpt

            def head(g, c):
                def chunk(jj, accs):
                    off = pl.multiple_of(jj * (nterm * L), nterm * L)
                    xs = [_sc_bf16(op_v[slot, tok, pl.ds(off + q * L, L)]) for q in range(nterm)]
                    new = []
                    for r in range(L):
                        p = _sc_bf16(rows_v[b, g * L + r, pl.ds(off, L)]) * xs[0]
                        for q in range(1, nterm):
                            p = p + _sc_bf16(rows_v[b, g * L + r, pl.ds(off + q * L, L)]) * xs[q]
                        lo, hi = plsc.unpack(p, format=plsc.PackFormat.INTERLEAVED)
                        new.append(accs[r] + lo + hi)
                    return tuple(new)

                accs = lax.fori_loop(0, hw // (nterm * L), chunk, tuple(jnp.zeros((L,), F32) for _ in range(L)))
                out = jnp.zeros((L,), F32)
                for r in range(L):
                    out = jnp.where(lane == r, jnp.sum(accs[r]), out)
                res_v[pl.ds(pl.multiple_of(slot * blk + hs * win + g * L, L), L)] = out
                return c

            lax.fori_loop(0, win // L, head, 0)

        def reduce_wsum(slot, hs, b):
            tok = hs // wpt
            wbase = slot * blk + hs * win
            first = b % wpt == 0

            def colgroup(cg, c):
                col = pl.multiple_of(cg * (nq * L), nq * L)
                if first:
                    accs = tuple(jnp.zeros((L,), F32) for _ in range(2 * nq))
                else:
                    accs = tuple(res_v[slot, tok, pl.ds(col + q * L, L)] for q in range(nq)) + \
                           tuple(res_v[slot, tok, pl.ds(hw + col + q * L, L)] for q in range(nq))

                def rowgroup(rg, accs):
                    accs = list(accs)
                    r = rg * nterm
                    wvec = op_v[pl.ds(pl.multiple_of(wbase + (r // L) * L, L), L)]
                    ws = [_sc_bf16(wvec.at[jnp.full((L,), r % L + k, jnp.int32)].get(mode="promise_in_bounds"))
                          for k in range(nterm)]
                    for q in range(nq):
                        p = _sc_bf16(rows_v[b, r, pl.ds(col + q * L, L)]) * ws[0]
                        for k in range(1, nterm):
                            p = p + _sc_bf16(rows_v[b, r + k, pl.ds(col + q * L, L)]) * ws[k]
                        lo, hi = plsc.unpack(p, format=plsc.PackFormat.INTERLEAVED)
                        accs[q] = accs[q] + lo
                        accs[nq + q] = accs[nq + q] + hi
                    return tuple(accs)

                accs = lax.fori_loop(0, win // nterm, rowgroup, accs)
                for q in range(nq):
                    res_v[slot, tok, pl.ds(col + q * L, L)] = accs[q]
                    res_v[slot, tok, pl.ds(hw + col + q * L, L)] = accs[nq + q]
                return c

            lax.fori_loop(0, hw // (nq * L), colgroup, 0)

        idx_copy(0, 0).start()
        op_start(0, 0)
        idx_copy(1, 1).start()
        op_start(1, 1)
        idx_copy(0, 0).wait()
        op_wait(0, 0)
        for a in range(ahead):
            gather(0, a, a).start()

        @pl.loop(0, nwin, step=nbuf)
        def _(i0):
            for b in range(nbuf):
                i = i0 + b
                ss = i // wps
                hs = i % wps
                slot = ss % 2
                gather(slot, hs, b).wait()
                nb = (b + ahead) % nbuf

                @pl.when(hs + ahead < wps)
                def _():
                    gather(slot, hs + ahead, nb).start()

                @pl.when(jnp.logical_and(hs + ahead >= wps, ss + 1 < nss))
                def _():
                    @pl.when(hs + ahead == wps)
                    def _():
                        idx_copy(ss + 1, 1 - slot).wait()
                        op_wait(ss + 1, 1 - slot)

                    gather(1 - slot, hs + ahead - wps, nb).start()

                @pl.when(jnp.logical_and(hs == 0, ss >= 2))
                def _():
                    out_copy(ss - 2, slot).wait()

                if dots:
                    reduce_dots(slot, hs, b)
                else:
                    @pl.when(hs == 0)
                    def _():
                        make_weights(slot)

                    reduce_wsum(slot, hs, b)

                @pl.when(hs + 1 == wps)
                def _():
                    out_copy(ss, slot).start()

                    @pl.when(ss + 2 < nss)
                    def _():
                        idx_copy(ss + 2, slot).start()
                        op_start(ss + 2, slot)

        out_copy(nss - 2, 0).wait()
        out_copy(nss - 1, 1).wait()

    op = operand if dots else operand.reshape(t * ne)
    return sc_kernel, scratch, out_type, (table, idx.reshape(t * ne), op)


def _sc_peer(tab_u, tab_v, idx, x_packed, gate):
    parts = [_sc_phase(tab_u, idx, x_packed, "dots"), _sc_phase(tab_v, idx, gate, "wsum")]
    names, types = [], []
    for _, scratch, _, _ in parts:
        for name, ty in scratch:
            if name not in names:
                names.append(name)
                types.append(ty)
    n = len(parts)

    def fused(*refs):
        ins, outs, scr = refs[:3 * n], refs[3 * n:4 * n], dict(zip(names, refs[4 * n:]))
        (dots_body, dots_scratch, _, _), (wsum_body, wsum_scratch, _, _) = parts
        dots_body(*ins[0:3], outs[0], *[scr[name] for name, _ in dots_scratch])
        wsum_body(*ins[3:6], outs[1], *[scr[name] for name, _ in wsum_scratch], outs[0])

    call = pl.kernel(
        fused, mesh=plsc.VectorSubcoreMesh(core_axis_name="c", subcore_axis_name="s"),
        out_type=tuple(p[2] for p in parts), scratch_types=types,
        compiler_params=pltpu.CompilerParams(needs_layout_passes=False),
        name="peer_dots_wsum",
    )
    return call(*[a for p in parts for a in p[3]])


def _final_kernel(x1_ref, p_ref, g2_ref, fg_ref, prev_ref, o_ref):
    del prev_ref
    x2 = x1_ref[0] + g2_ref[0] * p_ref[0]
    ms = jnp.mean(x2 * x2, axis=-1, keepdims=True)
    o_ref[0] = x2 * lax.rsqrt(ms + EPS) * fg_ref[...]


def _final(x1, peer, g2, fg, tm, out, b0, s0):
    bc, s, d = x1.shape
    assert s % tm == 0 and s0 % tm == 0
    j0 = s0 // tm
    act = pl.BlockSpec((1, tm, d), lambda i, j: (i, j, 0))
    return pl.pallas_call(
        _final_kernel, grid=(bc, s // tm),
        in_specs=[act, act, pl.BlockSpec((1, 1, d), lambda i, j: (i, 0, 0)),
                  pl.BlockSpec((1, d), lambda i, j: (0, 0)),
                  pl.BlockSpec(memory_space=pl.ANY)],
        out_specs=pl.BlockSpec((1, tm, d), lambda i, j: (i + b0, j + j0, 0)),
        out_shape=jax.ShapeDtypeStruct(out.shape, F32),
        input_output_aliases={4: 0},
        compiler_params=_cparams(("arbitrary", "arbitrary")), name="final_norm",
    )(x1, peer, g2, fg, out)


def kernel(x, c, ctx, c_ctx, ada_w, ada_b, norm1_g, norm2_g, w_in, na_rpb, gm_ln_g, gm_ws, gm_bs,
           w_proj_a, w_proj_b, w_out, peer_wq, peer_keys, peer_u, peer_v, final_g):
    b, s, d = x.shape
    naw = NA_HEADS * HEAD_DIM
    gw = gm_ln_g.shape[1]
    layer = 0

    pad = (-(b + 1)) % SUBLANES
    c_all = jnp.concatenate([c, c_ctx[None, :], jnp.zeros((pad, d), F32)], axis=0)
    mod = _adaln(c_all, ada_w[layer], ada_b[layer])
    sh1, sc1, g1, sh2, sc2, g2 = [mod[:b, i * d:(i + 1) * d].reshape(b, 1, d) for i in range(ADA_CHUNKS)]
    csh1 = jnp.broadcast_to(mod[b, 0:d].reshape(1, 1, d), (b, 1, d))
    csc1 = jnp.broadcast_to(mod[b, d:2 * d].reshape(1, 1, d), (b, 1, d))

    w = w_in[layer].astype(BF16)
    w_qkv, w_kv, w_gates = w[:, :3 * naw], w[:, naw:3 * naw], w[:, 3 * naw:]
    n1g = norm1_g[layer].reshape(1, d)
    bias = _bias_table(na_rpb[layer])
    bs_full = jnp.repeat(gm_bs[layer].T, gw // GM_GROUPS, axis=1)
    keys = peer_keys[layer].reshape(2 * PEER_HEADS, PEER_N_KEYS, PEER_HALF).astype(BF16)
    lng = gm_ln_g[layer].reshape(1, gw)
    ws = gm_ws[layer].astype(BF16)
    wpa, wpb, wout = w_proj_a[layer].astype(BF16), w_proj_b[layer].astype(BF16), w_out[layer].astype(BF16)
    n2g = norm2_g[layer].reshape(1, d)
    wq = peer_wq[layer].astype(BF16)
    tab_u, tab_v = _pack_table(peer_u[layer], TILE_PACK), _pack_table(peer_v[layer], TILE_PACK)
    fg = final_g.reshape(1, d)
    ne = PEER_HEADS * PEER_TOPK

    assert sum(BATCH_CHUNKS) == b
    out = jnp.zeros((b, s, d), F32)

    b0 = 0
    for bc in BATCH_CHUNKS:
        sl = slice(b0, b0 + bc)
        boff = b0
        b0 += bc
        t = bc * s
        xc = x[sl]
        q, k, v = _norm_proj(xc, n1g, sh1[sl], sc1[sl], w_qkv, (naw, naw, naw), TILE_PROJ)
        k_c, v_c = _norm_proj(ctx[sl], n1g, csh1[sl], csc1[sl], w_kv, (naw, naw), ctx.shape[1])
        y_a = _attention(q, k, v, k_c, v_c, bias)
        pieces = FIRST_CHUNK_SPLITS if boff == 0 else 1
        sp = s // pieces
        for s0 in range(0, s, sp):
            xs, ys = (xc, y_a) if pieces == 1 else (xc[:, s0:s0 + sp], y_a[:, s0:s0 + sp])
            t = bc * sp
            x1, h2, idx, gate = _mix(xs, ys, n1g, sh1[sl], sc1[sl], g1[sl], sh2[sl], sc2[sl], w_gates, lng, ws,
                                     bs_full, wpa, wpb, wout, n2g, wq, keys, TILE_MIX)
            _, peer = _sc_peer(tab_u, tab_v, idx.reshape(t, ne), h2.reshape(t, d // 2), gate.reshape(t, ne))
            out = _final(x1, peer.reshape(x1.shape), g2[sl], fg, TILE_FINAL, out, boff, s0)
    return out
```
